```python
import math
import jax, jax.numpy as jnp
from jax import lax
import numpy as np

D_MODEL = 1024
BATCH = 8
SEQ = 4096
DEPTH = 2

N_EVEN = (DEPTH + 1) // 2
N_ODD = DEPTH // 2
MIX_WIDTH = D_MODEL
MLSTM_WIDTH = MIX_WIDTH // 2
MLSTM_HEADS = 4
MLSTM_HEAD_DIM = MLSTM_WIDTH // MLSTM_HEADS
MLSTM_CONV = 4
MLSTM_CHUNK = 64
DIFF_WIDTH = MIX_WIDTH - MLSTM_WIDTH
DIFF_HEADS = 4
DIFF_V_DIM = DIFF_WIDTH // DIFF_HEADS
DIFF_HEAD_DIM = DIFF_V_DIM // 2
Q_BLOCK = 128
ROPE_THETA = 10000.0
CONV_WIDTH = 3
N_GROUPS = 4
EXPERTS_PER_GROUP = 8
EXPERT_FF = 256
EPS = 1e-6
AB_SPLITS = [2 * MLSTM_WIDTH, MLSTM_WIDTH, MLSTM_WIDTH, 2 * MLSTM_HEADS, DIFF_WIDTH, DIFF_WIDTH, DIFF_WIDTH]
AB_IN_COLS = sum(AB_SPLITS)

kernel_name = 'hybrid_mlstm_diffattn_shortconv_hmoe'


def rmsnorm(x, g):
    xf = x.astype(jnp.float32)
    y = xf * lax.rsqrt(jnp.mean(xf * xf, axis=-1, keepdims=True) + EPS)
    return (y * g.astype(jnp.float32)).astype(x.dtype)


def rope(x, positions):
    d = x.shape[-1]
    half = d // 2
    inv_freq = 1.0 / (ROPE_THETA ** (jnp.arange(half, dtype=jnp.float32) / half))
    ang = positions.astype(jnp.float32)[..., None] * inv_freq
    cos = jnp.cos(ang)[:, :, None, :]
    sin = jnp.sin(ang)[:, :, None, :]
    xf = x.astype(jnp.float32)
    x1, x2 = xf[..., :half], xf[..., half:]
    return jnp.concatenate([x1 * cos - x2 * sin, x2 * cos + x1 * sin], axis=-1).astype(x.dtype)


def causal_depthwise_conv(x, w):
    k, c = w.shape
    return lax.conv_general_dilated(
        x, w.reshape(k, 1, c).astype(x.dtype), window_strides=(1,), padding=[(k - 1, 0)],
        dimension_numbers=('NWC', 'WIO', 'NWC'), feature_group_count=c)


def mlstm_chunkwise(q, k, v, i_pre, f_pre):
    B, H, S, d = q.shape
    L = MLSTM_CHUNK
    nc = S // L
    to_chunks = lambda t: jnp.moveaxis(t.astype(jnp.float32).reshape(B, H, nc, L, *t.shape[3:]), 2, 0)
    qc, kc, vc = to_chunks(q), to_chunks(k), to_chunks(v)
    logf = to_chunks(jax.nn.log_sigmoid(f_pre.astype(jnp.float32)))
    ig = to_chunks(i_pre)
    tril = jnp.tril(jnp.ones((L, L), dtype=bool))

    def step(carry, xs):
        C, n, m = carry
        qb, kb, vb, lf, ib = xs
        b = jnp.cumsum(lf, axis=-1)
        dmat = jnp.where(tril, b[..., :, None] - b[..., None, :] + ib[..., None, :], -jnp.inf)
        inter = b + m[..., None]
        m_row = jnp.maximum(inter, jnp.max(dmat, axis=-1))
        w = jnp.exp(dmat - m_row[..., None])
        s = jnp.einsum('bhid,bhjd->bhij', qb, kb) * w
        inter_w = jnp.exp(inter - m_row)
        num = jnp.einsum('bhij,bhjd->bhid', s, vb) + inter_w[..., None] * jnp.einsum('bhid,bhde->bhie', qb, C)
        den = jnp.sum(s, axis=-1) + inter_w * jnp.einsum('bhid,bhd->bhi', qb, n)
        h = num / jnp.maximum(jnp.abs(den), jnp.exp(-m_row))[..., None]
        b_last = b[..., -1]
        decay = b_last[..., None] - b + ib
        m_new = jnp.maximum(b_last + m, jnp.max(decay, axis=-1))
        wr = jnp.exp(decay - m_new[..., None])
        carry_w = jnp.exp(b_last + m - m_new)
        C_new = carry_w[..., None, None] * C + jnp.einsum('bhj,bhjd,bhje->bhde', wr, kb, vb)
        n_new = carry_w[..., None] * n + jnp.einsum('bhj,bhjd->bhd', wr, kb)
        return (C_new, n_new, m_new), h

    init = (jnp.zeros((B, H, d, d), jnp.float32), jnp.zeros((B, H, d), jnp.float32), jnp.zeros((B, H), jnp.float32))
    _, hs = lax.scan(step, init, (qc, kc, vc, logf, ig))
    return jnp.moveaxis(hs, 0, 2).reshape(B, H, S, d)


def diff_attention(q, k, v, lam):
    B, H, _, S, d = q.shape
    nb = S // Q_BLOCK
    q_blocks = q.reshape(B, H, 2, nb, Q_BLOCK, d).transpose(3, 0, 1, 2, 4, 5)
    key_pos = jnp.arange(S)
    scale = d ** -0.5

    def one_block(args):
        qi, bi = args
        s = jnp.einsum('bhcqd,bhckd->bhcqk', qi, k).astype(jnp.float32) * scale
        q_pos = bi * Q_BLOCK + jnp.arange(Q_BLOCK)
        s = jnp.where(key_pos[None, :] <= q_pos[:, None], s, -jnp.inf)
        p = jax.nn.softmax(s, axis=-1)
        a = p[:, :, 0] - lam * p[:, :, 1]
        return jnp.einsum('bhqk,bhkd->bhqd', a.astype(v.dtype), v)

    out = lax.map(one_block, (q_blocks, jnp.arange(nb)))
    return out.transpose(1, 0, 3, 2, 4).reshape(B, S, H, v.shape[-1])


def mlstm_diff_mixer(h, positions, w_in, gate_bias, conv_w, mlstm_norm, lam_vecs, diff_norm, w_out, layer_idx):
    B, S, _ = h.shape
    idx = list(np.cumsum(AB_SPLITS)[:-1])
    m_qk, m_v, m_o, m_gates, d_q, d_k, d_v = jnp.split(h @ w_in, idx, axis=-1)
    m_qk = jax.nn.silu(causal_depthwise_conv(m_qk, conv_w))
    m_q, m_k = jnp.split(m_qk, 2, axis=-1)
    gates = (m_gates + gate_bias).astype(jnp.float32)
    i_pre = gates[..., :MLSTM_HEADS].transpose(0, 2, 1)
    f_pre = gates[..., MLSTM_HEADS:].transpose(0, 2, 1)
    heads = lambda t: t.reshape(B, S, MLSTM_HEADS, MLSTM_HEAD_DIM).transpose(0, 2, 1, 3)
    hm = mlstm_chunkwise(heads(m_q) * MLSTM_HEAD_DIM ** -0.5, heads(m_k), heads(m_v), i_pre, f_pre)
    hm = rmsnorm(hm.transpose(0, 2, 1, 3), mlstm_norm).astype(h.dtype)
    hm = hm * jax.nn.sigmoid(m_o.reshape(B, S, MLSTM_HEADS, MLSTM_HEAD_DIM))
    dq = rope(d_q.reshape(B, S, 2 * DIFF_HEADS, DIFF_HEAD_DIM), positions)
    dk = rope(d_k.reshape(B, S, 2 * DIFF_HEADS, DIFF_HEAD_DIM), positions)
    dq = dq.reshape(B, S, DIFF_HEADS, 2, DIFF_HEAD_DIM).transpose(0, 2, 3, 1, 4)
    dk = dk.reshape(B, S, DIFF_HEADS, 2, DIFF_HEAD_DIM).transpose(0, 2, 3, 1, 4)
    dv = d_v.reshape(B, S, DIFF_HEADS, DIFF_V_DIM).transpose(0, 2, 1, 3)
    lam_init = 0.8 - 0.6 * math.exp(-0.3 * layer_idx)
    lv = lam_vecs.astype(jnp.float32)
    lam = jnp.exp(jnp.sum(lv[0] * lv[1])) - jnp.exp(jnp.sum(lv[2] * lv[3])) + lam_init
    hd = diff_attention(dq, dk, dv, lam)
    hd = rmsnorm(hd, diff_norm) * (1.0 - lam_init)
    merged = jnp.concatenate([hm.reshape(B, S, MLSTM_WIDTH), hd.reshape(B, S, DIFF_WIDTH)], axis=-1)
    return merged @ w_out


def short_conv_mixer(h, w_in, conv_w, w_out):
    b_gate, c_gate, xt = jnp.split(h @ w_in, 3, axis=-1)
    return (b_gate * causal_depthwise_conv(c_gate * xt, conv_w)) @ w_out


def hier_moe(x, w_rg, b_rg, w_re, b_re, w_gate, w_up, w_down):
    B, S, D = x.shape
    xt = x.reshape(B * S, D)
    g_prob = jax.nn.softmax((xt @ w_rg + b_rg).astype(jnp.float32), axis=-1)
    g_val, g_idx = lax.top_k(g_prob, 1)
    e_logits = (xt @ w_re + b_re).astype(jnp.float32).reshape(-1, N_GROUPS, EXPERTS_PER_GROUP)
    e_sel = jnp.take_along_axis(e_logits, g_idx[:, :, None], axis=1)[:, 0]
    e_val, e_idx = lax.top_k(e_sel, 2)
    e_w = jax.nn.softmax(e_val, axis=-1)
    within = jnp.sum(jax.nn.one_hot(e_idx, EXPERTS_PER_GROUP, dtype=jnp.float32) * e_w[..., None], axis=1)
    gates = (g_val * jax.nn.one_hot(g_idx[:, 0], N_GROUPS, dtype=jnp.float32))[:, :, None] * within[:, None, :]
    gates = gates.astype(x.dtype)
    y = jnp.zeros_like(xt)
    for g in range(N_GROUPS):
        hg = jax.nn.silu(jnp.einsum('td,edf->tef', xt, w_gate[g])) * jnp.einsum('td,edf->tef', xt, w_up[g])
        y = y + jnp.einsum('tef,efd->td', hg * gates[:, g, :, None], w_down[g])
    return y.reshape(B, S, D)


def setup_inputs(seed: int = 0) -> dict:
    key = jax.random.key(seed)
    ks = jax.random.split(key, 24)
    nrm = lambda k, shape, s: jax.random.normal(k, shape, jnp.float32) * s
    D = D_MODEL
    i_bias = nrm(ks[5], (N_EVEN, MLSTM_HEADS), 0.1)
    f_bias = jnp.linspace(3.0, 6.0, MLSTM_HEADS, dtype=jnp.float32)[None, :] + nrm(ks[6], (N_EVEN, MLSTM_HEADS), 0.01)
    return {
        'x': nrm(ks[0], (BATCH, SEQ, D), 1.0),
        'positions': jnp.broadcast_to(jnp.arange(SEQ, dtype=jnp.int32), (BATCH, SEQ)),
        'norm_mix': 1.0 + nrm(ks[1], (DEPTH, D), 0.01),
        'norm_ffn': 1.0 + nrm(ks[2], (DEPTH, D), 0.01),
        'ab_w_in': nrm(ks[3], (N_EVEN, D, AB_IN_COLS), D ** -0.5),
        'ab_gate_bias': jnp.concatenate([i_bias, f_bias], axis=-1),
        'ab_conv_w': nrm(ks[4], (N_EVEN, MLSTM_CONV, 2 * MLSTM_WIDTH), MLSTM_CONV ** -0.5),
        'ab_mlstm_norm': 1.0 + nrm(ks[7], (N_EVEN, MLSTM_HEAD_DIM), 0.01),
        'ab_lambda': nrm(ks[8], (N_EVEN, 4, DIFF_HEAD_DIM), 0.1),
        'ab_diff_norm': 1.0 + nrm(ks[9], (N_EVEN, DIFF_V_DIM), 0.01),
        'ab_w_out': nrm(ks[10], (N_EVEN, MIX_WIDTH, D), MIX_WIDTH ** -0.5),
        'c_w_in': nrm(ks[11], (N_ODD, D, 3 * D), D ** -0.5),
        'c_conv_w': nrm(ks[12], (N_ODD, CONV_WIDTH, D), CONV_WIDTH ** -0.5),
        'c_w_out': nrm(ks[13], (N_ODD, D, D), D ** -0.5),
        'moe_w_rg': nrm(ks[14], (DEPTH, D, N_GROUPS), D ** -0.5),
        'moe_b_rg': nrm(ks[15], (DEPTH, N_GROUPS), 0.01),
        'moe_w_re': nrm(ks[16], (DEPTH, D, N_GROUPS * EXPERTS_PER_GROUP), D ** -0.5),
        'moe_b_re': nrm(ks[17], (DEPTH, N_GROUPS * EXPERTS_PER_GROUP), 0.01),
        'moe_w_gate': nrm(ks[18], (DEPTH, N_GROUPS, EXPERTS_PER_GROUP, D, EXPERT_FF), D ** -0.5),
        'moe_w_up': nrm(ks[19], (DEPTH, N_GROUPS, EXPERTS_PER_GROUP, D, EXPERT_FF), D ** -0.5),
        'moe_w_down': nrm(ks[20], (DEPTH, N_GROUPS, EXPERTS_PER_GROUP, EXPERT_FF, D), EXPERT_FF ** -0.5),
        'final_norm': 1.0 + nrm(ks[21], (D,), 0.01),
    }


def reference(x, positions, norm_mix, norm_ffn, ab_w_in, ab_gate_bias, ab_conv_w, ab_mlstm_norm,
              ab_lambda, ab_diff_norm, ab_w_out, c_w_in, c_conv_w, c_w_out, moe_w_rg, moe_b_rg,
              moe_w_re, moe_b_re, moe_w_gate, moe_w_up, moe_w_down, final_norm):
    for layer in range(DEPTH):
        h = rmsnorm(x, norm_mix[layer])
        j = layer // 2
        if layer % 2 == 0:
            mix = mlstm_diff_mixer(h, positions, ab_w_in[j], ab_gate_bias[j], ab_conv_w[j], ab_mlstm_norm[j],
                                   ab_lambda[j], ab_diff_norm[j], ab_w_out[j], layer)
        else:
            mix = short_conv_mixer(h, c_w_in[j], c_conv_w[j], c_w_out[j])
        x = x + mix
        x = x + hier_moe(rmsnorm(x, norm_ffn[layer]), moe_w_rg[layer], moe_b_rg[layer], moe_w_re[layer],
                         moe_b_re[layer], moe_w_gate[layer], moe_w_up[layer], moe_w_down[layer])
    return rmsnorm(x, final_norm)
```

```python
import functools
import math

import jax
import jax.numpy as jnp
from jax import lax
from jax.experimental import pallas as pl
from jax.experimental.pallas import tpu as pltpu

F32 = jnp.float32
BF16 = jnp.bfloat16

EPS = 1e-6
ROPE_THETA = 10000.0
LANES = 128
MLSTM_HEADS = 4
MLSTM_HEAD_DIM = 128
MLSTM_CONV = 4
DIFF_HEADS = 4
DIFF_HEAD_DIM = 64
DIFF_V_DIM = 128
CONV_WIDTH = 3
N_GROUPS = 4
EXPERTS_PER_GROUP = 8
EXPERT_FF = 256
HALO = 8

ROW_TILE = 512
MLSTM_CHUNK = 256
ATTN_BLOCK = 256
MOE_TILE = 512
PERMUTE_ROWS = 256
VMEM_LIMIT = 56 * 1024 * 1024


def _dot(a, b):
    return jnp.dot(a, b, preferred_element_type=F32)


def _dot_nt(a, b):
    return lax.dot_general(a, b, (((1,), (1,)), ((), ())), preferred_element_type=F32)


def _dot_tn(a, b):
    return lax.dot_general(a, b, (((0,), (0,)), ((), ())), preferred_element_type=F32)


def _split3(x):
    hi = x.astype(BF16)
    r1 = x - hi.astype(F32)
    mid = r1.astype(BF16)
    lo = (r1 - mid.astype(F32)).astype(BF16)
    return hi, mid, lo


def _dot_f32ish(x, w_hi, w_lo):
    xh = x.astype(BF16)
    xl = (x - xh.astype(F32)).astype(BF16)
    return _dot(xh, w_hi) + (_dot(xl, w_hi) + _dot(xh, w_lo))


def _rms(x, g):
    return x * lax.rsqrt(jnp.mean(x * x, axis=-1, keepdims=True) + EPS) * g


def _sigmoid(x):
    return 1.0 / (1.0 + jnp.exp(-x))


def _params(*sem):
    return pltpu.CompilerParams(dimension_semantics=sem, vmem_limit_bytes=VMEM_LIMIT)


def _full(shape):
    return pl.BlockSpec(shape, lambda *_: (0,) * len(shape))


def _proj0_kernel(x_ref, pos_ref, g_ref, w_ref, wgh_ref, wgl_ref, gb_ref, invf_ref,
                  mqk_ref, mv_ref, mo_ref, dq_ref, dk_ref, dv_ref, gates_ref):
    xn = _rms(x_ref[...], g_ref[...])
    xb = xn.astype(BF16)

    def seg(lo, hi):
        return _dot(xb, w_ref[:, lo:hi])

    mqk_ref[...] = seg(0, 1024).astype(BF16)
    mv_ref[...] = seg(1024, 1536).astype(BF16)
    mo_ref[...] = seg(1536, 2048).astype(BF16)
    dv_ref[...] = seg(3072, 3584).astype(BF16)

    ang = pos_ref[...].astype(F32) * invf_ref[...]
    c = jnp.cos(ang)
    s = jnp.sin(ang)
    c4 = jnp.concatenate([c, c, c, c], axis=1)
    s4 = jnp.concatenate([s, s, s, s], axis=1)
    q_scale = DIFF_HEAD_DIM ** -0.5
    dq_ref[...] = ((seg(2048, 2560) * c4 + seg(3584, 4096) * s4) * q_scale).astype(BF16)
    dk_ref[...] = (seg(2560, 3072) * c4 + seg(4096, 4608) * s4).astype(BF16)

    gates_ref[...] = _dot_f32ish(xn, wgh_ref[...], wgl_ref[...]) + gb_ref[...]


def _proj0(x, pos, g, w, wgh, wgl, gb, invf):
    t, d = x.shape
    tm = ROW_TILE
    row = lambda n: pl.BlockSpec((tm, n), lambda i: (i, 0))
    out_shapes = [
        jax.ShapeDtypeStruct((t, 1024), BF16), jax.ShapeDtypeStruct((t, 512), BF16),
        jax.ShapeDtypeStruct((t, 512), BF16), jax.ShapeDtypeStruct((t, 512), BF16),
        jax.ShapeDtypeStruct((t, 512), BF16), jax.ShapeDtypeStruct((t, 512), BF16),
        jax.ShapeDtypeStruct((t, LANES), F32),
    ]
    return pl.pallas_call(
        _proj0_kernel,
        grid=(t // tm,),
        in_specs=[row(d), row(1), _full(g.shape), _full(w.shape), _full(wgh.shape), _full(wgl.shape),
                  _full(gb.shape), _full(invf.shape)],
        out_specs=[row(1024), row(512), row(512), row(512), row(512), row(512), row(LANES)],
        out_shape=out_shapes,
        compiler_params=_params("parallel"),
        name="proj0",
    )(x, pos, g, w, wgh, wgl, gb, invf)


def _mlstm_kernel(mqk_ref, mv_ref, mo_ref, gates_ref, cw_ref, norm_ref, hm_ref,
                  xe_ref, c_ref, n_ref, m_ref):
    L = MLSTM_CHUNK
    hd = MLSTM_HEAD_DIM
    width = MLSTM_HEADS * hd

    @pl.when(pl.program_id(1) == 0)
    def _():
        xe_ref[0:HALO, :] = jnp.zeros((HALO, 2 * width), F32)
        c_ref[...] = jnp.zeros_like(c_ref)
        n_ref[...] = jnp.zeros_like(n_ref)
        m_ref[...] = jnp.zeros_like(m_ref)

    xe_ref[HALO:HALO + L, :] = mqk_ref[...].astype(F32)
    cw = cw_ref[...]
    conv = cw[0:1, :] * xe_ref[pl.ds(HALO - 3, L), :]
    for j in range(1, MLSTM_CONV):
        conv = conv + cw[j:j + 1, :] * xe_ref[pl.ds(HALO - 3 + j, L), :]
    xe_ref[0:HALO, :] = xe_ref[L:L + HALO, :]
    qk = conv * _sigmoid(conv)
    q = qk[:, :width] * (hd ** -0.5)
    k = qk[:, width:]
    qb = q.astype(BF16)
    kb = k.astype(BF16)
    vb = mv_ref[...]

    gt = gates_ref[...]
    logf = jnp.minimum(gt, 0.0) - jnp.log(1.0 + jnp.exp(-jnp.abs(gt)))
    row = lax.broadcasted_iota(jnp.int32, (L, L), 0)
    col = lax.broadcasted_iota(jnp.int32, (L, L), 1)
    causal = col <= row
    tril = jnp.where(causal, 1.0, 0.0).astype(BF16)
    f_hi, f_mid, f_lo = _split3(logf)
    bcum = _dot(tril, f_hi) + (_dot(tril, f_mid) + _dot(tril, f_lo))
    bcum_t = bcum.T
    gt_t = gt.T

    for h in range(MLSTM_HEADS):
        sl = slice(h * hd, (h + 1) * hd)
        fl = MLSTM_HEADS + h
        b_col = bcum[:, fl:fl + 1]
        b_row = bcum_t[fl:fl + 1, :]
        ig_row = gt_t[h:h + 1, :]
        ig_col = gt[:, h:h + 1]
        m_prev = m_ref[h:h + 1, 0:1]
        c_prev = c_ref[h]
        n_prev = n_ref[h:h + 1, :]

        dmat = jnp.where(causal, b_col - b_row + ig_row, -jnp.inf)
        inter = b_col + m_prev
        m_row = jnp.maximum(inter, jnp.max(dmat, axis=-1, keepdims=True))
        w = jnp.exp(dmat - m_row)
        s = _dot_nt(qb[:, sl], kb[:, sl]) * w
        inter_w = jnp.exp(inter - m_row)
        num = _dot(s.astype(BF16), vb[:, sl]) + inter_w * _dot(qb[:, sl], c_prev.astype(BF16))
        den = jnp.sum(s, axis=-1, keepdims=True) + inter_w * jnp.sum(q[:, sl] * n_prev, axis=-1, keepdims=True)
        hh = num / jnp.maximum(jnp.abs(den), jnp.exp(-m_row))

        b_last = bcum[L - 1:L, fl:fl + 1]
        decay = b_last - b_col + ig_col
        m_new = jnp.maximum(b_last + m_prev, jnp.max(decay, axis=0, keepdims=True))
        wr = jnp.exp(decay - m_new)
        carry_w = jnp.exp(b_last + m_prev - m_new)
        kw = k[:, sl] * wr
        c_ref[h] = carry_w * c_prev + _dot_tn(kw.astype(BF16), vb[:, sl])
        n_ref[h:h + 1, :] = carry_w * n_prev + jnp.sum(kw, axis=0, keepdims=True)
        m_ref[h:h + 1, :] = jnp.broadcast_to(m_new, (1, LANES))

        hn = _rms(hh, norm_ref[...])
        hm_ref[:, sl] = (hn * _sigmoid(mo_ref[:, sl].astype(F32))).astype(BF16)


def _mlstm(mqk, mv, mo, gates, cw, norm, batch, seq):
    t = mqk.shape[0]
    L = MLSTM_CHUNK
    nc = seq // L
    width = MLSTM_HEADS * MLSTM_HEAD_DIM
    row = lambda n: pl.BlockSpec((L, n), lambda b, c: (b * nc + c, 0))
    return pl.pallas_call(
        _mlstm_kernel,
        grid=(batch, nc),
        in_specs=[row(2 * width), row(width), row(width), row(LANES), _full(cw.shape), _full(norm.shape)],
        out_specs=row(width),
        out_shape=jax.ShapeDtypeStruct((t, width), BF16),
        scratch_shapes=[
            pltpu.VMEM((L + HALO, 2 * width), F32),
            pltpu.VMEM((MLSTM_HEADS, MLSTM_HEAD_DIM, MLSTM_HEAD_DIM), F32),
            pltpu.VMEM((8, MLSTM_HEAD_DIM), F32),
            pltpu.VMEM((8, LANES), F32),
        ],
        compiler_params=_params("parallel", "arbitrary"),
        name="mlstm",
    )(mqk, mv, mo, gates, cw, norm)


def _diff_attn_kernel(q_ref, k_ref, v_ref, lam_ref, norm_ref, o_ref, acc_ref, *, lam_init):
    tq = ATTN_BLOCK
    qi = pl.program_id(2)
    q = q_ref[...]
    lane = lax.broadcasted_iota(jnp.int32, q.shape, 1)
    zero = jnp.zeros_like(q)
    qs = (jnp.where(lane < DIFF_HEAD_DIM, q, zero), jnp.where(lane >= DIFF_HEAD_DIM, q, zero))
    row = lax.broadcasted_iota(jnp.int32, (tq, tq), 0)
    col = lax.broadcasted_iota(jnp.int32, (tq, tq), 1)
    causal = col <= row

    acc_ref[...] = jnp.zeros_like(acc_ref)

    def block(j, carry, masked):
        start = pl.multiple_of(j * tq, tq)
        kb = k_ref[pl.ds(start, tq), :]
        vb = v_ref[pl.ds(start, tq), :]
        out = []
        for c in range(2):
            m_old, l_old = carry[2 * c], carry[2 * c + 1]
            s = _dot_nt(qs[c], kb)
            if masked:
                s = jnp.where(causal, s, -jnp.inf)
            m_new = jnp.maximum(m_old, jnp.max(s, axis=-1, keepdims=True))
            p = jnp.exp(s - m_new)
            alpha = jnp.exp(m_old - m_new)
            l_new = alpha * l_old + jnp.sum(p, axis=-1, keepdims=True)
            acc_ref[c] = alpha * acc_ref[c] + _dot(p.astype(BF16), vb)
            out += [m_new, l_new]
        return tuple(out)

    neg = jnp.full((tq, 1), -jnp.inf, F32)
    zer = jnp.zeros((tq, 1), F32)
    carry = lax.fori_loop(0, qi, lambda j, cr: block(j, cr, False), (neg, zer, neg, zer))
    _, l1, _, l2 = block(qi, carry, True)

    lv = lam_ref[...]
    lam = (jnp.exp(jnp.sum(lv[0:1, :] * lv[1:2, :], axis=-1, keepdims=True))
           - jnp.exp(jnp.sum(lv[2:3, :] * lv[3:4, :], axis=-1, keepdims=True)) + lam_init)
    o = acc_ref[0] / l1 - lam * (acc_ref[1] / l2)
    o_ref[...] = (_rms(o, norm_ref[...]) * (1.0 - lam_init)).astype(BF16)


def _diff_attn(dq, dk, dv, lam_vecs, norm, batch, seq, lam_init):
    t = dq.shape[0]
    tq = ATTN_BLOCK
    nq = seq // tq
    return pl.pallas_call(
        functools.partial(_diff_attn_kernel, lam_init=lam_init),
        grid=(batch, DIFF_HEADS, nq),
        in_specs=[
            pl.BlockSpec((tq, DIFF_V_DIM), lambda b, h, i: (b * nq + i, h)),
            pl.BlockSpec((seq, DIFF_V_DIM), lambda b, h, i: (b, h)),
            pl.BlockSpec((seq, DIFF_V_DIM), lambda b, h, i: (b, h)),
            _full(lam_vecs.shape), _full(norm.shape),
        ],
        out_specs=pl.BlockSpec((tq, DIFF_V_DIM), lambda b, h, i: (b * nq + i, h)),
        out_shape=jax.ShapeDtypeStruct((t, DIFF_HEADS * DIFF_V_DIM), BF16),
        scratch_shapes=[pltpu.VMEM((2, tq, DIFF_V_DIM), F32)],
        compiler_params=_params("parallel", "parallel", "arbitrary"),
        name="diff_attn",
    )(dq, dk, dv, lam_vecs, norm)


GROUP_LANE = EXPERTS_PER_GROUP
EXPERT_LANE0 = 8


def _route(xn, wrh_ref, wrl_ref, br_ref):
    logits = _dot_f32ish(xn, wrh_ref[...], wrl_ref[...]) + br_ref[...]
    lane = lax.broadcasted_iota(jnp.int32, logits.shape, 1)
    ninf = -jnp.inf
    big = jnp.int32(LANES)

    is_g = lane < N_GROUPS
    gl = jnp.where(is_g, logits, ninf)
    gmax = jnp.max(gl, axis=-1, keepdims=True)
    gsum = jnp.sum(jnp.where(is_g, jnp.exp(gl - gmax), 0.0), axis=-1, keepdims=True)
    g_val = 1.0 / gsum
    g_idx = jnp.min(jnp.where(gl == gmax, lane, big), axis=-1, keepdims=True)

    e_sel = jnp.zeros_like(logits)
    for g in range(N_GROUPS):
        shift = LANES - (EXPERT_LANE0 + EXPERTS_PER_GROUP * g)
        e_sel = jnp.where(g_idx == g, pltpu.roll(logits, shift, 1), e_sel)
    is_e = lane < EXPERTS_PER_GROUP
    el = jnp.where(is_e, e_sel, ninf)
    v1 = jnp.max(el, axis=-1, keepdims=True)
    i1 = jnp.min(jnp.where(el == v1, lane, big), axis=-1, keepdims=True)
    el2 = jnp.where(lane == i1, ninf, el)
    v2 = jnp.max(el2, axis=-1, keepdims=True)
    i2 = jnp.min(jnp.where(el2 == v2, lane, big), axis=-1, keepdims=True)
    e21 = jnp.exp(v2 - v1)
    w1 = 1.0 / (1.0 + e21)
    w2 = e21 * w1
    gates = g_val * (jnp.where(lane == i1, w1, 0.0) + jnp.where(lane == i2, w2, 0.0))
    return jnp.where(lane == GROUP_LANE, g_idx.astype(F32), gates)


def _mix0_tail_kernel(x_ref, hm_ref, hd_ref, wo_ref, nf_ref, wrh_ref, wrl_ref, br_ref,
                      xmid_ref, xn_ref, route_ref):
    half = hm_ref.shape[1]
    mix = _dot(hm_ref[...], wo_ref[0:half, :]) + _dot(hd_ref[...], wo_ref[half:2 * half, :])
    xmid = x_ref[...] + mix
    xmid_ref[...] = xmid
    xn = _rms(xmid, nf_ref[...])
    xn_ref[...] = xn
    route_ref[...] = _route(xn, wrh_ref, wrl_ref, br_ref)


def _mix0_tail(x, hm, hd, wo, nf, wrh, wrl, br):
    t, d = x.shape
    tm = ROW_TILE
    row = lambda n: pl.BlockSpec((tm, n), lambda i: (i, 0))
    return pl.pallas_call(
        _mix0_tail_kernel,
        grid=(t // tm,),
        in_specs=[row(d), row(hm.shape[1]), row(hd.shape[1]), _full(wo.shape), _full(nf.shape),
                  _full(wrh.shape), _full(wrl.shape), _full(br.shape)],
        out_specs=[row(d), row(d), row(LANES)],
        out_shape=[jax.ShapeDtypeStruct((t, d), F32), jax.ShapeDtypeStruct((t, d), F32),
                   jax.ShapeDtypeStruct((t, LANES), F32)],
        compiler_params=_params("parallel"),
        name="mix0_tail",
    )(x, hm, hd, wo, nf, wrh, wrl, br)


def _proj1_kernel(x_ref, y_ref, g_ref, w_ref, x1_ref, b_ref, u_ref):
    d = x_ref.shape[1]
    x1 = x_ref[...] + y_ref[...]
    x1_ref[...] = x1
    xb = _rms(x1, g_ref[...]).astype(BF16)
    b_ref[...] = _dot(xb, w_ref[:, 0:d]).astype(BF16)
    u_ref[...] = (_dot(xb, w_ref[:, d:2 * d]) * _dot(xb, w_ref[:, 2 * d:3 * d])).astype(BF16)


def _proj1(x, y, g, w):
    t, d = x.shape
    tm = ROW_TILE
    row = lambda n: pl.BlockSpec((tm, n), lambda i: (i, 0))
    return pl.pallas_call(
        _proj1_kernel,
        grid=(t // tm,),
        in_specs=[row(d), row(d), _full(g.shape), _full(w.shape)],
        out_specs=[row(d), row(d), row(d)],
        out_shape=[jax.ShapeDtypeStruct((t, d), F32), jax.ShapeDtypeStruct((t, d), BF16),
                   jax.ShapeDtypeStruct((t, d), BF16)],
        compiler_params=_params("parallel"),
        name="proj1",
    )(x, y, g, w)


def _mix1_tail_kernel(x_ref, b_ref, u_ref, halo_ref, cw_ref, wo_ref, nf_ref, wrh_ref, wrl_ref, br_ref,
                      xmid_ref, xn_ref, route_ref, ue_ref, *, tiles_per_seq):
    tm = u_ref.shape[0]
    seq_start = pl.program_id(0) % tiles_per_seq == 0
    halo = halo_ref[...].astype(F32)
    ue_ref[0:HALO, :] = jnp.where(seq_start, jnp.zeros_like(halo), halo)
    ue_ref[HALO:HALO + tm, :] = u_ref[...].astype(F32)
    cw = cw_ref[...]
    conv = cw[0:1, :] * ue_ref[pl.ds(HALO - 2, tm), :]
    for j in range(1, CONV_WIDTH):
        conv = conv + cw[j:j + 1, :] * ue_ref[pl.ds(HALO - 2 + j, tm), :]
    mixed = (b_ref[...].astype(F32) * conv).astype(BF16)
    xmid = x_ref[...] + _dot(mixed, wo_ref[...])
    xmid_ref[...] = xmid
    xn = _rms(xmid, nf_ref[...])
    xn_ref[...] = xn
    route_ref[...] = _route(xn, wrh_ref, wrl_ref, br_ref)


def _mix1_tail(x, b, u, cw, wo, nf, wrh, wrl, br, seq):
    t, d = x.shape
    tm = ROW_TILE
    row = lambda n: pl.BlockSpec((tm, n), lambda i: (i, 0))
    halo = pl.BlockSpec((HALO, d), lambda i: (jnp.maximum(i * (tm // HALO) - 1, 0), 0))
    return pl.pallas_call(
        functools.partial(_mix1_tail_kernel, tiles_per_seq=seq // tm),
        grid=(t // tm,),
        in_specs=[row(d), row(d), row(d), halo, _full(cw.shape), _full(wo.shape), _full(nf.shape),
                  _full(wrh.shape), _full(wrl.shape), _full(br.shape)],
        out_specs=[row(d), row(d), row(LANES)],
        out_shape=[jax.ShapeDtypeStruct((t, d), F32), jax.ShapeDtypeStruct((t, d), F32),
                   jax.ShapeDtypeStruct((t, LANES), F32)],
        scratch_shapes=[pltpu.VMEM((tm + HALO, d), F32)],
        compiler_params=_params("parallel"),
        name="mix1_tail",
    )(x, b, u, u, cw, wo, nf, wrh, wrl, br)


def _permute_kernel(idx_ref, src_ref, dst_ref, sem, *, gather):
    r = PERMUTE_ROWS
    i = pl.program_id(0)
    slot = i % 2

    def row_copy(src_row, dst_row, s):
        return pltpu.make_async_copy(src_ref.at[pl.ds(src_row, 1)], dst_ref.at[pl.ds(dst_row, 1)], sem.at[s])

    def issue(j, _):
        other = idx_ref[0, 0, j]
        mine = i * r + j
        if gather:
            row_copy(other, mine, slot).start()
        else:
            row_copy(mine, other, slot).start()
        return 0

    lax.fori_loop(0, r, issue, 0)

    def drain(s):
        def body(j, _):
            row_copy(0, 0, s).wait()
            return 0
        lax.fori_loop(0, r, body, 0)

    @pl.when(i > 0)
    def _():
        drain(1 - slot)

    @pl.when(i == pl.num_programs(0) - 1)
    def _():
        drain(slot)


def _permute_rows(src, idx, gather):
    t, n = src.shape
    r = PERMUTE_ROWS
    idx3 = idx.reshape(t // r, 1, r)
    return pl.pallas_call(
        functools.partial(_permute_kernel, gather=gather),
        grid=(t // r,),
        in_specs=[pl.BlockSpec((1, 1, r), lambda i: (i, 0, 0), memory_space=pltpu.SMEM),
                  pl.BlockSpec(memory_space=pl.ANY)],
        out_specs=pl.BlockSpec(memory_space=pl.ANY),
        out_shape=jax.ShapeDtypeStruct((t, n), src.dtype),
        scratch_shapes=[pltpu.SemaphoreType.DMA((2,))],
        compiler_params=_params("arbitrary"),
        name="gather_rows" if gather else "scatter_rows",
    )(idx3, src)


def _moe_kernel(tile_ref, group_ref, valid_ref, first_ref, x_ref, r_ref, wg_ref, wu_ref, wd_ref, y_ref):
    k = pl.program_id(0)

    @pl.when(first_ref[k] == 1)
    def _():
        y_ref[...] = jnp.zeros_like(y_ref)

    @pl.when(valid_ref[k] == 1)
    def _():
        xb = x_ref[...].astype(BF16)
        r = r_ref[...]
        in_group = r[:, GROUP_LANE:GROUP_LANE + 1] == group_ref[k].astype(F32)
        gates = jnp.where(in_group, r, 0.0)
        acc = jnp.zeros(y_ref.shape, F32)
        for e in range(EXPERTS_PER_GROUP):
            hg = _dot(xb, wg_ref[0, e])
            up = _dot(xb, wu_ref[0, e])
            act = hg * _sigmoid(hg) * up * gates[:, e:e + 1]
            acc = acc + _dot(act.astype(BF16), wd_ref[0, e])
        y_ref[...] += acc


def _moe_mlp(items, xs, rs, wg, wu, wd):
    t, d = xs.shape
    tm = MOE_TILE
    n_items = items[0].shape[0]
    e, ff = wg.shape[1], wg.shape[3]
    grid_spec = pltpu.PrefetchScalarGridSpec(
        num_scalar_prefetch=4,
        grid=(n_items,),
        in_specs=[
            pl.BlockSpec((tm, d), lambda k, tl, gr, va, fi: (tl[k], 0)),
            pl.BlockSpec((tm, LANES), lambda k, tl, gr, va, fi: (tl[k], 0)),
            pl.BlockSpec((1, e, d, ff), lambda k, tl, gr, va, fi: (gr[k], 0, 0, 0)),
            pl.BlockSpec((1, e, d, ff), lambda k, tl, gr, va, fi: (gr[k], 0, 0, 0)),
            pl.BlockSpec((1, e, ff, d), lambda k, tl, gr, va, fi: (gr[k], 0, 0, 0)),
        ],
        out_specs=pl.BlockSpec((tm, d), lambda k, tl, gr, va, fi: (tl[k], 0)),
    )
    return pl.pallas_call(
        _moe_kernel,
        grid_spec=grid_spec,
        out_shape=jax.ShapeDtypeStruct((t, d), F32),
        compiler_params=_params("arbitrary"),
        name="moe_mlp",
    )(*items, xs, rs, wg, wu, wd)


def _moe_items(group, t):
    tm = MOE_TILE
    n_tiles = t // tm
    n_items = n_tiles + N_GROUPS - 1
    counts = jnp.sum((group[:, None] == jnp.arange(N_GROUPS)[None, :]).astype(jnp.int32), axis=0)
    ends = jnp.cumsum(counts)
    starts = ends - counts
    lo = jnp.arange(n_tiles, dtype=jnp.int32)[:, None] * tm
    overlap = (starts[None, :] < lo + tm) & (ends[None, :] > lo)
    flat = overlap.reshape(-1)
    n_valid = jnp.sum(flat.astype(jnp.int32))
    (pos,) = jnp.nonzero(flat, size=n_items, fill_value=0)
    pos = pos.astype(jnp.int32)
    k = jnp.arange(n_items, dtype=jnp.int32)
    valid = k < n_valid
    last = pos[jnp.maximum(n_valid - 1, 0)]
    pos = jnp.where(valid, pos, last)
    tile = pos // N_GROUPS
    grp = pos % N_GROUPS
    first = valid & ((k == 0) | (tile != jnp.roll(tile, 1)))
    return tile, grp, valid.astype(jnp.int32), first.astype(jnp.int32)


def _hier_moe(xn, route, wg, wu, wd):
    t = xn.shape[0]
    group = route[:, GROUP_LANE].astype(jnp.int32)
    perm = jnp.argsort(group, stable=True).astype(jnp.int32)
    items = _moe_items(group, t)
    xs = _permute_rows(xn, perm, gather=True)
    rs = _permute_rows(route, perm, gather=True)
    ys = _moe_mlp(items, xs, rs, wg, wu, wd)
    return _permute_rows(ys, perm, gather=False)


def _final_kernel(x_ref, y_ref, g_ref, o_ref):
    o_ref[...] = _rms(x_ref[...] + y_ref[...], g_ref[...])


def _final(x, y, g):
    t, d = x.shape
    tm = ROW_TILE
    row = pl.BlockSpec((tm, d), lambda i: (i, 0))
    return pl.pallas_call(
        _final_kernel,
        grid=(t // tm,),
        in_specs=[row, row, _full(g.shape)],
        out_specs=row,
        out_shape=jax.ShapeDtypeStruct((t, d), F32),
        compiler_params=_params("parallel"),
        name="final_norm",
    )(x, y, g)


def _hi_lo(w):
    hi = w.astype(BF16)
    return hi, (w - hi.astype(F32)).astype(BF16)


def _rotate_half_cols(w):
    d, n = w.shape
    w3 = w.reshape(d, n // DIFF_HEAD_DIM, DIFF_HEAD_DIM)
    half = DIFF_HEAD_DIM // 2
    return jnp.concatenate([-w3[..., half:], w3[..., :half]], axis=-1).reshape(d, n)


def _router_params(w_rg, b_rg, w_re, b_re):
    d = w_rg.shape[0]
    n_e = N_GROUPS * EXPERTS_PER_GROUP
    w = jnp.zeros((d, LANES), F32)
    w = w.at[:, 0:N_GROUPS].set(w_rg).at[:, EXPERT_LANE0:EXPERT_LANE0 + n_e].set(w_re)
    b = jnp.zeros((1, LANES), F32)
    b = b.at[0, 0:N_GROUPS].set(b_rg).at[0, EXPERT_LANE0:EXPERT_LANE0 + n_e].set(b_re)
    return _hi_lo(w) + (b,)


def kernel(x, positions, norm_mix, norm_ffn, ab_w_in, ab_gate_bias, ab_conv_w, ab_mlstm_norm, ab_lambda,
           ab_diff_norm, ab_w_out, c_w_in, c_conv_w, c_w_out, moe_w_rg, moe_b_rg, moe_w_re, moe_b_re,
           moe_w_gate, moe_w_up, moe_w_down, final_norm):
    batch, seq, d = x.shape
    t = batch * seq
    xf = x.reshape(t, d)
    pos = positions.reshape(t, 1).astype(jnp.int32)
    row2 = lambda v: v.reshape(1, -1).astype(F32)

    w_in = ab_w_in[0]
    mw = 2 * MLSTM_HEADS * MLSTM_HEAD_DIM
    hw = MLSTM_HEADS * MLSTM_HEAD_DIM
    ng = 2 * MLSTM_HEADS
    dw = DIFF_HEADS * DIFF_V_DIM
    o_v, o_o, o_g = mw, mw + hw, mw + 2 * hw
    o_q = o_g + ng
    o_k, o_dv = o_q + dw, o_q + 2 * dw
    w_q, w_k = w_in[:, o_q:o_k], w_in[:, o_k:o_dv]
    w0 = jnp.concatenate([w_in[:, :o_g], w_q, w_k, w_in[:, o_dv:o_dv + dw],
                          _rotate_half_cols(w_q), _rotate_half_cols(w_k)], axis=1).astype(BF16)
    w_gates = jnp.zeros((d, LANES), F32).at[:, :ng].set(w_in[:, o_g:o_q])
    wgh, wgl = _hi_lo(w_gates)
    gb = jnp.zeros((1, LANES), F32).at[0, :ng].set(ab_gate_bias[0])
    half = DIFF_HEAD_DIM // 2
    inv_freq = 1.0 / (ROPE_THETA ** (jnp.arange(half, dtype=F32) / half))
    invf = jnp.tile(inv_freq, LANES // half).reshape(1, LANES)

    mqk, mv, mo, dq, dk, dv, gates = _proj0(xf, pos, row2(norm_mix[0]), w0, wgh, wgl, gb, invf)
    hm = _mlstm(mqk, mv, mo, gates, ab_conv_w[0].astype(F32), row2(ab_mlstm_norm[0]), batch, seq)
    lam_init = 0.8 - 0.6 * math.exp(-0.3 * 0)
    hd = _diff_attn(dq, dk, dv, ab_lambda[0].astype(F32), row2(ab_diff_norm[0]), batch, seq, lam_init)

    wrh, wrl, br = _router_params(moe_w_rg[0], moe_b_rg[0], moe_w_re[0], moe_b_re[0])
    xmid, xn, route = _mix0_tail(xf, hm, hd, ab_w_out[0].astype(BF16), row2(norm_ffn[0]), wrh, wrl, br)
    y = _hier_moe(xn, route, moe_w_gate[0].astype(BF16), moe_w_up[0].astype(BF16), moe_w_down[0].astype(BF16))

    x1, bgate, u = _proj1(xmid, y, row2(norm_mix[1]), c_w_in[0].astype(BF16))
    wrh, wrl, br = _router_params(moe_w_rg[1], moe_b_rg[1], moe_w_re[1], moe_b_re[1])
    xmid, xn, route = _mix1_tail(x1, bgate, u, c_conv_w[0].astype(F32), c_w_out[0].astype(BF16),
                                 row2(norm_ffn[1]), wrh, wrl, br, seq)
    y = _hier_moe(xn, route, moe_w_gate[1].astype(BF16), moe_w_up[1].astype(BF16), moe_w_down[1].astype(BF16))

    return _final(xmid, y, row2(final_norm)).reshape(batch, seq, d)
```

```python
import functools
import math

import jax
import jax.numpy as jnp
from jax import lax
from jax.experimental import pallas as pl
from jax.experimental.pallas import tpu as pltpu

F32 = jnp.float32
BF16 = jnp.bfloat16

EPS = 1e-6
ROPE_THETA = 10000.0
LANES = 128
MLSTM_HEADS = 4
MLSTM_HEAD_DIM = 128
MLSTM_CONV = 4
DIFF_HEADS = 4
DIFF_HEAD_DIM = 64
DIFF_V_DIM = 128
CONV_WIDTH = 3
N_GROUPS = 4
EXPERTS_PER_GROUP = 8
EXPERT_FF = 256
HALO = 8

ROW_TILE = 512
MLSTM_CHUNK = 256
ATTN_BLOCK = 256
MOE_BLOCK = 1024
MOE_SUB = 128
MOE_ROWS = MOE_BLOCK + N_GROUPS * MOE_SUB
MOE_CHUNK = 256
VMEM_LIMIT = 56 * 1024 * 1024


def _dot(a, b):
    return jnp.dot(a, b, preferred_element_type=F32)


def _dot_nt(a, b):
    return lax.dot_general(a, b, (((1,), (1,)), ((), ())), preferred_element_type=F32)


def _dot_tn(a, b):
    return lax.dot_general(a, b, (((0,), (0,)), ((), ())), preferred_element_type=F32)


def _split3(x):
    hi = x.astype(BF16)
    r1 = x - hi.astype(F32)
    mid = r1.astype(BF16)
    lo = (r1 - mid.astype(F32)).astype(BF16)
    return hi, mid, lo


def _dot_f32ish(x, w_hi, w_lo):
    xh = x.astype(BF16)
    xl = (x - xh.astype(F32)).astype(BF16)
    return _dot(xh, w_hi) + (_dot(xl, w_hi) + _dot(xh, w_lo))


def _rms(x, g):
    return x * lax.rsqrt(jnp.mean(x * x, axis=-1, keepdims=True) + EPS) * g


def _sigmoid(x):
    return 1.0 / (1.0 + jnp.exp(-x))


def _params(*sem):
    return pltpu.CompilerParams(dimension_semantics=sem, vmem_limit_bytes=VMEM_LIMIT)


def _full(shape):
    return pl.BlockSpec(shape, lambda *_: (0,) * len(shape))


def _proj0_kernel(x_ref, pos_ref, g_ref, w_ref, wgh_ref, wgl_ref, gb_ref, invf_ref,
                  mqk_ref, mv_ref, mo_ref, dq_ref, dk_ref, dv_ref, gates_ref):
    xn = _rms(x_ref[...], g_ref[...])
    xb = xn.astype(BF16)

    def seg(lo, hi):
        return _dot(xb, w_ref[:, lo:hi])

    mqk_ref[...] = seg(0, 1024).astype(BF16)
    mv_ref[...] = seg(1024, 1536).astype(BF16)
    mo_ref[...] = seg(1536, 2048).astype(BF16)
    dv_ref[...] = seg(3072, 3584).astype(BF16)

    ang = pos_ref[...].astype(F32) * invf_ref[...]
    c = jnp.cos(ang)
    s = jnp.sin(ang)
    c4 = jnp.concatenate([c, c, c, c], axis=1)
    s4 = jnp.concatenate([s, s, s, s], axis=1)
    q_scale = DIFF_HEAD_DIM ** -0.5
    dq_ref[...] = ((seg(2048, 2560) * c4 + seg(3584, 4096) * s4) * q_scale).astype(BF16)
    dk_ref[...] = (seg(2560, 3072) * c4 + seg(4096, 4608) * s4).astype(BF16)

    gates_ref[...] = _dot_f32ish(xn, wgh_ref[...], wgl_ref[...]) + gb_ref[...]


def _proj0(x, pos, g, w, wgh, wgl, gb, invf):
    t, d = x.shape
    tm = ROW_TILE
    row = lambda n: pl.BlockSpec((tm, n), lambda i: (i, 0))
    out_shapes = [
        jax.ShapeDtypeStruct((t, 1024), BF16), jax.ShapeDtypeStruct((t, 512), BF16),
        jax.ShapeDtypeStruct((t, 512), BF16), jax.ShapeDtypeStruct((t, 512), BF16),
        jax.ShapeDtypeStruct((t, 512), BF16), jax.ShapeDtypeStruct((t, 512), BF16),
        jax.ShapeDtypeStruct((t, LANES), F32),
    ]
    return pl.pallas_call(
        _proj0_kernel,
        grid=(t // tm,),
        in_specs=[row(d), row(1), _full(g.shape), _full(w.shape), _full(wgh.shape), _full(wgl.shape),
                  _full(gb.shape), _full(invf.shape)],
        out_specs=[row(1024), row(512), row(512), row(512), row(512), row(512), row(LANES)],
        out_shape=out_shapes,
        compiler_params=_params("parallel"),
        name="proj0",
    )(x, pos, g, w, wgh, wgl, gb, invf)


def _mlstm_kernel(mqk_ref, mv_ref, mo_ref, gates_ref, cw_ref, norm_ref, hm_ref,
                  xe_ref, c_ref, n_ref, m_ref):
    L = MLSTM_CHUNK
    hd = MLSTM_HEAD_DIM
    width = MLSTM_HEADS * hd

    @pl.when(pl.program_id(1) == 0)
    def _():
        xe_ref[0:HALO, :] = jnp.zeros((HALO, 2 * width), F32)
        c_ref[...] = jnp.zeros_like(c_ref)
        n_ref[...] = jnp.zeros_like(n_ref)
        m_ref[...] = jnp.zeros_like(m_ref)

    xe_ref[HALO:HALO + L, :] = mqk_ref[...].astype(F32)
    cw = cw_ref[...]
    conv = cw[0:1, :] * xe_ref[pl.ds(HALO - 3, L), :]
    for j in range(1, MLSTM_CONV):
        conv = conv + cw[j:j + 1, :] * xe_ref[pl.ds(HALO - 3 + j, L), :]
    xe_ref[0:HALO, :] = xe_ref[L:L + HALO, :]
    qk = conv * _sigmoid(conv)
    q = qk[:, :width] * (hd ** -0.5)
    k = qk[:, width:]
    qb = q.astype(BF16)
    kb = k.astype(BF16)
    vb = mv_ref[...]

    gt = gates_ref[...]
    logf = jnp.minimum(gt, 0.0) - jnp.log(1.0 + jnp.exp(-jnp.abs(gt)))
    row = lax.broadcasted_iota(jnp.int32, (L, L), 0)
    col = lax.broadcasted_iota(jnp.int32, (L, L), 1)
    causal = col <= row
    tril = jnp.where(causal, 1.0, 0.0).astype(BF16)
    f_hi, f_mid, f_lo = _split3(logf)
    bcum = _dot(tril, f_hi) + (_dot(tril, f_mid) + _dot(tril, f_lo))
    bcum_t = bcum.T
    gt_t = gt.T

    for h in range(MLSTM_HEADS):
        sl = slice(h * hd, (h + 1) * hd)
        fl = MLSTM_HEADS + h
        b_col = bcum[:, fl:fl + 1]
        b_row = bcum_t[fl:fl + 1, :]
        ig_row = gt_t[h:h + 1, :]
        ig_col = gt[:, h:h + 1]
        m_prev = m_ref[h:h + 1, 0:1]
        c_prev = c_ref[h]
        n_prev = n_ref[h:h + 1, :]

        dmat = jnp.where(causal, b_col - b_row + ig_row, -jnp.inf)
        inter = b_col + m_prev
        m_row = jnp.maximum(inter, jnp.max(dmat, axis=-1, keepdims=True))
        w = jnp.exp(dmat - m_row)
        s = _dot_nt(qb[:, sl], kb[:, sl]) * w
        inter_w = jnp.exp(inter - m_row)
        num = _dot(s.astype(BF16), vb[:, sl]) + inter_w * _dot(qb[:, sl], c_prev.astype(BF16))
        den = jnp.sum(s, axis=-1, keepdims=True) + inter_w * jnp.sum(q[:, sl] * n_prev, axis=-1, keepdims=True)
        hh = num / jnp.maximum(jnp.abs(den), jnp.exp(-m_row))

        b_last = bcum[L - 1:L, fl:fl + 1]
        decay = b_last - b_col + ig_col
        m_new = jnp.maximum(b_last + m_prev, jnp.max(decay, axis=0, keepdims=True))
        wr = jnp.exp(decay - m_new)
        carry_w = jnp.exp(b_last + m_prev - m_new)
        kw = k[:, sl] * wr
        c_ref[h] = carry_w * c_prev + _dot_tn(kw.astype(BF16), vb[:, sl])
        n_ref[h:h + 1, :] = carry_w * n_prev + jnp.sum(kw, axis=0, keepdims=True)
        m_ref[h:h + 1, :] = jnp.broadcast_to(m_new, (1, LANES))

        hn = _rms(hh, norm_ref[...])
        hm_ref[:, sl] = (hn * _sigmoid(mo_ref[:, sl].astype(F32))).astype(BF16)


def _mlstm(mqk, mv, mo, gates, cw, norm, batch, seq):
    t = mqk.shape[0]
    L = MLSTM_CHUNK
    nc = seq // L
    width = MLSTM_HEADS * MLSTM_HEAD_DIM
    row = lambda n: pl.BlockSpec((L, n), lambda b, c: (b * nc + c, 0))
    return pl.pallas_call(
        _mlstm_kernel,
        grid=(batch, nc),
        in_specs=[row(2 * width), row(width), row(width), row(LANES), _full(cw.shape), _full(norm.shape)],
        out_specs=row(width),
        out_shape=jax.ShapeDtypeStruct((t, width), BF16),
        scratch_shapes=[
            pltpu.VMEM((L + HALO, 2 * width), F32),
            pltpu.VMEM((MLSTM_HEADS, MLSTM_HEAD_DIM, MLSTM_HEAD_DIM), F32),
            pltpu.VMEM((8, MLSTM_HEAD_DIM), F32),
            pltpu.VMEM((8, LANES), F32),
        ],
        compiler_params=_params("parallel", "arbitrary"),
        name="mlstm",
    )(mqk, mv, mo, gates, cw, norm)


def _diff_attn_kernel(q_ref, k_ref, v_ref, lam_ref, norm_ref, o_ref, acc_ref, *, lam_init):
    tq = ATTN_BLOCK
    qi = pl.program_id(2)
    q = q_ref[...]
    lane = lax.broadcasted_iota(jnp.int32, q.shape, 1)
    zero = jnp.zeros_like(q)
    qs = (jnp.where(lane < DIFF_HEAD_DIM, q, zero), jnp.where(lane >= DIFF_HEAD_DIM, q, zero))
    row = lax.broadcasted_iota(jnp.int32, (tq, tq), 0)
    col = lax.broadcasted_iota(jnp.int32, (tq, tq), 1)
    causal = col <= row

    acc_ref[...] = jnp.zeros_like(acc_ref)

    def block(j, carry, masked):
        start = pl.multiple_of(j * tq, tq)
        kb = k_ref[pl.ds(start, tq), :]
        vb = v_ref[pl.ds(start, tq), :]
        out = []
        for c in range(2):
            m_old, l_old = carry[2 * c], carry[2 * c + 1]
            s = _dot_nt(qs[c], kb)
            if masked:
                s = jnp.where(causal, s, -jnp.inf)
            m_new = jnp.maximum(m_old, jnp.max(s, axis=-1, keepdims=True))
            p = jnp.exp(s - m_new)
            alpha = jnp.exp(m_old - m_new)
            l_new = alpha * l_old + jnp.sum(p, axis=-1, keepdims=True)
            acc_ref[c] = alpha * acc_ref[c] + _dot(p.astype(BF16), vb)
            out += [m_new, l_new]
        return tuple(out)

    neg = jnp.full((tq, 1), -jnp.inf, F32)
    zer = jnp.zeros((tq, 1), F32)
    carry = lax.fori_loop(0, qi, lambda j, cr: block(j, cr, False), (neg, zer, neg, zer))
    _, l1, _, l2 = block(qi, carry, True)

    lv = lam_ref[...]
    lam = (jnp.exp(jnp.sum(lv[0:1, :] * lv[1:2, :], axis=-1, keepdims=True))
           - jnp.exp(jnp.sum(lv[2:3, :] * lv[3:4, :], axis=-1, keepdims=True)) + lam_init)
    o = acc_ref[0] / l1 - lam * (acc_ref[1] / l2)
    o_ref[...] = (_rms(o, norm_ref[...]) * (1.0 - lam_init)).astype(BF16)


def _diff_attn(dq, dk, dv, lam_vecs, norm, batch, seq, lam_init):
    t = dq.shape[0]
    tq = ATTN_BLOCK
    nq = seq // tq
    return pl.pallas_call(
        functools.partial(_diff_attn_kernel, lam_init=lam_init),
        grid=(batch, DIFF_HEADS, nq),
        in_specs=[
            pl.BlockSpec((tq, DIFF_V_DIM), lambda b, h, i: (b * nq + i, h)),
            pl.BlockSpec((seq, DIFF_V_DIM), lambda b, h, i: (b, h)),
            pl.BlockSpec((seq, DIFF_V_DIM), lambda b, h, i: (b, h)),
            _full(lam_vecs.shape), _full(norm.shape),
        ],
        out_specs=pl.BlockSpec((tq, DIFF_V_DIM), lambda b, h, i: (b * nq + i, h)),
        out_shape=jax.ShapeDtypeStruct((t, DIFF_HEADS * DIFF_V_DIM), BF16),
        scratch_shapes=[pltpu.VMEM((2, tq, DIFF_V_DIM), F32)],
        compiler_params=_params("parallel", "parallel", "arbitrary"),
        name="diff_attn",
    )(dq, dk, dv, lam_vecs, norm)


GROUP_LANE = EXPERTS_PER_GROUP
EXPERT_LANE0 = 8


def _route(xn, wrh_ref, wrl_ref, br_ref):
    logits = _dot_f32ish(xn, wrh_ref[...], wrl_ref[...]) + br_ref[...]
    lane = lax.broadcasted_iota(jnp.int32, logits.shape, 1)
    ninf = -jnp.inf
    big = jnp.int32(LANES)

    is_g = lane < N_GROUPS
    gl = jnp.where(is_g, logits, ninf)
    gmax = jnp.max(gl, axis=-1, keepdims=True)
    gsum = jnp.sum(jnp.where(is_g, jnp.exp(gl - gmax), 0.0), axis=-1, keepdims=True)
    g_val = 1.0 / gsum
    g_idx = jnp.min(jnp.where(gl == gmax, lane, big), axis=-1, keepdims=True)

    e_sel = jnp.zeros_like(logits)
    for g in range(N_GROUPS):
        shift = LANES - (EXPERT_LANE0 + EXPERTS_PER_GROUP * g)
        e_sel = jnp.where(g_idx == g, pltpu.roll(logits, shift, 1), e_sel)
    is_e = lane < EXPERTS_PER_GROUP
    el = jnp.where(is_e, e_sel, ninf)
    v1 = jnp.max(el, axis=-1, keepdims=True)
    i1 = jnp.min(jnp.where(el == v1, lane, big), axis=-1, keepdims=True)
    el2 = jnp.where(lane == i1, ninf, el)
    v2 = jnp.max(el2, axis=-1, keepdims=True)
    i2 = jnp.min(jnp.where(el2 == v2, lane, big), axis=-1, keepdims=True)
    e21 = jnp.exp(v2 - v1)
    w1 = 1.0 / (1.0 + e21)
    w2 = e21 * w1
    gates = g_val * (jnp.where(lane == i1, w1, 0.0) + jnp.where(lane == i2, w2, 0.0))
    return jnp.where(lane == GROUP_LANE, g_idx.astype(F32), gates)


def _mix0_tail_kernel(x_ref, hm_ref, hd_ref, wo_ref, nf_ref, wrh_ref, wrl_ref, br_ref,
                      xmid_ref, xn_ref, route_ref):
    half = hm_ref.shape[1]
    mix = _dot(hm_ref[...], wo_ref[0:half, :]) + _dot(hd_ref[...], wo_ref[half:2 * half, :])
    xmid = x_ref[...] + mix
    xmid_ref[...] = xmid
    xn = _rms(xmid, nf_ref[...])
    xn_ref[...] = xn.astype(BF16)
    route_ref[...] = _route(xn, wrh_ref, wrl_ref, br_ref)


def _mix0_tail(x, hm, hd, wo, nf, wrh, wrl, br):
    t, d = x.shape
    tm = ROW_TILE
    row = lambda n: pl.BlockSpec((tm, n), lambda i: (i, 0))
    return pl.pallas_call(
        _mix0_tail_kernel,
        grid=(t // tm,),
        in_specs=[row(d), row(hm.shape[1]), row(hd.shape[1]), _full(wo.shape), _full(nf.shape),
                  _full(wrh.shape), _full(wrl.shape), _full(br.shape)],
        out_specs=[row(d), row(d), row(LANES)],
        out_shape=[jax.ShapeDtypeStruct((t, d), F32), jax.ShapeDtypeStruct((t, d), BF16),
                   jax.ShapeDtypeStruct((t, LANES), F32)],
        compiler_params=_params("parallel"),
        name="mix0_tail",
    )(x, hm, hd, wo, nf, wrh, wrl, br)


def _proj1_kernel(x_ref, y_ref, g_ref, w_ref, x1_ref, b_ref, u_ref):
    d = x_ref.shape[1]
    x1 = x_ref[...] + y_ref[...].astype(F32)
    x1_ref[...] = x1
    xb = _rms(x1, g_ref[...]).astype(BF16)
    b_ref[...] = _dot(xb, w_ref[:, 0:d]).astype(BF16)
    u_ref[...] = (_dot(xb, w_ref[:, d:2 * d]) * _dot(xb, w_ref[:, 2 * d:3 * d])).astype(BF16)


def _proj1(x, y, g, w):
    t, d = x.shape
    tm = ROW_TILE
    row = lambda n: pl.BlockSpec((tm, n), lambda i: (i, 0))
    return pl.pallas_call(
        _proj1_kernel,
        grid=(t // tm,),
        in_specs=[row(d), row(d), _full(g.shape), _full(w.shape)],
        out_specs=[row(d), row(d), row(d)],
        out_shape=[jax.ShapeDtypeStruct((t, d), F32), jax.ShapeDtypeStruct((t, d), BF16),
                   jax.ShapeDtypeStruct((t, d), BF16)],
        compiler_params=_params("parallel"),
        name="proj1",
    )(x, y, g, w)


def _mix1_tail_kernel(x_ref, b_ref, u_ref, halo_ref, cw_ref, wo_ref, nf_ref, wrh_ref, wrl_ref, br_ref,
                      xmid_ref, xn_ref, route_ref, ue_ref, *, tiles_per_seq):
    tm = u_ref.shape[0]
    seq_start = pl.program_id(0) % tiles_per_seq == 0
    halo = halo_ref[...].astype(F32)
    ue_ref[0:HALO, :] = jnp.where(seq_start, jnp.zeros_like(halo), halo)
    ue_ref[HALO:HALO + tm, :] = u_ref[...].astype(F32)
    cw = cw_ref[...]
    conv = cw[0:1, :] * ue_ref[pl.ds(HALO - 2, tm), :]
    for j in range(1, CONV_WIDTH):
        conv = conv + cw[j:j + 1, :] * ue_ref[pl.ds(HALO - 2 + j, tm), :]
    mixed = (b_ref[...].astype(F32) * conv).astype(BF16)
    xmid = x_ref[...] + _dot(mixed, wo_ref[...])
    xmid_ref[...] = xmid
    xn = _rms(xmid, nf_ref[...])
    xn_ref[...] = xn.astype(BF16)
    route_ref[...] = _route(xn, wrh_ref, wrl_ref, br_ref)


def _mix1_tail(x, b, u, cw, wo, nf, wrh, wrl, br, seq):
    t, d = x.shape
    tm = ROW_TILE
    row = lambda n: pl.BlockSpec((tm, n), lambda i: (i, 0))
    halo = pl.BlockSpec((HALO, d), lambda i: (jnp.maximum(i * (tm // HALO) - 1, 0), 0))
    return pl.pallas_call(
        functools.partial(_mix1_tail_kernel, tiles_per_seq=seq // tm),
        grid=(t // tm,),
        in_specs=[row(d), row(d), row(d), halo, _full(cw.shape), _full(wo.shape), _full(nf.shape),
                  _full(wrh.shape), _full(wrl.shape), _full(br.shape)],
        out_specs=[row(d), row(d), row(LANES)],
        out_shape=[jax.ShapeDtypeStruct((t, d), F32), jax.ShapeDtypeStruct((t, d), BF16),
                   jax.ShapeDtypeStruct((t, LANES), F32)],
        scratch_shapes=[pltpu.VMEM((tm + HALO, d), F32)],
        compiler_params=_params("parallel"),
        name="mix1_tail",
    )(x, b, u, u, cw, wo, nf, wrh, wrl, br)


def _moe_kernel(off_ref, nsub_ref, x_ref, r_ref, wg_ref, wu_ref, wd_ref, y_ref,
                xs_ref, gs_ref, ys_ref, dest_ref):
    i = pl.program_id(0)
    g = pl.program_id(1)
    tb = MOE_BLOCK
    rows = MOE_ROWS
    ch = MOE_CHUNK

    @pl.when(g == 0)
    def _():
        r = r_ref[...]
        lane = lax.broadcasted_iota(jnp.int32, (tb, LANES), 1)
        grp = r[:, GROUP_LANE:GROUP_LANE + 1]
        onehot = jnp.where(lane.astype(F32) == grp, 1.0, 0.0)
        onehot_b = onehot.astype(BF16)
        lane1 = lax.broadcasted_iota(jnp.int32, (1, LANES), 1)
        off_row = jnp.zeros((1, LANES), F32)
        for gg in range(N_GROUPS):
            off_row = jnp.where(lane1 == gg, off_ref[i * N_GROUPS + gg].astype(F32), off_row)
        for c in range(tb // ch):
            rr = lax.broadcasted_iota(jnp.int32, (ch, tb), 0) + c * ch
            cc = lax.broadcasted_iota(jnp.int32, (ch, tb), 1)
            lower = jnp.where(cc <= rr, 1.0, 0.0).astype(BF16)
            rank = _dot(lower, onehot_b)
            sel = onehot[c * ch:(c + 1) * ch, :]
            dest = jnp.sum(sel * (rank + off_row), axis=-1, keepdims=True) - 1.0
            dest_ref[c * ch:(c + 1) * ch, :] = jnp.broadcast_to(dest, (ch, LANES))
        dest_row = dest_ref[...].T[0:1, :]
        gates = jnp.where(lane < EXPERTS_PER_GROUP, r, 0.0)
        g_hi = gates.astype(BF16)
        g_lo = (gates - g_hi.astype(F32)).astype(BF16)
        xb = x_ref[...]
        for c in range(rows // ch):
            rr = (lax.broadcasted_iota(jnp.int32, (ch, tb), 0) + c * ch).astype(F32)
            p = jnp.where(rr == dest_row, 1.0, 0.0).astype(BF16)
            xs_ref[c * ch:(c + 1) * ch, :] = _dot(p, xb).astype(BF16)
            gs_ref[c * ch:(c + 1) * ch, :] = _dot(p, g_hi) + _dot(p, g_lo)
        ys_ref[...] = jnp.zeros_like(ys_ref)

    off = off_ref[i * N_GROUPS + g]

    def sub(j, carry):
        r0 = pl.multiple_of(off + j * MOE_SUB, MOE_SUB)
        xt = xs_ref[pl.ds(r0, MOE_SUB), :]
        gt = gs_ref[pl.ds(r0, MOE_SUB), :]
        acc = jnp.zeros((MOE_SUB, y_ref.shape[1]), F32)
        for e in range(EXPERTS_PER_GROUP):
            hg = _dot(xt, wg_ref[0, e])
            up = _dot(xt, wu_ref[0, e])
            act = hg * _sigmoid(hg) * up * gt[:, e:e + 1]
            acc = acc + _dot(act.astype(BF16), wd_ref[0, e])
        ys_ref[pl.ds(r0, MOE_SUB), :] = acc.astype(BF16)
        return carry

    lax.fori_loop(0, nsub_ref[i * N_GROUPS + g], sub, 0)

    @pl.when(g == N_GROUPS - 1)
    def _():
        ys = ys_ref[...]
        for c in range(tb // ch):
            cc = lax.broadcasted_iota(jnp.int32, (ch, rows), 1).astype(F32)
            pt = jnp.where(cc == dest_ref[c * ch:(c + 1) * ch, 0:1], 1.0, 0.0).astype(BF16)
            y_ref[c * ch:(c + 1) * ch, :] = _dot(pt, ys).astype(y_ref.dtype)


def _hier_moe(xn, route, wg, wu, wd):
    t, d = xn.shape
    tb = MOE_BLOCK
    nb = t // tb
    e, ff = wg.shape[1], wg.shape[3]
    group = route[:, GROUP_LANE].astype(jnp.int32).reshape(nb, tb)
    counts = jnp.sum((group[:, :, None] == jnp.arange(N_GROUPS)[None, None, :]).astype(jnp.int32), axis=1)
    nsub = (counts + MOE_SUB - 1) // MOE_SUB
    padded = nsub * MOE_SUB
    off = jnp.cumsum(padded, axis=1) - padded
    grid_spec = pltpu.PrefetchScalarGridSpec(
        num_scalar_prefetch=2,
        grid=(nb, N_GROUPS),
        in_specs=[
            pl.BlockSpec((tb, d), lambda i, g, *_: (i, 0)),
            pl.BlockSpec((tb, LANES), lambda i, g, *_: (i, 0)),
            pl.BlockSpec((1, e, d, ff), lambda i, g, *_: (g, 0, 0, 0)),
            pl.BlockSpec((1, e, d, ff), lambda i, g, *_: (g, 0, 0, 0)),
            pl.BlockSpec((1, e, ff, d), lambda i, g, *_: (g, 0, 0, 0)),
        ],
        out_specs=pl.BlockSpec((tb, d), lambda i, g, *_: (i, 0)),
        scratch_shapes=[
            pltpu.VMEM((MOE_ROWS, d), BF16),
            pltpu.VMEM((MOE_ROWS, LANES), F32),
            pltpu.VMEM((MOE_ROWS, d), BF16),
            pltpu.VMEM((tb, LANES), F32),
        ],
    )
    return pl.pallas_call(
        _moe_kernel,
        grid_spec=grid_spec,
        out_shape=jax.ShapeDtypeStruct((t, d), BF16),
        compiler_params=_params("parallel", "arbitrary"),
        name="moe",
    )(off.reshape(-1).astype(jnp.int32), nsub.reshape(-1).astype(jnp.int32), xn, route, wg, wu, wd)


def _final_kernel(x_ref, y_ref, g_ref, o_ref):
    o_ref[...] = _rms(x_ref[...] + y_ref[...].astype(F32), g_ref[...])


def _final(x, y, g):
    t, d = x.shape
    tm = ROW_TILE
    row = pl.BlockSpec((tm, d), lambda i: (i, 0))
    return pl.pallas_call(
        _final_kernel,
        grid=(t // tm,),
        in_specs=[row, row, _full(g.shape)],
        out_specs=row,
        out_shape=jax.ShapeDtypeStruct((t, d), F32),
        compiler_params=_params("parallel"),
        name="final_norm",
    )(x, y, g)


def _hi_lo(w):
    hi = w.astype(BF16)
    return hi, (w - hi.astype(F32)).astype(BF16)


def _rotate_half_cols(w):
    d, n = w.shape
    w3 = w.reshape(d, n // DIFF_HEAD_DIM, DIFF_HEAD_DIM)
    half = DIFF_HEAD_DIM // 2
    return jnp.concatenate([-w3[..., half:], w3[..., :half]], axis=-1).reshape(d, n)


def _router_params(w_rg, b_rg, w_re, b_re):
    d = w_rg.shape[0]
    n_e = N_GROUPS * EXPERTS_PER_GROUP
    pad = jnp.zeros((d, EXPERT_LANE0 - N_GROUPS), F32)
    tail = jnp.zeros((d, LANES - EXPERT_LANE0 - n_e), F32)
    w = jnp.concatenate([w_rg, pad, w_re, tail], axis=1)
    b = jnp.concatenate([b_rg, pad[0], b_re, tail[0]]).reshape(1, LANES)
    return _hi_lo(w) + (b,)


def kernel(x, positions, norm_mix, norm_ffn, ab_w_in, ab_gate_bias, ab_conv_w, ab_mlstm_norm, ab_lambda,
           ab_diff_norm, ab_w_out, c_w_in, c_conv_w, c_w_out, moe_w_rg, moe_b_rg, moe_w_re, moe_b_re,
           moe_w_gate, moe_w_up, moe_w_down, final_norm):
    batch, seq, d = x.shape
    t = batch * seq
    xf = x.reshape(t, d)
    pos = positions.reshape(t, 1).astype(jnp.int32)
    row2 = lambda v: v.reshape(1, -1).astype(F32)

    w_in = ab_w_in[0]
    mw = 2 * MLSTM_HEADS * MLSTM_HEAD_DIM
    hw = MLSTM_HEADS * MLSTM_HEAD_DIM
    ng = 2 * MLSTM_HEADS
    dw = DIFF_HEADS * DIFF_V_DIM
    o_g = mw + 2 * hw
    o_q = o_g + ng
    o_k, o_dv = o_q + dw, o_q + 2 * dw
    w_q, w_k = w_in[:, o_q:o_k], w_in[:, o_k:o_dv]
    w0 = jnp.concatenate([w_in[:, :o_g], w_q, w_k, w_in[:, o_dv:o_dv + dw],
                          _rotate_half_cols(w_q), _rotate_half_cols(w_k)], axis=1).astype(BF16)
    w_gates = jnp.concatenate([w_in[:, o_g:o_q], jnp.zeros((d, LANES - ng), F32)], axis=1)
    wgh, wgl = _hi_lo(w_gates)
    gb = jnp.concatenate([ab_gate_bias[0].astype(F32), jnp.zeros((LANES - ng,), F32)]).reshape(1, LANES)
    half = DIFF_HEAD_DIM // 2
    inv_freq = 1.0 / (ROPE_THETA ** (jnp.arange(half, dtype=F32) / half))
    invf = jnp.tile(inv_freq, LANES // half).reshape(1, LANES)

    mqk, mv, mo, dq, dk, dv, gates = _proj0(xf, pos, row2(norm_mix[0]), w0, wgh, wgl, gb, invf)
    hm = _mlstm(mqk, mv, mo, gates, ab_conv_w[0].astype(F32), row2(ab_mlstm_norm[0]), batch, seq)
    lam_init = 0.8 - 0.6 * math.exp(-0.3 * 0)
    hd = _diff_attn(dq, dk, dv, ab_lambda[0].astype(F32), row2(ab_diff_norm[0]), batch, seq, lam_init)

    wrh, wrl, br = _router_params(moe_w_rg[0], moe_b_rg[0], moe_w_re[0], moe_b_re[0])
    xmid, xn, route = _mix0_tail(xf, hm, hd, ab_w_out[0].astype(BF16), row2(norm_ffn[0]), wrh, wrl, br)
    y = _hier_moe(xn, route, moe_w_gate[0].astype(BF16), moe_w_up[0].astype(BF16), moe_w_down[0].astype(BF16))

    x1, bgate, u = _proj1(xmid, y, row2(norm_mix[1]), c_w_in[0].astype(BF16))
    wrh, wrl, br = _router_params(moe_w_rg[1], moe_b_rg[1], moe_w_re[1], moe_b_re[1])
    xmid, xn, route = _mix1_tail(x1, bgate, u, c_conv_w[0].astype(F32), c_w_out[0].astype(BF16),
                                 row2(norm_ffn[1]), wrh, wrl, br, seq)
    y = _hier_moe(xn, route, moe_w_gate[1].astype(BF16), moe_w_up[1].astype(BF16), moe_w_down[1].astype(BF16))

    return _final(xmid, y, row2(final_norm)).reshape(batch, seq, d)
```

```python
import functools
import math

import jax
import jax.numpy as jnp
from jax import lax
from jax.experimental import pallas as pl
from jax.experimental.pallas import tpu as pltpu

F32 = jnp.float32
BF16 = jnp.bfloat16

EPS = 1e-6
ROPE_THETA = 10000.0
LANES = 128
SUBLANES = 8
MLSTM_HEADS = 4
MLSTM_HEAD_DIM = 128
MLSTM_CONV = 4
DIFF_HEADS = 4
DIFF_HEAD_DIM = 64
DIFF_V_DIM = 128
CONV_WIDTH = 3
N_GROUPS = 4
EXPERTS_PER_GROUP = 8
EXPERT_FF = 256
HALO = 8

ROW_TILE = 512
MLSTM_CHUNK = 256
ATTN_BLOCK = 512
MOE_BLOCK = 1024
MOE_SUB = 128
MOE_ROWS = MOE_BLOCK + N_GROUPS * MOE_SUB
MOE_CHUNK = 256
VMEM_LIMIT = 56 * 1024 * 1024


def _dot(a, b):
    return jnp.dot(a, b, preferred_element_type=F32)


def _dot_nt(a, b):
    return lax.dot_general(a, b, (((1,), (1,)), ((), ())), preferred_element_type=F32)


def _dot_tn(a, b):
    return lax.dot_general(a, b, (((0,), (0,)), ((), ())), preferred_element_type=F32)


def _split3(x):
    hi = x.astype(BF16)
    r1 = x - hi.astype(F32)
    mid = r1.astype(BF16)
    lo = (r1 - mid.astype(F32)).astype(BF16)
    return hi, mid, lo


def _dot_f32ish(x, w_hi, w_lo):
    xh = x.astype(BF16)
    xl = (x - xh.astype(F32)).astype(BF16)
    return _dot(xh, w_hi) + (_dot(xl, w_hi) + _dot(xh, w_lo))


def _rms(x, g):
    return x * lax.rsqrt(jnp.mean(x * x, axis=-1, keepdims=True) + EPS) * g


def _sigmoid(x):
    return 1.0 / (1.0 + jnp.exp(-x))


def _params(*sem):
    return pltpu.CompilerParams(dimension_semantics=sem, vmem_limit_bytes=VMEM_LIMIT)


def _full(shape):
    return pl.BlockSpec(shape, lambda *_: (0,) * len(shape))


def _proj0_kernel(x_ref, pos_ref, g_ref, w_ref, wvt_ref, wgh_ref, wgl_ref, gb_ref, invf_ref,
                  mqk_ref, mv_ref, mo_ref, dq_ref, dk_ref, dvt_ref, gates_ref):
    xn = _rms(x_ref[...], g_ref[...])
    xb = xn.astype(BF16)

    def seg(lo, hi):
        return _dot(xb, w_ref[:, lo:hi])

    mqk_ref[...] = seg(0, 1024).astype(BF16)
    mv_ref[...] = seg(1024, 1536).astype(BF16)
    mo_ref[...] = seg(1536, 2048).astype(BF16)
    dvt_ref[...] = _dot_nt(wvt_ref[...], xb).astype(BF16)

    ang = pos_ref[...].astype(F32) * invf_ref[...]
    c = jnp.cos(ang)
    s = jnp.sin(ang)
    c4 = jnp.concatenate([c, c, c, c], axis=1)
    s4 = jnp.concatenate([s, s, s, s], axis=1)
    q_scale = math.log2(math.e) * DIFF_HEAD_DIM ** -0.5
    dq_ref[...] = ((seg(2048, 2560) * c4 + seg(3072, 3584) * s4) * q_scale).astype(BF16)
    dk_ref[...] = (seg(2560, 3072) * c4 + seg(3584, 4096) * s4).astype(BF16)

    gates_ref[...] = _dot_f32ish(xn, wgh_ref[...], wgl_ref[...]) + gb_ref[...]


def _proj0(x, pos, g, w, wvt, wgh, wgl, gb, invf):
    t, d = x.shape
    tm = ROW_TILE
    row = lambda n: pl.BlockSpec((tm, n), lambda i: (i, 0))
    out_shapes = [
        jax.ShapeDtypeStruct((t, 1024), BF16), jax.ShapeDtypeStruct((t, 512), BF16),
        jax.ShapeDtypeStruct((t, 512), BF16), jax.ShapeDtypeStruct((t, 512), BF16),
        jax.ShapeDtypeStruct((t, 512), BF16), jax.ShapeDtypeStruct((512, t), BF16),
        jax.ShapeDtypeStruct((t, LANES), F32),
    ]
    return pl.pallas_call(
        _proj0_kernel,
        grid=(t // tm,),
        in_specs=[row(d), row(1), _full(g.shape), _full(w.shape), _full(wvt.shape), _full(wgh.shape),
                  _full(wgl.shape), _full(gb.shape), _full(invf.shape)],
        out_specs=[row(1024), row(512), row(512), row(512), row(512),
                   pl.BlockSpec((512, tm), lambda i: (0, i)), row(LANES)],
        out_shape=out_shapes,
        compiler_params=_params("parallel"),
        name="proj0",
    )(x, pos, g, w, wvt, wgh, wgl, gb, invf)


def _mlstm_kernel(mqk_ref, mv_ref, mo_ref, gates_ref, cw_ref, norm_ref, hm_ref,
                  xe_ref, c_ref, n_ref, m_ref):
    L = MLSTM_CHUNK
    hd = MLSTM_HEAD_DIM
    width = MLSTM_HEADS * hd

    @pl.when(pl.program_id(1) == 0)
    def _():
        xe_ref[0:HALO, :] = jnp.zeros((HALO, 2 * width), F32)
        c_ref[...] = jnp.zeros_like(c_ref)
        n_ref[...] = jnp.zeros_like(n_ref)
        m_ref[...] = jnp.zeros_like(m_ref)

    xe_ref[HALO:HALO + L, :] = mqk_ref[...].astype(F32)
    cw = cw_ref[...]
    conv = cw[0:1, :] * xe_ref[pl.ds(HALO - 3, L), :]
    for j in range(1, MLSTM_CONV):
        conv = conv + cw[j:j + 1, :] * xe_ref[pl.ds(HALO - 3 + j, L), :]
    xe_ref[0:HALO, :] = xe_ref[L:L + HALO, :]
    qk = conv * _sigmoid(conv)
    q = qk[:, :width] * (hd ** -0.5)
    k = qk[:, width:]
    qb = q.astype(BF16)
    kb = k.astype(BF16)
    vb = mv_ref[...]

    gt = gates_ref[...]
    logf = jnp.minimum(gt, 0.0) - jnp.log(1.0 + jnp.exp(-jnp.abs(gt)))
    row = lax.broadcasted_iota(jnp.int32, (L, L), 0)
    col = lax.broadcasted_iota(jnp.int32, (L, L), 1)
    causal = col <= row
    tril = jnp.where(causal, 1.0, 0.0).astype(BF16)
    f_hi, f_mid, f_lo = _split3(logf)
    bcum = _dot(tril, f_hi) + (_dot(tril, f_mid) + _dot(tril, f_lo))
    bcum_t = bcum.T
    gt_t = gt.T

    for h in range(MLSTM_HEADS):
        sl = slice(h * hd, (h + 1) * hd)
        fl = MLSTM_HEADS + h
        b_col = bcum[:, fl:fl + 1]
        b_row = bcum_t[fl:fl + 1, :]
        ig_row = gt_t[h:h + 1, :]
        ig_col = gt[:, h:h + 1]
        m_prev = m_ref[h:h + 1, 0:1]
        c_prev = c_ref[h]
        n_prev = n_ref[h:h + 1, :]

        dmat = jnp.where(causal, b_col - b_row + ig_row, -jnp.inf)
        inter = b_col + m_prev
        m_row = jnp.maximum(inter, jnp.max(dmat, axis=-1, keepdims=True))
        w = jnp.exp(dmat - m_row)
        s = _dot_nt(qb[:, sl], kb[:, sl]) * w
        inter_w = jnp.exp(inter - m_row)
        num = _dot(s.astype(BF16), vb[:, sl]) + inter_w * _dot(qb[:, sl], c_prev.astype(BF16))
        den = jnp.sum(s, axis=-1, keepdims=True) + inter_w * jnp.sum(q[:, sl] * n_prev, axis=-1, keepdims=True)
        hh = num / jnp.maximum(jnp.abs(den), jnp.exp(-m_row))

        b_last = bcum[L - 1:L, fl:fl + 1]
        decay = b_last - b_col + ig_col
        m_new = jnp.maximum(b_last + m_prev, jnp.max(decay, axis=0, keepdims=True))
        wr = jnp.exp(decay - m_new)
        carry_w = jnp.exp(b_last + m_prev - m_new)
        kw = k[:, sl] * wr
        c_ref[h] = carry_w * c_prev + _dot_tn(kw.astype(BF16), vb[:, sl])
        n_ref[h:h + 1, :] = carry_w * n_prev + jnp.sum(kw, axis=0, keepdims=True)
        m_ref[h:h + 1, :] = jnp.broadcast_to(m_new, (1, LANES))

        hn = _rms(hh, norm_ref[...])
        hm_ref[:, sl] = (hn * _sigmoid(mo_ref[:, sl].astype(F32))).astype(BF16)


def _mlstm(mqk, mv, mo, gates, cw, norm, batch, seq):
    t = mqk.shape[0]
    L = MLSTM_CHUNK
    nc = seq // L
    width = MLSTM_HEADS * MLSTM_HEAD_DIM
    row = lambda n: pl.BlockSpec((L, n), lambda b, c: (b * nc + c, 0))
    return pl.pallas_call(
        _mlstm_kernel,
        grid=(batch, nc),
        in_specs=[row(2 * width), row(width), row(width), row(LANES), _full(cw.shape), _full(norm.shape)],
        out_specs=row(width),
        out_shape=jax.ShapeDtypeStruct((t, width), BF16),
        scratch_shapes=[
            pltpu.VMEM((L + HALO, 2 * width), F32),
            pltpu.VMEM((MLSTM_HEADS, MLSTM_HEAD_DIM, MLSTM_HEAD_DIM), F32),
            pltpu.VMEM((8, MLSTM_HEAD_DIM), F32),
            pltpu.VMEM((8, LANES), F32),
        ],
        compiler_params=_params("parallel", "arbitrary"),
        name="mlstm",
    )(mqk, mv, mo, gates, cw, norm)


def _diff_attn_kernel(q_ref, k_ref, vt_ref, lam_ref, norm_ref, o_ref, acc0_ref, acc1_ref, *, lam_init):
    tq = ATTN_BLOCK
    dv = DIFF_V_DIM
    qi = pl.program_id(2)
    q = q_ref[...]
    lane = lax.broadcasted_iota(jnp.int32, q.shape, 1)
    zero = jnp.zeros_like(q)
    qs = (jnp.where(lane < DIFF_HEAD_DIM, q, zero), jnp.where(lane >= DIFF_HEAD_DIM, q, zero))
    krow = lax.broadcasted_iota(jnp.int32, (tq, tq), 0)
    qcol = lax.broadcasted_iota(jnp.int32, (tq, tq), 1)
    causal_t = krow <= qcol
    ones = jnp.ones((SUBLANES, tq), BF16)

    accs = (acc0_ref, acc1_ref)
    for acc in accs:
        acc[...] = jnp.zeros_like(acc)

    def block(j, carry, masked):
        start = pl.multiple_of(j * tq, tq)
        kb = k_ref[pl.ds(start, tq), :]
        vt1 = jnp.concatenate([vt_ref[:, pl.ds(start, tq)], ones], axis=0)
        s = [_dot_nt(kb, qs[c]) for c in range(2)]
        if masked:
            s = [jnp.where(causal_t, sc, -jnp.inf) for sc in s]
        m_new = [jnp.maximum(carry[c], jnp.max(s[c], axis=0, keepdims=True)) for c in range(2)]
        p = [jnp.exp2(s[c] - m_new[c]).astype(BF16) for c in range(2)]
        alpha = [jnp.exp2(carry[c] - m_new[c]) for c in range(2)]
        pv = [_dot(vt1, p[c]) for c in range(2)]
        for c in range(2):
            accs[c][...] = alpha[c] * accs[c][...] + pv[c]
        return tuple(m_new)

    neg = jnp.full((1, tq), -jnp.inf, F32)
    carry = lax.fori_loop(0, qi, lambda j, cr: block(j, cr, False), (neg, neg))
    block(qi, carry, True)

    lv = lam_ref[...]
    lam = (jnp.exp(jnp.sum(lv[0:1, :] * lv[1:2, :], axis=-1, keepdims=True))
           - jnp.exp(jnp.sum(lv[2:3, :] * lv[3:4, :], axis=-1, keepdims=True)) + lam_init)
    a1 = acc0_ref[...]
    a2 = acc1_ref[...]
    o_t = a1[0:dv, :] / a1[dv:dv + 1, :] - lam * (a2[0:dv, :] / a2[dv:dv + 1, :])
    o_ref[...] = (_rms(o_t.T, norm_ref[...]) * (1.0 - lam_init)).astype(BF16)


def _diff_attn(dq, dk, dvt, lam_vecs, norm, batch, seq, lam_init):
    t = dq.shape[0]
    tq = ATTN_BLOCK
    nq = seq // tq
    return pl.pallas_call(
        functools.partial(_diff_attn_kernel, lam_init=lam_init),
        grid=(batch, DIFF_HEADS, nq),
        in_specs=[
            pl.BlockSpec((tq, DIFF_V_DIM), lambda b, h, i: (b * nq + i, h)),
            pl.BlockSpec((seq, DIFF_V_DIM), lambda b, h, i: (b, h)),
            pl.BlockSpec((DIFF_V_DIM, seq), lambda b, h, i: (h, b)),
            _full(lam_vecs.shape), _full(norm.shape),
        ],
        out_specs=pl.BlockSpec((tq, DIFF_V_DIM), lambda b, h, i: (b * nq + i, h)),
        out_shape=jax.ShapeDtypeStruct((t, DIFF_HEADS * DIFF_V_DIM), BF16),
        scratch_shapes=[pltpu.VMEM((DIFF_V_DIM + SUBLANES, tq), F32)] * 2,
        compiler_params=_params("parallel", "parallel", "arbitrary"),
        name="diff_attn",
    )(dq, dk, dvt, lam_vecs, norm)


GROUP_LANE = EXPERTS_PER_GROUP
EXPERT_LANE0 = 8


def _route(xn, wrh_ref, wrl_ref, br_ref):
    logits = _dot_f32ish(xn, wrh_ref[...], wrl_ref[...]) + br_ref[...]
    lane = lax.broadcasted_iota(jnp.int32, logits.shape, 1)
    ninf = -jnp.inf
    big = jnp.int32(LANES)

    is_g = lane < N_GROUPS
    gl = jnp.where(is_g, logits, ninf)
    gmax = jnp.max(gl, axis=-1, keepdims=True)
    gsum = jnp.sum(jnp.where(is_g, jnp.exp(gl - gmax), 0.0), axis=-1, keepdims=True)
    g_val = 1.0 / gsum
    g_idx = jnp.min(jnp.where(gl == gmax, lane, big), axis=-1, keepdims=True)

    e_sel = jnp.zeros_like(logits)
    for g in range(N_GROUPS):
        shift = LANES - (EXPERT_LANE0 + EXPERTS_PER_GROUP * g)
        e_sel = jnp.where(g_idx == g, pltpu.roll(logits, shift, 1), e_sel)
    is_e = lane < EXPERTS_PER_GROUP
    el = jnp.where(is_e, e_sel, ninf)
    v1 = jnp.max(el, axis=-1, keepdims=True)
    i1 = jnp.min(jnp.where(el == v1, lane, big), axis=-1, keepdims=True)
    el2 = jnp.where(lane == i1, ninf, el)
    v2 = jnp.max(el2, axis=-1, keepdims=True)
    i2 = jnp.min(jnp.where(el2 == v2, lane, big), axis=-1, keepdims=True)
    e21 = jnp.exp(v2 - v1)
    w1 = 1.0 / (1.0 + e21)
    w2 = e21 * w1
    gates = g_val * (jnp.where(lane == i1, w1, 0.0) + jnp.where(lane == i2, w2, 0.0))
    return jnp.where(lane == GROUP_LANE, g_idx.astype(F32), gates)


def _mix0_tail_kernel(x_ref, hm_ref, hd_ref, wo_ref, nf_ref, wrh_ref, wrl_ref, br_ref,
                      xmid_ref, xn_ref, route_ref):
    half = hm_ref.shape[1]
    mix = _dot(hm_ref[...], wo_ref[0:half, :]) + _dot(hd_ref[...], wo_ref[half:2 * half, :])
    xmid = x_ref[...] + mix
    xmid_ref[...] = xmid
    xn = _rms(xmid, nf_ref[...])
    xn_ref[...] = xn.astype(BF16)
    route_ref[...] = _route(xn, wrh_ref, wrl_ref, br_ref)


def _mix0_tail(x, hm, hd, wo, nf, wrh, wrl, br):
    t, d = x.shape
    tm = ROW_TILE
    row = lambda n: pl.BlockSpec((tm, n), lambda i: (i, 0))
    return pl.pallas_call(
        _mix0_tail_kernel,
        grid=(t // tm,),
        in_specs=[row(d), row(hm.shape[1]), row(hd.shape[1]), _full(wo.shape), _full(nf.shape),
                  _full(wrh.shape), _full(wrl.shape), _full(br.shape)],
        out_specs=[row(d), row(d), row(LANES)],
        out_shape=[jax.ShapeDtypeStruct((t, d), F32), jax.ShapeDtypeStruct((t, d), BF16),
                   jax.ShapeDtypeStruct((t, LANES), F32)],
        compiler_params=_params("parallel"),
        name="mix0_tail",
    )(x, hm, hd, wo, nf, wrh, wrl, br)


def _proj1_kernel(x_ref, y_ref, g_ref, w_ref, x1_ref, b_ref, u_ref):
    d = x_ref.shape[1]
    x1 = x_ref[...] + y_ref[...].astype(F32)
    x1_ref[...] = x1
    xb = _rms(x1, g_ref[...]).astype(BF16)
    b_ref[...] = _dot(xb, w_ref[:, 0:d]).astype(BF16)
    u_ref[...] = (_dot(xb, w_ref[:, d:2 * d]) * _dot(xb, w_ref[:, 2 * d:3 * d])).astype(BF16)


def _proj1(x, y, g, w):
    t, d = x.shape
    tm = ROW_TILE
    row = lambda n: pl.BlockSpec((tm, n), lambda i: (i, 0))
    return pl.pallas_call(
        _proj1_kernel,
        grid=(t // tm,),
        in_specs=[row(d), row(d), _full(g.shape), _full(w.shape)],
        out_specs=[row(d), row(d), row(d)],
        out_shape=[jax.ShapeDtypeStruct((t, d), F32), jax.ShapeDtypeStruct((t, d), BF16),
                   jax.ShapeDtypeStruct((t, d), BF16)],
        compiler_params=_params("parallel"),
        name="proj1",
    )(x, y, g, w)


def _mix1_tail_kernel(x_ref, b_ref, u_ref, halo_ref, cw_ref, wo_ref, nf_ref, wrh_ref, wrl_ref, br_ref,
                      xmid_ref, xn_ref, route_ref, ue_ref, *, tiles_per_seq):
    tm = u_ref.shape[0]
    seq_start = pl.program_id(0) % tiles_per_seq == 0
    halo = halo_ref[...].astype(F32)
    ue_ref[0:HALO, :] = jnp.where(seq_start, jnp.zeros_like(halo), halo)
    ue_ref[HALO:HALO + tm, :] = u_ref[...].astype(F32)
    cw = cw_ref[...]
    conv = cw[0:1, :] * ue_ref[pl.ds(HALO - 2, tm), :]
    for j in range(1, CONV_WIDTH):
        conv = conv + cw[j:j + 1, :] * ue_ref[pl.ds(HALO - 2 + j, tm), :]
    mixed = (b_ref[...].astype(F32) * conv).astype(BF16)
    xmid = x_ref[...] + _dot(mixed, wo_ref[...])
    xmid_ref[...] = xmid
    xn = _rms(xmid, nf_ref[...])
    xn_ref[...] = xn.astype(BF16)
    route_ref[...] = _route(xn, wrh_ref, wrl_ref, br_ref)


def _mix1_tail(x, b, u, cw, wo, nf, wrh, wrl, br, seq):
    t, d = x.shape
    tm = ROW_TILE
    row = lambda n: pl.BlockSpec((tm, n), lambda i: (i, 0))
    halo = pl.BlockSpec((HALO, d), lambda i: (jnp.maximum(i * (tm // HALO) - 1, 0), 0))
    return pl.pallas_call(
        functools.partial(_mix1_tail_kernel, tiles_per_seq=seq // tm),
        grid=(t // tm,),
        in_specs=[row(d), row(d), row(d), halo, _full(cw.shape), _full(wo.shape), _full(nf.shape),
                  _full(wrh.shape), _full(wrl.shape), _full(br.shape)],
        out_specs=[row(d), row(d), row(LANES)],
        out_shape=[jax.ShapeDtypeStruct((t, d), F32), jax.ShapeDtypeStruct((t, d), BF16),
                   jax.ShapeDtypeStruct((t, LANES), F32)],
        scratch_shapes=[pltpu.VMEM((tm + HALO, d), F32)],
        compiler_params=_params("parallel"),
        name="mix1_tail",
    )(x, b, u, u, cw, wo, nf, wrh, wrl, br)


def _moe_kernel(off_ref, nsub_ref, x_ref, r_ref, wg_ref, wu_ref, wd_ref, y_ref,
                xs_ref, gs_ref, ys_ref, dest_ref):
    i = pl.program_id(0)
    g = pl.program_id(1)
    tb = MOE_BLOCK
    rows = MOE_ROWS
    ch = MOE_CHUNK
    d = x_ref.shape[1]

    @pl.when(g == 0)
    def _():
        r = r_ref[...]
        lane = lax.broadcasted_iota(jnp.int32, (tb, LANES), 1)
        grp = r[:, GROUP_LANE:GROUP_LANE + 1]
        onehot = jnp.where(lane.astype(F32) == grp, 1.0, 0.0)
        onehot_b = onehot.astype(BF16)
        lane1 = lax.broadcasted_iota(jnp.int32, (1, LANES), 1)
        off_row = jnp.zeros((1, LANES), F32)
        for gg in range(N_GROUPS):
            off_row = jnp.where(lane1 == gg, off_ref[i * N_GROUPS + gg].astype(F32), off_row)
        rr = lax.broadcasted_iota(jnp.int32, (ch, ch), 0)
        cc = lax.broadcasted_iota(jnp.int32, (ch, ch), 1)
        lower = jnp.where(cc <= rr, 1.0, 0.0).astype(BF16)
        base = off_row
        for c in range(tb // ch):
            sel = onehot[c * ch:(c + 1) * ch, :]
            rank = _dot(lower, onehot_b[c * ch:(c + 1) * ch, :]) + base
            dest = jnp.sum(sel * rank, axis=-1, keepdims=True) - 1.0
            dest_ref[c * ch:(c + 1) * ch, :] = jnp.broadcast_to(dest, (ch, LANES))
            base = base + jnp.sum(sel, axis=0, keepdims=True)
        dest_row = dest_ref[...].T[0:1, :]
        gates = jnp.where(lane < EXPERTS_PER_GROUP, r, 0.0)
        g_hi = gates.astype(BF16)
        g_lo = (gates - g_hi.astype(F32)).astype(BF16)
        xcat = jnp.concatenate([x_ref[...], g_hi, g_lo], axis=1)
        for c in range(rows // ch):
            rr = (lax.broadcasted_iota(jnp.int32, (ch, tb), 0) + c * ch).astype(F32)
            p = jnp.where(rr == dest_row, 1.0, 0.0).astype(BF16)
            moved = _dot(p, xcat)
            xs_ref[c * ch:(c + 1) * ch, :] = moved[:, 0:d].astype(BF16)
            gs_ref[c * ch:(c + 1) * ch, :] = moved[:, d:d + LANES] + moved[:, d + LANES:d + 2 * LANES]
        ys_ref[...] = jnp.zeros_like(ys_ref)

    off = off_ref[i * N_GROUPS + g]
    n_sub = nsub_ref[i * N_GROUPS + g]

    def expert_rows(r0, size):
        xt = xs_ref[pl.ds(r0, size), :]
        gt = gs_ref[pl.ds(r0, size), :]
        acc = jnp.zeros((size, d), F32)
        for e in range(EXPERTS_PER_GROUP):
            hg = _dot(xt, wg_ref[0, e])
            up = _dot(xt, wu_ref[0, e])
            act = hg * _sigmoid(hg) * up * gt[:, e:e + 1]
            acc = acc + _dot(act.astype(BF16), wd_ref[0, e])
        ys_ref[pl.ds(r0, size), :] = acc.astype(BF16)

    def pair(j, carry):
        expert_rows(pl.multiple_of(off + j * (2 * MOE_SUB), MOE_SUB), 2 * MOE_SUB)
        return carry

    lax.fori_loop(0, n_sub // 2, pair, 0)

    @pl.when(n_sub % 2 == 1)
    def _():
        expert_rows(pl.multiple_of(off + (n_sub - 1) * MOE_SUB, MOE_SUB), MOE_SUB)

    @pl.when(g == N_GROUPS - 1)
    def _():
        ys = ys_ref[...]
        for c in range(tb // ch):
            cc = lax.broadcasted_iota(jnp.int32, (ch, rows), 1).astype(F32)
            pt = jnp.where(cc == dest_ref[c * ch:(c + 1) * ch, 0:1], 1.0, 0.0).astype(BF16)
            y_ref[c * ch:(c + 1) * ch, :] = _dot(pt, ys).astype(y_ref.dtype)


def _hier_moe(xn, route, wg, wu, wd):
    t, d = xn.shape
    tb = MOE_BLOCK
    nb = t // tb
    e, ff = wg.shape[1], wg.shape[3]
    group = route[:, GROUP_LANE].astype(jnp.int32).reshape(nb, tb)
    counts = jnp.sum((group[:, :, None] == jnp.arange(N_GROUPS)[None, None, :]).astype(jnp.int32), axis=1)
    nsub = (counts + MOE_SUB - 1) // MOE_SUB
    padded = nsub * MOE_SUB
    off = jnp.cumsum(padded, axis=1) - padded
    grid_spec = pltpu.PrefetchScalarGridSpec(
        num_scalar_prefetch=2,
        grid=(nb, N_GROUPS),
        in_specs=[
            pl.BlockSpec((tb, d), lambda i, g, *_: (i, 0)),
            pl.BlockSpec((tb, LANES), lambda i, g, *_: (i, 0)),
            pl.BlockSpec((1, e, d, ff), lambda i, g, *_: (g, 0, 0, 0)),
            pl.BlockSpec((1, e, d, ff), lambda i, g, *_: (g, 0, 0, 0)),
            pl.BlockSpec((1, e, ff, d), lambda i, g, *_: (g, 0, 0, 0)),
        ],
        out_specs=pl.BlockSpec((tb, d), lambda i, g, *_: (i, 0)),
        scratch_shapes=[
            pltpu.VMEM((MOE_ROWS, d), BF16),
            pltpu.VMEM((MOE_ROWS, LANES), F32),
            pltpu.VMEM((MOE_ROWS, d), BF16),
            pltpu.VMEM((tb, LANES), F32),
        ],
    )
    return pl.pallas_call(
        _moe_kernel,
        grid_spec=grid_spec,
        out_shape=jax.ShapeDtypeStruct((t, d), BF16),
        compiler_params=_params("parallel", "arbitrary"),
        name="moe",
    )(off.reshape(-1).astype(jnp.int32), nsub.reshape(-1).astype(jnp.int32), xn, route, wg, wu, wd)


def _final_kernel(x_ref, y_ref, g_ref, o_ref):
    o_ref[...] = _rms(x_ref[...] + y_ref[...].astype(F32), g_ref[...])


def _final(x, y, g):
    t, d = x.shape
    tm = ROW_TILE
    row = pl.BlockSpec((tm, d), lambda i: (i, 0))
    return pl.pallas_call(
        _final_kernel,
        grid=(t // tm,),
        in_specs=[row, row, _full(g.shape)],
        out_specs=row,
        out_shape=jax.ShapeDtypeStruct((t, d), F32),
        compiler_params=_params("parallel"),
        name="final_norm",
    )(x, y, g)


def _hi_lo(w):
    hi = w.astype(BF16)
    return hi, (w - hi.astype(F32)).astype(BF16)


def _rotate_half_cols(w):
    d, n = w.shape
    w3 = w.reshape(d, n // DIFF_HEAD_DIM, DIFF_HEAD_DIM)
    half = DIFF_HEAD_DIM // 2
    return jnp.concatenate([-w3[..., half:], w3[..., :half]], axis=-1).reshape(d, n)


def _router_params(w_rg, b_rg, w_re, b_re):
    d = w_rg.shape[0]
    n_e = N_GROUPS * EXPERTS_PER_GROUP
    pad = jnp.zeros((d, EXPERT_LANE0 - N_GROUPS), F32)
    tail = jnp.zeros((d, LANES - EXPERT_LANE0 - n_e), F32)
    w = jnp.concatenate([w_rg, pad, w_re, tail], axis=1)
    b = jnp.concatenate([b_rg, pad[0], b_re, tail[0]]).reshape(1, LANES)
    return _hi_lo(w) + (b,)


def kernel(x, positions, norm_mix, norm_ffn, ab_w_in, ab_gate_bias, ab_conv_w, ab_mlstm_norm, ab_lambda,
           ab_diff_norm, ab_w_out, c_w_in, c_conv_w, c_w_out, moe_w_rg, moe_b_rg, moe_w_re, moe_b_re,
           moe_w_gate, moe_w_up, moe_w_down, final_norm):
    batch, seq, d = x.shape
    t = batch * seq
    xf = x.reshape(t, d)
    pos = positions.reshape(t, 1).astype(jnp.int32)
    row2 = lambda v: v.reshape(1, -1).astype(F32)

    w_in = ab_w_in[0]
    mw = 2 * MLSTM_HEADS * MLSTM_HEAD_DIM
    hw = MLSTM_HEADS * MLSTM_HEAD_DIM
    ng = 2 * MLSTM_HEADS
    dw = DIFF_HEADS * DIFF_V_DIM
    o_g = mw + 2 * hw
    o_q = o_g + ng
    o_k, o_dv = o_q + dw, o_q + 2 * dw
    w_q, w_k = w_in[:, o_q:o_k], w_in[:, o_k:o_dv]
    w0 = jnp.concatenate([w_in[:, :o_g], w_q, w_k,
                          _rotate_half_cols(w_q), _rotate_half_cols(w_k)], axis=1).astype(BF16)
    wvt = w_in[:, o_dv:o_dv + dw].T.astype(BF16)
    w_gates = jnp.concatenate([w_in[:, o_g:o_q], jnp.zeros((d, LANES - ng), F32)], axis=1)
    wgh, wgl = _hi_lo(w_gates)
    gb = jnp.concatenate([ab_gate_bias[0].astype(F32), jnp.zeros((LANES - ng,), F32)]).reshape(1, LANES)
    half = DIFF_HEAD_DIM // 2
    inv_freq = 1.0 / (ROPE_THETA ** (jnp.arange(half, dtype=F32) / half))
    invf = jnp.tile(inv_freq, LANES // half).reshape(1, LANES)

    mqk, mv, mo, dq, dk, dvt, gates = _proj0(xf, pos, row2(norm_mix[0]), w0, wvt, wgh, wgl, gb, invf)
    hm = _mlstm(mqk, mv, mo, gates, ab_conv_w[0].astype(F32), row2(ab_mlstm_norm[0]), batch, seq)
    lam_init = 0.8 - 0.6 * math.exp(-0.3 * 0)
    hd = _diff_attn(dq, dk, dvt, ab_lambda[0].astype(F32), row2(ab_diff_norm[0]), batch, seq, lam_init)

    wrh, wrl, br = _router_params(moe_w_rg[0], moe_b_rg[0], moe_w_re[0], moe_b_re[0])
    xmid, xn, route = _mix0_tail(xf, hm, hd, ab_w_out[0].astype(BF16), row2(norm_ffn[0]), wrh, wrl, br)
    y = _hier_moe(xn, route, moe_w_gate[0].astype(BF16), moe_w_up[0].astype(BF16), moe_w_down[0].astype(BF16))

    x1, bgate, u = _proj1(xmid, y, row2(norm_mix[1]), c_w_in[0].astype(BF16))
    wrh, wrl, br = _router_params(moe_w_rg[1], moe_b_rg[1], moe_w_re[1], moe_b_re[1])
    xmid, xn, route = _mix1_tail(x1, bgate, u, c_conv_w[0].astype(F32), c_w_out[0].astype(BF16),
                                 row2(norm_ffn[1]), wrh, wrl, br, seq)
    y = _hier_moe(xn, route, moe_w_gate[1].astype(BF16), moe_w_up[1].astype(BF16), moe_w_down[1].astype(BF16))

    return _final(xmid, y, row2(final_norm)).reshape(batch, seq, d)
```

```python
import functools
import math

import jax
import jax.numpy as jnp
from jax import lax
from jax.experimental import pallas as pl
from jax.experimental.pallas import tpu as pltpu

F32 = jnp.float32
BF16 = jnp.bfloat16

EPS = 1e-6
LOG2E = math.log2(math.e)
ROPE_THETA = 10000.0
LANES = 128
SUBLANES = 8
MLSTM_HEADS = 4
MLSTM_HEAD_DIM = 128
MLSTM_CONV = 4
DIFF_HEADS = 4
DIFF_HEAD_DIM = 64
DIFF_V_DIM = 128
CONV_WIDTH = 3
N_GROUPS = 4
EXPERTS_PER_GROUP = 8
EXPERT_FF = 256
HALO = 8

ROW_TILE = 512
MLSTM_CHUNK = 256
ATTN_BLOCK = 512
MOE_BLOCK = 1024
MOE_SUB = 128
MOE_ROWS = MOE_BLOCK + N_GROUPS * MOE_SUB
MOE_CHUNK = 256
VMEM_LIMIT = 56 * 1024 * 1024


def _dot(a, b):
    return jnp.dot(a, b, preferred_element_type=F32)


def _dot_nt(a, b):
    return lax.dot_general(a, b, (((1,), (1,)), ((), ())), preferred_element_type=F32)


def _dot_tn(a, b):
    return lax.dot_general(a, b, (((0,), (0,)), ((), ())), preferred_element_type=F32)


def _split3(x):
    hi = x.astype(BF16)
    r1 = x - hi.astype(F32)
    mid = r1.astype(BF16)
    lo = (r1 - mid.astype(F32)).astype(BF16)
    return hi, mid, lo


def _dot_f32ish(x, w_hi, w_lo):
    xh = x.astype(BF16)
    xl = (x - xh.astype(F32)).astype(BF16)
    return _dot(xh, w_hi) + (_dot(xl, w_hi) + _dot(xh, w_lo))


def _rms(x, g):
    return x * lax.rsqrt(jnp.mean(x * x, axis=-1, keepdims=True) + EPS) * g


def _sigmoid(x):
    return 1.0 / (1.0 + jnp.exp(-x))


def _params(*sem):
    return pltpu.CompilerParams(dimension_semantics=sem, vmem_limit_bytes=VMEM_LIMIT)


def _full(shape):
    return pl.BlockSpec(shape, lambda *_: (0,) * len(shape))


def _proj0_kernel(x_ref, pos_ref, g_ref, w_ref, wvt_ref, gb_ref, invf_ref,
                  mqk_ref, mo_ref, dq_ref, dk_ref, mvt_ref, dvt_ref, gates_ref):
    xn = _rms(x_ref[...], g_ref[...])
    xb = xn.astype(BF16)
    xl = (xn - xb.astype(F32)).astype(BF16)

    def seg(lo, hi):
        return _dot(xb, w_ref[:, lo:hi])

    mqk_ref[...] = seg(0, 1024).astype(BF16)
    mo_ref[...] = seg(1024, 1536).astype(BF16)
    vt = _dot_nt(wvt_ref[...], xb)
    mvt_ref[...] = vt[0:512, :].astype(BF16)
    dvt_ref[...] = vt[512:1024, :].astype(BF16)
    gates_ref[...] = (vt[1024:1024 + LANES, :] + (vt[1024 + LANES:1024 + 2 * LANES, :]
                      + _dot_nt(wvt_ref[1024:1024 + LANES, :], xl))) + gb_ref[...]

    ang = pos_ref[...].astype(F32) * invf_ref[...]
    c = jnp.cos(ang)
    s = jnp.sin(ang)
    c4 = jnp.concatenate([c, c, c, c], axis=1)
    s4 = jnp.concatenate([s, s, s, s], axis=1)
    q_scale = math.log2(math.e) * DIFF_HEAD_DIM ** -0.5
    dq_ref[...] = ((seg(1536, 2048) * c4 + seg(2560, 3072) * s4) * q_scale).astype(BF16)
    dk_ref[...] = (seg(2048, 2560) * c4 + seg(3072, 3584) * s4).astype(BF16)


def _proj0(x, pos, g, w, wvt, gb, invf):
    t, d = x.shape
    tm = ROW_TILE
    row = lambda n: pl.BlockSpec((tm, n), lambda i: (i, 0))
    col = lambda n: pl.BlockSpec((n, tm), lambda i: (0, i))
    out_shapes = [
        jax.ShapeDtypeStruct((t, 1024), BF16), jax.ShapeDtypeStruct((t, 512), BF16),
        jax.ShapeDtypeStruct((t, 512), BF16), jax.ShapeDtypeStruct((t, 512), BF16),
        jax.ShapeDtypeStruct((512, t), BF16), jax.ShapeDtypeStruct((512, t), BF16),
        jax.ShapeDtypeStruct((LANES, t), F32),
    ]
    return pl.pallas_call(
        _proj0_kernel,
        grid=(t // tm,),
        in_specs=[row(d), row(1), _full(g.shape), _full(w.shape), _full(wvt.shape), _full(gb.shape),
                  _full(invf.shape)],
        out_specs=[row(1024), row(512), row(512), row(512), col(512), col(512), col(LANES)],
        out_shape=out_shapes,
        compiler_params=_params("parallel"),
        name="proj0",
    )(x, pos, g, w, wvt, gb, invf)


def _mlstm_kernel(mqk_ref, mvt_ref, mo_ref, gates_ref, shift_ref, cw_ref, norm_ref, hm_ref,
                  tail_ref, cn_ref, m_ref):
    L = MLSTM_CHUNK
    hd = MLSTM_HEAD_DIM
    width = MLSTM_HEADS * hd

    @pl.when(pl.program_id(1) == 0)
    def _():
        tail_ref[...] = jnp.zeros_like(tail_ref)
        cn_ref[...] = jnp.zeros_like(cn_ref)
        m_ref[...] = jnp.zeros_like(m_ref)

    xb = mqk_ref[...]
    x32 = xb.astype(F32)
    shifted = _dot(shift_ref[...], xb)
    cw = cw_ref[...]
    last = MLSTM_CONV - 1
    conv = cw[last:last + 1, :] * x32
    for sft in range(1, MLSTM_CONV):
        conv = conv + cw[last - sft:last - sft + 1, :] * shifted[(sft - 1) * L:sft * L, :]
    head = jnp.concatenate([tail_ref[...], x32[0:HALO, :]], axis=0)
    rowi = lax.broadcasted_iota(jnp.int32, (HALO, 2 * width), 0)
    fix = jnp.zeros((HALO, 2 * width), F32)
    for sft in range(1, MLSTM_CONV):
        fix = fix + cw[last - sft:last - sft + 1, :] * jnp.where(rowi < sft, head[HALO - sft:2 * HALO - sft, :], 0.0)
    conv = jnp.concatenate([conv[0:HALO, :] + fix, conv[HALO:, :]], axis=0)
    tail_ref[...] = x32[L - HALO:L, :]
    qk = conv * _sigmoid(conv)
    qb = (qk[:, :width] * (hd ** -0.5)).astype(BF16)
    k = qk[:, width:]
    kb = k.astype(BF16)

    g8 = gates_ref[0:SUBLANES, :]
    logf = (jnp.minimum(g8, 0.0) - jnp.log(1.0 + jnp.exp(-jnp.abs(g8)))) * LOG2E
    krow = lax.broadcasted_iota(jnp.int32, (L, L), 0)
    qcol = lax.broadcasted_iota(jnp.int32, (L, L), 1)
    causal_t = krow <= qcol
    triu = jnp.where(causal_t, 1.0, 0.0).astype(BF16)
    f_hi, f_mid, f_lo = _split3(logf)
    b8 = _dot(f_hi, triu) + (_dot(f_mid, triu) + _dot(f_lo, triu))
    cj8 = g8 * LOG2E - pltpu.roll(b8, MLSTM_HEADS, 0)
    cj_all = jnp.concatenate([cj8, jnp.zeros((LANES - SUBLANES, L), F32)], axis=0).T
    ones = jnp.ones((SUBLANES, L), BF16)

    for h in range(MLSTM_HEADS):
        sl = slice(h * hd, (h + 1) * hd)
        b_row = b8[MLSTM_HEADS + h:MLSTM_HEADS + h + 1, :]
        m_prev = m_ref[h:h + 1, 0:1]
        cn_prev = cn_ref[h]
        vt1 = jnp.concatenate([mvt_ref[sl, :], ones], axis=0)

        cmat = jnp.where(causal_t, cj_all[:, h:h + 1], -jnp.inf)
        mm = jnp.maximum(m_prev, jnp.max(cmat, axis=0, keepdims=True))
        w_t = jnp.exp2(cmat - mm)
        s_t = _dot_nt(kb[:, sl], qb[:, sl]) * w_t
        inter_w = jnp.exp2(m_prev - mm)
        tot = _dot(vt1, s_t.astype(BF16)) + inter_w * _dot_nt(cn_prev.astype(BF16), qb[:, sl])
        den = tot[hd:hd + 1, :]
        h_t = tot[0:hd, :] / jnp.maximum(jnp.abs(den), jnp.exp2(-(b_row + mm)))

        kw = k[:, sl] * w_t[:, L - 1:L]
        cn_ref[h] = inter_w[:, L - 1:L] * cn_prev + _dot(vt1, kw.astype(BF16))
        m_ref[h:h + 1, :] = jnp.broadcast_to(b_row[:, L - 1:L] + mm[:, L - 1:L], (1, LANES))

        scale = lax.rsqrt(jnp.mean(h_t * h_t, axis=0, keepdims=True) + EPS)
        hn = (h_t * scale).T * norm_ref[...]
        hm_ref[:, sl] = (hn * _sigmoid(mo_ref[:, sl].astype(F32))).astype(BF16)


def _mlstm(mqk, mvt, mo, gates, cw, norm, batch, seq):
    t = mqk.shape[0]
    L = MLSTM_CHUNK
    nc = seq // L
    width = MLSTM_HEADS * MLSTM_HEAD_DIM
    row = lambda n: pl.BlockSpec((L, n), lambda b, c: (b * nc + c, 0))
    col = lambda n: pl.BlockSpec((n, L), lambda b, c: (0, b * nc + c))
    r = jnp.arange(L)
    shift = jnp.concatenate([r[:, None] - s == r[None, :] for s in range(1, MLSTM_CONV)], axis=0).astype(BF16)
    return pl.pallas_call(
        _mlstm_kernel,
        grid=(batch, nc),
        in_specs=[row(2 * width), col(width), row(width), col(LANES), _full(shift.shape), _full(cw.shape),
                  _full(norm.shape)],
        out_specs=row(width),
        out_shape=jax.ShapeDtypeStruct((t, width), BF16),
        scratch_shapes=[
            pltpu.VMEM((HALO, 2 * width), F32),
            pltpu.VMEM((MLSTM_HEADS, MLSTM_HEAD_DIM + SUBLANES, MLSTM_HEAD_DIM), F32),
            pltpu.VMEM((SUBLANES, LANES), F32),
        ],
        compiler_params=_params("parallel", "arbitrary"),
        name="mlstm",
    )(mqk, mvt, mo, gates, shift, cw, norm)


def _diff_attn_kernel(q_ref, k_ref, vt_ref, lam_ref, norm_ref, o_ref,
                      acc0_ref, acc1_ref, m_ref, sa0_ref, sa1_ref, sb0_ref, sb1_ref, *, lam_init):
    tq = ATTN_BLOCK
    dv = DIFF_V_DIM
    qi = pl.program_id(2)
    q = q_ref[...]
    lane = lax.broadcasted_iota(jnp.int32, q.shape, 1)
    zero = jnp.zeros_like(q)
    qs = (jnp.where(lane < DIFF_HEAD_DIM, q, zero), jnp.where(lane >= DIFF_HEAD_DIM, q, zero))
    krow = lax.broadcasted_iota(jnp.int32, (tq, tq), 0)
    qcol = lax.broadcasted_iota(jnp.int32, (tq, tq), 1)
    causal_t = krow <= qcol
    ones = jnp.ones((SUBLANES, tq), BF16)
    accs = (acc0_ref, acc1_ref)
    buf_a = (sa0_ref, sa1_ref)
    buf_b = (sb0_ref, sb1_ref)

    for acc in accs:
        acc[...] = jnp.zeros_like(acc)
    m_ref[...] = jnp.full(m_ref.shape, -jnp.inf, F32)

    def scores(j, dst):
        kb = k_ref[pl.ds(pl.multiple_of(j * tq, tq), tq), :]
        for c in range(2):
            dst[c][...] = _dot_nt(kb, qs[c])

    def absorb(j, src, masked):
        start = pl.multiple_of(j * tq, tq)
        vt1 = jnp.concatenate([vt_ref[:, pl.ds(start, tq)], ones], axis=0)
        s = [src[c][...] for c in range(2)]
        if masked:
            s = [jnp.where(causal_t, sc, -jnp.inf) for sc in s]
        m_old = [m_ref[c:c + 1, :] for c in range(2)]
        m_new = [jnp.maximum(m_old[c], jnp.max(s[c], axis=0, keepdims=True)) for c in range(2)]
        p = [jnp.exp2(s[c] - m_new[c]).astype(BF16) for c in range(2)]
        alpha = [jnp.exp2(m_old[c] - m_new[c]) for c in range(2)]
        pv = [_dot(vt1, p[c]) for c in range(2)]
        for c in range(2):
            accs[c][...] = alpha[c] * accs[c][...] + pv[c]
            m_ref[c:c + 1, :] = m_new[c]

    scores(0, buf_a)

    def pair(jj, carry):
        j = 2 * jj
        scores(j + 1, buf_b)
        absorb(j, buf_a, False)
        scores(j + 2, buf_a)
        absorb(j + 1, buf_b, False)
        return carry

    lax.fori_loop(0, qi // 2, pair, 0)

    @pl.when(qi % 2 == 0)
    def _():
        absorb(qi, buf_a, True)

    @pl.when(qi % 2 == 1)
    def _():
        scores(qi, buf_b)
        absorb(qi - 1, buf_a, False)
        absorb(qi, buf_b, True)

    lv = lam_ref[...]
    lam = (jnp.exp(jnp.sum(lv[0:1, :] * lv[1:2, :], axis=-1, keepdims=True))
           - jnp.exp(jnp.sum(lv[2:3, :] * lv[3:4, :], axis=-1, keepdims=True)) + lam_init)
    a1 = acc0_ref[...]
    a2 = acc1_ref[...]
    o_t = a1[0:dv, :] / a1[dv:dv + 1, :] - lam * (a2[0:dv, :] / a2[dv:dv + 1, :])
    o_ref[...] = (_rms(o_t.T, norm_ref[...]) * (1.0 - lam_init)).astype(BF16)


def _diff_attn(dq, dk, dvt, lam_vecs, norm, batch, seq, lam_init):
    t = dq.shape[0]
    tq = ATTN_BLOCK
    nq = seq // tq
    return pl.pallas_call(
        functools.partial(_diff_attn_kernel, lam_init=lam_init),
        grid=(batch, DIFF_HEADS, nq),
        in_specs=[
            pl.BlockSpec((tq, DIFF_V_DIM), lambda b, h, i: (b * nq + i, h)),
            pl.BlockSpec((seq, DIFF_V_DIM), lambda b, h, i: (b, h)),
            pl.BlockSpec((DIFF_V_DIM, seq), lambda b, h, i: (h, b)),
            _full(lam_vecs.shape), _full(norm.shape),
        ],
        out_specs=pl.BlockSpec((tq, DIFF_V_DIM), lambda b, h, i: (b * nq + i, h)),
        out_shape=jax.ShapeDtypeStruct((t, DIFF_HEADS * DIFF_V_DIM), BF16),
        scratch_shapes=([pltpu.VMEM((DIFF_V_DIM + SUBLANES, tq), F32)] * 2 + [pltpu.VMEM((SUBLANES, tq), F32)]
                        + [pltpu.VMEM((tq, tq), F32)] * 4),
        compiler_params=_params("parallel", "parallel", "arbitrary"),
        name="diff_attn",
    )(dq, dk, dvt, lam_vecs, norm)


GROUP_LANE = EXPERTS_PER_GROUP
EXPERT_LANE0 = 8


def _route(xn, wrh_ref, wrl_ref, br_ref):
    logits = _dot_f32ish(xn, wrh_ref[...], wrl_ref[...]) + br_ref[...]
    lane = lax.broadcasted_iota(jnp.int32, logits.shape, 1)
    ninf = -jnp.inf
    big = jnp.int32(LANES)

    is_g = lane < N_GROUPS
    gl = jnp.where(is_g, logits, ninf)
    gmax = jnp.max(gl, axis=-1, keepdims=True)
    gsum = jnp.sum(jnp.where(is_g, jnp.exp(gl - gmax), 0.0), axis=-1, keepdims=True)
    g_val = 1.0 / gsum
    g_idx = jnp.min(jnp.where(gl == gmax, lane, big), axis=-1, keepdims=True)

    e_sel = jnp.zeros_like(logits)
    for g in range(N_GROUPS):
        shift = LANES - (EXPERT_LANE0 + EXPERTS_PER_GROUP * g)
        e_sel = jnp.where(g_idx == g, pltpu.roll(logits, shift, 1), e_sel)
    is_e = lane < EXPERTS_PER_GROUP
    el = jnp.where(is_e, e_sel, ninf)
    v1 = jnp.max(el, axis=-1, keepdims=True)
    i1 = jnp.min(jnp.where(el == v1, lane, big), axis=-1, keepdims=True)
    el2 = jnp.where(lane == i1, ninf, el)
    v2 = jnp.max(el2, axis=-1, keepdims=True)
    i2 = jnp.min(jnp.where(el2 == v2, lane, big), axis=-1, keepdims=True)
    e21 = jnp.exp(v2 - v1)
    w1 = 1.0 / (1.0 + e21)
    w2 = e21 * w1
    gates = g_val * (jnp.where(lane == i1, w1, 0.0) + jnp.where(lane == i2, w2, 0.0))
    return jnp.where(lane == GROUP_LANE, g_idx.astype(F32), gates)


def _mix0_tail_kernel(x_ref, hm_ref, hd_ref, wo_ref, nf_ref, wrh_ref, wrl_ref, br_ref,
                      xmid_ref, xn_ref, route_ref):
    half = hm_ref.shape[1]
    mix = _dot(hm_ref[...], wo_ref[0:half, :]) + _dot(hd_ref[...], wo_ref[half:2 * half, :])
    xmid = x_ref[...] + mix
    xmid_ref[...] = xmid
    xn = _rms(xmid, nf_ref[...])
    xn_ref[...] = xn.astype(BF16)
    route_ref[...] = _route(xn, wrh_ref, wrl_ref, br_ref)


def _mix0_tail(x, hm, hd, wo, nf, wrh, wrl, br):
    t, d = x.shape
    tm = ROW_TILE
    row = lambda n: pl.BlockSpec((tm, n), lambda i: (i, 0))
    return pl.pallas_call(
        _mix0_tail_kernel,
        grid=(t // tm,),
        in_specs=[row(d), row(hm.shape[1]), row(hd.shape[1]), _full(wo.shape), _full(nf.shape),
                  _full(wrh.shape), _full(wrl.shape), _full(br.shape)],
        out_specs=[row(d), row(d), row(LANES)],
        out_shape=[jax.ShapeDtypeStruct((t, d), F32), jax.ShapeDtypeStruct((t, d), BF16),
                   jax.ShapeDtypeStruct((t, LANES), F32)],
        compiler_params=_params("parallel"),
        name="mix0_tail",
    )(x, hm, hd, wo, nf, wrh, wrl, br)


def _proj1_kernel(x_ref, y_ref, g_ref, w_ref, x1_ref, b_ref, u_ref):
    d = x_ref.shape[1]
    x1 = x_ref[...] + y_ref[...].astype(F32)
    x1_ref[...] = x1
    xb = _rms(x1, g_ref[...]).astype(BF16)
    b_ref[...] = _dot(xb, w_ref[:, 0:d]).astype(BF16)
    u_ref[...] = (_dot(xb, w_ref[:, d:2 * d]) * _dot(xb, w_ref[:, 2 * d:3 * d])).astype(BF16)


def _proj1(x, y, g, w):
    t, d = x.shape
    tm = ROW_TILE
    row = lambda n: pl.BlockSpec((tm, n), lambda i: (i, 0))
    return pl.pallas_call(
        _proj1_kernel,
        grid=(t // tm,),
        in_specs=[row(d), row(d), _full(g.shape), _full(w.shape)],
        out_specs=[row(d), row(d), row(d)],
        out_shape=[jax.ShapeDtypeStruct((t, d), F32), jax.ShapeDtypeStruct((t, d), BF16),
                   jax.ShapeDtypeStruct((t, d), BF16)],
        compiler_params=_params("parallel"),
        name="proj1",
    )(x, y, g, w)


def _mix1_tail_kernel(x_ref, b_ref, u_ref, halo_ref, cw_ref, wo_ref, nf_ref, wrh_ref, wrl_ref, br_ref,
                      xmid_ref, xn_ref, route_ref, ue_ref, *, tiles_per_seq):
    tm = u_ref.shape[0]
    seq_start = pl.program_id(0) % tiles_per_seq == 0
    halo = halo_ref[...].astype(F32)
    ue_ref[0:HALO, :] = jnp.where(seq_start, jnp.zeros_like(halo), halo)
    ue_ref[HALO:HALO + tm, :] = u_ref[...].astype(F32)
    cw = cw_ref[...]
    conv = cw[0:1, :] * ue_ref[pl.ds(HALO - 2, tm), :]
    for j in range(1, CONV_WIDTH):
        conv = conv + cw[j:j + 1, :] * ue_ref[pl.ds(HALO - 2 + j, tm), :]
    mixed = (b_ref[...].astype(F32) * conv).astype(BF16)
    xmid = x_ref[...] + _dot(mixed, wo_ref[...])
    xmid_ref[...] = xmid
    xn = _rms(xmid, nf_ref[...])
    xn_ref[...] = xn.astype(BF16)
    route_ref[...] = _route(xn, wrh_ref, wrl_ref, br_ref)


def _mix1_tail(x, b, u, cw, wo, nf, wrh, wrl, br, seq):
    t, d = x.shape
    tm = ROW_TILE
    row = lambda n: pl.BlockSpec((tm, n), lambda i: (i, 0))
    halo = pl.BlockSpec((HALO, d), lambda i: (jnp.maximum(i * (tm // HALO) - 1, 0), 0))
    return pl.pallas_call(
        functools.partial(_mix1_tail_kernel, tiles_per_seq=seq // tm),
        grid=(t // tm,),
        in_specs=[row(d), row(d), row(d), halo, _full(cw.shape), _full(wo.shape), _full(nf.shape),
                  _full(wrh.shape), _full(wrl.shape), _full(br.shape)],
        out_specs=[row(d), row(d), row(LANES)],
        out_shape=[jax.ShapeDtypeStruct((t, d), F32), jax.ShapeDtypeStruct((t, d), BF16),
                   jax.ShapeDtypeStruct((t, LANES), F32)],
        scratch_shapes=[pltpu.VMEM((tm + HALO, d), F32)],
        compiler_params=_params("parallel"),
        name="mix1_tail",
    )(x, b, u, u, cw, wo, nf, wrh, wrl, br)


def _moe_kernel(off_ref, nsub_ref, x_ref, r_ref, wg_ref, wu_ref, wd_ref, y_ref,
                xs_ref, gs_ref, ys_ref, dest_ref):
    i = pl.program_id(0)
    g = pl.program_id(1)
    tb = MOE_BLOCK
    rows = MOE_ROWS
    ch = MOE_CHUNK
    d = x_ref.shape[1]

    @pl.when(g == 0)
    def _():
        r = r_ref[...]
        lane = lax.broadcasted_iota(jnp.int32, (tb, LANES), 1)
        grp = r[:, GROUP_LANE:GROUP_LANE + 1]
        onehot = jnp.where(lane.astype(F32) == grp, 1.0, 0.0)
        onehot_b = onehot.astype(BF16)
        lane1 = lax.broadcasted_iota(jnp.int32, (1, LANES), 1)
        off_row = jnp.zeros((1, LANES), F32)
        for gg in range(N_GROUPS):
            off_row = jnp.where(lane1 == gg, off_ref[i * N_GROUPS + gg].astype(F32), off_row)
        rr = lax.broadcasted_iota(jnp.int32, (ch, ch), 0)
        cc = lax.broadcasted_iota(jnp.int32, (ch, ch), 1)
        lower = jnp.where(cc <= rr, 1.0, 0.0).astype(BF16)
        base = off_row
        for c in range(tb // ch):
            sel = onehot[c * ch:(c + 1) * ch, :]
            rank = _dot(lower, onehot_b[c * ch:(c + 1) * ch, :]) + base
            dest = jnp.sum(sel * rank, axis=-1, keepdims=True) - 1.0
            dest_ref[c * ch:(c + 1) * ch, :] = jnp.broadcast_to(dest, (ch, LANES))
            base = base + jnp.sum(sel, axis=0, keepdims=True)
        dest_row = dest_ref[...].T[0:1, :]
        gates = jnp.where(lane < EXPERTS_PER_GROUP, r, 0.0)
        g_hi = gates.astype(BF16)
        g_lo = (gates - g_hi.astype(F32)).astype(BF16)
        xcat = jnp.concatenate([x_ref[...], g_hi, g_lo], axis=1)
        for c in range(rows // ch):
            rr = (lax.broadcasted_iota(jnp.int32, (ch, tb), 0) + c * ch).astype(F32)
            p = jnp.where(rr == dest_row, 1.0, 0.0).astype(BF16)
            moved = _dot(p, xcat)
            xs_ref[c * ch:(c + 1) * ch, :] = moved[:, 0:d].astype(BF16)
            gs_ref[c * ch:(c + 1) * ch, :] = moved[:, d:d + LANES] + moved[:, d + LANES:d + 2 * LANES]
        ys_ref[...] = jnp.zeros_like(ys_ref)

    off = off_ref[i * N_GROUPS + g]
    n_sub = nsub_ref[i * N_GROUPS + g]

    def expert_rows(r0, size):
        xt = xs_ref[pl.ds(r0, size), :]
        gt = gs_ref[pl.ds(r0, size), :]
        acc = jnp.zeros((size, d), F32)
        for e in range(EXPERTS_PER_GROUP):
            hg = _dot(xt, wg_ref[0, e])
            up = _dot(xt, wu_ref[0, e])
            act = hg * _sigmoid(hg) * up * gt[:, e:e + 1]
            acc = acc + _dot(act.astype(BF16), wd_ref[0, e])
        ys_ref[pl.ds(r0, size), :] = acc.astype(BF16)

    def pair(j, carry):
        expert_rows(pl.multiple_of(off + j * (2 * MOE_SUB), MOE_SUB), 2 * MOE_SUB)
        return carry

    lax.fori_loop(0, n_sub // 2, pair, 0)

    @pl.when(n_sub % 2 == 1)
    def _():
        expert_rows(pl.multiple_of(off + (n_sub - 1) * MOE_SUB, MOE_SUB), MOE_SUB)

    @pl.when(g == N_GROUPS - 1)
    def _():
        ys = ys_ref[...]
        for c in range(tb // ch):
            cc = lax.broadcasted_iota(jnp.int32, (ch, rows), 1).astype(F32)
            pt = jnp.where(cc == dest_ref[c * ch:(c + 1) * ch, 0:1], 1.0, 0.0).astype(BF16)
            y_ref[c * ch:(c + 1) * ch, :] = _dot(pt, ys).astype(y_ref.dtype)


def _hier_moe(xn, route, wg, wu, wd):
    t, d = xn.shape
    tb = MOE_BLOCK
    nb = t // tb
    e, ff = wg.shape[1], wg.shape[3]
    group = route[:, GROUP_LANE].astype(jnp.int32).reshape(nb, tb)
    counts = jnp.sum((group[:, :, None] == jnp.arange(N_GROUPS)[None, None, :]).astype(jnp.int32), axis=1)
    nsub = (counts + MOE_SUB - 1) // MOE_SUB
    padded = nsub * MOE_SUB
    off = jnp.cumsum(padded, axis=1) - padded
    grid_spec = pltpu.PrefetchScalarGridSpec(
        num_scalar_prefetch=2,
        grid=(nb, N_GROUPS),
        in_specs=[
            pl.BlockSpec((tb, d), lambda i, g, *_: (i, 0)),
            pl.BlockSpec((tb, LANES), lambda i, g, *_: (i, 0)),
            pl.BlockSpec((1, e, d, ff), lambda i, g, *_: (g, 0, 0, 0)),
            pl.BlockSpec((1, e, d, ff), lambda i, g, *_: (g, 0, 0, 0)),
            pl.BlockSpec((1, e, ff, d), lambda i, g, *_: (g, 0, 0, 0)),
        ],
        out_specs=pl.BlockSpec((tb, d), lambda i, g, *_: (i, 0)),
        scratch_shapes=[
            pltpu.VMEM((MOE_ROWS, d), BF16),
            pltpu.VMEM((MOE_ROWS, LANES), F32),
            pltpu.VMEM((MOE_ROWS, d), BF16),
            pltpu.VMEM((tb, LANES), F32),
        ],
    )
    return pl.pallas_call(
        _moe_kernel,
        grid_spec=grid_spec,
        out_shape=jax.ShapeDtypeStruct((t, d), BF16),
        compiler_params=_params("parallel", "arbitrary"),
        name="moe",
    )(off.reshape(-1).astype(jnp.int32), nsub.reshape(-1).astype(jnp.int32), xn, route, wg, wu, wd)


def _final_kernel(x_ref, y_ref, g_ref, o_ref):
    o_ref[...] = _rms(x_ref[...] + y_ref[...].astype(F32), g_ref[...])


def _final(x, y, g):
    t, d = x.shape
    tm = ROW_TILE
    row = pl.BlockSpec((tm, d), lambda i: (i, 0))
    return pl.pallas_call(
        _final_kernel,
        grid=(t // tm,),
        in_specs=[row, row, _full(g.shape)],
        out_specs=row,
        out_shape=jax.ShapeDtypeStruct((t, d), F32),
        compiler_params=_params("parallel"),
        name="final_norm",
    )(x, y, g)


def _hi_lo(w):
    hi = w.astype(BF16)
    return hi, (w - hi.astype(F32)).astype(BF16)


def _rotate_half_cols(w):
    d, n = w.shape
    w3 = w.reshape(d, n // DIFF_HEAD_DIM, DIFF_HEAD_DIM)
    half = DIFF_HEAD_DIM // 2
    return jnp.concatenate([-w3[..., half:], w3[..., :half]], axis=-1).reshape(d, n)


def _router_params(w_rg, b_rg, w_re, b_re):
    d = w_rg.shape[0]
    n_e = N_GROUPS * EXPERTS_PER_GROUP
    pad = jnp.zeros((d, EXPERT_LANE0 - N_GROUPS), F32)
    tail = jnp.zeros((d, LANES - EXPERT_LANE0 - n_e), F32)
    w = jnp.concatenate([w_rg, pad, w_re, tail], axis=1)
    b = jnp.concatenate([b_rg, pad[0], b_re, tail[0]]).reshape(1, LANES)
    return _hi_lo(w) + (b,)


def kernel(x, positions, norm_mix, norm_ffn, ab_w_in, ab_gate_bias, ab_conv_w, ab_mlstm_norm, ab_lambda,
           ab_diff_norm, ab_w_out, c_w_in, c_conv_w, c_w_out, moe_w_rg, moe_b_rg, moe_w_re, moe_b_re,
           moe_w_gate, moe_w_up, moe_w_down, final_norm):
    batch, seq, d = x.shape
    t = batch * seq
    xf = x.reshape(t, d)
    pos = positions.reshape(t, 1).astype(jnp.int32)
    row2 = lambda v: v.reshape(1, -1).astype(F32)

    w_in = ab_w_in[0]
    mw = 2 * MLSTM_HEADS * MLSTM_HEAD_DIM
    hw = MLSTM_HEADS * MLSTM_HEAD_DIM
    ng = 2 * MLSTM_HEADS
    dw = DIFF_HEADS * DIFF_V_DIM
    o_g = mw + 2 * hw
    o_q = o_g + ng
    o_k, o_dv = o_q + dw, o_q + 2 * dw
    w_q, w_k = w_in[:, o_q:o_k], w_in[:, o_k:o_dv]
    w0 = jnp.concatenate([w_in[:, :mw], w_in[:, mw + hw:o_g], w_q, w_k,
                          _rotate_half_cols(w_q), _rotate_half_cols(w_k)], axis=1).astype(BF16)
    w_gates = jnp.concatenate([w_in[:, o_g:o_q], jnp.zeros((d, LANES - ng), F32)], axis=1)
    wgh, wgl = _hi_lo(w_gates)
    wvt = jnp.concatenate([w_in[:, mw:mw + hw].astype(BF16), w_in[:, o_dv:o_dv + dw].astype(BF16), wgh, wgl],
                          axis=1).T
    gb = jnp.concatenate([ab_gate_bias[0].astype(F32), jnp.zeros((LANES - ng,), F32)]).reshape(LANES, 1)
    half = DIFF_HEAD_DIM // 2
    inv_freq = 1.0 / (ROPE_THETA ** (jnp.arange(half, dtype=F32) / half))
    invf = jnp.tile(inv_freq, LANES // half).reshape(1, LANES)

    mqk, mo, dq, dk, mvt, dvt, gates = _proj0(xf, pos, row2(norm_mix[0]), w0, wvt, gb, invf)
    hm = _mlstm(mqk, mvt, mo, gates, ab_conv_w[0].astype(F32), row2(ab_mlstm_norm[0]), batch, seq)
    lam_init = 0.8 - 0.6 * math.exp(-0.3 * 0)
    hd = _diff_attn(dq, dk, dvt, ab_lambda[0].astype(F32), row2(ab_diff_norm[0]), batch, seq, lam_init)

    wrh, wrl, br = _router_params(moe_w_rg[0], moe_b_rg[0], moe_w_re[0], moe_b_re[0])
    xmid, xn, route = _mix0_tail(xf, hm, hd, ab_w_out[0].astype(BF16), row2(norm_ffn[0]), wrh, wrl, br)
    y = _hier_moe(xn, route, moe_w_gate[0].astype(BF16), moe_w_up[0].astype(BF16), moe_w_down[0].astype(BF16))

    x1, bgate, u = _proj1(xmid, y, row2(norm_mix[1]), c_w_in[0].astype(BF16))
    wrh, wrl, br = _router_params(moe_w_rg[1], moe_b_rg[1], moe_w_re[1], moe_b_re[1])
    xmid, xn, route = _mix1_tail(x1, bgate, u, c_conv_w[0].astype(F32), c_w_out[0].astype(BF16),
                                 row2(norm_ffn[1]), wrh, wrl, br, seq)
    y = _hier_moe(xn, route, moe_w_gate[1].astype(BF16), moe_w_up[1].astype(BF16), moe_w_down[1].astype(BF16))

    return _final(xmid, y, row2(final_norm)).reshape(batch, seq, d)
```

```python
import functools
import math

import jax
import jax.numpy as jnp
from jax import lax
from jax.experimental import pallas as pl
from jax.experimental.pallas import tpu as pltpu

F32 = jnp.float32
BF16 = jnp.bfloat16

EPS = 1e-6
LOG2E = math.log2(math.e)
ROPE_THETA = 10000.0
LANES = 128
SUBLANES = 8
MLSTM_HEADS = 4
MLSTM_HEAD_DIM = 128
MLSTM_CONV = 4
DIFF_HEADS = 4
DIFF_HEAD_DIM = 64
DIFF_V_DIM = 128
CONV_WIDTH = 3
N_GROUPS = 4
EXPERTS_PER_GROUP = 8
EXPERT_FF = 256
HALO = 8

ROW_TILE = 512
MLSTM_CHUNK = 256
ATTN_BLOCK = 512
MOE_BLOCK = 1024
MOE_SUB = 128
MOE_ROWS = MOE_BLOCK + N_GROUPS * MOE_SUB
MOE_CHUNK = 256
VMEM_LIMIT = 56 * 1024 * 1024


def _dot(a, b):
    return jnp.dot(a, b, preferred_element_type=F32)


def _dot_nt(a, b):
    return lax.dot_general(a, b, (((1,), (1,)), ((), ())), preferred_element_type=F32)


def _dot_tn(a, b):
    return lax.dot_general(a, b, (((0,), (0,)), ((), ())), preferred_element_type=F32)


def _split3(x):
    hi = x.astype(BF16)
    r1 = x - hi.astype(F32)
    mid = r1.astype(BF16)
    lo = (r1 - mid.astype(F32)).astype(BF16)
    return hi, mid, lo


def _dot_f32ish(x, w_hi, w_lo):
    xh = x.astype(BF16)
    xl = (x - xh.astype(F32)).astype(BF16)
    return _dot(xh, w_hi) + (_dot(xl, w_hi) + _dot(xh, w_lo))


def _rms(x, g):
    return x * lax.rsqrt(jnp.mean(x * x, axis=-1, keepdims=True) + EPS) * g


def _sigmoid(x):
    return 1.0 / (1.0 + jnp.exp(-x))


def _shift_matrix(rows, taps):
    r = jnp.arange(rows)
    return jnp.concatenate([r[:, None] - s == r[None, :] for s in range(1, taps)], axis=0).astype(BF16)


def _shift_conv(xb, shift_ref, cw, tail, taps):
    rows, c = xb.shape
    x32 = xb.astype(F32)
    shifted = _dot(shift_ref[...], xb)
    last = taps - 1
    conv = cw[last:last + 1, :] * x32
    for s in range(1, taps):
        conv = conv + cw[last - s:last - s + 1, :] * shifted[(s - 1) * rows:s * rows, :]
    head = jnp.concatenate([tail, x32[0:HALO, :]], axis=0)
    rowi = lax.broadcasted_iota(jnp.int32, (HALO, c), 0)
    fix = jnp.zeros((HALO, c), F32)
    for s in range(1, taps):
        fix = fix + cw[last - s:last - s + 1, :] * jnp.where(rowi < s, head[HALO - s:2 * HALO - s, :], 0.0)
    return jnp.concatenate([conv[0:HALO, :] + fix, conv[HALO:, :]], axis=0)


def _params(*sem):
    return pltpu.CompilerParams(dimension_semantics=sem, vmem_limit_bytes=VMEM_LIMIT)


def _full(shape):
    return pl.BlockSpec(shape, lambda *_: (0,) * len(shape))


def _proj0_kernel(x_ref, pos_ref, g_ref, w_ref, wvt_ref, gb_ref, invf_ref,
                  mqk_ref, mo_ref, dq_ref, dk_ref, mvt_ref, dvt_ref, gates_ref):
    xn = _rms(x_ref[...], g_ref[...])
    xb = xn.astype(BF16)
    xl = (xn - xb.astype(F32)).astype(BF16)

    def seg(lo, hi):
        return _dot(xb, w_ref[:, lo:hi])

    mqk_ref[...] = seg(0, 1024).astype(BF16)
    mo_ref[...] = seg(1024, 1536).astype(BF16)
    vt = _dot_nt(wvt_ref[...], xb)
    mvt_ref[...] = vt[0:512, :].astype(BF16)
    dvt_ref[...] = vt[512:1024, :].astype(BF16)
    gates_ref[...] = (vt[1024:1024 + LANES, :] + (vt[1024 + LANES:1024 + 2 * LANES, :]
                      + _dot_nt(wvt_ref[1024:1024 + LANES, :], xl))) + gb_ref[...]

    ang = pos_ref[...].astype(F32) * invf_ref[...]
    c = jnp.cos(ang)
    s = jnp.sin(ang)
    half = DIFF_HEAD_DIM // 2
    lane = lax.broadcasted_iota(jnp.int32, c.shape, 1)
    first = (lane & (DIFF_HEAD_DIM - 1)) < half
    s = jnp.where(first, -s, s)
    c4 = jnp.concatenate([c, c, c, c], axis=1)
    s4 = jnp.concatenate([s, s, s, s], axis=1)
    lane4 = lax.broadcasted_iota(jnp.int32, c4.shape, 1)
    first4 = (lane4 & (DIFF_HEAD_DIM - 1)) < half

    def rope(v):
        n = v.shape[1]
        partner = jnp.where(first4, pltpu.roll(v, n - half, 1), pltpu.roll(v, half, 1))
        return v * c4 + partner * s4

    q_scale = LOG2E * DIFF_HEAD_DIM ** -0.5
    dq_ref[...] = (rope(seg(1536, 2048)) * q_scale).astype(BF16)
    dk_ref[...] = rope(seg(2048, 2560)).astype(BF16)


def _proj0(x, pos, g, w, wvt, gb, invf):
    t, d = x.shape
    tm = ROW_TILE
    row = lambda n: pl.BlockSpec((tm, n), lambda i: (i, 0))
    col = lambda n: pl.BlockSpec((n, tm), lambda i: (0, i))
    out_shapes = [
        jax.ShapeDtypeStruct((t, 1024), BF16), jax.ShapeDtypeStruct((t, 512), BF16),
        jax.ShapeDtypeStruct((t, 512), BF16), jax.ShapeDtypeStruct((t, 512), BF16),
        jax.ShapeDtypeStruct((512, t), BF16), jax.ShapeDtypeStruct((512, t), BF16),
        jax.ShapeDtypeStruct((LANES, t), F32),
    ]
    return pl.pallas_call(
        _proj0_kernel,
        grid=(t // tm,),
        in_specs=[row(d), row(1), _full(g.shape), _full(w.shape), _full(wvt.shape), _full(gb.shape),
                  _full(invf.shape)],
        out_specs=[row(1024), row(512), row(512), row(512), col(512), col(512), col(LANES)],
        out_shape=out_shapes,
        compiler_params=_params("parallel"),
        name="proj0",
    )(x, pos, g, w, wvt, gb, invf)


def _mlstm_kernel(mqk_ref, mvt_ref, mo_ref, gates_ref, shift_ref, cw_ref, norm_ref, hm_ref,
                  tail_ref, cn_ref, m_ref):
    L = MLSTM_CHUNK
    hd = MLSTM_HEAD_DIM
    width = MLSTM_HEADS * hd

    @pl.when(pl.program_id(1) == 0)
    def _():
        tail_ref[...] = jnp.zeros_like(tail_ref)
        cn_ref[...] = jnp.zeros_like(cn_ref)
        m_ref[...] = jnp.zeros_like(m_ref)

    conv = _shift_conv(mqk_ref[...], shift_ref, cw_ref[...], tail_ref[...], MLSTM_CONV)
    tail_ref[...] = mqk_ref[L - HALO:L, :].astype(F32)
    qk = conv * _sigmoid(conv)
    qb = (qk[:, :width] * (hd ** -0.5)).astype(BF16)
    k = qk[:, width:]
    kb = k.astype(BF16)

    g8 = gates_ref[0:SUBLANES, :]
    logf = (jnp.minimum(g8, 0.0) - jnp.log(1.0 + jnp.exp(-jnp.abs(g8)))) * LOG2E
    krow = lax.broadcasted_iota(jnp.int32, (L, L), 0)
    qcol = lax.broadcasted_iota(jnp.int32, (L, L), 1)
    causal_t = krow <= qcol
    triu = jnp.where(causal_t, 1.0, 0.0).astype(BF16)
    f_hi, f_mid, f_lo = _split3(logf)
    b8 = _dot(f_hi, triu) + (_dot(f_mid, triu) + _dot(f_lo, triu))
    cj8 = g8 * LOG2E - pltpu.roll(b8, MLSTM_HEADS, 0)
    cj_all = jnp.concatenate([cj8, jnp.zeros((LANES - SUBLANES, L), F32)], axis=0).T
    ones = jnp.ones((SUBLANES, L), BF16)

    for h in range(MLSTM_HEADS):
        sl = slice(h * hd, (h + 1) * hd)
        b_row = b8[MLSTM_HEADS + h:MLSTM_HEADS + h + 1, :]
        m_prev = m_ref[h:h + 1, 0:1]
        cn_prev = cn_ref[h]
        vt1 = jnp.concatenate([mvt_ref[sl, :], ones], axis=0)

        cmat = jnp.where(causal_t, cj_all[:, h:h + 1], -jnp.inf)
        mm = jnp.maximum(m_prev, jnp.max(cmat, axis=0, keepdims=True))
        w_t = jnp.exp2(cmat - mm)
        s_t = _dot_nt(kb[:, sl], qb[:, sl]) * w_t
        inter_w = jnp.exp2(m_prev - mm)
        tot = _dot(vt1, s_t.astype(BF16)) + inter_w * _dot_nt(cn_prev.astype(BF16), qb[:, sl])
        den = tot[hd:hd + 1, :]
        h_t = tot[0:hd, :] / jnp.maximum(jnp.abs(den), jnp.exp2(-(b_row + mm)))

        kw = k[:, sl] * w_t[:, L - 1:L]
        cn_ref[h] = inter_w[:, L - 1:L] * cn_prev + _dot(vt1, kw.astype(BF16))
        m_ref[h:h + 1, :] = jnp.broadcast_to(b_row[:, L - 1:L] + mm[:, L - 1:L], (1, LANES))

        scale = lax.rsqrt(jnp.mean(h_t * h_t, axis=0, keepdims=True) + EPS)
        hn = (h_t * scale).T * norm_ref[...]
        hm_ref[:, sl] = (hn * _sigmoid(mo_ref[:, sl].astype(F32))).astype(BF16)


def _mlstm(mqk, mvt, mo, gates, cw, norm, batch, seq):
    t = mqk.shape[0]
    L = MLSTM_CHUNK
    nc = seq // L
    width = MLSTM_HEADS * MLSTM_HEAD_DIM
    row = lambda n: pl.BlockSpec((L, n), lambda b, c: (b * nc + c, 0))
    col = lambda n: pl.BlockSpec((n, L), lambda b, c: (0, b * nc + c))
    shift = _shift_matrix(L, MLSTM_CONV)
    return pl.pallas_call(
        _mlstm_kernel,
        grid=(batch, nc),
        in_specs=[row(2 * width), col(width), row(width), col(LANES), _full(shift.shape), _full(cw.shape),
                  _full(norm.shape)],
        out_specs=row(width),
        out_shape=jax.ShapeDtypeStruct((t, width), BF16),
        scratch_shapes=[
            pltpu.VMEM((HALO, 2 * width), F32),
            pltpu.VMEM((MLSTM_HEADS, MLSTM_HEAD_DIM + SUBLANES, MLSTM_HEAD_DIM), F32),
            pltpu.VMEM((SUBLANES, LANES), F32),
        ],
        compiler_params=_params("parallel", "arbitrary"),
        name="mlstm",
    )(mqk, mvt, mo, gates, shift, cw, norm)


def _diff_attn_kernel(q_ref, k_ref, vt_ref, lam_ref, norm_ref, o_ref,
                      acc0_ref, acc1_ref, m_ref, sa0_ref, sa1_ref, sb0_ref, sb1_ref, *, lam_init):
    tq = ATTN_BLOCK
    dv = DIFF_V_DIM
    qi = pl.program_id(2)
    q = q_ref[...]
    lane = lax.broadcasted_iota(jnp.int32, q.shape, 1)
    zero = jnp.zeros_like(q)
    qs = (jnp.where(lane < DIFF_HEAD_DIM, q, zero), jnp.where(lane >= DIFF_HEAD_DIM, q, zero))
    krow = lax.broadcasted_iota(jnp.int32, (tq, tq), 0)
    qcol = lax.broadcasted_iota(jnp.int32, (tq, tq), 1)
    causal_t = krow <= qcol
    ones = jnp.ones((SUBLANES, tq), BF16)
    accs = (acc0_ref, acc1_ref)
    buf_a = (sa0_ref, sa1_ref)
    buf_b = (sb0_ref, sb1_ref)

    for acc in accs:
        acc[...] = jnp.zeros_like(acc)
    m_ref[...] = jnp.full(m_ref.shape, -jnp.inf, F32)

    def scores(j, dst):
        kb = k_ref[pl.ds(pl.multiple_of(j * tq, tq), tq), :]
        for c in range(2):
            dst[c][...] = _dot_nt(kb, qs[c])

    def absorb(j, src, masked):
        start = pl.multiple_of(j * tq, tq)
        vt1 = jnp.concatenate([vt_ref[:, pl.ds(start, tq)], ones], axis=0)
        s = [src[c][...] for c in range(2)]
        if masked:
            s = [jnp.where(causal_t, sc, -jnp.inf) for sc in s]
        m_old = [m_ref[c:c + 1, :] for c in range(2)]
        m_new = [jnp.maximum(m_old[c], jnp.max(s[c], axis=0, keepdims=True)) for c in range(2)]
        p = [jnp.exp2(s[c] - m_new[c]).astype(BF16) for c in range(2)]
        alpha = [jnp.exp2(m_old[c] - m_new[c]) for c in range(2)]
        pv = [_dot(vt1, p[c]) for c in range(2)]
        for c in range(2):
            accs[c][...] = alpha[c] * accs[c][...] + pv[c]
            m_ref[c:c + 1, :] = m_new[c]

    scores(0, buf_a)

    def pair(jj, carry):
        j = 2 * jj
        scores(j + 1, buf_b)
        absorb(j, buf_a, False)
        scores(j + 2, buf_a)
        absorb(j + 1, buf_b, False)
        return carry

    lax.fori_loop(0, qi // 2, pair, 0)

    @pl.when(qi % 2 == 0)
    def _():
        absorb(qi, buf_a, True)

    @pl.when(qi % 2 == 1)
    def _():
        scores(qi, buf_b)
        absorb(qi - 1, buf_a, False)
        absorb(qi, buf_b, True)

    lv = lam_ref[...]
    lam = (jnp.exp(jnp.sum(lv[0:1, :] * lv[1:2, :], axis=-1, keepdims=True))
           - jnp.exp(jnp.sum(lv[2:3, :] * lv[3:4, :], axis=-1, keepdims=True)) + lam_init)
    a1 = acc0_ref[...]
    a2 = acc1_ref[...]
    o_t = a1[0:dv, :] / a1[dv:dv + 1, :] - lam * (a2[0:dv, :] / a2[dv:dv + 1, :])
    o_ref[...] = (_rms(o_t.T, norm_ref[...]) * (1.0 - lam_init)).astype(BF16)


def _diff_attn(dq, dk, dvt, lam_vecs, norm, batch, seq, lam_init):
    t = dq.shape[0]
    tq = ATTN_BLOCK
    nq = seq // tq
    return pl.pallas_call(
        functools.partial(_diff_attn_kernel, lam_init=lam_init),
        grid=(batch, DIFF_HEADS, nq),
        in_specs=[
            pl.BlockSpec((tq, DIFF_V_DIM), lambda b, h, i: (b * nq + i, h)),
            pl.BlockSpec((seq, DIFF_V_DIM), lambda b, h, i: (b, h)),
            pl.BlockSpec((DIFF_V_DIM, seq), lambda b, h, i: (h, b)),
            _full(lam_vecs.shape), _full(norm.shape),
        ],
        out_specs=pl.BlockSpec((tq, DIFF_V_DIM), lambda b, h, i: (b * nq + i, h)),
        out_shape=jax.ShapeDtypeStruct((t, DIFF_HEADS * DIFF_V_DIM), BF16),
        scratch_shapes=([pltpu.VMEM((DIFF_V_DIM + SUBLANES, tq), F32)] * 2 + [pltpu.VMEM((SUBLANES, tq), F32)]
                        + [pltpu.VMEM((tq, tq), F32)] * 4),
        compiler_params=_params("parallel", "parallel", "arbitrary"),
        name="diff_attn",
    )(dq, dk, dvt, lam_vecs, norm)


ROUTE_ROWS = 2 * SUBLANES
GROUP_LANE = EXPERTS_PER_GROUP
DEST_LANE = EXPERTS_PER_GROUP
EXPERT_LANE0 = 8


def _route(xn, wrt_ref, br_ref):
    xh = xn.astype(BF16)
    xl = (xn - xh.astype(F32)).astype(BF16)
    both = _dot_nt(wrt_ref[...], xh)
    logits = both[0:LANES, :] + (both[LANES:2 * LANES, :] + _dot_nt(wrt_ref[0:LANES, :], xl)) + br_ref[...]
    e = EXPERTS_PER_GROUP
    tm = logits.shape[1]
    row = lax.broadcasted_iota(jnp.int32, (e, tm), 0)
    ninf = -jnp.inf
    big = jnp.int32(LANES)

    is_g = row < N_GROUPS
    gl = jnp.where(is_g, logits[0:e, :], ninf)
    gmax = jnp.max(gl, axis=0, keepdims=True)
    gsum = jnp.sum(jnp.where(is_g, jnp.exp(gl - gmax), 0.0), axis=0, keepdims=True)
    g_val = 1.0 / gsum
    g_idx = jnp.min(jnp.where(gl == gmax, row, big), axis=0, keepdims=True)

    el = logits[EXPERT_LANE0:EXPERT_LANE0 + e, :]
    for g in range(1, N_GROUPS):
        lo = EXPERT_LANE0 + e * g
        el = jnp.where(g_idx == g, logits[lo:lo + e, :], el)
    v1 = jnp.max(el, axis=0, keepdims=True)
    i1 = jnp.min(jnp.where(el == v1, row, big), axis=0, keepdims=True)
    el2 = jnp.where(row == i1, ninf, el)
    v2 = jnp.max(el2, axis=0, keepdims=True)
    i2 = jnp.min(jnp.where(el2 == v2, row, big), axis=0, keepdims=True)
    e21 = jnp.exp(v2 - v1)
    w1 = 1.0 / (1.0 + e21)
    w2 = e21 * w1
    gates = g_val * (jnp.where(row == i1, w1, 0.0) + jnp.where(row == i2, w2, 0.0))
    tail = jnp.where(row == 0, g_idx.astype(F32), 0.0)
    return xh, jnp.concatenate([gates, tail], axis=0)


def _mix0_tail_kernel(x_ref, hm_ref, hd_ref, wo_ref, nf_ref, wrt_ref, br_ref,
                      xmid_ref, xn_ref, route_ref):
    half = hm_ref.shape[1]
    mix = _dot(hm_ref[...], wo_ref[0:half, :]) + _dot(hd_ref[...], wo_ref[half:2 * half, :])
    xmid = x_ref[...] + mix
    xmid_ref[...] = xmid
    xn_ref[...], route_ref[...] = _route(_rms(xmid, nf_ref[...]), wrt_ref, br_ref)


def _mix0_tail(x, hm, hd, wo, nf, wrt, br):
    t, d = x.shape
    tm = ROW_TILE
    row = lambda n: pl.BlockSpec((tm, n), lambda i: (i, 0))
    return pl.pallas_call(
        _mix0_tail_kernel,
        grid=(t // tm,),
        in_specs=[row(d), row(hm.shape[1]), row(hd.shape[1]), _full(wo.shape), _full(nf.shape),
                  _full(wrt.shape), _full(br.shape)],
        out_specs=[row(d), row(d), pl.BlockSpec((ROUTE_ROWS, tm), lambda i: (0, i))],
        out_shape=[jax.ShapeDtypeStruct((t, d), F32), jax.ShapeDtypeStruct((t, d), BF16),
                   jax.ShapeDtypeStruct((ROUTE_ROWS, t), F32)],
        compiler_params=_params("parallel"),
        name="mix0_tail",
    )(x, hm, hd, wo, nf, wrt, br)


def _proj1_kernel(x_ref, y_ref, g_ref, w_ref, x1_ref, b_ref, u_ref):
    d = x_ref.shape[1]
    x1 = x_ref[...] + y_ref[...].astype(F32)
    x1_ref[...] = x1
    xb = _rms(x1, g_ref[...]).astype(BF16)
    b_ref[...] = _dot(xb, w_ref[:, 0:d]).astype(BF16)
    u_ref[...] = (_dot(xb, w_ref[:, d:2 * d]) * _dot(xb, w_ref[:, 2 * d:3 * d])).astype(BF16)


def _proj1(x, y, g, w):
    t, d = x.shape
    tm = ROW_TILE
    row = lambda n: pl.BlockSpec((tm, n), lambda i: (i, 0))
    return pl.pallas_call(
        _proj1_kernel,
        grid=(t // tm,),
        in_specs=[row(d), row(d), _full(g.shape), _full(w.shape)],
        out_specs=[row(d), row(d), row(d)],
        out_shape=[jax.ShapeDtypeStruct((t, d), F32), jax.ShapeDtypeStruct((t, d), BF16),
                   jax.ShapeDtypeStruct((t, d), BF16)],
        compiler_params=_params("parallel"),
        name="proj1",
    )(x, y, g, w)


def _mix1_tail_kernel(x_ref, b_ref, u_ref, halo_ref, shift_ref, cw_ref, wo_ref, nf_ref, wrt_ref, br_ref,
                      xmid_ref, xn_ref, route_ref, *, tiles_per_seq):
    seq_start = pl.program_id(0) % tiles_per_seq == 0
    halo = halo_ref[...].astype(F32)
    tail = jnp.where(seq_start, jnp.zeros_like(halo), halo)
    conv = _shift_conv(u_ref[...], shift_ref, cw_ref[...], tail, CONV_WIDTH)
    mixed = (b_ref[...].astype(F32) * conv).astype(BF16)
    xmid = x_ref[...] + _dot(mixed, wo_ref[...])
    xmid_ref[...] = xmid
    xn_ref[...], route_ref[...] = _route(_rms(xmid, nf_ref[...]), wrt_ref, br_ref)


def _mix1_tail(x, b, u, cw, wo, nf, wrt, br, seq):
    t, d = x.shape
    tm = ROW_TILE
    row = lambda n: pl.BlockSpec((tm, n), lambda i: (i, 0))
    halo = pl.BlockSpec((HALO, d), lambda i: (jnp.maximum(i * (tm // HALO) - 1, 0), 0))
    shift = _shift_matrix(tm, CONV_WIDTH)
    return pl.pallas_call(
        functools.partial(_mix1_tail_kernel, tiles_per_seq=seq // tm),
        grid=(t // tm,),
        in_specs=[row(d), row(d), row(d), halo, _full(shift.shape), _full(cw.shape), _full(wo.shape),
                  _full(nf.shape), _full(wrt.shape), _full(br.shape)],
        out_specs=[row(d), row(d), pl.BlockSpec((ROUTE_ROWS, tm), lambda i: (0, i))],
        out_shape=[jax.ShapeDtypeStruct((t, d), F32), jax.ShapeDtypeStruct((t, d), BF16),
                   jax.ShapeDtypeStruct((ROUTE_ROWS, t), F32)],
        compiler_params=_params("parallel"),
        name="mix1_tail",
    )(x, b, u, u, shift, cw, wo, nf, wrt, br)


def _moe_kernel(off_ref, nsub_ref, x_ref, r_ref, wg_ref, wu_ref, wd_ref, y_ref,
                xs_ref, gs_ref, ys_ref, dest_ref):
    i = pl.program_id(0)
    g = pl.program_id(1)
    tb = MOE_BLOCK
    rows = MOE_ROWS
    ch = MOE_CHUNK
    d = x_ref.shape[1]

    @pl.when(g == 0)
    def _():
        r = r_ref[...]
        grp = r[GROUP_LANE:GROUP_LANE + 1, :]
        row8 = lax.broadcasted_iota(jnp.int32, (SUBLANES, tb), 0)
        onehot = jnp.where(row8.astype(F32) == grp, 1.0, 0.0)
        row1 = lax.broadcasted_iota(jnp.int32, (SUBLANES, 1), 0)
        base = jnp.zeros((SUBLANES, 1), F32)
        for gg in range(N_GROUPS):
            base = jnp.where(row1 == gg, off_ref[i * N_GROUPS + gg].astype(F32), base)
        rr = lax.broadcasted_iota(jnp.int32, (ch, ch), 0)
        cc = lax.broadcasted_iota(jnp.int32, (ch, ch), 1)
        upper = jnp.where(rr <= cc, 1.0, 0.0).astype(BF16)
        parts = []
        for c in range(tb // ch):
            sel = onehot[:, c * ch:(c + 1) * ch]
            rank = _dot(sel.astype(BF16), upper) + base
            parts.append(jnp.sum(sel * rank, axis=0, keepdims=True) - 1.0)
            base = base + jnp.sum(sel, axis=1, keepdims=True)
        dest_row = jnp.concatenate(parts, axis=1)
        side = jnp.concatenate([r[0:SUBLANES, :], jnp.broadcast_to(dest_row, (SUBLANES, tb)),
                                jnp.zeros((LANES - 2 * SUBLANES, tb), F32)], axis=0).T
        dest_ref[...] = side
        lane = lax.broadcasted_iota(jnp.int32, (tb, LANES), 1)
        gates = jnp.where(lane < EXPERTS_PER_GROUP, side, 0.0)
        g_hi = gates.astype(BF16)
        g_lo = (gates - g_hi.astype(F32)).astype(BF16)
        xcat = jnp.concatenate([x_ref[...], g_hi, g_lo], axis=1)
        for c in range(rows // ch):
            rr = (lax.broadcasted_iota(jnp.int32, (ch, tb), 0) + c * ch).astype(F32)
            p = jnp.where(rr == dest_row, 1.0, 0.0).astype(BF16)
            moved = _dot(p, xcat)
            xs_ref[c * ch:(c + 1) * ch, :] = moved[:, 0:d].astype(BF16)
            gs_ref[c * ch:(c + 1) * ch, :] = moved[:, d:d + LANES] + moved[:, d + LANES:d + 2 * LANES]
        ys_ref[...] = jnp.zeros_like(ys_ref)

    off = off_ref[i * N_GROUPS + g]
    n_sub = nsub_ref[i * N_GROUPS + g]

    def expert_rows(r0, size):
        xt = xs_ref[pl.ds(r0, size), :]
        gt = gs_ref[pl.ds(r0, size), :]
        acc = jnp.zeros((size, d), F32)
        for e in range(EXPERTS_PER_GROUP):
            hg = _dot(xt, wg_ref[0, e])
            up = _dot(xt, wu_ref[0, e])
            act = hg * _sigmoid(hg) * up * gt[:, e:e + 1]
            acc = acc + _dot(act.astype(BF16), wd_ref[0, e])
        ys_ref[pl.ds(r0, size), :] = acc.astype(BF16)

    def pair(j, carry):
        expert_rows(pl.multiple_of(off + j * (2 * MOE_SUB), MOE_SUB), 2 * MOE_SUB)
        return carry

    lax.fori_loop(0, n_sub // 2, pair, 0)

    @pl.when(n_sub % 2 == 1)
    def _():
        expert_rows(pl.multiple_of(off + (n_sub - 1) * MOE_SUB, MOE_SUB), MOE_SUB)

    @pl.when(g == N_GROUPS - 1)
    def _():
        ys = ys_ref[...]
        for c in range(tb // ch):
            cc = lax.broadcasted_iota(jnp.int32, (ch, rows), 1).astype(F32)
            pt = jnp.where(cc == dest_ref[c * ch:(c + 1) * ch, DEST_LANE:DEST_LANE + 1], 1.0, 0.0).astype(BF16)
            y_ref[c * ch:(c + 1) * ch, :] = _dot(pt, ys).astype(y_ref.dtype)


def _hier_moe(xn, route, wg, wu, wd):
    t, d = xn.shape
    tb = MOE_BLOCK
    nb = t // tb
    e, ff = wg.shape[1], wg.shape[3]
    group = route[GROUP_LANE, :].astype(jnp.int32).reshape(nb, tb)
    counts = jnp.sum((group[:, :, None] == jnp.arange(N_GROUPS)[None, None, :]).astype(jnp.int32), axis=1)
    nsub = (counts + MOE_SUB - 1) // MOE_SUB
    padded = nsub * MOE_SUB
    off = jnp.cumsum(padded, axis=1) - padded
    grid_spec = pltpu.PrefetchScalarGridSpec(
        num_scalar_prefetch=2,
        grid=(nb, N_GROUPS),
        in_specs=[
            pl.BlockSpec((tb, d), lambda i, g, *_: (i, 0)),
            pl.BlockSpec((ROUTE_ROWS, tb), lambda i, g, *_: (0, i)),
            pl.BlockSpec((1, e, d, ff), lambda i, g, *_: (g, 0, 0, 0)),
            pl.BlockSpec((1, e, d, ff), lambda i, g, *_: (g, 0, 0, 0)),
            pl.BlockSpec((1, e, ff, d), lambda i, g, *_: (g, 0, 0, 0)),
        ],
        out_specs=pl.BlockSpec((tb, d), lambda i, g, *_: (i, 0)),
        scratch_shapes=[
            pltpu.VMEM((MOE_ROWS, d), BF16),
            pltpu.VMEM((MOE_ROWS, LANES), F32),
            pltpu.VMEM((MOE_ROWS, d), BF16),
            pltpu.VMEM((tb, LANES), F32),
        ],
    )
    return pl.pallas_call(
        _moe_kernel,
        grid_spec=grid_spec,
        out_shape=jax.ShapeDtypeStruct((t, d), BF16),
        compiler_params=_params("parallel", "arbitrary"),
        name="moe",
    )(off.reshape(-1).astype(jnp.int32), nsub.reshape(-1).astype(jnp.int32), xn, route, wg, wu, wd)


def _final_kernel(x_ref, y_ref, g_ref, o_ref):
    o_ref[...] = _rms(x_ref[...] + y_ref[...].astype(F32), g_ref[...])


def _final(x, y, g):
    t, d = x.shape
    tm = ROW_TILE
    row = pl.BlockSpec((tm, d), lambda i: (i, 0))
    return pl.pallas_call(
        _final_kernel,
        grid=(t // tm,),
        in_specs=[row, row, _full(g.shape)],
        out_specs=row,
        out_shape=jax.ShapeDtypeStruct((t, d), F32),
        compiler_params=_params("parallel"),
        name="final_norm",
    )(x, y, g)


def _hi_lo(w):
    hi = w.astype(BF16)
    return hi, (w - hi.astype(F32)).astype(BF16)


def _router_params(w_rg, b_rg, w_re, b_re):
    d = w_rg.shape[0]
    n_e = N_GROUPS * EXPERTS_PER_GROUP
    pad = jnp.zeros((d, EXPERT_LANE0 - N_GROUPS), F32)
    tail = jnp.zeros((d, LANES - EXPERT_LANE0 - n_e), F32)
    w = jnp.concatenate([w_rg, pad, w_re, tail], axis=1)
    b = jnp.concatenate([b_rg, pad[0], b_re, tail[0]]).reshape(LANES, 1)
    hi, lo = _hi_lo(w)
    return jnp.concatenate([hi, lo], axis=1).T, b


def kernel(x, positions, norm_mix, norm_ffn, ab_w_in, ab_gate_bias, ab_conv_w, ab_mlstm_norm, ab_lambda,
           ab_diff_norm, ab_w_out, c_w_in, c_conv_w, c_w_out, moe_w_rg, moe_b_rg, moe_w_re, moe_b_re,
           moe_w_gate, moe_w_up, moe_w_down, final_norm):
    batch, seq, d = x.shape
    t = batch * seq
    xf = x.reshape(t, d)
    pos = positions.reshape(t, 1).astype(jnp.int32)
    row2 = lambda v: v.reshape(1, -1).astype(F32)

    w_in = ab_w_in[0]
    mw = 2 * MLSTM_HEADS * MLSTM_HEAD_DIM
    hw = MLSTM_HEADS * MLSTM_HEAD_DIM
    ng = 2 * MLSTM_HEADS
    dw = DIFF_HEADS * DIFF_V_DIM
    o_g = mw + 2 * hw
    o_q = o_g + ng
    o_k, o_dv = o_q + dw, o_q + 2 * dw
    w0 = jnp.concatenate([w_in[:, :mw], w_in[:, mw + hw:o_g], w_in[:, o_q:o_dv]], axis=1).astype(BF16)
    w_gates = jnp.concatenate([w_in[:, o_g:o_q], jnp.zeros((d, LANES - ng), F32)], axis=1)
    wgh, wgl = _hi_lo(w_gates)
    wvt = jnp.concatenate([w_in[:, mw:mw + hw].astype(BF16), w_in[:, o_dv:o_dv + dw].astype(BF16), wgh, wgl],
                          axis=1).T
    gb = jnp.concatenate([ab_gate_bias[0].astype(F32), jnp.zeros((LANES - ng,), F32)]).reshape(LANES, 1)
    half = DIFF_HEAD_DIM // 2
    inv_freq = 1.0 / (ROPE_THETA ** (jnp.arange(half, dtype=F32) / half))
    invf = jnp.tile(inv_freq, LANES // half).reshape(1, LANES)

    mqk, mo, dq, dk, mvt, dvt, gates = _proj0(xf, pos, row2(norm_mix[0]), w0, wvt, gb, invf)
    hm = _mlstm(mqk, mvt, mo, gates, ab_conv_w[0].astype(F32), row2(ab_mlstm_norm[0]), batch, seq)
    lam_init = 0.8 - 0.6 * math.exp(-0.3 * 0)
    hd = _diff_attn(dq, dk, dvt, ab_lambda[0].astype(F32), row2(ab_diff_norm[0]), batch, seq, lam_init)

    wrt, br = _router_params(moe_w_rg[0], moe_b_rg[0], moe_w_re[0], moe_b_re[0])
    xmid, xn, route = _mix0_tail(xf, hm, hd, ab_w_out[0].astype(BF16), row2(norm_ffn[0]), wrt, br)
    y = _hier_moe(xn, route, moe_w_gate[0].astype(BF16), moe_w_up[0].astype(BF16), moe_w_down[0].astype(BF16))

    x1, bgate, u = _proj1(xmid, y, row2(norm_mix[1]), c_w_in[0].astype(BF16))
    wrt, br = _router_params(moe_w_rg[1], moe_b_rg[1], moe_w_re[1], moe_b_re[1])
    xmid, xn, route = _mix1_tail(x1, bgate, u, c_conv_w[0].astype(F32), c_w_out[0].astype(BF16),
                                 row2(norm_ffn[1]), wrt, br, seq)
    y = _hier_moe(xn, route, moe_w_gate[1].astype(BF16), moe_w_up[1].astype(BF16), moe_w_down[1].astype(BF16))

    return _final(xmid, y, row2(final_norm)).reshape(batch, seq, d)
```

```python
import functools
import math

import jax
import jax.numpy as jnp
from jax import lax
from jax.experimental import pallas as pl
from jax.experimental.pallas import tpu as pltpu

F32 = jnp.float32
BF16 = jnp.bfloat16

EPS = 1e-6
LOG2E = math.log2(math.e)
ROPE_THETA = 10000.0
LANES = 128
SUBLANES = 8
MLSTM_HEADS = 4
MLSTM_HEAD_DIM = 128
MLSTM_CONV = 4
DIFF_HEADS = 4
DIFF_HEAD_DIM = 64
DIFF_V_DIM = 128
CONV_WIDTH = 3
N_GROUPS = 4
EXPERTS_PER_GROUP = 8
EXPERT_FF = 256
HALO = 8

ROW_TILE = 512
MLSTM_CHUNK = 256
ATTN_BLOCK = 512
MOE_BLOCK = 1024
MOE_SUB = 128
MOE_ROWS = MOE_BLOCK + N_GROUPS * MOE_SUB
MOE_CHUNK = 256
VMEM_LIMIT = 56 * 1024 * 1024


def _dot(a, b):
    return jnp.dot(a, b, preferred_element_type=F32)


def _dot_nt(a, b):
    return lax.dot_general(a, b, (((1,), (1,)), ((), ())), preferred_element_type=F32)


def _dot_tn(a, b):
    return lax.dot_general(a, b, (((0,), (0,)), ((), ())), preferred_element_type=F32)


def _split3(x):
    hi = x.astype(BF16)
    r1 = x - hi.astype(F32)
    mid = r1.astype(BF16)
    lo = (r1 - mid.astype(F32)).astype(BF16)
    return hi, mid, lo


def _dot_f32ish(x, w_hi, w_lo):
    xh = x.astype(BF16)
    xl = (x - xh.astype(F32)).astype(BF16)
    return _dot(xh, w_hi) + (_dot(xl, w_hi) + _dot(xh, w_lo))


def _rms(x, g):
    return x * lax.rsqrt(jnp.mean(x * x, axis=-1, keepdims=True) + EPS) * g


def _sigmoid(x):
    return 0.5 * jnp.tanh(0.5 * x) + 0.5


def _shift_matrix(rows, taps):
    r = jnp.arange(rows)
    return jnp.concatenate([r[:, None] - s == r[None, :] for s in range(1, taps)], axis=0).astype(BF16)


def _shift_conv(xb, shift_ref, cw, tail, taps):
    rows, c = xb.shape
    x32 = xb.astype(F32)
    shifted = _dot(shift_ref[...], xb)
    last = taps - 1
    conv = cw[last:last + 1, :] * x32
    for s in range(1, taps):
        conv = conv + cw[last - s:last - s + 1, :] * shifted[(s - 1) * rows:s * rows, :]
    head = jnp.concatenate([tail, x32[0:HALO, :]], axis=0)
    rowi = lax.broadcasted_iota(jnp.int32, (HALO, c), 0)
    fix = jnp.zeros((HALO, c), F32)
    for s in range(1, taps):
        fix = fix + cw[last - s:last - s + 1, :] * jnp.where(rowi < s, head[HALO - s:2 * HALO - s, :], 0.0)
    return jnp.concatenate([conv[0:HALO, :] + fix, conv[HALO:, :]], axis=0)


def _params(*sem):
    return pltpu.CompilerParams(dimension_semantics=sem, vmem_limit_bytes=VMEM_LIMIT)


def _full(shape):
    return pl.BlockSpec(shape, lambda *_: (0,) * len(shape))


def _proj0_kernel(x_ref, cos_ref, sin_ref, g_ref, w_ref, wvt_ref, gb_ref,
                  mqk_ref, mo_ref, dq_ref, dk_ref, mvt_ref, dvt_ref, gates_ref):
    xn = _rms(x_ref[...], g_ref[...])
    xb = xn.astype(BF16)
    xl = (xn - xb.astype(F32)).astype(BF16)

    def seg(lo, hi):
        return _dot(xb, w_ref[:, lo:hi])

    mqk_ref[...] = seg(0, 1024).astype(BF16)
    mo_ref[...] = seg(1024, 1536).astype(BF16)
    vt = _dot_nt(wvt_ref[...], xb)
    mvt_ref[...] = vt[0:512, :].astype(BF16)
    dvt_ref[...] = vt[512:1024, :].astype(BF16)
    gates_ref[...] = (vt[1024:1024 + LANES, :] + (vt[1024 + LANES:1024 + 2 * LANES, :]
                      + _dot_nt(wvt_ref[1024:1024 + LANES, :], xl))) + gb_ref[...]

    half = DIFF_HEAD_DIM // 2
    heads = 2 * DIFF_HEADS
    c = cos_ref[...]
    s = sin_ref[...]
    c4 = jnp.concatenate([c, c] * heads, axis=1)
    s4 = jnp.concatenate([-s, s] * heads, axis=1)
    lane4 = lax.broadcasted_iota(jnp.int32, c4.shape, 1)
    first4 = (lane4 & (DIFF_HEAD_DIM - 1)) < half

    def rope(v):
        n = v.shape[1]
        partner = jnp.where(first4, pltpu.roll(v, n - half, 1), pltpu.roll(v, half, 1))
        return v * c4 + partner * s4

    q_scale = LOG2E * DIFF_HEAD_DIM ** -0.5
    dq_ref[...] = (rope(seg(1536, 2048)) * q_scale).astype(BF16)
    dk_ref[...] = rope(seg(2048, 2560)).astype(BF16)


def _rope_table_kernel(pos_ref, invf_ref, cos_ref, sin_ref):
    ang = pos_ref[...].astype(F32) * invf_ref[...]
    cos_ref[...] = jnp.cos(ang)
    sin_ref[...] = jnp.sin(ang)


def _rope_table(positions):
    t = positions.shape[0]
    half = DIFF_HEAD_DIM // 2
    per_row = LANES // half
    rows = t // per_row
    pos_rep = jnp.repeat(positions.reshape(rows, per_row), half, axis=1)
    inv_freq = 1.0 / (ROPE_THETA ** (jnp.arange(half, dtype=F32) / half))
    invf = jnp.tile(inv_freq, per_row).reshape(1, LANES)
    tr = min(rows, 1024)
    spec = pl.BlockSpec((tr, LANES), lambda i: (i, 0))
    cos, sin = pl.pallas_call(
        _rope_table_kernel,
        grid=(rows // tr,),
        in_specs=[spec, _full(invf.shape)],
        out_specs=[spec, spec],
        out_shape=[jax.ShapeDtypeStruct((rows, LANES), F32)] * 2,
        compiler_params=_params("parallel"),
        name="rope_table",
    )(pos_rep, invf)
    return cos.reshape(t, half), sin.reshape(t, half)


def _proj0(x, cos, sin, g, w, wvt, gb):
    t, d = x.shape
    tm = ROW_TILE
    row = lambda n: pl.BlockSpec((tm, n), lambda i: (i, 0))
    col = lambda n: pl.BlockSpec((n, tm), lambda i: (0, i))
    out_shapes = [
        jax.ShapeDtypeStruct((t, 1024), BF16), jax.ShapeDtypeStruct((t, 512), BF16),
        jax.ShapeDtypeStruct((t, 512), BF16), jax.ShapeDtypeStruct((t, 512), BF16),
        jax.ShapeDtypeStruct((512, t), BF16), jax.ShapeDtypeStruct((512, t), BF16),
        jax.ShapeDtypeStruct((LANES, t), F32),
    ]
    return pl.pallas_call(
        _proj0_kernel,
        grid=(t // tm,),
        in_specs=[row(d), row(cos.shape[1]), row(sin.shape[1]), _full(g.shape), _full(w.shape),
                  _full(wvt.shape), _full(gb.shape)],
        out_specs=[row(1024), row(512), row(512), row(512), col(512), col(512), col(LANES)],
        out_shape=out_shapes,
        compiler_params=_params("parallel"),
        name="proj0",
    )(x, cos, sin, g, w, wvt, gb)


def _mlstm_kernel(mqk_ref, mvt_ref, mo_ref, gates_ref, shift_ref, cw_ref, norm_ref, hm_ref,
                  tail_ref, cn_ref, m_ref):
    L = MLSTM_CHUNK
    hd = MLSTM_HEAD_DIM
    width = MLSTM_HEADS * hd

    @pl.when(pl.program_id(1) == 0)
    def _():
        tail_ref[...] = jnp.zeros_like(tail_ref)
        cn_ref[...] = jnp.zeros_like(cn_ref)
        m_ref[...] = jnp.zeros_like(m_ref)

    conv = _shift_conv(mqk_ref[...], shift_ref, cw_ref[...], tail_ref[...], MLSTM_CONV)
    tail_ref[...] = mqk_ref[L - HALO:L, :].astype(F32)
    qk = conv * _sigmoid(conv)
    qb = (qk[:, :width] * (hd ** -0.5)).astype(BF16)
    k = qk[:, width:]
    kb = k.astype(BF16)

    g8 = gates_ref[0:SUBLANES, :]
    logf = (jnp.minimum(g8, 0.0) - jnp.log(1.0 + jnp.exp(-jnp.abs(g8)))) * LOG2E
    krow = lax.broadcasted_iota(jnp.int32, (L, L), 0)
    qcol = lax.broadcasted_iota(jnp.int32, (L, L), 1)
    causal_t = krow <= qcol
    triu = jnp.where(causal_t, 1.0, 0.0).astype(BF16)
    f_hi, f_mid, f_lo = _split3(logf)
    b8 = _dot(f_hi, triu) + (_dot(f_mid, triu) + _dot(f_lo, triu))
    cj8 = g8 * LOG2E - pltpu.roll(b8, MLSTM_HEADS, 0)
    cj_all = jnp.concatenate([cj8, jnp.zeros((LANES - SUBLANES, L), F32)], axis=0).T
    ones = jnp.ones((SUBLANES, L), BF16)

    for h in range(MLSTM_HEADS):
        sl = slice(h * hd, (h + 1) * hd)
        b_row = b8[MLSTM_HEADS + h:MLSTM_HEADS + h + 1, :]
        m_prev = m_ref[h:h + 1, 0:1]
        cn_prev = cn_ref[h]
        vt1 = jnp.concatenate([mvt_ref[sl, :], ones], axis=0)

        cmat = jnp.where(causal_t, cj_all[:, h:h + 1], -jnp.inf)
        mm = jnp.maximum(m_prev, jnp.max(cmat, axis=0, keepdims=True))
        w_t = jnp.exp2(cmat - mm)
        s_t = _dot_nt(kb[:, sl], qb[:, sl]) * w_t
        inter_w = jnp.exp2(m_prev - mm)
        tot = _dot(vt1, s_t.astype(BF16)) + inter_w * _dot_nt(cn_prev.astype(BF16), qb[:, sl])
        den = tot[hd:hd + 1, :]
        h_t = tot[0:hd, :] / jnp.maximum(jnp.abs(den), jnp.exp2(-(b_row + mm)))

        kw = k[:, sl] * w_t[:, L - 1:L]
        cn_ref[h] = inter_w[:, L - 1:L] * cn_prev + _dot(vt1, kw.astype(BF16))
        m_ref[h:h + 1, :] = jnp.broadcast_to(b_row[:, L - 1:L] + mm[:, L - 1:L], (1, LANES))

        scale = lax.rsqrt(jnp.mean(h_t * h_t, axis=0, keepdims=True) + EPS)
        hn = (h_t * scale).T * norm_ref[...]
        hm_ref[:, sl] = (hn * _sigmoid(mo_ref[:, sl].astype(F32))).astype(BF16)


def _mlstm(mqk, mvt, mo, gates, cw, norm, batch, seq):
    t = mqk.shape[0]
    L = MLSTM_CHUNK
    nc = seq // L
    width = MLSTM_HEADS * MLSTM_HEAD_DIM
    row = lambda n: pl.BlockSpec((L, n), lambda b, c: (b * nc + c, 0))
    col = lambda n: pl.BlockSpec((n, L), lambda b, c: (0, b * nc + c))
    shift = _shift_matrix(L, MLSTM_CONV)
    return pl.pallas_call(
        _mlstm_kernel,
        grid=(batch, nc),
        in_specs=[row(2 * width), col(width), row(width), col(LANES), _full(shift.shape), _full(cw.shape),
                  _full(norm.shape)],
        out_specs=row(width),
        out_shape=jax.ShapeDtypeStruct((t, width), BF16),
        scratch_shapes=[
            pltpu.VMEM((HALO, 2 * width), F32),
            pltpu.VMEM((MLSTM_HEADS, MLSTM_HEAD_DIM + SUBLANES, MLSTM_HEAD_DIM), F32),
            pltpu.VMEM((SUBLANES, LANES), F32),
        ],
        compiler_params=_params("parallel", "arbitrary"),
        name="mlstm",
    )(mqk, mvt, mo, gates, shift, cw, norm)


def _diff_attn_kernel(q_ref, k_ref, vt_ref, lam_ref, norm_ref, o_ref,
                      acc0_ref, acc1_ref, m_ref, sa0_ref, sa1_ref, sb0_ref, sb1_ref, *, lam_init):
    tq = ATTN_BLOCK
    dv = DIFF_V_DIM
    qi = pl.program_id(2)
    q = q_ref[...]
    lane = lax.broadcasted_iota(jnp.int32, q.shape, 1)
    zero = jnp.zeros_like(q)
    qs = (jnp.where(lane < DIFF_HEAD_DIM, q, zero), jnp.where(lane >= DIFF_HEAD_DIM, q, zero))
    krow = lax.broadcasted_iota(jnp.int32, (tq, tq), 0)
    qcol = lax.broadcasted_iota(jnp.int32, (tq, tq), 1)
    causal_t = krow <= qcol
    ones = jnp.ones((SUBLANES, tq), BF16)
    accs = (acc0_ref, acc1_ref)
    buf_a = (sa0_ref, sa1_ref)
    buf_b = (sb0_ref, sb1_ref)

    for acc in accs:
        acc[...] = jnp.zeros_like(acc)
    m_ref[...] = jnp.full(m_ref.shape, -jnp.inf, F32)

    def scores(j, dst):
        kb = k_ref[pl.ds(pl.multiple_of(j * tq, tq), tq), :]
        for c in range(2):
            dst[c][...] = _dot_nt(kb, qs[c])

    def absorb(j, src, masked):
        start = pl.multiple_of(j * tq, tq)
        vt1 = jnp.concatenate([vt_ref[:, pl.ds(start, tq)], ones], axis=0)
        s = [src[c][...] for c in range(2)]
        if masked:
            s = [jnp.where(causal_t, sc, -jnp.inf) for sc in s]
        m_old = [m_ref[c:c + 1, :] for c in range(2)]
        m_new = [jnp.maximum(m_old[c], jnp.max(s[c], axis=0, keepdims=True)) for c in range(2)]
        p = [jnp.exp2(s[c] - m_new[c]).astype(BF16) for c in range(2)]
        alpha = [jnp.exp2(m_old[c] - m_new[c]) for c in range(2)]
        pv = [_dot(vt1, p[c]) for c in range(2)]
        for c in range(2):
            accs[c][...] = alpha[c] * accs[c][...] + pv[c]
            m_ref[c:c + 1, :] = m_new[c]

    scores(0, buf_a)

    def pair(jj, carry):
        j = 2 * jj
        scores(j + 1, buf_b)
        absorb(j, buf_a, False)
        scores(j + 2, buf_a)
        absorb(j + 1, buf_b, False)
        return carry

    lax.fori_loop(0, qi // 2, pair, 0)

    @pl.when(qi % 2 == 0)
    def _():
        absorb(qi, buf_a, True)

    @pl.when(qi % 2 == 1)
    def _():
        scores(qi, buf_b)
        absorb(qi - 1, buf_a, False)
        absorb(qi, buf_b, True)

    lv = lam_ref[...]
    lam = (jnp.exp(jnp.sum(lv[0:1, :] * lv[1:2, :], axis=-1, keepdims=True))
           - jnp.exp(jnp.sum(lv[2:3, :] * lv[3:4, :], axis=-1, keepdims=True)) + lam_init)
    a1 = acc0_ref[...]
    a2 = acc1_ref[...]
    o_t = a1[0:dv, :] / a1[dv:dv + 1, :] - lam * (a2[0:dv, :] / a2[dv:dv + 1, :])
    o_ref[...] = (_rms(o_t.T, norm_ref[...]) * (1.0 - lam_init)).astype(BF16)


def _diff_attn(dq, dk, dvt, lam_vecs, norm, batch, seq, lam_init):
    t = dq.shape[0]
    tq = ATTN_BLOCK
    nq = seq // tq
    return pl.pallas_call(
        functools.partial(_diff_attn_kernel, lam_init=lam_init),
        grid=(batch, DIFF_HEADS, nq),
        in_specs=[
            pl.BlockSpec((tq, DIFF_V_DIM), lambda b, h, i: (b * nq + i, h)),
            pl.BlockSpec((seq, DIFF_V_DIM), lambda b, h, i: (b, h)),
            pl.BlockSpec((DIFF_V_DIM, seq), lambda b, h, i: (h, b)),
            _full(lam_vecs.shape), _full(norm.shape),
        ],
        out_specs=pl.BlockSpec((tq, DIFF_V_DIM), lambda b, h, i: (b * nq + i, h)),
        out_shape=jax.ShapeDtypeStruct((t, DIFF_HEADS * DIFF_V_DIM), BF16),
        scratch_shapes=([pltpu.VMEM((DIFF_V_DIM + SUBLANES, tq), F32)] * 2 + [pltpu.VMEM((SUBLANES, tq), F32)]
                        + [pltpu.VMEM((tq, tq), F32)] * 4),
        compiler_params=_params("parallel", "parallel", "arbitrary"),
        name="diff_attn",
    )(dq, dk, dvt, lam_vecs, norm)


ROUTE_ROWS = 2 * SUBLANES
GROUP_LANE = EXPERTS_PER_GROUP
DEST_LANE = EXPERTS_PER_GROUP
EXPERT_LANE0 = 8


def _route(xn, wrt_ref, br_ref):
    xh = xn.astype(BF16)
    xl = (xn - xh.astype(F32)).astype(BF16)
    both = _dot_nt(wrt_ref[...], xh)
    logits = both[0:LANES, :] + (both[LANES:2 * LANES, :] + _dot_nt(wrt_ref[0:LANES, :], xl)) + br_ref[...]
    e = EXPERTS_PER_GROUP
    tm = logits.shape[1]
    row = lax.broadcasted_iota(jnp.int32, (e, tm), 0)
    ninf = -jnp.inf
    big = jnp.int32(LANES)

    is_g = row < N_GROUPS
    gl = jnp.where(is_g, logits[0:e, :], ninf)
    gmax = jnp.max(gl, axis=0, keepdims=True)
    gsum = jnp.sum(jnp.where(is_g, jnp.exp(gl - gmax), 0.0), axis=0, keepdims=True)
    g_val = 1.0 / gsum
    g_idx = jnp.min(jnp.where(gl == gmax, row, big), axis=0, keepdims=True)

    el = logits[EXPERT_LANE0:EXPERT_LANE0 + e, :]
    for g in range(1, N_GROUPS):
        lo = EXPERT_LANE0 + e * g
        el = jnp.where(g_idx == g, logits[lo:lo + e, :], el)
    v1 = jnp.max(el, axis=0, keepdims=True)
    i1 = jnp.min(jnp.where(el == v1, row, big), axis=0, keepdims=True)
    el2 = jnp.where(row == i1, ninf, el)
    v2 = jnp.max(el2, axis=0, keepdims=True)
    i2 = jnp.min(jnp.where(el2 == v2, row, big), axis=0, keepdims=True)
    e21 = jnp.exp(v2 - v1)
    w1 = 1.0 / (1.0 + e21)
    w2 = e21 * w1
    gates = g_val * (jnp.where(row == i1, w1, 0.0) + jnp.where(row == i2, w2, 0.0))
    tail = jnp.where(row == 0, g_idx.astype(F32), 0.0)
    return xh, jnp.concatenate([gates, tail], axis=0)


def _mix0_tail_kernel(x_ref, hm_ref, hd_ref, wo_ref, nf_ref, wrt_ref, br_ref,
                      xmid_ref, xn_ref, route_ref):
    half = hm_ref.shape[1]
    mix = _dot(hm_ref[...], wo_ref[0:half, :]) + _dot(hd_ref[...], wo_ref[half:2 * half, :])
    xmid = x_ref[...] + mix
    xmid_ref[...] = xmid
    xn_ref[...], route_ref[...] = _route(_rms(xmid, nf_ref[...]), wrt_ref, br_ref)


def _mix0_tail(x, hm, hd, wo, nf, wrt, br):
    t, d = x.shape
    tm = ROW_TILE
    row = lambda n: pl.BlockSpec((tm, n), lambda i: (i, 0))
    return pl.pallas_call(
        _mix0_tail_kernel,
        grid=(t // tm,),
        in_specs=[row(d), row(hm.shape[1]), row(hd.shape[1]), _full(wo.shape), _full(nf.shape),
                  _full(wrt.shape), _full(br.shape)],
        out_specs=[row(d), row(d), pl.BlockSpec((ROUTE_ROWS, tm), lambda i: (0, i))],
        out_shape=[jax.ShapeDtypeStruct((t, d), F32), jax.ShapeDtypeStruct((t, d), BF16),
                   jax.ShapeDtypeStruct((ROUTE_ROWS, t), F32)],
        compiler_params=_params("parallel"),
        name="mix0_tail",
    )(x, hm, hd, wo, nf, wrt, br)


def _proj1_kernel(x_ref, y_ref, g_ref, w_ref, x1_ref, b_ref, u_ref):
    d = x_ref.shape[1]
    x1 = x_ref[...] + y_ref[...].astype(F32)
    x1_ref[...] = x1
    xb = _rms(x1, g_ref[...]).astype(BF16)
    b_ref[...] = _dot(xb, w_ref[:, 0:d]).astype(BF16)
    u_ref[...] = (_dot(xb, w_ref[:, d:2 * d]) * _dot(xb, w_ref[:, 2 * d:3 * d])).astype(BF16)


def _proj1(x, y, g, w):
    t, d = x.shape
    tm = ROW_TILE
    row = lambda n: pl.BlockSpec((tm, n), lambda i: (i, 0))
    return pl.pallas_call(
        _proj1_kernel,
        grid=(t // tm,),
        in_specs=[row(d), row(d), _full(g.shape), _full(w.shape)],
        out_specs=[row(d), row(d), row(d)],
        out_shape=[jax.ShapeDtypeStruct((t, d), F32), jax.ShapeDtypeStruct((t, d), BF16),
                   jax.ShapeDtypeStruct((t, d), BF16)],
        compiler_params=_params("parallel"),
        name="proj1",
    )(x, y, g, w)


def _mix1_tail_kernel(x_ref, b_ref, u_ref, halo_ref, shift_ref, cw_ref, wo_ref, nf_ref, wrt_ref, br_ref,
                      xmid_ref, xn_ref, route_ref, *, tiles_per_seq):
    seq_start = pl.program_id(0) % tiles_per_seq == 0
    halo = halo_ref[...].astype(F32)
    tail = jnp.where(seq_start, jnp.zeros_like(halo), halo)
    conv = _shift_conv(u_ref[...], shift_ref, cw_ref[...], tail, CONV_WIDTH)
    mixed = (b_ref[...].astype(F32) * conv).astype(BF16)
    xmid = x_ref[...] + _dot(mixed, wo_ref[...])
    xmid_ref[...] = xmid
    xn_ref[...], route_ref[...] = _route(_rms(xmid, nf_ref[...]), wrt_ref, br_ref)


def _mix1_tail(x, b, u, cw, wo, nf, wrt, br, seq):
    t, d = x.shape
    tm = ROW_TILE
    row = lambda n: pl.BlockSpec((tm, n), lambda i: (i, 0))
    halo = pl.BlockSpec((HALO, d), lambda i: (jnp.maximum(i * (tm // HALO) - 1, 0), 0))
    shift = _shift_matrix(tm, CONV_WIDTH)
    return pl.pallas_call(
        functools.partial(_mix1_tail_kernel, tiles_per_seq=seq // tm),
        grid=(t // tm,),
        in_specs=[row(d), row(d), row(d), halo, _full(shift.shape), _full(cw.shape), _full(wo.shape),
                  _full(nf.shape), _full(wrt.shape), _full(br.shape)],
        out_specs=[row(d), row(d), pl.BlockSpec((ROUTE_ROWS, tm), lambda i: (0, i))],
        out_shape=[jax.ShapeDtypeStruct((t, d), F32), jax.ShapeDtypeStruct((t, d), BF16),
                   jax.ShapeDtypeStruct((ROUTE_ROWS, t), F32)],
        compiler_params=_params("parallel"),
        name="mix1_tail",
    )(x, b, u, u, shift, cw, wo, nf, wrt, br)


def _moe_kernel(off_ref, nsub_ref, x_ref, r_ref, wg_ref, wu_ref, wd_ref, y_ref,
                xs_ref, gs_ref, ys_ref, dest_ref):
    i = pl.program_id(0)
    g = pl.program_id(1)
    tb = MOE_BLOCK
    rows = MOE_ROWS
    ch = MOE_CHUNK
    d = x_ref.shape[1]

    @pl.when(g == 0)
    def _():
        r = r_ref[...]
        grp = r[GROUP_LANE:GROUP_LANE + 1, :]
        row8 = lax.broadcasted_iota(jnp.int32, (SUBLANES, tb), 0)
        onehot = jnp.where(row8.astype(F32) == grp, 1.0, 0.0)
        row1 = lax.broadcasted_iota(jnp.int32, (SUBLANES, 1), 0)
        base = jnp.zeros((SUBLANES, 1), F32)
        for gg in range(N_GROUPS):
            base = jnp.where(row1 == gg, off_ref[i * N_GROUPS + gg].astype(F32), base)
        rr = lax.broadcasted_iota(jnp.int32, (ch, ch), 0)
        cc = lax.broadcasted_iota(jnp.int32, (ch, ch), 1)
        upper = jnp.where(rr <= cc, 1.0, 0.0).astype(BF16)
        parts = []
        for c in range(tb // ch):
            sel = onehot[:, c * ch:(c + 1) * ch]
            rank = _dot(sel.astype(BF16), upper) + base
            parts.append(jnp.sum(sel * rank, axis=0, keepdims=True) - 1.0)
            base = base + jnp.sum(sel, axis=1, keepdims=True)
        dest_row = jnp.concatenate(parts, axis=1)
        side = jnp.concatenate([r[0:SUBLANES, :], jnp.broadcast_to(dest_row, (SUBLANES, tb)),
                                jnp.zeros((LANES - 2 * SUBLANES, tb), F32)], axis=0).T
        dest_ref[...] = side
        lane = lax.broadcasted_iota(jnp.int32, (tb, LANES), 1)
        gates = jnp.where(lane < EXPERTS_PER_GROUP, side, 0.0)
        g_hi = gates.astype(BF16)
        g_lo = (gates - g_hi.astype(F32)).astype(BF16)
        xcat = jnp.concatenate([x_ref[...], g_hi, g_lo], axis=1)
        for c in range(rows // ch):
            rr = (lax.broadcasted_iota(jnp.int32, (ch, tb), 0) + c * ch).astype(F32)
            p = jnp.where(rr == dest_row, 1.0, 0.0).astype(BF16)
            moved = _dot(p, xcat)
            xs_ref[c * ch:(c + 1) * ch, :] = moved[:, 0:d].astype(BF16)
            gs_ref[c * ch:(c + 1) * ch, :] = moved[:, d:d + LANES] + moved[:, d + LANES:d + 2 * LANES]
        ys_ref[...] = jnp.zeros_like(ys_ref)

    off = off_ref[i * N_GROUPS + g]
    n_sub = nsub_ref[i * N_GROUPS + g]

    def expert_rows(r0, size):
        xt = xs_ref[pl.ds(r0, size), :]
        gt = gs_ref[pl.ds(r0, size), :]
        acc = jnp.zeros((size, d), F32)
        for e in range(EXPERTS_PER_GROUP):
            hg = _dot(xt, wg_ref[0, 0, e])
            up = _dot(xt, wu_ref[0, 0, e])
            act = hg * _sigmoid(hg) * up * gt[:, e:e + 1]
            acc = acc + _dot(act.astype(BF16), wd_ref[0, 0, e])
        ys_ref[pl.ds(r0, size), :] = acc.astype(BF16)

    def pair(j, carry):
        expert_rows(pl.multiple_of(off + j * (2 * MOE_SUB), MOE_SUB), 2 * MOE_SUB)
        return carry

    lax.fori_loop(0, n_sub // 2, pair, 0)

    @pl.when(n_sub % 2 == 1)
    def _():
        expert_rows(pl.multiple_of(off + (n_sub - 1) * MOE_SUB, MOE_SUB), MOE_SUB)

    @pl.when(g == N_GROUPS - 1)
    def _():
        ys = ys_ref[...]
        for c in range(tb // ch):
            cc = lax.broadcasted_iota(jnp.int32, (ch, rows), 1).astype(F32)
            pt = jnp.where(cc == dest_ref[c * ch:(c + 1) * ch, DEST_LANE:DEST_LANE + 1], 1.0, 0.0).astype(BF16)
            y_ref[c * ch:(c + 1) * ch, :] = _dot(pt, ys).astype(y_ref.dtype)


def _hier_moe(xn, route, wg, wu, wd, layer):
    t, d = xn.shape
    tb = MOE_BLOCK
    nb = t // tb
    e, ff = wg.shape[2], wg.shape[4]
    group = route[GROUP_LANE, :].astype(jnp.int32).reshape(nb, tb)
    counts = jnp.sum((group[:, :, None] == jnp.arange(N_GROUPS)[None, None, :]).astype(jnp.int32), axis=1)
    nsub = (counts + MOE_SUB - 1) // MOE_SUB
    padded = nsub * MOE_SUB
    off = jnp.cumsum(padded, axis=1) - padded
    grid_spec = pltpu.PrefetchScalarGridSpec(
        num_scalar_prefetch=2,
        grid=(nb, N_GROUPS),
        in_specs=[
            pl.BlockSpec((tb, d), lambda i, g, *_: (i, 0)),
            pl.BlockSpec((ROUTE_ROWS, tb), lambda i, g, *_: (0, i)),
            pl.BlockSpec((1, 1, e, d, ff), lambda i, g, *_: (layer, g, 0, 0, 0)),
            pl.BlockSpec((1, 1, e, d, ff), lambda i, g, *_: (layer, g, 0, 0, 0)),
            pl.BlockSpec((1, 1, e, ff, d), lambda i, g, *_: (layer, g, 0, 0, 0)),
        ],
        out_specs=pl.BlockSpec((tb, d), lambda i, g, *_: (i, 0)),
        scratch_shapes=[
            pltpu.VMEM((MOE_ROWS, d), BF16),
            pltpu.VMEM((MOE_ROWS, LANES), F32),
            pltpu.VMEM((MOE_ROWS, d), BF16),
            pltpu.VMEM((tb, LANES), F32),
        ],
    )
    return pl.pallas_call(
        _moe_kernel,
        grid_spec=grid_spec,
        out_shape=jax.ShapeDtypeStruct((t, d), BF16),
        compiler_params=_params("parallel", "arbitrary"),
        name="moe",
    )(off.reshape(-1).astype(jnp.int32), nsub.reshape(-1).astype(jnp.int32), xn, route, wg, wu, wd)


def _final_kernel(x_ref, y_ref, g_ref, o_ref):
    o_ref[...] = _rms(x_ref[...] + y_ref[...].astype(F32), g_ref[...])


def _final(x, y, g):
    t, d = x.shape
    tm = ROW_TILE
    row = pl.BlockSpec((tm, d), lambda i: (i, 0))
    return pl.pallas_call(
        _final_kernel,
        grid=(t // tm,),
        in_specs=[row, row, _full(g.shape)],
        out_specs=row,
        out_shape=jax.ShapeDtypeStruct((t, d), F32),
        compiler_params=_params("parallel"),
        name="final_norm",
    )(x, y, g)


def _hi_lo(w):
    hi = w.astype(BF16)
    return hi, (w - hi.astype(F32)).astype(BF16)


def _router_params(w_rg, b_rg, w_re, b_re):
    d = w_rg.shape[0]
    n_e = N_GROUPS * EXPERTS_PER_GROUP
    pad = jnp.zeros((d, EXPERT_LANE0 - N_GROUPS), F32)
    tail = jnp.zeros((d, LANES - EXPERT_LANE0 - n_e), F32)
    w = jnp.concatenate([w_rg, pad, w_re, tail], axis=1)
    b = jnp.concatenate([b_rg, pad[0], b_re, tail[0]]).reshape(LANES, 1)
    hi, lo = _hi_lo(w)
    return jnp.concatenate([hi, lo], axis=1).T, b


def kernel(x, positions, norm_mix, norm_ffn, ab_w_in, ab_gate_bias, ab_conv_w, ab_mlstm_norm, ab_lambda,
           ab_diff_norm, ab_w_out, c_w_in, c_conv_w, c_w_out, moe_w_rg, moe_b_rg, moe_w_re, moe_b_re,
           moe_w_gate, moe_w_up, moe_w_down, final_norm):
    batch, seq, d = x.shape
    t = batch * seq
    xf = x.reshape(t, d)
    row2 = lambda v: v.reshape(1, -1).astype(F32)

    w_in = ab_w_in[0]
    mw = 2 * MLSTM_HEADS * MLSTM_HEAD_DIM
    hw = MLSTM_HEADS * MLSTM_HEAD_DIM
    ng = 2 * MLSTM_HEADS
    dw = DIFF_HEADS * DIFF_V_DIM
    o_g = mw + 2 * hw
    o_q = o_g + ng
    o_k, o_dv = o_q + dw, o_q + 2 * dw
    w0 = jnp.concatenate([w_in[:, :mw], w_in[:, mw + hw:o_g], w_in[:, o_q:o_dv]], axis=1).astype(BF16)
    w_gates = jnp.concatenate([w_in[:, o_g:o_q], jnp.zeros((d, LANES - ng), F32)], axis=1)
    wgh, wgl = _hi_lo(w_gates)
    wvt = jnp.concatenate([w_in[:, mw:mw + hw].astype(BF16), w_in[:, o_dv:o_dv + dw].astype(BF16), wgh, wgl],
                          axis=1).T
    gb = jnp.concatenate([ab_gate_bias[0].astype(F32), jnp.zeros((LANES - ng,), F32)]).reshape(LANES, 1)

    cos, sin = _rope_table(positions.reshape(t).astype(jnp.int32))
    mqk, mo, dq, dk, mvt, dvt, gates = _proj0(xf, cos, sin, row2(norm_mix[0]), w0, wvt, gb)
    hm = _mlstm(mqk, mvt, mo, gates, ab_conv_w[0].astype(F32), row2(ab_mlstm_norm[0]), batch, seq)
    lam_init = 0.8 - 0.6 * math.exp(-0.3 * 0)
    hd = _diff_attn(dq, dk, dvt, ab_lambda[0].astype(F32), row2(ab_diff_norm[0]), batch, seq, lam_init)

    wrt, br = _router_params(moe_w_rg[0], moe_b_rg[0], moe_w_re[0], moe_b_re[0])
    xmid, xn, route = _mix0_tail(xf, hm, hd, ab_w_out[0].astype(BF16), row2(norm_ffn[0]), wrt, br)
    wg, wu, wd = moe_w_gate.astype(BF16), moe_w_up.astype(BF16), moe_w_down.astype(BF16)
    y = _hier_moe(xn, route, wg, wu, wd, 0)

    x1, bgate, u = _proj1(xmid, y, row2(norm_mix[1]), c_w_in[0].astype(BF16))
    wrt, br = _router_params(moe_w_rg[1], moe_b_rg[1], moe_w_re[1], moe_b_re[1])
    xmid, xn, route = _mix1_tail(x1, bgate, u, c_conv_w[0].astype(F32), c_w_out[0].astype(BF16),
                                 row2(norm_ffn[1]), wrt, br, seq)
    y = _hier_moe(xn, route, wg, wu, wd, 1)

    return _final(xmid, y, row2(final_norm)).reshape(batch, seq, d)
```

```python
import functools
import math

import jax
import jax.numpy as jnp
from jax import lax
from jax.experimental import pallas as pl
from jax.experimental.pallas import tpu as pltpu

F32 = jnp.float32
BF16 = jnp.bfloat16

EPS = 1e-6
LOG2E = math.log2(math.e)
ROPE_THETA = 10000.0
LANES = 128
SUBLANES = 8
MLSTM_HEADS = 4
MLSTM_HEAD_DIM = 128
MLSTM_CONV = 4
DIFF_HEADS = 4
DIFF_HEAD_DIM = 64
DIFF_V_DIM = 128
CONV_WIDTH = 3
N_GROUPS = 4
EXPERTS_PER_GROUP = 8
EXPERT_FF = 256
HALO = 8

ROW_TILE = 1024
CONV_ROW_TILE = 512
MLSTM_CHUNK = 256
ATTN_BLOCK = 512
MOE_BLOCK = 1024
MOE_SUB = 128
MOE_ROWS = MOE_BLOCK + N_GROUPS * MOE_SUB
MOE_CHUNK = 256
VMEM_LIMIT = 56 * 1024 * 1024


def _dot(a, b):
    return jnp.dot(a, b, preferred_element_type=F32)


def _dot_nt(a, b):
    return lax.dot_general(a, b, (((1,), (1,)), ((), ())), preferred_element_type=F32)


def _dot_tn(a, b):
    return lax.dot_general(a, b, (((0,), (0,)), ((), ())), preferred_element_type=F32)


def _split3(x):
    hi = x.astype(BF16)
    r1 = x - hi.astype(F32)
    mid = r1.astype(BF16)
    lo = (r1 - mid.astype(F32)).astype(BF16)
    return hi, mid, lo


def _dot_f32ish(x, w_hi, w_lo):
    xh = x.astype(BF16)
    xl = (x - xh.astype(F32)).astype(BF16)
    return _dot(xh, w_hi) + (_dot(xl, w_hi) + _dot(xh, w_lo))


def _rms(x, g):
    return x * lax.rsqrt(jnp.mean(x * x, axis=-1, keepdims=True) + EPS) * g


def _sigmoid(x):
    return 0.5 * jnp.tanh(0.5 * x) + 0.5


def _shift_matrix(rows, taps):
    r = jnp.arange(rows)
    return jnp.concatenate([r[:, None] - s == r[None, :] for s in range(1, taps)], axis=0).astype(BF16)


def _shift_conv(xb, shift_ref, cw, tail, taps):
    rows, c = xb.shape
    x32 = xb.astype(F32)
    shifted = _dot(shift_ref[...], xb)
    last = taps - 1
    conv = cw[last:last + 1, :] * x32
    for s in range(1, taps):
        conv = conv + cw[last - s:last - s + 1, :] * shifted[(s - 1) * rows:s * rows, :]
    head = jnp.concatenate([tail, x32[0:HALO, :]], axis=0)
    rowi = lax.broadcasted_iota(jnp.int32, (HALO, c), 0)
    fix = jnp.zeros((HALO, c), F32)
    for s in range(1, taps):
        fix = fix + cw[last - s:last - s + 1, :] * jnp.where(rowi < s, head[HALO - s:2 * HALO - s, :], 0.0)
    return jnp.concatenate([conv[0:HALO, :] + fix, conv[HALO:, :]], axis=0)


def _params(*sem):
    return pltpu.CompilerParams(dimension_semantics=sem, vmem_limit_bytes=VMEM_LIMIT)


def _full(shape):
    return pl.BlockSpec(shape, lambda *_: (0,) * len(shape))


def _proj0_kernel(x_ref, cos_ref, sin_ref, g_ref, w_ref, wvt_ref, gb_ref,
                  mqk_ref, mo_ref, dq_ref, dk_ref, mvt_ref, dvt_ref, gates_ref):
    xn = _rms(x_ref[...], g_ref[...])
    xb = xn.astype(BF16)
    xl = (xn - xb.astype(F32)).astype(BF16)

    def seg(lo, hi):
        return _dot(xb, w_ref[:, lo:hi])

    mqk_ref[...] = seg(0, 1024).astype(BF16)
    mo_ref[...] = seg(1024, 1536).astype(BF16)
    vt = _dot_nt(wvt_ref[...], xb)
    mvt_ref[...] = vt[0:512, :].astype(BF16)
    dvt_ref[...] = vt[512:1024, :].astype(BF16)
    gates_ref[...] = (vt[1024:1024 + LANES, :] + (vt[1024 + LANES:1024 + 2 * LANES, :]
                      + _dot_nt(wvt_ref[1024:1024 + LANES, :], xl))) + gb_ref[...]

    half = DIFF_HEAD_DIM // 2
    heads = 2 * DIFF_HEADS
    c = cos_ref[...]
    s = sin_ref[...]
    c4 = jnp.concatenate([c, c] * heads, axis=1)
    s4 = jnp.concatenate([-s, s] * heads, axis=1)
    lane4 = lax.broadcasted_iota(jnp.int32, c4.shape, 1)
    first4 = (lane4 & (DIFF_HEAD_DIM - 1)) < half

    def rope(v):
        n = v.shape[1]
        partner = jnp.where(first4, pltpu.roll(v, n - half, 1), pltpu.roll(v, half, 1))
        return v * c4 + partner * s4

    q_scale = LOG2E * DIFF_HEAD_DIM ** -0.5
    dq = (rope(seg(1536, 2048)) * q_scale).astype(BF16)
    dk = rope(seg(2048, 2560)).astype(BF16)
    for h in range(DIFF_HEADS):
        dq_ref[h] = dq[:, h * DIFF_V_DIM:(h + 1) * DIFF_V_DIM]
        dk_ref[h] = dk[:, h * DIFF_V_DIM:(h + 1) * DIFF_V_DIM]


def _rope_table_kernel(pos_ref, invf_ref, cos_ref, sin_ref):
    ang = pos_ref[...].astype(F32) * invf_ref[...]
    cos_ref[...] = jnp.cos(ang)
    sin_ref[...] = jnp.sin(ang)


def _rope_table(positions):
    t = positions.shape[0]
    half = DIFF_HEAD_DIM // 2
    per_row = LANES // half
    rows = t // per_row
    pos_rep = jnp.repeat(positions.reshape(rows, per_row), half, axis=1)
    inv_freq = 1.0 / (ROPE_THETA ** (jnp.arange(half, dtype=F32) / half))
    invf = jnp.tile(inv_freq, per_row).reshape(1, LANES)
    tr = min(rows, 1024)
    spec = pl.BlockSpec((tr, LANES), lambda i: (i, 0))
    cos, sin = pl.pallas_call(
        _rope_table_kernel,
        grid=(rows // tr,),
        in_specs=[spec, _full(invf.shape)],
        out_specs=[spec, spec],
        out_shape=[jax.ShapeDtypeStruct((rows, LANES), F32)] * 2,
        compiler_params=_params("parallel"),
        name="rope_table",
    )(pos_rep, invf)
    return cos.reshape(t, half), sin.reshape(t, half)


def _proj0(x, cos, sin, g, w, wvt, gb):
    t, d = x.shape
    tm = ROW_TILE
    row = lambda n: pl.BlockSpec((tm, n), lambda i: (i, 0))
    col = lambda n: pl.BlockSpec((n, tm), lambda i: (0, i))
    heads = pl.BlockSpec((DIFF_HEADS, tm, DIFF_V_DIM), lambda i: (0, i, 0))
    head_major = jax.ShapeDtypeStruct((DIFF_HEADS, t, DIFF_V_DIM), BF16)
    out_shapes = [
        jax.ShapeDtypeStruct((t, 1024), BF16), jax.ShapeDtypeStruct((t, 512), BF16), head_major, head_major,
        jax.ShapeDtypeStruct((512, t), BF16), jax.ShapeDtypeStruct((512, t), BF16),
        jax.ShapeDtypeStruct((LANES, t), F32),
    ]
    return pl.pallas_call(
        _proj0_kernel,
        grid=(t // tm,),
        in_specs=[row(d), row(cos.shape[1]), row(sin.shape[1]), _full(g.shape), _full(w.shape),
                  _full(wvt.shape), _full(gb.shape)],
        out_specs=[row(1024), row(512), heads, heads, col(512), col(512), col(LANES)],
        out_shape=out_shapes,
        compiler_params=_params("parallel"),
        name="proj0",
    )(x, cos, sin, g, w, wvt, gb)


def _mlstm_kernel(mqk_ref, mvt_ref, mo_ref, gates_ref, shift_ref, cw_ref, norm_ref, hm_ref,
                  tail_ref, cn_ref, m_ref):
    L = MLSTM_CHUNK
    hd = MLSTM_HEAD_DIM
    width = MLSTM_HEADS * hd

    @pl.when(pl.program_id(1) == 0)
    def _():
        tail_ref[...] = jnp.zeros_like(tail_ref)
        cn_ref[...] = jnp.zeros_like(cn_ref)
        m_ref[...] = jnp.zeros_like(m_ref)

    conv = _shift_conv(mqk_ref[...], shift_ref, cw_ref[...], tail_ref[...], MLSTM_CONV)
    tail_ref[...] = mqk_ref[L - HALO:L, :].astype(F32)
    qk = conv * _sigmoid(conv)
    qb = (qk[:, :width] * (hd ** -0.5)).astype(BF16)
    k = qk[:, width:]
    kb = k.astype(BF16)

    g8 = gates_ref[0:SUBLANES, :]
    logf = (jnp.minimum(g8, 0.0) - jnp.log(1.0 + jnp.exp(-jnp.abs(g8)))) * LOG2E
    krow = lax.broadcasted_iota(jnp.int32, (L, L), 0)
    qcol = lax.broadcasted_iota(jnp.int32, (L, L), 1)
    causal_t = krow <= qcol
    triu = jnp.where(causal_t, 1.0, 0.0).astype(BF16)
    f_hi, f_mid, f_lo = _split3(logf)
    b8 = _dot(f_hi, triu) + (_dot(f_mid, triu) + _dot(f_lo, triu))
    cj8 = g8 * LOG2E - pltpu.roll(b8, MLSTM_HEADS, 0)
    cj_all = jnp.concatenate([cj8, jnp.zeros((LANES - SUBLANES, L), F32)], axis=0).T
    ones = jnp.ones((SUBLANES, L), BF16)

    for h in range(MLSTM_HEADS):
        sl = slice(h * hd, (h + 1) * hd)
        b_row = b8[MLSTM_HEADS + h:MLSTM_HEADS + h + 1, :]
        m_prev = m_ref[h:h + 1, 0:1]
        cn_prev = cn_ref[h]
        vt1 = jnp.concatenate([mvt_ref[sl, :], ones], axis=0)

        cmat = jnp.where(causal_t, cj_all[:, h:h + 1], -jnp.inf)
        mm = jnp.maximum(m_prev, jnp.max(cmat, axis=0, keepdims=True))
        w_t = jnp.exp2(cmat - mm)
        s_t = _dot_nt(kb[:, sl], qb[:, sl]) * w_t
        inter_w = jnp.exp2(m_prev - mm)
        tot = _dot(vt1, s_t.astype(BF16)) + inter_w * _dot_nt(cn_prev.astype(BF16), qb[:, sl])
        den = tot[hd:hd + 1, :]
        h_t = tot[0:hd, :] / jnp.maximum(jnp.abs(den), jnp.exp2(-(b_row + mm)))

        kw = k[:, sl] * w_t[:, L - 1:L]
        cn_ref[h] = inter_w[:, L - 1:L] * cn_prev + _dot(vt1, kw.astype(BF16))
        m_ref[h:h + 1, :] = jnp.broadcast_to(b_row[:, L - 1:L] + mm[:, L - 1:L], (1, LANES))

        scale = lax.rsqrt(jnp.mean(h_t * h_t, axis=0, keepdims=True) + EPS)
        hn = (h_t * scale).T * norm_ref[...]
        hm_ref[:, sl] = (hn * _sigmoid(mo_ref[:, sl].astype(F32))).astype(BF16)


def _mlstm(mqk, mvt, mo, gates, cw, norm, batch, seq):
    t = mqk.shape[0]
    L = MLSTM_CHUNK
    nc = seq // L
    width = MLSTM_HEADS * MLSTM_HEAD_DIM
    row = lambda n: pl.BlockSpec((L, n), lambda b, c: (b * nc + c, 0))
    col = lambda n: pl.BlockSpec((n, L), lambda b, c: (0, b * nc + c))
    shift = _shift_matrix(L, MLSTM_CONV)
    return pl.pallas_call(
        _mlstm_kernel,
        grid=(batch, nc),
        in_specs=[row(2 * width), col(width), row(width), col(LANES), _full(shift.shape), _full(cw.shape),
                  _full(norm.shape)],
        out_specs=row(width),
        out_shape=jax.ShapeDtypeStruct((t, width), BF16),
        scratch_shapes=[
            pltpu.VMEM((HALO, 2 * width), F32),
            pltpu.VMEM((MLSTM_HEADS, MLSTM_HEAD_DIM + SUBLANES, MLSTM_HEAD_DIM), F32),
            pltpu.VMEM((SUBLANES, LANES), F32),
        ],
        compiler_params=_params("parallel", "arbitrary"),
        name="mlstm",
    )(mqk, mvt, mo, gates, shift, cw, norm)


def _diff_attn_kernel(q_ref, k_ref, vt_ref, lam_ref, norm_ref, o_ref,
                      acc0_ref, acc1_ref, m_ref, sa0_ref, sa1_ref, sb0_ref, sb1_ref, *, lam_init):
    tq = ATTN_BLOCK
    dv = DIFF_V_DIM
    qi = pl.program_id(2)
    q = q_ref[0]
    lane = lax.broadcasted_iota(jnp.int32, q.shape, 1)
    zero = jnp.zeros_like(q)
    qs = (jnp.where(lane < DIFF_HEAD_DIM, q, zero), jnp.where(lane >= DIFF_HEAD_DIM, q, zero))
    krow = lax.broadcasted_iota(jnp.int32, (tq, tq), 0)
    qcol = lax.broadcasted_iota(jnp.int32, (tq, tq), 1)
    causal_t = krow <= qcol
    ones = jnp.ones((SUBLANES, tq), BF16)
    accs = (acc0_ref, acc1_ref)
    buf_a = (sa0_ref, sa1_ref)
    buf_b = (sb0_ref, sb1_ref)

    for acc in accs:
        acc[...] = jnp.zeros_like(acc)
    m_ref[...] = jnp.full(m_ref.shape, -jnp.inf, F32)

    def scores(j, dst):
        kb = k_ref[0, pl.ds(pl.multiple_of(j * tq, tq), tq), :]
        for c in range(2):
            dst[c][...] = _dot_nt(kb, qs[c])

    def absorb(j, src, masked):
        start = pl.multiple_of(j * tq, tq)
        vt1 = jnp.concatenate([vt_ref[:, pl.ds(start, tq)], ones], axis=0)
        s = [src[c][...] for c in range(2)]
        if masked:
            s = [jnp.where(causal_t, sc, -jnp.inf) for sc in s]
        m_old = [m_ref[c:c + 1, :] for c in range(2)]
        m_new = [jnp.maximum(m_old[c], jnp.max(s[c], axis=0, keepdims=True)) for c in range(2)]
        p = [jnp.exp2(s[c] - m_new[c]).astype(BF16) for c in range(2)]
        alpha = [jnp.exp2(m_old[c] - m_new[c]) for c in range(2)]
        pv = [_dot(vt1, p[c]) for c in range(2)]
        for c in range(2):
            accs[c][...] = alpha[c] * accs[c][...] + pv[c]
            m_ref[c:c + 1, :] = m_new[c]

    scores(0, buf_a)

    def pair(jj, carry):
        j = 2 * jj
        scores(j + 1, buf_b)
        absorb(j, buf_a, False)
        scores(j + 2, buf_a)
        absorb(j + 1, buf_b, False)
        return carry

    lax.fori_loop(0, qi // 2, pair, 0)

    @pl.when(qi % 2 == 0)
    def _():
        absorb(qi, buf_a, True)

    @pl.when(qi % 2 == 1)
    def _():
        scores(qi, buf_b)
        absorb(qi - 1, buf_a, False)
        absorb(qi, buf_b, True)

    lv = lam_ref[...]
    lam = (jnp.exp(jnp.sum(lv[0:1, :] * lv[1:2, :], axis=-1, keepdims=True))
           - jnp.exp(jnp.sum(lv[2:3, :] * lv[3:4, :], axis=-1, keepdims=True)) + lam_init)
    a1 = acc0_ref[...]
    a2 = acc1_ref[...]
    o_t = a1[0:dv, :] / a1[dv:dv + 1, :] - lam * (a2[0:dv, :] / a2[dv:dv + 1, :])
    o_ref[0] = (_rms(o_t.T, norm_ref[...]) * (1.0 - lam_init)).astype(BF16)


def _diff_attn(dq, dk, dvt, lam_vecs, norm, batch, seq, lam_init):
    t = dq.shape[1]
    tq = ATTN_BLOCK
    nq = seq // tq
    return pl.pallas_call(
        functools.partial(_diff_attn_kernel, lam_init=lam_init),
        grid=(batch, DIFF_HEADS, nq),
        in_specs=[
            pl.BlockSpec((1, tq, DIFF_V_DIM), lambda b, h, i: (h, b * nq + i, 0)),
            pl.BlockSpec((1, seq, DIFF_V_DIM), lambda b, h, i: (h, b, 0)),
            pl.BlockSpec((DIFF_V_DIM, seq), lambda b, h, i: (h, b)),
            _full(lam_vecs.shape), _full(norm.shape),
        ],
        out_specs=pl.BlockSpec((1, tq, DIFF_V_DIM), lambda b, h, i: (h, b * nq + i, 0)),
        out_shape=jax.ShapeDtypeStruct((DIFF_HEADS, t, DIFF_V_DIM), BF16),
        scratch_shapes=([pltpu.VMEM((DIFF_V_DIM + SUBLANES, tq), F32)] * 2 + [pltpu.VMEM((SUBLANES, tq), F32)]
                        + [pltpu.VMEM((tq, tq), F32)] * 4),
        compiler_params=_params("parallel", "parallel", "arbitrary"),
        name="diff_attn",
    )(dq, dk, dvt, lam_vecs, norm)


ROUTE_ROWS = 2 * SUBLANES
GROUP_LANE = EXPERTS_PER_GROUP
DEST_LANE = EXPERTS_PER_GROUP
EXPERT_LANE0 = 8


def _route(xn, wrt_ref, br_ref):
    xh = xn.astype(BF16)
    xl = (xn - xh.astype(F32)).astype(BF16)
    both = _dot_nt(wrt_ref[...], xh)
    logits = both[0:LANES, :] + (both[LANES:2 * LANES, :] + _dot_nt(wrt_ref[0:LANES, :], xl)) + br_ref[...]
    e = EXPERTS_PER_GROUP
    tm = logits.shape[1]
    row = lax.broadcasted_iota(jnp.int32, (e, tm), 0)
    ninf = -jnp.inf
    big = jnp.int32(LANES)

    is_g = row < N_GROUPS
    gl = jnp.where(is_g, logits[0:e, :], ninf)
    gmax = jnp.max(gl, axis=0, keepdims=True)
    gsum = jnp.sum(jnp.where(is_g, jnp.exp(gl - gmax), 0.0), axis=0, keepdims=True)
    g_val = 1.0 / gsum
    g_idx = jnp.min(jnp.where(gl == gmax, row, big), axis=0, keepdims=True)

    el = logits[EXPERT_LANE0:EXPERT_LANE0 + e, :]
    for g in range(1, N_GROUPS):
        lo = EXPERT_LANE0 + e * g
        el = jnp.where(g_idx == g, logits[lo:lo + e, :], el)
    v1 = jnp.max(el, axis=0, keepdims=True)
    i1 = jnp.min(jnp.where(el == v1, row, big), axis=0, keepdims=True)
    el2 = jnp.where(row == i1, ninf, el)
    v2 = jnp.max(el2, axis=0, keepdims=True)
    i2 = jnp.min(jnp.where(el2 == v2, row, big), axis=0, keepdims=True)
    e21 = jnp.exp(v2 - v1)
    w1 = 1.0 / (1.0 + e21)
    w2 = e21 * w1
    gates = g_val * (jnp.where(row == i1, w1, 0.0) + jnp.where(row == i2, w2, 0.0))
    tail = jnp.where(row == 0, g_idx.astype(F32), 0.0)
    return xh, jnp.concatenate([gates, tail], axis=0)


def _mix0_tail_kernel(x_ref, hm_ref, hd_ref, wo_ref, nf_ref, wrt_ref, br_ref,
                      xmid_ref, xn_ref, route_ref):
    half = hm_ref.shape[1]
    hd = jnp.concatenate([hd_ref[h] for h in range(DIFF_HEADS)], axis=1)
    mix = _dot(hm_ref[...], wo_ref[0:half, :]) + _dot(hd, wo_ref[half:2 * half, :])
    xmid = x_ref[...] + mix
    xmid_ref[...] = xmid
    xn_ref[...], route_ref[...] = _route(_rms(xmid, nf_ref[...]), wrt_ref, br_ref)


def _mix0_tail(x, hm, hd, wo, nf, wrt, br):
    t, d = x.shape
    tm = ROW_TILE
    row = lambda n: pl.BlockSpec((tm, n), lambda i: (i, 0))
    return pl.pallas_call(
        _mix0_tail_kernel,
        grid=(t // tm,),
        in_specs=[row(d), row(hm.shape[1]), pl.BlockSpec((DIFF_HEADS, tm, DIFF_V_DIM), lambda i: (0, i, 0)),
                  _full(wo.shape), _full(nf.shape), _full(wrt.shape), _full(br.shape)],
        out_specs=[row(d), row(d), pl.BlockSpec((ROUTE_ROWS, tm), lambda i: (0, i))],
        out_shape=[jax.ShapeDtypeStruct((t, d), F32), jax.ShapeDtypeStruct((t, d), BF16),
                   jax.ShapeDtypeStruct((ROUTE_ROWS, t), F32)],
        compiler_params=_params("parallel"),
        name="mix0_tail",
    )(x, hm, hd, wo, nf, wrt, br)


def _proj1_kernel(x_ref, y_ref, g_ref, w_ref, x1_ref, b_ref, u_ref):
    d = x_ref.shape[1]
    x1 = x_ref[...] + y_ref[...].astype(F32)
    x1_ref[...] = x1
    xb = _rms(x1, g_ref[...]).astype(BF16)
    b_ref[...] = _dot(xb, w_ref[:, 0:d]).astype(BF16)
    u_ref[...] = (_dot(xb, w_ref[:, d:2 * d]) * _dot(xb, w_ref[:, 2 * d:3 * d])).astype(BF16)


def _proj1(x, y, g, w):
    t, d = x.shape
    tm = ROW_TILE
    row = lambda n: pl.BlockSpec((tm, n), lambda i: (i, 0))
    return pl.pallas_call(
        _proj1_kernel,
        grid=(t // tm,),
        in_specs=[row(d), row(d), _full(g.shape), _full(w.shape)],
        out_specs=[row(d), row(d), row(d)],
        out_shape=[jax.ShapeDtypeStruct((t, d), F32), jax.ShapeDtypeStruct((t, d), BF16),
                   jax.ShapeDtypeStruct((t, d), BF16)],
        compiler_params=_params("parallel"),
        name="proj1",
    )(x, y, g, w)


def _mix1_tail_kernel(x_ref, b_ref, u_ref, halo_ref, shift_ref, cw_ref, wo_ref, nf_ref, wrt_ref, br_ref,
                      xmid_ref, xn_ref, route_ref, *, tiles_per_seq):
    seq_start = pl.program_id(0) % tiles_per_seq == 0
    halo = halo_ref[...].astype(F32)
    tail = jnp.where(seq_start, jnp.zeros_like(halo), halo)
    conv = _shift_conv(u_ref[...], shift_ref, cw_ref[...], tail, CONV_WIDTH)
    mixed = (b_ref[...].astype(F32) * conv).astype(BF16)
    xmid = x_ref[...] + _dot(mixed, wo_ref[...])
    xmid_ref[...] = xmid
    xn_ref[...], route_ref[...] = _route(_rms(xmid, nf_ref[...]), wrt_ref, br_ref)


def _mix1_tail(x, b, u, cw, wo, nf, wrt, br, seq):
    t, d = x.shape
    tm = CONV_ROW_TILE
    row = lambda n: pl.BlockSpec((tm, n), lambda i: (i, 0))
    halo = pl.BlockSpec((HALO, d), lambda i: (jnp.maximum(i * (tm // HALO) - 1, 0), 0))
    shift = _shift_matrix(tm, CONV_WIDTH)
    return pl.pallas_call(
        functools.partial(_mix1_tail_kernel, tiles_per_seq=seq // tm),
        grid=(t // tm,),
        in_specs=[row(d), row(d), row(d), halo, _full(shift.shape), _full(cw.shape), _full(wo.shape),
                  _full(nf.shape), _full(wrt.shape), _full(br.shape)],
        out_specs=[row(d), row(d), pl.BlockSpec((ROUTE_ROWS, tm), lambda i: (0, i))],
        out_shape=[jax.ShapeDtypeStruct((t, d), F32), jax.ShapeDtypeStruct((t, d), BF16),
                   jax.ShapeDtypeStruct((ROUTE_ROWS, t), F32)],
        compiler_params=_params("parallel"),
        name="mix1_tail",
    )(x, b, u, u, shift, cw, wo, nf, wrt, br)


def _moe_kernel(off_ref, nsub_ref, x_ref, r_ref, wg_ref, wu_ref, wd_ref, y_ref,
                xs_ref, gs_ref, ys_ref, dest_ref):
    i = pl.program_id(0)
    g = pl.program_id(1)
    tb = MOE_BLOCK
    rows = MOE_ROWS
    ch = MOE_CHUNK
    d = x_ref.shape[1]

    @pl.when(g == 0)
    def _():
        r = r_ref[...]
        grp = r[GROUP_LANE:GROUP_LANE + 1, :]
        row8 = lax.broadcasted_iota(jnp.int32, (SUBLANES, tb), 0)
        onehot = jnp.where(row8.astype(F32) == grp, 1.0, 0.0)
        row1 = lax.broadcasted_iota(jnp.int32, (SUBLANES, 1), 0)
        base = jnp.zeros((SUBLANES, 1), F32)
        for gg in range(N_GROUPS):
            base = jnp.where(row1 == gg, off_ref[i * N_GROUPS + gg].astype(F32), base)
        rr = lax.broadcasted_iota(jnp.int32, (ch, ch), 0)
        cc = lax.broadcasted_iota(jnp.int32, (ch, ch), 1)
        upper = jnp.where(rr <= cc, 1.0, 0.0).astype(BF16)
        parts = []
        for c in range(tb // ch):
            sel = onehot[:, c * ch:(c + 1) * ch]
            rank = _dot(sel.astype(BF16), upper) + base
            parts.append(jnp.sum(sel * rank, axis=0, keepdims=True) - 1.0)
            base = base + jnp.sum(sel, axis=1, keepdims=True)
        dest_row = jnp.concatenate(parts, axis=1)
        side = jnp.concatenate([r[0:SUBLANES, :], jnp.broadcast_to(dest_row, (SUBLANES, tb)),
                                jnp.zeros((LANES - 2 * SUBLANES, tb), F32)], axis=0).T
        dest_ref[...] = side
        lane = lax.broadcasted_iota(jnp.int32, (tb, LANES), 1)
        gates = jnp.where(lane < EXPERTS_PER_GROUP, side, 0.0)
        g_hi = gates.astype(BF16)
        g_lo = (gates - g_hi.astype(F32)).astype(BF16)
        xcat = jnp.concatenate([x_ref[...], g_hi, g_lo], axis=1)
        for c in range(rows // ch):
            rr = (lax.broadcasted_iota(jnp.int32, (ch, tb), 0) + c * ch).astype(F32)
            p = jnp.where(rr == dest_row, 1.0, 0.0).astype(BF16)
            moved = _dot(p, xcat)
            xs_ref[c * ch:(c + 1) * ch, :] = moved[:, 0:d].astype(BF16)
            gs_ref[c * ch:(c + 1) * ch, :] = moved[:, d:d + LANES] + moved[:, d + LANES:d + 2 * LANES]
        ys_ref[...] = jnp.zeros_like(ys_ref)

    off = off_ref[i * N_GROUPS + g]
    n_sub = nsub_ref[i * N_GROUPS + g]

    def expert_rows(r0, size):
        xt = xs_ref[pl.ds(r0, size), :]
        gt = gs_ref[pl.ds(r0, size), :]
        acc = jnp.zeros((size, d), F32)
        for e in range(EXPERTS_PER_GROUP):
            hg = _dot(xt, wg_ref[0, 0, e])
            up = _dot(xt, wu_ref[0, 0, e])
            act = hg * _sigmoid(hg) * up * gt[:, e:e + 1]
            acc = acc + _dot(act.astype(BF16), wd_ref[0, 0, e])
        ys_ref[pl.ds(r0, size), :] = acc.astype(BF16)

    def pair(j, carry):
        expert_rows(pl.multiple_of(off + j * (2 * MOE_SUB), MOE_SUB), 2 * MOE_SUB)
        return carry

    lax.fori_loop(0, n_sub // 2, pair, 0)

    @pl.when(n_sub % 2 == 1)
    def _():
        expert_rows(pl.multiple_of(off + (n_sub - 1) * MOE_SUB, MOE_SUB), MOE_SUB)

    @pl.when(g == N_GROUPS - 1)
    def _():
        ys = ys_ref[...]
        for c in range(tb // ch):
            cc = lax.broadcasted_iota(jnp.int32, (ch, rows), 1).astype(F32)
            pt = jnp.where(cc == dest_ref[c * ch:(c + 1) * ch, DEST_LANE:DEST_LANE + 1], 1.0, 0.0).astype(BF16)
            y_ref[c * ch:(c + 1) * ch, :] = _dot(pt, ys).astype(y_ref.dtype)


def _hier_moe(xn, route, wg, wu, wd, layer):
    t, d = xn.shape
    tb = MOE_BLOCK
    nb = t // tb
    e, ff = wg.shape[2], wg.shape[4]
    group = route[GROUP_LANE, :].astype(jnp.int32).reshape(nb, tb)
    counts = jnp.sum((group[:, :, None] == jnp.arange(N_GROUPS)[None, None, :]).astype(jnp.int32), axis=1)
    nsub = (counts + MOE_SUB - 1) // MOE_SUB
    padded = nsub * MOE_SUB
    off = jnp.cumsum(padded, axis=1) - padded
    grid_spec = pltpu.PrefetchScalarGridSpec(
        num_scalar_prefetch=2,
        grid=(nb, N_GROUPS),
        in_specs=[
            pl.BlockSpec((tb, d), lambda i, g, *_: (i, 0)),
            pl.BlockSpec((ROUTE_ROWS, tb), lambda i, g, *_: (0, i)),
            pl.BlockSpec((1, 1, e, d, ff), lambda i, g, *_: (layer, g, 0, 0, 0)),
            pl.BlockSpec((1, 1, e, d, ff), lambda i, g, *_: (layer, g, 0, 0, 0)),
            pl.BlockSpec((1, 1, e, ff, d), lambda i, g, *_: (layer, g, 0, 0, 0)),
        ],
        out_specs=pl.BlockSpec((tb, d), lambda i, g, *_: (i, 0)),
        scratch_shapes=[
            pltpu.VMEM((MOE_ROWS, d), BF16),
            pltpu.VMEM((MOE_ROWS, LANES), F32),
            pltpu.VMEM((MOE_ROWS, d), BF16),
            pltpu.VMEM((tb, LANES), F32),
        ],
    )
    return pl.pallas_call(
        _moe_kernel,
        grid_spec=grid_spec,
        out_shape=jax.ShapeDtypeStruct((t, d), BF16),
        compiler_params=_params("parallel", "arbitrary"),
        name="moe",
    )(off.reshape(-1).astype(jnp.int32), nsub.reshape(-1).astype(jnp.int32), xn, route, wg, wu, wd)


def _final_kernel(x_ref, y_ref, g_ref, o_ref):
    o_ref[...] = _rms(x_ref[...] + y_ref[...].astype(F32), g_ref[...])


def _final(x, y, g):
    t, d = x.shape
    tm = ROW_TILE
    row = pl.BlockSpec((tm, d), lambda i: (i, 0))
    return pl.pallas_call(
        _final_kernel,
        grid=(t // tm,),
        in_specs=[row, row, _full(g.shape)],
        out_specs=row,
        out_shape=jax.ShapeDtypeStruct((t, d), F32),
        compiler_params=_params("parallel"),
        name="final_norm",
    )(x, y, g)


def _hi_lo(w):
    hi = w.astype(BF16)
    return hi, (w - hi.astype(F32)).astype(BF16)


def _router_params(w_rg, b_rg, w_re, b_re):
    d = w_rg.shape[0]
    n_e = N_GROUPS * EXPERTS_PER_GROUP
    pad = jnp.zeros((d, EXPERT_LANE0 - N_GROUPS), F32)
    tail = jnp.zeros((d, LANES - EXPERT_LANE0 - n_e), F32)
    w = jnp.concatenate([w_rg, pad, w_re, tail], axis=1)
    b = jnp.concatenate([b_rg, pad[0], b_re, tail[0]]).reshape(LANES, 1)
    hi, lo = _hi_lo(w)
    return jnp.concatenate([hi, lo], axis=1).T, b


def kernel(x, positions, norm_mix, norm_ffn, ab_w_in, ab_gate_bias, ab_conv_w, ab_mlstm_norm, ab_lambda,
           ab_diff_norm, ab_w_out, c_w_in, c_conv_w, c_w_out, moe_w_rg, moe_b_rg, moe_w_re, moe_b_re,
           moe_w_gate, moe_w_up, moe_w_down, final_norm):
    batch, seq, d = x.shape
    t = batch * seq
    xf = x.reshape(t, d)
    row2 = lambda v: v.reshape(1, -1).astype(F32)

    w_in = ab_w_in[0]
    mw = 2 * MLSTM_HEADS * MLSTM_HEAD_DIM
    hw = MLSTM_HEADS * MLSTM_HEAD_DIM
    ng = 2 * MLSTM_HEADS
    dw = DIFF_HEADS * DIFF_V_DIM
    o_g = mw + 2 * hw
    o_q = o_g + ng
    o_k, o_dv = o_q + dw, o_q + 2 * dw
    w0 = jnp.concatenate([w_in[:, :mw], w_in[:, mw + hw:o_g], w_in[:, o_q:o_dv]], axis=1).astype(BF16)
    w_gates = jnp.concatenate([w_in[:, o_g:o_q], jnp.zeros((d, LANES - ng), F32)], axis=1)
    wgh, wgl = _hi_lo(w_gates)
    wvt = jnp.concatenate([w_in[:, mw:mw + hw].astype(BF16), w_in[:, o_dv:o_dv + dw].astype(BF16), wgh, wgl],
                          axis=1).T
    gb = jnp.concatenate([ab_gate_bias[0].astype(F32), jnp.zeros((LANES - ng,), F32)]).reshape(LANES, 1)

    cos, sin = _rope_table(positions.reshape(t).astype(jnp.int32))
    mqk, mo, dq, dk, mvt, dvt, gates = _proj0(xf, cos, sin, row2(norm_mix[0]), w0, wvt, gb)
    hm = _mlstm(mqk, mvt, mo, gates, ab_conv_w[0].astype(F32), row2(ab_mlstm_norm[0]), batch, seq)
    lam_init = 0.8 - 0.6 * math.exp(-0.3 * 0)
    hd = _diff_attn(dq, dk, dvt, ab_lambda[0].astype(F32), row2(ab_diff_norm[0]), batch, seq, lam_init)

    wrt, br = _router_params(moe_w_rg[0], moe_b_rg[0], moe_w_re[0], moe_b_re[0])
    xmid, xn, route = _mix0_tail(xf, hm, hd, ab_w_out[0].astype(BF16), row2(norm_ffn[0]), wrt, br)
    wg, wu, wd = moe_w_gate.astype(BF16), moe_w_up.astype(BF16), moe_w_down.astype(BF16)
    y = _hier_moe(xn, route, wg, wu, wd, 0)

    x1, bgate, u = _proj1(xmid, y, row2(norm_mix[1]), c_w_in[0].astype(BF16))
    wrt, br = _router_params(moe_w_rg[1], moe_b_rg[1], moe_w_re[1], moe_b_re[1])
    xmid, xn, route = _mix1_tail(x1, bgate, u, c_conv_w[0].astype(F32), c_w_out[0].astype(BF16),
                                 row2(norm_ffn[1]), wrt, br, seq)
    y = _hier_moe(xn, route, wg, wu, wd, 1)

    return _final(xmid, y, row2(final_norm)).reshape(batch, seq, d)
```

```python
import functools
import math

import jax
import jax.numpy as jnp
from jax import lax
from jax.experimental import pallas as pl
from jax.experimental.pallas import tpu as pltpu

F32 = jnp.float32
BF16 = jnp.bfloat16

EPS = 1e-6
LOG2E = math.log2(math.e)
ROPE_THETA = 10000.0
LANES = 128
SUBLANES = 8
MLSTM_HEADS = 4
MLSTM_HEAD_DIM = 128
MLSTM_CONV = 4
DIFF_HEADS = 4
DIFF_HEAD_DIM = 64
DIFF_V_DIM = 128
CONV_WIDTH = 3
N_GROUPS = 4
EXPERTS_PER_GROUP = 8
EXPERT_FF = 256
HALO = 8

ROW_TILE = 1024
CONV_ROW_TILE = 512
MLSTM_CHUNK = 256
ATTN_BLOCK = 512
MOE_BLOCK = 1024
MOE_SUB = 128
MOE_ROWS = MOE_BLOCK + N_GROUPS * MOE_SUB
MOE_CHUNK = 256
VMEM_LIMIT = 56 * 1024 * 1024


def _dot(a, b):
    return jnp.dot(a, b, preferred_element_type=F32)


def _dot_nt(a, b):
    return lax.dot_general(a, b, (((1,), (1,)), ((), ())), preferred_element_type=F32)


def _dot_tn(a, b):
    return lax.dot_general(a, b, (((0,), (0,)), ((), ())), preferred_element_type=F32)


def _split3(x):
    hi = x.astype(BF16)
    r1 = x - hi.astype(F32)
    mid = r1.astype(BF16)
    lo = (r1 - mid.astype(F32)).astype(BF16)
    return hi, mid, lo


def _dot_f32ish(x, w_hi, w_lo):
    xh = x.astype(BF16)
    xl = (x - xh.astype(F32)).astype(BF16)
    return _dot(xh, w_hi) + (_dot(xl, w_hi) + _dot(xh, w_lo))


def _rms(x, g):
    return x * lax.rsqrt(jnp.mean(x * x, axis=-1, keepdims=True) + EPS) * g


def _sigmoid(x):
    return 0.5 * jnp.tanh(0.5 * x) + 0.5


def _shift_matrix(rows, taps):
    r = jnp.arange(rows)
    return jnp.concatenate([r[:, None] - s == r[None, :] for s in range(1, taps)], axis=0).astype(BF16)


def _shift_conv(xb, shift_ref, cw, tail, taps):
    rows, c = xb.shape
    x32 = xb.astype(F32)
    shifted = _dot(shift_ref[...], xb)
    last = taps - 1
    conv = cw[last:last + 1, :] * x32
    for s in range(1, taps):
        conv = conv + cw[last - s:last - s + 1, :] * shifted[(s - 1) * rows:s * rows, :]
    head = jnp.concatenate([tail, x32[0:HALO, :]], axis=0)
    rowi = lax.broadcasted_iota(jnp.int32, (HALO, c), 0)
    fix = jnp.zeros((HALO, c), F32)
    for s in range(1, taps):
        fix = fix + cw[last - s:last - s + 1, :] * jnp.where(rowi < s, head[HALO - s:2 * HALO - s, :], 0.0)
    return jnp.concatenate([conv[0:HALO, :] + fix, conv[HALO:, :]], axis=0)


def _params(*sem):
    return pltpu.CompilerParams(dimension_semantics=sem, vmem_limit_bytes=VMEM_LIMIT)


def _full(shape):
    return pl.BlockSpec(shape, lambda *_: (0,) * len(shape))


def _proj0_kernel(x_ref, cos_ref, sin_ref, g_ref, w_ref, wvt_ref, gb_ref,
                  mqk_ref, mo_ref, dq_ref, dk_ref, mvt_ref, dvt_ref, gates_ref):
    xn = _rms(x_ref[...], g_ref[...])
    xb = xn.astype(BF16)
    xl = (xn - xb.astype(F32)).astype(BF16)

    def seg(lo, hi):
        return _dot(xb, w_ref[:, lo:hi])

    mqk_ref[...] = seg(0, 1024).astype(BF16)
    mo_ref[...] = seg(1024, 1536).astype(BF16)
    vt = _dot_nt(wvt_ref[...], xb)
    mvt_ref[...] = vt[0:512, :].astype(BF16)
    dvt_ref[...] = vt[512:1024, :].astype(BF16)
    gates_ref[...] = (vt[1024:1024 + LANES, :] + (vt[1024 + LANES:1024 + 2 * LANES, :]
                      + _dot_nt(wvt_ref[1024:1024 + LANES, :], xl))) + gb_ref[...]

    half = DIFF_HEAD_DIM // 2
    heads = 2 * DIFF_HEADS
    c = cos_ref[...]
    s = sin_ref[...]
    c4 = jnp.concatenate([c, c] * heads, axis=1)
    s4 = jnp.concatenate([-s, s] * heads, axis=1)
    lane4 = lax.broadcasted_iota(jnp.int32, c4.shape, 1)
    first4 = (lane4 & (DIFF_HEAD_DIM - 1)) < half

    def rope(v):
        n = v.shape[1]
        partner = jnp.where(first4, pltpu.roll(v, n - half, 1), pltpu.roll(v, half, 1))
        return v * c4 + partner * s4

    q_scale = LOG2E * DIFF_HEAD_DIM ** -0.5
    dq = (rope(seg(1536, 2048)) * q_scale).astype(BF16)
    dk = rope(seg(2048, 2560)).astype(BF16)
    for h in range(DIFF_HEADS):
        dq_ref[h] = dq[:, h * DIFF_V_DIM:(h + 1) * DIFF_V_DIM]
        dk_ref[h] = dk[:, h * DIFF_V_DIM:(h + 1) * DIFF_V_DIM]


def _rope_table_kernel(pos_ref, invf_ref, cos_ref, sin_ref):
    ang = pos_ref[...].astype(F32) * invf_ref[...]
    cos_ref[...] = jnp.cos(ang)
    sin_ref[...] = jnp.sin(ang)


def _rope_table(positions):
    t = positions.shape[0]
    half = DIFF_HEAD_DIM // 2
    per_row = LANES // half
    rows = t // per_row
    pos_rep = jnp.repeat(positions.reshape(rows, per_row), half, axis=1)
    inv_freq = 1.0 / (ROPE_THETA ** (jnp.arange(half, dtype=F32) / half))
    invf = jnp.tile(inv_freq, per_row).reshape(1, LANES)
    tr = min(rows, 1024)
    spec = pl.BlockSpec((tr, LANES), lambda i: (i, 0))
    cos, sin = pl.pallas_call(
        _rope_table_kernel,
        grid=(rows // tr,),
        in_specs=[spec, _full(invf.shape)],
        out_specs=[spec, spec],
        out_shape=[jax.ShapeDtypeStruct((rows, LANES), F32)] * 2,
        compiler_params=_params("parallel"),
        name="rope_table",
    )(pos_rep, invf)
    return cos.reshape(t, half), sin.reshape(t, half)


def _proj0(x, cos, sin, g, w, wvt, gb):
    t, d = x.shape
    tm = ROW_TILE
    row = lambda n: pl.BlockSpec((tm, n), lambda i: (i, 0))
    col = lambda n: pl.BlockSpec((n, tm), lambda i: (0, i))
    heads = pl.BlockSpec((DIFF_HEADS, tm, DIFF_V_DIM), lambda i: (0, i, 0))
    head_major = jax.ShapeDtypeStruct((DIFF_HEADS, t, DIFF_V_DIM), BF16)
    out_shapes = [
        jax.ShapeDtypeStruct((t, 1024), BF16), jax.ShapeDtypeStruct((t, 512), BF16), head_major, head_major,
        jax.ShapeDtypeStruct((512, t), BF16), jax.ShapeDtypeStruct((512, t), BF16),
        jax.ShapeDtypeStruct((LANES, t), F32),
    ]
    return pl.pallas_call(
        _proj0_kernel,
        grid=(t // tm,),
        in_specs=[row(d), row(cos.shape[1]), row(sin.shape[1]), _full(g.shape), _full(w.shape),
                  _full(wvt.shape), _full(gb.shape)],
        out_specs=[row(1024), row(512), heads, heads, col(512), col(512), col(LANES)],
        out_shape=out_shapes,
        compiler_params=_params("parallel"),
        name="proj0",
    )(x, cos, sin, g, w, wvt, gb)


def _mlstm_kernel(mqk_ref, mvt_ref, mo_ref, gates_ref, shift_ref, cw_ref, norm_ref, hm_ref,
                  tail_ref, cn_ref, m_ref):
    L = MLSTM_CHUNK
    hd = MLSTM_HEAD_DIM
    width = MLSTM_HEADS * hd

    @pl.when(pl.program_id(1) == 0)
    def _():
        tail_ref[...] = jnp.zeros_like(tail_ref)
        cn_ref[...] = jnp.zeros_like(cn_ref)
        m_ref[...] = jnp.zeros_like(m_ref)

    conv = _shift_conv(mqk_ref[...], shift_ref, cw_ref[...], tail_ref[...], MLSTM_CONV)
    tail_ref[...] = mqk_ref[L - HALO:L, :].astype(F32)
    qk = conv * _sigmoid(conv)
    qb = (qk[:, :width] * (hd ** -0.5)).astype(BF16)
    k = qk[:, width:]
    kb = k.astype(BF16)

    g8 = gates_ref[0:SUBLANES, :]
    logf = (jnp.minimum(g8, 0.0) - jnp.log(1.0 + jnp.exp(-jnp.abs(g8)))) * LOG2E
    krow = lax.broadcasted_iota(jnp.int32, (L, L), 0)
    qcol = lax.broadcasted_iota(jnp.int32, (L, L), 1)
    causal_t = krow <= qcol
    triu = jnp.where(causal_t, 1.0, 0.0).astype(BF16)
    f_hi, f_mid, f_lo = _split3(logf)
    b8 = _dot(f_hi, triu) + (_dot(f_mid, triu) + _dot(f_lo, triu))
    cj8 = g8 * LOG2E - pltpu.roll(b8, MLSTM_HEADS, 0)
    cj_all = jnp.concatenate([cj8, jnp.zeros((LANES - SUBLANES, L), F32)], axis=0).T
    ones = jnp.ones((SUBLANES, L), BF16)

    for h in range(MLSTM_HEADS):
        sl = slice(h * hd, (h + 1) * hd)
        b_row = b8[MLSTM_HEADS + h:MLSTM_HEADS + h + 1, :]
        m_prev = m_ref[h:h + 1, 0:1]
        cn_prev = cn_ref[h]
        vt1 = jnp.concatenate([mvt_ref[sl, :], ones], axis=0)

        cmat = jnp.where(causal_t, cj_all[:, h:h + 1], -jnp.inf)
        mm = jnp.maximum(m_prev, jnp.max(cmat, axis=0, keepdims=True))
        w_t = jnp.exp2(cmat - mm)
        s_t = _dot_nt(kb[:, sl], qb[:, sl]) * w_t
        inter_w = jnp.exp2(m_prev - mm)
        tot = _dot(vt1, s_t.astype(BF16)) + inter_w * _dot_nt(cn_prev.astype(BF16), qb[:, sl])
        den = tot[hd:hd + 1, :]
        h_t = tot[0:hd, :] / jnp.maximum(jnp.abs(den), jnp.exp2(-(b_row + mm)))

        kw = k[:, sl] * w_t[:, L - 1:L]
        cn_ref[h] = inter_w[:, L - 1:L] * cn_prev + _dot(vt1, kw.astype(BF16))
        m_ref[h:h + 1, :] = jnp.broadcast_to(b_row[:, L - 1:L] + mm[:, L - 1:L], (1, LANES))

        scale = lax.rsqrt(jnp.mean(h_t * h_t, axis=0, keepdims=True) + EPS)
        hn = (h_t * scale).T * norm_ref[...]
        hm_ref[:, sl] = (hn * _sigmoid(mo_ref[:, sl].astype(F32))).astype(BF16)


def _mlstm(mqk, mvt, mo, gates, cw, norm, batch, seq):
    t = mqk.shape[0]
    L = MLSTM_CHUNK
    nc = seq // L
    width = MLSTM_HEADS * MLSTM_HEAD_DIM
    row = lambda n: pl.BlockSpec((L, n), lambda b, c: (b * nc + c, 0))
    col = lambda n: pl.BlockSpec((n, L), lambda b, c: (0, b * nc + c))
    shift = _shift_matrix(L, MLSTM_CONV)
    return pl.pallas_call(
        _mlstm_kernel,
        grid=(batch, nc),
        in_specs=[row(2 * width), col(width), row(width), col(LANES), _full(shift.shape), _full(cw.shape),
                  _full(norm.shape)],
        out_specs=row(width),
        out_shape=jax.ShapeDtypeStruct((t, width), BF16),
        scratch_shapes=[
            pltpu.VMEM((HALO, 2 * width), F32),
            pltpu.VMEM((MLSTM_HEADS, MLSTM_HEAD_DIM + SUBLANES, MLSTM_HEAD_DIM), F32),
            pltpu.VMEM((SUBLANES, LANES), F32),
        ],
        compiler_params=_params("parallel", "arbitrary"),
        name="mlstm",
    )(mqk, mvt, mo, gates, shift, cw, norm)


def _diff_attn_kernel(q_ref, k_ref, vt_ref, lam_ref, norm_ref, o_ref,
                      acc0_ref, acc1_ref, m_ref, sa0_ref, sa1_ref, sb0_ref, sb1_ref, *, lam_init):
    tq = ATTN_BLOCK
    dv = DIFF_V_DIM
    nq = q_ref.shape[1] // tq
    lane = lax.broadcasted_iota(jnp.int32, (tq, dv), 1)
    krow = lax.broadcasted_iota(jnp.int32, (tq, tq), 0)
    qcol = lax.broadcasted_iota(jnp.int32, (tq, tq), 1)
    causal_t = krow <= qcol
    ones = jnp.ones((SUBLANES, tq), BF16)
    accs = (acc0_ref, acc1_ref)
    buf_a = (sa0_ref, sa1_ref)
    buf_b = (sb0_ref, sb1_ref)
    lv = lam_ref[...]
    lam = (jnp.exp(jnp.sum(lv[0:1, :] * lv[1:2, :], axis=-1, keepdims=True))
           - jnp.exp(jnp.sum(lv[2:3, :] * lv[3:4, :], axis=-1, keepdims=True)) + lam_init)

    def q_block(qi, carry):
        qrow = pl.multiple_of(qi * tq, tq)
        q = q_ref[0, pl.ds(qrow, tq), :]
        zero = jnp.zeros_like(q)
        qs = (jnp.where(lane < DIFF_HEAD_DIM, q, zero), jnp.where(lane >= DIFF_HEAD_DIM, q, zero))
        for acc in accs:
            acc[...] = jnp.zeros_like(acc)
        m_ref[...] = jnp.full(m_ref.shape, -jnp.inf, F32)

        def scores(j, dst):
            kb = k_ref[0, pl.ds(pl.multiple_of(j * tq, tq), tq), :]
            for c in range(2):
                dst[c][...] = _dot_nt(kb, qs[c])

        def absorb(j, src, masked):
            start = pl.multiple_of(j * tq, tq)
            vt1 = jnp.concatenate([vt_ref[:, pl.ds(start, tq)], ones], axis=0)
            s = [src[c][...] for c in range(2)]
            if masked:
                s = [jnp.where(causal_t, sc, -jnp.inf) for sc in s]
            m_old = [m_ref[c:c + 1, :] for c in range(2)]
            m_new = [jnp.maximum(m_old[c], jnp.max(s[c], axis=0, keepdims=True)) for c in range(2)]
            p = [jnp.exp2(s[c] - m_new[c]).astype(BF16) for c in range(2)]
            alpha = [jnp.exp2(m_old[c] - m_new[c]) for c in range(2)]
            pv = [_dot(vt1, p[c]) for c in range(2)]
            for c in range(2):
                accs[c][...] = alpha[c] * accs[c][...] + pv[c]
                m_ref[c:c + 1, :] = m_new[c]

        scores(0, buf_a)

        def pair(jj, inner):
            j = 2 * jj
            scores(j + 1, buf_b)
            absorb(j, buf_a, False)
            scores(j + 2, buf_a)
            absorb(j + 1, buf_b, False)
            return inner

        lax.fori_loop(0, qi // 2, pair, 0)

        @pl.when(qi % 2 == 0)
        def _():
            absorb(qi, buf_a, True)

        @pl.when(qi % 2 == 1)
        def _():
            scores(qi, buf_b)
            absorb(qi - 1, buf_a, False)
            absorb(qi, buf_b, True)

        a1 = acc0_ref[...]
        a2 = acc1_ref[...]
        o_t = a1[0:dv, :] / a1[dv:dv + 1, :] - lam * (a2[0:dv, :] / a2[dv:dv + 1, :])
        o_ref[0, pl.ds(qrow, tq), :] = (_rms(o_t.T, norm_ref[...]) * (1.0 - lam_init)).astype(BF16)
        return carry

    lax.fori_loop(0, nq, q_block, 0)


def _diff_attn(dq, dk, dvt, lam_vecs, norm, batch, seq, lam_init):
    t = dq.shape[1]
    tq = ATTN_BLOCK
    rows = pl.BlockSpec((1, seq, DIFF_V_DIM), lambda b, h: (h, b, 0))
    return pl.pallas_call(
        functools.partial(_diff_attn_kernel, lam_init=lam_init),
        grid=(batch, DIFF_HEADS),
        in_specs=[rows, rows, pl.BlockSpec((DIFF_V_DIM, seq), lambda b, h: (h, b)),
                  _full(lam_vecs.shape), _full(norm.shape)],
        out_specs=rows,
        out_shape=jax.ShapeDtypeStruct((DIFF_HEADS, t, DIFF_V_DIM), BF16),
        scratch_shapes=([pltpu.VMEM((DIFF_V_DIM + SUBLANES, tq), F32)] * 2 + [pltpu.VMEM((SUBLANES, tq), F32)]
                        + [pltpu.VMEM((tq, tq), F32)] * 4),
        compiler_params=_params("parallel", "parallel"),
        name="diff_attn",
    )(dq, dk, dvt, lam_vecs, norm)


ROUTE_ROWS = 2 * SUBLANES
GROUP_LANE = EXPERTS_PER_GROUP
DEST_LANE = EXPERTS_PER_GROUP
EXPERT_LANE0 = 8


def _route(xn, wrt_ref, br_ref):
    xh = xn.astype(BF16)
    xl = (xn - xh.astype(F32)).astype(BF16)
    both = _dot_nt(wrt_ref[...], xh)
    logits = both[0:LANES, :] + (both[LANES:2 * LANES, :] + _dot_nt(wrt_ref[0:LANES, :], xl)) + br_ref[...]
    e = EXPERTS_PER_GROUP
    tm = logits.shape[1]
    row = lax.broadcasted_iota(jnp.int32, (e, tm), 0)
    ninf = -jnp.inf
    big = jnp.int32(LANES)

    is_g = row < N_GROUPS
    gl = jnp.where(is_g, logits[0:e, :], ninf)
    gmax = jnp.max(gl, axis=0, keepdims=True)
    gsum = jnp.sum(jnp.where(is_g, jnp.exp(gl - gmax), 0.0), axis=0, keepdims=True)
    g_val = 1.0 / gsum
    g_idx = jnp.min(jnp.where(gl == gmax, row, big), axis=0, keepdims=True)

    el = logits[EXPERT_LANE0:EXPERT_LANE0 + e, :]
    for g in range(1, N_GROUPS):
        lo = EXPERT_LANE0 + e * g
        el = jnp.where(g_idx == g, logits[lo:lo + e, :], el)
    v1 = jnp.max(el, axis=0, keepdims=True)
    i1 = jnp.min(jnp.where(el == v1, row, big), axis=0, keepdims=True)
    el2 = jnp.where(row == i1, ninf, el)
    v2 = jnp.max(el2, axis=0, keepdims=True)
    i2 = jnp.min(jnp.where(el2 == v2, row, big), axis=0, keepdims=True)
    e21 = jnp.exp(v2 - v1)
    w1 = 1.0 / (1.0 + e21)
    w2 = e21 * w1
    gates = g_val * (jnp.where(row == i1, w1, 0.0) + jnp.where(row == i2, w2, 0.0))
    tail = jnp.where(row == 0, g_idx.astype(F32), 0.0)
    return xh, jnp.concatenate([gates, tail], axis=0)


def _mix0_tail_kernel(x_ref, hm_ref, hd_ref, wo_ref, nf_ref, wrt_ref, br_ref,
                      xmid_ref, xn_ref, route_ref):
    half = hm_ref.shape[1]
    hd = jnp.concatenate([hd_ref[h] for h in range(DIFF_HEADS)], axis=1)
    mix = _dot(hm_ref[...], wo_ref[0:half, :]) + _dot(hd, wo_ref[half:2 * half, :])
    xmid = x_ref[...] + mix
    xmid_ref[...] = xmid
    xn_ref[...], route_ref[...] = _route(_rms(xmid, nf_ref[...]), wrt_ref, br_ref)


def _mix0_tail(x, hm, hd, wo, nf, wrt, br):
    t, d = x.shape
    tm = ROW_TILE
    row = lambda n: pl.BlockSpec((tm, n), lambda i: (i, 0))
    return pl.pallas_call(
        _mix0_tail_kernel,
        grid=(t // tm,),
        in_specs=[row(d), row(hm.shape[1]), pl.BlockSpec((DIFF_HEADS, tm, DIFF_V_DIM), lambda i: (0, i, 0)),
                  _full(wo.shape), _full(nf.shape), _full(wrt.shape), _full(br.shape)],
        out_specs=[row(d), row(d), pl.BlockSpec((ROUTE_ROWS, tm), lambda i: (0, i))],
        out_shape=[jax.ShapeDtypeStruct((t, d), F32), jax.ShapeDtypeStruct((t, d), BF16),
                   jax.ShapeDtypeStruct((ROUTE_ROWS, t), F32)],
        compiler_params=_params("parallel"),
        name="mix0_tail",
    )(x, hm, hd, wo, nf, wrt, br)


def _proj1_kernel(x_ref, y_ref, g_ref, w_ref, x1_ref, b_ref, u_ref):
    d = x_ref.shape[1]
    x1 = x_ref[...] + y_ref[...].astype(F32)
    x1_ref[...] = x1
    xb = _rms(x1, g_ref[...]).astype(BF16)
    b_ref[...] = _dot(xb, w_ref[:, 0:d]).astype(BF16)
    u_ref[...] = (_dot(xb, w_ref[:, d:2 * d]) * _dot(xb, w_ref[:, 2 * d:3 * d])).astype(BF16)


def _proj1(x, y, g, w):
    t, d = x.shape
    tm = ROW_TILE
    row = lambda n: pl.BlockSpec((tm, n), lambda i: (i, 0))
    return pl.pallas_call(
        _proj1_kernel,
        grid=(t // tm,),
        in_specs=[row(d), row(d), _full(g.shape), _full(w.shape)],
        out_specs=[row(d), row(d), row(d)],
        out_shape=[jax.ShapeDtypeStruct((t, d), F32), jax.ShapeDtypeStruct((t, d), BF16),
                   jax.ShapeDtypeStruct((t, d), BF16)],
        compiler_params=_params("parallel"),
        name="proj1",
    )(x, y, g, w)


def _mix1_tail_kernel(x_ref, b_ref, u_ref, halo_ref, shift_ref, cw_ref, wo_ref, nf_ref, wrt_ref, br_ref,
                      xmid_ref, xn_ref, route_ref, *, tiles_per_seq):
    seq_start = pl.program_id(0) % tiles_per_seq == 0
    halo = halo_ref[...].astype(F32)
    tail = jnp.where(seq_start, jnp.zeros_like(halo), halo)
    conv = _shift_conv(u_ref[...], shift_ref, cw_ref[...], tail, CONV_WIDTH)
    mixed = (b_ref[...].astype(F32) * conv).astype(BF16)
    xmid = x_ref[...] + _dot(mixed, wo_ref[...])
    xmid_ref[...] = xmid
    xn_ref[...], route_ref[...] = _route(_rms(xmid, nf_ref[...]), wrt_ref, br_ref)


def _mix1_tail(x, b, u, cw, wo, nf, wrt, br, seq):
    t, d = x.shape
    tm = CONV_ROW_TILE
    row = lambda n: pl.BlockSpec((tm, n), lambda i: (i, 0))
    halo = pl.BlockSpec((HALO, d), lambda i: (jnp.maximum(i * (tm // HALO) - 1, 0), 0))
    shift = _shift_matrix(tm, CONV_WIDTH)
    return pl.pallas_call(
        functools.partial(_mix1_tail_kernel, tiles_per_seq=seq // tm),
        grid=(t // tm,),
        in_specs=[row(d), row(d), row(d), halo, _full(shift.shape), _full(cw.shape), _full(wo.shape),
                  _full(nf.shape), _full(wrt.shape), _full(br.shape)],
        out_specs=[row(d), row(d), pl.BlockSpec((ROUTE_ROWS, tm), lambda i: (0, i))],
        out_shape=[jax.ShapeDtypeStruct((t, d), F32), jax.ShapeDtypeStruct((t, d), BF16),
                   jax.ShapeDtypeStruct((ROUTE_ROWS, t), F32)],
        compiler_params=_params("parallel"),
        name="mix1_tail",
    )(x, b, u, u, shift, cw, wo, nf, wrt, br)


def _moe_kernel(off_ref, nsub_ref, x_ref, r_ref, wg_ref, wu_ref, wd_ref, y_ref,
                xs_ref, gs_ref, ys_ref, dest_ref):
    i = pl.program_id(0)
    g = pl.program_id(1)
    tb = MOE_BLOCK
    rows = MOE_ROWS
    ch = MOE_CHUNK
    d = x_ref.shape[1]

    @pl.when(g == 0)
    def _():
        r = r_ref[...]
        grp = r[GROUP_LANE:GROUP_LANE + 1, :]
        row8 = lax.broadcasted_iota(jnp.int32, (SUBLANES, tb), 0)
        onehot = jnp.where(row8.astype(F32) == grp, 1.0, 0.0)
        row1 = lax.broadcasted_iota(jnp.int32, (SUBLANES, 1), 0)
        base = jnp.zeros((SUBLANES, 1), F32)
        for gg in range(N_GROUPS):
            base = jnp.where(row1 == gg, off_ref[i * N_GROUPS + gg].astype(F32), base)
        rr = lax.broadcasted_iota(jnp.int32, (ch, ch), 0)
        cc = lax.broadcasted_iota(jnp.int32, (ch, ch), 1)
        upper = jnp.where(rr <= cc, 1.0, 0.0).astype(BF16)
        parts = []
        for c in range(tb // ch):
            sel = onehot[:, c * ch:(c + 1) * ch]
            rank = _dot(sel.astype(BF16), upper) + base
            parts.append(jnp.sum(sel * rank, axis=0, keepdims=True) - 1.0)
            base = base + jnp.sum(sel, axis=1, keepdims=True)
        dest_row = jnp.concatenate(parts, axis=1)
        side = jnp.concatenate([r[0:SUBLANES, :], jnp.broadcast_to(dest_row, (SUBLANES, tb)),
                                jnp.zeros((LANES - 2 * SUBLANES, tb), F32)], axis=0).T
        dest_ref[...] = side
        lane = lax.broadcasted_iota(jnp.int32, (tb, LANES), 1)
        gates = jnp.where(lane < EXPERTS_PER_GROUP, side, 0.0)
        g_hi = gates.astype(BF16)
        g_lo = (gates - g_hi.astype(F32)).astype(BF16)
        xcat = jnp.concatenate([x_ref[...], g_hi, g_lo], axis=1)
        for c in range(rows // ch):
            rr = (lax.broadcasted_iota(jnp.int32, (ch, tb), 0) + c * ch).astype(F32)
            p = jnp.where(rr == dest_row, 1.0, 0.0).astype(BF16)
            moved = _dot(p, xcat)
            xs_ref[c * ch:(c + 1) * ch, :] = moved[:, 0:d].astype(BF16)
            gs_ref[c * ch:(c + 1) * ch, :] = moved[:, d:d + LANES] + moved[:, d + LANES:d + 2 * LANES]
        ys_ref[...] = jnp.zeros_like(ys_ref)

    off = off_ref[i * N_GROUPS + g]
    n_sub = nsub_ref[i * N_GROUPS + g]

    def expert_rows(r0, size):
        xt = xs_ref[pl.ds(r0, size), :]
        gt = gs_ref[pl.ds(r0, size), :]
        acc = jnp.zeros((size, d), F32)
        for e in range(EXPERTS_PER_GROUP):
            hg = _dot(xt, wg_ref[0, 0, e])
            up = _dot(xt, wu_ref[0, 0, e])
            act = hg * _sigmoid(hg) * up * gt[:, e:e + 1]
            acc = acc + _dot(act.astype(BF16), wd_ref[0, 0, e])
        ys_ref[pl.ds(r0, size), :] = acc.astype(BF16)

    def pair(j, carry):
        expert_rows(pl.multiple_of(off + j * (2 * MOE_SUB), MOE_SUB), 2 * MOE_SUB)
        return carry

    lax.fori_loop(0, n_sub // 2, pair, 0)

    @pl.when(n_sub % 2 == 1)
    def _():
        expert_rows(pl.multiple_of(off + (n_sub - 1) * MOE_SUB, MOE_SUB), MOE_SUB)

    @pl.when(g == N_GROUPS - 1)
    def _():
        ys = ys_ref[...]
        for c in range(tb // ch):
            cc = lax.broadcasted_iota(jnp.int32, (ch, rows), 1).astype(F32)
            pt = jnp.where(cc == dest_ref[c * ch:(c + 1) * ch, DEST_LANE:DEST_LANE + 1], 1.0, 0.0).astype(BF16)
            y_ref[c * ch:(c + 1) * ch, :] = _dot(pt, ys).astype(y_ref.dtype)


def _hier_moe(xn, route, wg, wu, wd, layer):
    t, d = xn.shape
    tb = MOE_BLOCK
    nb = t // tb
    e, ff = wg.shape[2], wg.shape[4]
    group = route[GROUP_LANE, :].astype(jnp.int32).reshape(nb, tb)
    counts = jnp.sum((group[:, :, None] == jnp.arange(N_GROUPS)[None, None, :]).astype(jnp.int32), axis=1)
    nsub = (counts + MOE_SUB - 1) // MOE_SUB
    padded = nsub * MOE_SUB
    off = jnp.cumsum(padded, axis=1) - padded
    grid_spec = pltpu.PrefetchScalarGridSpec(
        num_scalar_prefetch=2,
        grid=(nb, N_GROUPS),
        in_specs=[
            pl.BlockSpec((tb, d), lambda i, g, *_: (i, 0)),
            pl.BlockSpec((ROUTE_ROWS, tb), lambda i, g, *_: (0, i)),
            pl.BlockSpec((1, 1, e, d, ff), lambda i, g, *_: (layer, g, 0, 0, 0)),
            pl.BlockSpec((1, 1, e, d, ff), lambda i, g, *_: (layer, g, 0, 0, 0)),
            pl.BlockSpec((1, 1, e, ff, d), lambda i, g, *_: (layer, g, 0, 0, 0)),
        ],
        out_specs=pl.BlockSpec((tb, d), lambda i, g, *_: (i, 0)),
        scratch_shapes=[
            pltpu.VMEM((MOE_ROWS, d), BF16),
            pltpu.VMEM((MOE_ROWS, LANES), F32),
            pltpu.VMEM((MOE_ROWS, d), BF16),
            pltpu.VMEM((tb, LANES), F32),
        ],
    )
    return pl.pallas_call(
        _moe_kernel,
        grid_spec=grid_spec,
        out_shape=jax.ShapeDtypeStruct((t, d), BF16),
        compiler_params=_params("parallel", "arbitrary"),
        name="moe",
    )(off.reshape(-1).astype(jnp.int32), nsub.reshape(-1).astype(jnp.int32), xn, route, wg, wu, wd)


def _final_kernel(x_ref, y_ref, g_ref, o_ref):
    o_ref[...] = _rms(x_ref[...] + y_ref[...].astype(F32), g_ref[...])


def _final(x, y, g):
    t, d = x.shape
    tm = ROW_TILE
    row = pl.BlockSpec((tm, d), lambda i: (i, 0))
    return pl.pallas_call(
        _final_kernel,
        grid=(t // tm,),
        in_specs=[row, row, _full(g.shape)],
        out_specs=row,
        out_shape=jax.ShapeDtypeStruct((t, d), F32),
        compiler_params=_params("parallel"),
        name="final_norm",
    )(x, y, g)


def _hi_lo(w):
    hi = w.astype(BF16)
    return hi, (w - hi.astype(F32)).astype(BF16)


def _router_params(w_rg, b_rg, w_re, b_re):
    d = w_rg.shape[0]
    n_e = N_GROUPS * EXPERTS_PER_GROUP
    pad = jnp.zeros((d, EXPERT_LANE0 - N_GROUPS), F32)
    tail = jnp.zeros((d, LANES - EXPERT_LANE0 - n_e), F32)
    w = jnp.concatenate([w_rg, pad, w_re, tail], axis=1)
    b = jnp.concatenate([b_rg, pad[0], b_re, tail[0]]).reshape(LANES, 1)
    hi, lo = _hi_lo(w)
    return jnp.concatenate([hi, lo], axis=1).T, b


def kernel(x, positions, norm_mix, norm_ffn, ab_w_in, ab_gate_bias, ab_conv_w, ab_mlstm_norm, ab_lambda,
           ab_diff_norm, ab_w_out, c_w_in, c_conv_w, c_w_out, moe_w_rg, moe_b_rg, moe_w_re, moe_b_re,
           moe_w_gate, moe_w_up, moe_w_down, final_norm):
    batch, seq, d = x.shape
    t = batch * seq
    xf = x.reshape(t, d)
    row2 = lambda v: v.reshape(1, -1).astype(F32)

    w_in = ab_w_in[0]
    mw = 2 * MLSTM_HEADS * MLSTM_HEAD_DIM
    hw = MLSTM_HEADS * MLSTM_HEAD_DIM
    ng = 2 * MLSTM_HEADS
    dw = DIFF_HEADS * DIFF_V_DIM
    o_g = mw + 2 * hw
    o_q = o_g + ng
    o_k, o_dv = o_q + dw, o_q + 2 * dw
    w0 = jnp.concatenate([w_in[:, :mw], w_in[:, mw + hw:o_g], w_in[:, o_q:o_dv]], axis=1).astype(BF16)
    w_gates = jnp.concatenate([w_in[:, o_g:o_q], jnp.zeros((d, LANES - ng), F32)], axis=1)
    wgh, wgl = _hi_lo(w_gates)
    wvt = jnp.concatenate([w_in[:, mw:mw + hw].astype(BF16), w_in[:, o_dv:o_dv + dw].astype(BF16), wgh, wgl],
                          axis=1).T
    gb = jnp.concatenate([ab_gate_bias[0].astype(F32), jnp.zeros((LANES - ng,), F32)]).reshape(LANES, 1)

    cos, sin = _rope_table(positions.reshape(t).astype(jnp.int32))
    mqk, mo, dq, dk, mvt, dvt, gates = _proj0(xf, cos, sin, row2(norm_mix[0]), w0, wvt, gb)
    hm = _mlstm(mqk, mvt, mo, gates, ab_conv_w[0].astype(F32), row2(ab_mlstm_norm[0]), batch, seq)
    lam_init = 0.8 - 0.6 * math.exp(-0.3 * 0)
    hd = _diff_attn(dq, dk, dvt, ab_lambda[0].astype(F32), row2(ab_diff_norm[0]), batch, seq, lam_init)

    wrt, br = _router_params(moe_w_rg[0], moe_b_rg[0], moe_w_re[0], moe_b_re[0])
    xmid, xn, route = _mix0_tail(xf, hm, hd, ab_w_out[0].astype(BF16), row2(norm_ffn[0]), wrt, br)
    wg, wu, wd = moe_w_gate.astype(BF16), moe_w_up.astype(BF16), moe_w_down.astype(BF16)
    y = _hier_moe(xn, route, wg, wu, wd, 0)

    x1, bgate, u = _proj1(xmid, y, row2(norm_mix[1]), c_w_in[0].astype(BF16))
    wrt, br = _router_params(moe_w_rg[1], moe_b_rg[1], moe_w_re[1], moe_b_re[1])
    xmid, xn, route = _mix1_tail(x1, bgate, u, c_conv_w[0].astype(F32), c_w_out[0].astype(BF16),
                                 row2(norm_ffn[1]), wrt, br, seq)
    y = _hier_moe(xn, route, wg, wu, wd, 1)

    return _final(xmid, y, row2(final_norm)).reshape(batch, seq, d)
```

```python
import functools
import math

import jax
import jax.numpy as jnp
from jax import lax
from jax.experimental import pallas as pl
from jax.experimental.pallas import tpu as pltpu

F32 = jnp.float32
BF16 = jnp.bfloat16

EPS = 1e-6
LOG2E = math.log2(math.e)
ROPE_THETA = 10000.0
LANES = 128
SUBLANES = 8
MLSTM_HEADS = 4
MLSTM_HEAD_DIM = 128
MLSTM_CONV = 4
DIFF_HEADS = 4
DIFF_HEAD_DIM = 64
DIFF_V_DIM = 128
CONV_WIDTH = 3
N_GROUPS = 4
EXPERTS_PER_GROUP = 8
EXPERT_FF = 256
HALO = 8

ROW_TILE = 1024
CONV_ROW_TILE = 512
MLSTM_CHUNK = 256
ATTN_BLOCK = 512
MOE_BLOCK = 1024
MOE_SUB = 128
MOE_ROWS = MOE_BLOCK + N_GROUPS * MOE_SUB
MOE_CHUNK = 256
VMEM_LIMIT = 56 * 1024 * 1024


def _dot(a, b):
    return jnp.dot(a, b, preferred_element_type=F32)


def _dot_nt(a, b):
    return lax.dot_general(a, b, (((1,), (1,)), ((), ())), preferred_element_type=F32)


def _dot_tn(a, b):
    return lax.dot_general(a, b, (((0,), (0,)), ((), ())), preferred_element_type=F32)


def _split3(x):
    hi = x.astype(BF16)
    r1 = x - hi.astype(F32)
    mid = r1.astype(BF16)
    lo = (r1 - mid.astype(F32)).astype(BF16)
    return hi, mid, lo


def _dot_f32ish(x, w_hi, w_lo):
    xh = x.astype(BF16)
    xl = (x - xh.astype(F32)).astype(BF16)
    return _dot(xh, w_hi) + (_dot(xl, w_hi) + _dot(xh, w_lo))


def _rms(x, g):
    return x * lax.rsqrt(jnp.mean(x * x, axis=-1, keepdims=True) + EPS) * g


def _sigmoid(x):
    return 0.5 * jnp.tanh(0.5 * x) + 0.5


def _shift_matrix(rows, taps):
    r = jnp.arange(rows)
    return jnp.concatenate([r[:, None] - s == r[None, :] for s in range(1, taps)], axis=0).astype(BF16)


def _shift_conv(xb, shift_ref, cw, tail, taps):
    rows, c = xb.shape
    x32 = xb.astype(F32)
    shifted = _dot(shift_ref[...], xb)
    last = taps - 1
    conv = cw[last:last + 1, :] * x32
    for s in range(1, taps):
        conv = conv + cw[last - s:last - s + 1, :] * shifted[(s - 1) * rows:s * rows, :]
    head = jnp.concatenate([tail, x32[0:HALO, :]], axis=0)
    rowi = lax.broadcasted_iota(jnp.int32, (HALO, c), 0)
    fix = jnp.zeros((HALO, c), F32)
    for s in range(1, taps):
        fix = fix + cw[last - s:last - s + 1, :] * jnp.where(rowi < s, head[HALO - s:2 * HALO - s, :], 0.0)
    return jnp.concatenate([conv[0:HALO, :] + fix, conv[HALO:, :]], axis=0)


def _params(*sem):
    return pltpu.CompilerParams(dimension_semantics=sem, vmem_limit_bytes=VMEM_LIMIT)


def _full(shape):
    return pl.BlockSpec(shape, lambda *_: (0,) * len(shape))


def _proj0_kernel(x_ref, cos_ref, sin_ref, cost_ref, sint_ref, g_ref, w_ref, wvt_ref, gb_ref,
                  mqk_ref, mo_ref, dqt_ref, dk_ref, mvt_ref, dvt_ref, gates_ref):
    xn = _rms(x_ref[...], g_ref[...])
    xb = xn.astype(BF16)
    xl = (xn - xb.astype(F32)).astype(BF16)
    half = DIFF_HEAD_DIM // 2
    dw = DIFF_HEADS * DIFF_V_DIM

    def seg(lo, hi):
        return _dot(xb, w_ref[:, lo:hi])

    mqk_ref[...] = seg(0, 1024).astype(BF16)
    mo_ref[...] = seg(1024, 1536).astype(BF16)
    vt = _dot_nt(wvt_ref[...], xb)
    mvt_ref[...] = vt[0:dw, :].astype(BF16)
    dvt_ref[...] = vt[dw:2 * dw, :].astype(BF16)
    g0 = 3 * dw
    gates_ref[...] = (vt[g0:g0 + LANES, :] + (vt[g0 + LANES:g0 + 2 * LANES, :]
                      + _dot_nt(wvt_ref[g0:g0 + LANES, :], xl))) + gb_ref[...]

    ct = cost_ref[...]
    st = sint_ref[...]
    q_scale = LOG2E * DIFF_HEAD_DIM ** -0.5
    for r in range(2 * DIFF_HEADS):
        lo = 2 * dw + r * DIFF_HEAD_DIM
        x1 = vt[lo:lo + half, :]
        x2 = vt[lo + half:lo + 2 * half, :]
        out = lo - 2 * dw
        dqt_ref[out:out + half, :] = ((x1 * ct - x2 * st) * q_scale).astype(BF16)
        dqt_ref[out + half:out + 2 * half, :] = ((x2 * ct + x1 * st) * q_scale).astype(BF16)

    heads = 2 * DIFF_HEADS
    c = cos_ref[...]
    s = sin_ref[...]
    c4 = jnp.concatenate([c, c] * heads, axis=1)
    s4 = jnp.concatenate([-s, s] * heads, axis=1)
    lane4 = lax.broadcasted_iota(jnp.int32, c4.shape, 1)
    first4 = (lane4 & (DIFF_HEAD_DIM - 1)) < half
    k = seg(1536, 2048)
    partner = jnp.where(first4, pltpu.roll(k, dw - half, 1), pltpu.roll(k, half, 1))
    dk = (k * c4 + partner * s4).astype(BF16)
    for h in range(DIFF_HEADS):
        dk_ref[h] = dk[:, h * DIFF_V_DIM:(h + 1) * DIFF_V_DIM]


def _rope_table_kernel(pos_ref, invf_ref, cos_ref, sin_ref):
    ang = invf_ref[...] * pos_ref[...].astype(F32)
    cos_ref[...] = jnp.cos(ang)
    sin_ref[...] = jnp.sin(ang)


def _rope_table(positions):
    t = positions.shape[0]
    half = DIFF_HEAD_DIM // 2
    inv_freq = (1.0 / (ROPE_THETA ** (jnp.arange(half, dtype=F32) / half))).reshape(half, 1)
    tl = min(t, 4096)
    return pl.pallas_call(
        _rope_table_kernel,
        grid=(t // tl,),
        in_specs=[pl.BlockSpec((1, tl), lambda i: (0, i)), _full(inv_freq.shape)],
        out_specs=[pl.BlockSpec((half, tl), lambda i: (0, i))] * 2,
        out_shape=[jax.ShapeDtypeStruct((half, t), F32)] * 2,
        compiler_params=_params("parallel"),
        name="rope_table",
    )(positions.reshape(1, t), inv_freq)


def _proj0(x, cost, sint, g, w, wvt, gb):
    t, d = x.shape
    tm = ROW_TILE
    row = lambda n: pl.BlockSpec((tm, n), lambda i: (i, 0))
    col = lambda n: pl.BlockSpec((n, tm), lambda i: (0, i))
    heads = pl.BlockSpec((DIFF_HEADS, tm, DIFF_V_DIM), lambda i: (0, i, 0))
    cos, sin = cost.T, sint.T
    half = cost.shape[0]
    out_shapes = [
        jax.ShapeDtypeStruct((t, 1024), BF16), jax.ShapeDtypeStruct((t, 512), BF16),
        jax.ShapeDtypeStruct((512, t), BF16), jax.ShapeDtypeStruct((DIFF_HEADS, t, DIFF_V_DIM), BF16),
        jax.ShapeDtypeStruct((512, t), BF16), jax.ShapeDtypeStruct((512, t), BF16),
        jax.ShapeDtypeStruct((LANES, t), F32),
    ]
    return pl.pallas_call(
        _proj0_kernel,
        grid=(t // tm,),
        in_specs=[row(d), row(half), row(half), col(half), col(half), _full(g.shape), _full(w.shape),
                  _full(wvt.shape), _full(gb.shape)],
        out_specs=[row(1024), row(512), col(512), heads, col(512), col(512), col(LANES)],
        out_shape=out_shapes,
        compiler_params=_params("parallel"),
        name="proj0",
    )(x, cos, sin, cost, sint, g, w, wvt, gb)


def _mlstm_kernel(mqk_ref, mvt_ref, mo_ref, gates_ref, shift_ref, cw_ref, norm_ref, hm_ref,
                  tail_ref, cn_ref, m_ref):
    L = MLSTM_CHUNK
    hd = MLSTM_HEAD_DIM
    width = MLSTM_HEADS * hd

    @pl.when(pl.program_id(1) == 0)
    def _():
        tail_ref[...] = jnp.zeros_like(tail_ref)
        cn_ref[...] = jnp.zeros_like(cn_ref)
        m_ref[...] = jnp.zeros_like(m_ref)

    conv = _shift_conv(mqk_ref[...], shift_ref, cw_ref[...], tail_ref[...], MLSTM_CONV)
    tail_ref[...] = mqk_ref[L - HALO:L, :].astype(F32)
    qk = conv * _sigmoid(conv)
    qb = (qk[:, :width] * (hd ** -0.5)).astype(BF16)
    k = qk[:, width:]
    kb = k.astype(BF16)

    g8 = gates_ref[0:SUBLANES, :]
    logf = (jnp.minimum(g8, 0.0) - jnp.log(1.0 + jnp.exp(-jnp.abs(g8)))) * LOG2E
    krow = lax.broadcasted_iota(jnp.int32, (L, L), 0)
    qcol = lax.broadcasted_iota(jnp.int32, (L, L), 1)
    causal_t = krow <= qcol
    triu = jnp.where(causal_t, 1.0, 0.0).astype(BF16)
    f_hi, f_mid, f_lo = _split3(logf)
    b8 = _dot(f_hi, triu) + (_dot(f_mid, triu) + _dot(f_lo, triu))
    cj8 = g8 * LOG2E - pltpu.roll(b8, MLSTM_HEADS, 0)
    cj_all = jnp.concatenate([cj8, jnp.zeros((LANES - SUBLANES, L), F32)], axis=0).T
    ones = jnp.ones((SUBLANES, L), BF16)

    for h in range(MLSTM_HEADS):
        sl = slice(h * hd, (h + 1) * hd)
        b_row = b8[MLSTM_HEADS + h:MLSTM_HEADS + h + 1, :]
        m_prev = m_ref[h:h + 1, 0:1]
        cn_prev = cn_ref[h]
        vt1 = jnp.concatenate([mvt_ref[sl, :], ones], axis=0)

        cmat = jnp.where(causal_t, cj_all[:, h:h + 1], -jnp.inf)
        mm = jnp.maximum(m_prev, jnp.max(cmat, axis=0, keepdims=True))
        w_t = jnp.exp2(cmat - mm)
        s_t = _dot_nt(kb[:, sl], qb[:, sl]) * w_t
        inter_w = jnp.exp2(m_prev - mm)
        tot = _dot(vt1, s_t.astype(BF16)) + inter_w * _dot_nt(cn_prev.astype(BF16), qb[:, sl])
        den = tot[hd:hd + 1, :]
        h_t = tot[0:hd, :] / jnp.maximum(jnp.abs(den), jnp.exp2(-(b_row + mm)))

        kw = k[:, sl] * w_t[:, L - 1:L]
        cn_ref[h] = inter_w[:, L - 1:L] * cn_prev + _dot(vt1, kw.astype(BF16))
        m_ref[h:h + 1, :] = jnp.broadcast_to(b_row[:, L - 1:L] + mm[:, L - 1:L], (1, LANES))

        scale = lax.rsqrt(jnp.mean(h_t * h_t, axis=0, keepdims=True) + EPS)
        hn = (h_t * scale).T * norm_ref[...]
        hm_ref[:, sl] = (hn * _sigmoid(mo_ref[:, sl].astype(F32))).astype(BF16)


def _mlstm(mqk, mvt, mo, gates, cw, norm, batch, seq):
    t = mqk.shape[0]
    L = MLSTM_CHUNK
    nc = seq // L
    width = MLSTM_HEADS * MLSTM_HEAD_DIM
    row = lambda n: pl.BlockSpec((L, n), lambda b, c: (b * nc + c, 0))
    col = lambda n: pl.BlockSpec((n, L), lambda b, c: (0, b * nc + c))
    shift = _shift_matrix(L, MLSTM_CONV)
    return pl.pallas_call(
        _mlstm_kernel,
        grid=(batch, nc),
        in_specs=[row(2 * width), col(width), row(width), col(LANES), _full(shift.shape), _full(cw.shape),
                  _full(norm.shape)],
        out_specs=row(width),
        out_shape=jax.ShapeDtypeStruct((t, width), BF16),
        scratch_shapes=[
            pltpu.VMEM((HALO, 2 * width), F32),
            pltpu.VMEM((MLSTM_HEADS, MLSTM_HEAD_DIM + SUBLANES, MLSTM_HEAD_DIM), F32),
            pltpu.VMEM((SUBLANES, LANES), F32),
        ],
        compiler_params=_params("parallel", "arbitrary"),
        name="mlstm",
    )(mqk, mvt, mo, gates, shift, cw, norm)


def _diff_attn_kernel(qt_ref, k_ref, vt_ref, lam_ref, norm_ref, o_ref,
                      acc0_ref, acc1_ref, m_ref, sa0_ref, sa1_ref, sb0_ref, sb1_ref, *, lam_init):
    tq = ATTN_BLOCK
    dv = DIFF_V_DIM
    nq = k_ref.shape[1] // tq
    frow = lax.broadcasted_iota(jnp.int32, (dv, tq), 0)
    krow = lax.broadcasted_iota(jnp.int32, (tq, tq), 0)
    qcol = lax.broadcasted_iota(jnp.int32, (tq, tq), 1)
    causal_t = krow <= qcol
    ones = jnp.ones((SUBLANES, tq), BF16)
    accs = (acc0_ref, acc1_ref)
    buf_a = (sa0_ref, sa1_ref)
    buf_b = (sb0_ref, sb1_ref)
    lv = lam_ref[...]
    lam = (jnp.exp(jnp.sum(lv[0:1, :] * lv[1:2, :], axis=-1, keepdims=True))
           - jnp.exp(jnp.sum(lv[2:3, :] * lv[3:4, :], axis=-1, keepdims=True)) + lam_init)

    def q_block(qi, carry):
        qrow = pl.multiple_of(qi * tq, tq)
        qt = qt_ref[:, pl.ds(qrow, tq)]
        zero = jnp.zeros_like(qt)
        qs = (jnp.where(frow < DIFF_HEAD_DIM, qt, zero), jnp.where(frow >= DIFF_HEAD_DIM, qt, zero))
        for acc in accs:
            acc[...] = jnp.zeros_like(acc)
        m_ref[...] = jnp.full(m_ref.shape, -jnp.inf, F32)

        def scores(j, dst):
            kb = k_ref[0, pl.ds(pl.multiple_of(j * tq, tq), tq), :]
            for c in range(2):
                dst[c][...] = _dot(kb, qs[c])

        def absorb(j, src, masked):
            start = pl.multiple_of(j * tq, tq)
            vt1 = jnp.concatenate([vt_ref[:, pl.ds(start, tq)], ones], axis=0)
            s = [src[c][...] for c in range(2)]
            if masked:
                s = [jnp.where(causal_t, sc, -jnp.inf) for sc in s]
            m_old = [m_ref[c:c + 1, :] for c in range(2)]
            m_new = [jnp.maximum(m_old[c], jnp.max(s[c], axis=0, keepdims=True)) for c in range(2)]
            p = [jnp.exp2(s[c] - m_new[c]).astype(BF16) for c in range(2)]
            alpha = [jnp.exp2(m_old[c] - m_new[c]) for c in range(2)]
            pv = [_dot(vt1, p[c]) for c in range(2)]
            for c in range(2):
                accs[c][...] = alpha[c] * accs[c][...] + pv[c]
                m_ref[c:c + 1, :] = m_new[c]

        scores(0, buf_a)

        def pair(jj, inner):
            j = 2 * jj
            scores(j + 1, buf_b)
            absorb(j, buf_a, False)
            scores(j + 2, buf_a)
            absorb(j + 1, buf_b, False)
            return inner

        lax.fori_loop(0, qi // 2, pair, 0)

        @pl.when(qi % 2 == 0)
        def _():
            absorb(qi, buf_a, True)

        @pl.when(qi % 2 == 1)
        def _():
            scores(qi, buf_b)
            absorb(qi - 1, buf_a, False)
            absorb(qi, buf_b, True)

        a1 = acc0_ref[...]
        a2 = acc1_ref[...]
        o_t = a1[0:dv, :] / a1[dv:dv + 1, :] - lam * (a2[0:dv, :] / a2[dv:dv + 1, :])
        o_ref[0, pl.ds(qrow, tq), :] = (_rms(o_t.T, norm_ref[...]) * (1.0 - lam_init)).astype(BF16)
        return carry

    lax.fori_loop(0, nq, q_block, 0)


def _diff_attn(dqt, dk, dvt, lam_vecs, norm, batch, seq, lam_init):
    t = dk.shape[1]
    tq = ATTN_BLOCK
    rows = pl.BlockSpec((1, seq, DIFF_V_DIM), lambda b, h: (h, b, 0))
    cols = pl.BlockSpec((DIFF_V_DIM, seq), lambda b, h: (h, b))
    return pl.pallas_call(
        functools.partial(_diff_attn_kernel, lam_init=lam_init),
        grid=(batch, DIFF_HEADS),
        in_specs=[cols, rows, cols, _full(lam_vecs.shape), _full(norm.shape)],
        out_specs=rows,
        out_shape=jax.ShapeDtypeStruct((DIFF_HEADS, t, DIFF_V_DIM), BF16),
        scratch_shapes=([pltpu.VMEM((DIFF_V_DIM + SUBLANES, tq), F32)] * 2 + [pltpu.VMEM((SUBLANES, tq), F32)]
                        + [pltpu.VMEM((tq, tq), F32)] * 4),
        compiler_params=_params("parallel", "parallel"),
        name="diff_attn",
    )(dqt, dk, dvt, lam_vecs, norm)


ROUTE_ROWS = 2 * SUBLANES
GROUP_LANE = EXPERTS_PER_GROUP
DEST_LANE = EXPERTS_PER_GROUP
EXPERT_LANE0 = 8


def _route(xn, wrt_ref, br_ref):
    xh = xn.astype(BF16)
    xl = (xn - xh.astype(F32)).astype(BF16)
    both = _dot_nt(wrt_ref[...], xh)
    logits = both[0:LANES, :] + (both[LANES:2 * LANES, :] + _dot_nt(wrt_ref[0:LANES, :], xl)) + br_ref[...]
    e = EXPERTS_PER_GROUP
    tm = logits.shape[1]
    row = lax.broadcasted_iota(jnp.int32, (e, tm), 0)
    ninf = -jnp.inf
    big = jnp.int32(LANES)

    is_g = row < N_GROUPS
    gl = jnp.where(is_g, logits[0:e, :], ninf)
    gmax = jnp.max(gl, axis=0, keepdims=True)
    gsum = jnp.sum(jnp.where(is_g, jnp.exp(gl - gmax), 0.0), axis=0, keepdims=True)
    g_val = 1.0 / gsum
    g_idx = jnp.min(jnp.where(gl == gmax, row, big), axis=0, keepdims=True)

    el = logits[EXPERT_LANE0:EXPERT_LANE0 + e, :]
    for g in range(1, N_GROUPS):
        lo = EXPERT_LANE0 + e * g
        el = jnp.where(g_idx == g, logits[lo:lo + e, :], el)
    v1 = jnp.max(el, axis=0, keepdims=True)
    i1 = jnp.min(jnp.where(el == v1, row, big), axis=0, keepdims=True)
    el2 = jnp.where(row == i1, ninf, el)
    v2 = jnp.max(el2, axis=0, keepdims=True)
    i2 = jnp.min(jnp.where(el2 == v2, row, big), axis=0, keepdims=True)
    e21 = jnp.exp(v2 - v1)
    w1 = 1.0 / (1.0 + e21)
    w2 = e21 * w1
    gates = g_val * (jnp.where(row == i1, w1, 0.0) + jnp.where(row == i2, w2, 0.0))
    tail = jnp.where(row == 0, g_idx.astype(F32), 0.0)
    return xh, jnp.concatenate([gates, tail], axis=0)


def _mix0_tail_kernel(x_ref, hm_ref, hd_ref, wo_ref, nf_ref, wrt_ref, br_ref,
                      xmid_ref, xn_ref, route_ref):
    half = hm_ref.shape[1]
    hd = jnp.concatenate([hd_ref[h] for h in range(DIFF_HEADS)], axis=1)
    mix = _dot(hm_ref[...], wo_ref[0:half, :]) + _dot(hd, wo_ref[half:2 * half, :])
    xmid = x_ref[...] + mix
    xmid_ref[...] = xmid
    xn_ref[...], route_ref[...] = _route(_rms(xmid, nf_ref[...]), wrt_ref, br_ref)


def _mix0_tail(x, hm, hd, wo, nf, wrt, br):
    t, d = x.shape
    tm = ROW_TILE
    row = lambda n: pl.BlockSpec((tm, n), lambda i: (i, 0))
    return pl.pallas_call(
        _mix0_tail_kernel,
        grid=(t // tm,),
        in_specs=[row(d), row(hm.shape[1]), pl.BlockSpec((DIFF_HEADS, tm, DIFF_V_DIM), lambda i: (0, i, 0)),
                  _full(wo.shape), _full(nf.shape), _full(wrt.shape), _full(br.shape)],
        out_specs=[row(d), row(d), pl.BlockSpec((ROUTE_ROWS, tm), lambda i: (0, i))],
        out_shape=[jax.ShapeDtypeStruct((t, d), F32), jax.ShapeDtypeStruct((t, d), BF16),
                   jax.ShapeDtypeStruct((ROUTE_ROWS, t), F32)],
        compiler_params=_params("parallel"),
        name="mix0_tail",
    )(x, hm, hd, wo, nf, wrt, br)


def _proj1_kernel(x_ref, y_ref, g_ref, w_ref, x1_ref, b_ref, u_ref):
    d = x_ref.shape[1]
    x1 = x_ref[...] + y_ref[...].astype(F32)
    x1_ref[...] = x1
    xb = _rms(x1, g_ref[...]).astype(BF16)
    b_ref[...] = _dot(xb, w_ref[:, 0:d]).astype(BF16)
    u_ref[...] = (_dot(xb, w_ref[:, d:2 * d]) * _dot(xb, w_ref[:, 2 * d:3 * d])).astype(BF16)


def _proj1(x, y, g, w):
    t, d = x.shape
    tm = ROW_TILE
    row = lambda n: pl.BlockSpec((tm, n), lambda i: (i, 0))
    return pl.pallas_call(
        _proj1_kernel,
        grid=(t // tm,),
        in_specs=[row(d), row(d), _full(g.shape), _full(w.shape)],
        out_specs=[row(d), row(d), row(d)],
        out_shape=[jax.ShapeDtypeStruct((t, d), F32), jax.ShapeDtypeStruct((t, d), BF16),
                   jax.ShapeDtypeStruct((t, d), BF16)],
        compiler_params=_params("parallel"),
        name="proj1",
    )(x, y, g, w)


def _mix1_tail_kernel(x_ref, b_ref, u_ref, halo_ref, shift_ref, cw_ref, wo_ref, nf_ref, wrt_ref, br_ref,
                      xmid_ref, xn_ref, route_ref, *, tiles_per_seq):
    seq_start = pl.program_id(0) % tiles_per_seq == 0
    halo = halo_ref[...].astype(F32)
    tail = jnp.where(seq_start, jnp.zeros_like(halo), halo)
    conv = _shift_conv(u_ref[...], shift_ref, cw_ref[...], tail, CONV_WIDTH)
    mixed = (b_ref[...].astype(F32) * conv).astype(BF16)
    xmid = x_ref[...] + _dot(mixed, wo_ref[...])
    xmid_ref[...] = xmid
    xn_ref[...], route_ref[...] = _route(_rms(xmid, nf_ref[...]), wrt_ref, br_ref)


def _mix1_tail(x, b, u, cw, wo, nf, wrt, br, seq):
    t, d = x.shape
    tm = CONV_ROW_TILE
    row = lambda n: pl.BlockSpec((tm, n), lambda i: (i, 0))
    halo = pl.BlockSpec((HALO, d), lambda i: (jnp.maximum(i * (tm // HALO) - 1, 0), 0))
    shift = _shift_matrix(tm, CONV_WIDTH)
    return pl.pallas_call(
        functools.partial(_mix1_tail_kernel, tiles_per_seq=seq // tm),
        grid=(t // tm,),
        in_specs=[row(d), row(d), row(d), halo, _full(shift.shape), _full(cw.shape), _full(wo.shape),
                  _full(nf.shape), _full(wrt.shape), _full(br.shape)],
        out_specs=[row(d), row(d), pl.BlockSpec((ROUTE_ROWS, tm), lambda i: (0, i))],
        out_shape=[jax.ShapeDtypeStruct((t, d), F32), jax.ShapeDtypeStruct((t, d), BF16),
                   jax.ShapeDtypeStruct((ROUTE_ROWS, t), F32)],
        compiler_params=_params("parallel"),
        name="mix1_tail",
    )(x, b, u, u, shift, cw, wo, nf, wrt, br)


def _moe_kernel(off_ref, nsub_ref, x_ref, r_ref, wg_ref, wu_ref, wd_ref, y_ref,
                xs_ref, gs_ref, ys_ref, dest_ref):
    i = pl.program_id(0)
    g = pl.program_id(1)
    tb = MOE_BLOCK
    rows = MOE_ROWS
    ch = MOE_CHUNK
    d = x_ref.shape[1]

    @pl.when(g == 0)
    def _():
        r = r_ref[...]
        grp = r[GROUP_LANE:GROUP_LANE + 1, :]
        row8 = lax.broadcasted_iota(jnp.int32, (SUBLANES, tb), 0)
        onehot = jnp.where(row8.astype(F32) == grp, 1.0, 0.0)
        row1 = lax.broadcasted_iota(jnp.int32, (SUBLANES, 1), 0)
        base = jnp.zeros((SUBLANES, 1), F32)
        for gg in range(N_GROUPS):
            base = jnp.where(row1 == gg, off_ref[i * N_GROUPS + gg].astype(F32), base)
        rr = lax.broadcasted_iota(jnp.int32, (ch, ch), 0)
        cc = lax.broadcasted_iota(jnp.int32, (ch, ch), 1)
        upper = jnp.where(rr <= cc, 1.0, 0.0).astype(BF16)
        parts = []
        for c in range(tb // ch):
            sel = onehot[:, c * ch:(c + 1) * ch]
            rank = _dot(sel.astype(BF16), upper) + base
            parts.append(jnp.sum(sel * rank, axis=0, keepdims=True) - 1.0)
            base = base + jnp.sum(sel, axis=1, keepdims=True)
        dest_row = jnp.concatenate(parts, axis=1)
        side = jnp.concatenate([r[0:SUBLANES, :], jnp.broadcast_to(dest_row, (SUBLANES, tb)),
                                jnp.zeros((LANES - 2 * SUBLANES, tb), F32)], axis=0).T
        dest_ref[...] = side
        lane = lax.broadcasted_iota(jnp.int32, (tb, LANES), 1)
        gates = jnp.where(lane < EXPERTS_PER_GROUP, side, 0.0)
        g_hi = gates.astype(BF16)
        g_lo = (gates - g_hi.astype(F32)).astype(BF16)
        xcat = jnp.concatenate([x_ref[...], g_hi, g_lo], axis=1)
        for c in range(rows // ch):
            rr = (lax.broadcasted_iota(jnp.int32, (ch, tb), 0) + c * ch).astype(F32)
            p = jnp.where(rr == dest_row, 1.0, 0.0).astype(BF16)
            moved = _dot(p, xcat)
            xs_ref[c * ch:(c + 1) * ch, :] = moved[:, 0:d].astype(BF16)
            gs_ref[c * ch:(c + 1) * ch, :] = moved[:, d:d + LANES] + moved[:, d + LANES:d + 2 * LANES]
        ys_ref[...] = jnp.zeros_like(ys_ref)

    off = off_ref[i * N_GROUPS + g]
    n_sub = nsub_ref[i * N_GROUPS + g]

    def expert_rows(r0, size):
        xt = xs_ref[pl.ds(r0, size), :]
        gt = gs_ref[pl.ds(r0, size), :]
        acc = jnp.zeros((size, d), F32)
        for e in range(EXPERTS_PER_GROUP):
            hg = _dot(xt, wg_ref[0, 0, e])
            up = _dot(xt, wu_ref[0, 0, e])
            act = hg * _sigmoid(hg) * up * gt[:, e:e + 1]
            acc = acc + _dot(act.astype(BF16), wd_ref[0, 0, e])
        ys_ref[pl.ds(r0, size), :] = acc.astype(BF16)

    def pair(j, carry):
        expert_rows(pl.multiple_of(off + j * (2 * MOE_SUB), MOE_SUB), 2 * MOE_SUB)
        return carry

    lax.fori_loop(0, n_sub // 2, pair, 0)

    @pl.when(n_sub % 2 == 1)
    def _():
        expert_rows(pl.multiple_of(off + (n_sub - 1) * MOE_SUB, MOE_SUB), MOE_SUB)

    @pl.when(g == N_GROUPS - 1)
    def _():
        ys = ys_ref[...]
        for c in range(tb // ch):
            cc = lax.broadcasted_iota(jnp.int32, (ch, rows), 1).astype(F32)
            pt = jnp.where(cc == dest_ref[c * ch:(c + 1) * ch, DEST_LANE:DEST_LANE + 1], 1.0, 0.0).astype(BF16)
            y_ref[c * ch:(c + 1) * ch, :] = _dot(pt, ys).astype(y_ref.dtype)


def _hier_moe(xn, route, wg, wu, wd, layer):
    t, d = xn.shape
    tb = MOE_BLOCK
    nb = t // tb
    e, ff = wg.shape[2], wg.shape[4]
    group = route[GROUP_LANE, :].astype(jnp.int32).reshape(nb, tb)
    counts = jnp.sum((group[:, :, None] == jnp.arange(N_GROUPS)[None, None, :]).astype(jnp.int32), axis=1)
    nsub = (counts + MOE_SUB - 1) // MOE_SUB
    padded = nsub * MOE_SUB
    off = jnp.cumsum(padded, axis=1) - padded
    grid_spec = pltpu.PrefetchScalarGridSpec(
        num_scalar_prefetch=2,
        grid=(nb, N_GROUPS),
        in_specs=[
            pl.BlockSpec((tb, d), lambda i, g, *_: (i, 0)),
            pl.BlockSpec((ROUTE_ROWS, tb), lambda i, g, *_: (0, i)),
            pl.BlockSpec((1, 1, e, d, ff), lambda i, g, *_: (layer, g, 0, 0, 0)),
            pl.BlockSpec((1, 1, e, d, ff), lambda i, g, *_: (layer, g, 0, 0, 0)),
            pl.BlockSpec((1, 1, e, ff, d), lambda i, g, *_: (layer, g, 0, 0, 0)),
        ],
        out_specs=pl.BlockSpec((tb, d), lambda i, g, *_: (i, 0)),
        scratch_shapes=[
            pltpu.VMEM((MOE_ROWS, d), BF16),
            pltpu.VMEM((MOE_ROWS, LANES), F32),
            pltpu.VMEM((MOE_ROWS, d), BF16),
            pltpu.VMEM((tb, LANES), F32),
        ],
    )
    return pl.pallas_call(
        _moe_kernel,
        grid_spec=grid_spec,
        out_shape=jax.ShapeDtypeStruct((t, d), BF16),
        compiler_params=_params("parallel", "arbitrary"),
        name="moe",
    )(off.reshape(-1).astype(jnp.int32), nsub.reshape(-1).astype(jnp.int32), xn, route, wg, wu, wd)


def _final_kernel(x_ref, y_ref, g_ref, o_ref):
    o_ref[...] = _rms(x_ref[...] + y_ref[...].astype(F32), g_ref[...])


def _final(x, y, g):
    t, d = x.shape
    tm = ROW_TILE
    row = pl.BlockSpec((tm, d), lambda i: (i, 0))
    return pl.pallas_call(
        _final_kernel,
        grid=(t // tm,),
        in_specs=[row, row, _full(g.shape)],
        out_specs=row,
        out_shape=jax.ShapeDtypeStruct((t, d), F32),
        compiler_params=_params("parallel"),
        name="final_norm",
    )(x, y, g)


def _hi_lo(w):
    hi = w.astype(BF16)
    return hi, (w - hi.astype(F32)).astype(BF16)


def _router_params(w_rg, b_rg, w_re, b_re):
    d = w_rg.shape[0]
    n_e = N_GROUPS * EXPERTS_PER_GROUP
    pad = jnp.zeros((d, EXPERT_LANE0 - N_GROUPS), F32)
    tail = jnp.zeros((d, LANES - EXPERT_LANE0 - n_e), F32)
    w = jnp.concatenate([w_rg, pad, w_re, tail], axis=1)
    b = jnp.concatenate([b_rg, pad[0], b_re, tail[0]]).reshape(LANES, 1)
    hi, lo = _hi_lo(w)
    return jnp.concatenate([hi, lo], axis=1).T, b


def kernel(x, positions, norm_mix, norm_ffn, ab_w_in, ab_gate_bias, ab_conv_w, ab_mlstm_norm, ab_lambda,
           ab_diff_norm, ab_w_out, c_w_in, c_conv_w, c_w_out, moe_w_rg, moe_b_rg, moe_w_re, moe_b_re,
           moe_w_gate, moe_w_up, moe_w_down, final_norm):
    batch, seq, d = x.shape
    t = batch * seq
    xf = x.reshape(t, d)
    row2 = lambda v: v.reshape(1, -1).astype(F32)

    w_in = ab_w_in[0]
    mw = 2 * MLSTM_HEADS * MLSTM_HEAD_DIM
    hw = MLSTM_HEADS * MLSTM_HEAD_DIM
    ng = 2 * MLSTM_HEADS
    dw = DIFF_HEADS * DIFF_V_DIM
    o_g = mw + 2 * hw
    o_q = o_g + ng
    o_k, o_dv = o_q + dw, o_q + 2 * dw
    w0 = jnp.concatenate([w_in[:, :mw], w_in[:, mw + hw:o_g], w_in[:, o_k:o_dv]], axis=1).astype(BF16)
    w_gates = jnp.concatenate([w_in[:, o_g:o_q], jnp.zeros((d, LANES - ng), F32)], axis=1)
    wgh, wgl = _hi_lo(w_gates)
    wvt = jnp.concatenate([w_in[:, mw:mw + hw].astype(BF16), w_in[:, o_dv:o_dv + dw].astype(BF16),
                           w_in[:, o_q:o_k].astype(BF16), wgh, wgl], axis=1).T
    gb = jnp.concatenate([ab_gate_bias[0].astype(F32), jnp.zeros((LANES - ng,), F32)]).reshape(LANES, 1)

    cost, sint = _rope_table(positions.reshape(t).astype(jnp.int32))
    mqk, mo, dqt, dk, mvt, dvt, gates = _proj0(xf, cost, sint, row2(norm_mix[0]), w0, wvt, gb)
    hm = _mlstm(mqk, mvt, mo, gates, ab_conv_w[0].astype(F32), row2(ab_mlstm_norm[0]), batch, seq)
    lam_init = 0.8 - 0.6 * math.exp(-0.3 * 0)
    hd = _diff_attn(dqt, dk, dvt, ab_lambda[0].astype(F32), row2(ab_diff_norm[0]), batch, seq, lam_init)

    wrt, br = _router_params(moe_w_rg[0], moe_b_rg[0], moe_w_re[0], moe_b_re[0])
    xmid, xn, route = _mix0_tail(xf, hm, hd, ab_w_out[0].astype(BF16), row2(norm_ffn[0]), wrt, br)
    wg, wu, wd = moe_w_gate.astype(BF16), moe_w_up.astype(BF16), moe_w_down.astype(BF16)
    y = _hier_moe(xn, route, wg, wu, wd, 0)

    x1, bgate, u = _proj1(xmid, y, row2(norm_mix[1]), c_w_in[0].astype(BF16))
    wrt, br = _router_params(moe_w_rg[1], moe_b_rg[1], moe_w_re[1], moe_b_re[1])
    xmid, xn, route = _mix1_tail(x1, bgate, u, c_conv_w[0].astype(F32), c_w_out[0].astype(BF16),
                                 row2(norm_ffn[1]), wrt, br, seq)
    y = _hier_moe(xn, route, wg, wu, wd, 1)

    return _final(xmid, y, row2(final_norm)).reshape(batch, seq, d)
```

```python
import functools
import math

import jax
import jax.numpy as jnp
from jax import lax
from jax.experimental import pallas as pl
from jax.experimental.pallas import tpu as pltpu

F32 = jnp.float32
BF16 = jnp.bfloat16

EPS = 1e-6
LOG2E = math.log2(math.e)
ROPE_THETA = 10000.0
LANES = 128
SUBLANES = 8
MLSTM_HEADS = 4
MLSTM_HEAD_DIM = 128
MLSTM_CONV = 4
DIFF_HEADS = 4
DIFF_HEAD_DIM = 64
DIFF_V_DIM = 128
CONV_WIDTH = 3
N_GROUPS = 4
EXPERTS_PER_GROUP = 8
EXPERT_FF = 256
HALO = 8

ROW_TILE = 1024
CONV_ROW_TILE = 512
MLSTM_CHUNK = 256
ATTN_BLOCK = 512
MOE_BLOCK = 1024
MOE_SUB = 128
MOE_ROWS = MOE_BLOCK + N_GROUPS * MOE_SUB
MOE_CHUNK = 256
VMEM_LIMIT = 56 * 1024 * 1024


def _dot(a, b):
    return jnp.dot(a, b, preferred_element_type=F32)


def _dot_nt(a, b):
    return lax.dot_general(a, b, (((1,), (1,)), ((), ())), preferred_element_type=F32)


def _dot_tn(a, b):
    return lax.dot_general(a, b, (((0,), (0,)), ((), ())), preferred_element_type=F32)


def _split3(x):
    hi = x.astype(BF16)
    r1 = x - hi.astype(F32)
    mid = r1.astype(BF16)
    lo = (r1 - mid.astype(F32)).astype(BF16)
    return hi, mid, lo


def _dot_f32ish(x, w_hi, w_lo):
    xh = x.astype(BF16)
    xl = (x - xh.astype(F32)).astype(BF16)
    return _dot(xh, w_hi) + (_dot(xl, w_hi) + _dot(xh, w_lo))


def _rms(x, g):
    return x * lax.rsqrt(jnp.mean(x * x, axis=-1, keepdims=True) + EPS) * g


def _sigmoid(x):
    return 0.5 * jnp.tanh(0.5 * x) + 0.5


def _shift_matrix(rows, taps):
    r = jnp.arange(rows)
    return jnp.concatenate([r[:, None] - s == r[None, :] for s in range(1, taps)], axis=0).astype(BF16)


def _shift_conv(xb, shift_ref, cw, tail, taps):
    rows, c = xb.shape
    x32 = xb.astype(F32)
    shifted = _dot(shift_ref[...], xb)
    last = taps - 1
    conv = cw[last:last + 1, :] * x32
    for s in range(1, taps):
        conv = conv + cw[last - s:last - s + 1, :] * shifted[(s - 1) * rows:s * rows, :]
    head = jnp.concatenate([tail, x32[0:HALO, :]], axis=0)
    rowi = lax.broadcasted_iota(jnp.int32, (HALO, c), 0)
    fix = jnp.zeros((HALO, c), F32)
    for s in range(1, taps):
        fix = fix + cw[last - s:last - s + 1, :] * jnp.where(rowi < s, head[HALO - s:2 * HALO - s, :], 0.0)
    return jnp.concatenate([conv[0:HALO, :] + fix, conv[HALO:, :]], axis=0)


def _roll_conv(xb, cw, tail, taps):
    rows, c = xb.shape
    x32 = xb.astype(F32)
    last = taps - 1
    conv = cw[last:last + 1, :] * x32
    head = jnp.concatenate([tail, x32[0:HALO, :]], axis=0)
    first = cw[last:last + 1, :] * x32[0:HALO, :]
    for s in range(1, taps):
        w = cw[last - s:last - s + 1, :]
        conv = conv + w * pltpu.roll(x32, s, 0)
        first = first + w * head[HALO - s:2 * HALO - s, :]
    return jnp.concatenate([first, conv[HALO:, :]], axis=0)


def _params(*sem):
    return pltpu.CompilerParams(dimension_semantics=sem, vmem_limit_bytes=VMEM_LIMIT)


def _full(shape):
    return pl.BlockSpec(shape, lambda *_: (0,) * len(shape))


def _proj0_kernel(x_ref, cos_ref, sin_ref, cost_ref, sint_ref, g_ref, w_ref, wvt_ref, gb_ref,
                  mqk_ref, mo_ref, dqt_ref, dk_ref, mvt_ref, dvt_ref, gates_ref):
    xn = _rms(x_ref[...], g_ref[...])
    xb = xn.astype(BF16)
    xl = (xn - xb.astype(F32)).astype(BF16)
    half = DIFF_HEAD_DIM // 2
    dw = DIFF_HEADS * DIFF_V_DIM

    def seg(lo, hi):
        return _dot(xb, w_ref[:, lo:hi])

    mqk_ref[...] = seg(0, 1024).astype(BF16)
    mo_ref[...] = seg(1024, 1536).astype(BF16)
    vt = _dot_nt(wvt_ref[...], xb)
    mvt_ref[...] = vt[0:dw, :].astype(BF16)
    dvt_ref[...] = vt[dw:2 * dw, :].astype(BF16)
    g0 = 3 * dw
    gates_ref[...] = (vt[g0:g0 + LANES, :] + (vt[g0 + LANES:g0 + 2 * LANES, :]
                      + _dot_nt(wvt_ref[g0:g0 + LANES, :], xl))) + gb_ref[...]

    ct = cost_ref[...]
    st = sint_ref[...]
    q_scale = LOG2E * DIFF_HEAD_DIM ** -0.5
    for r in range(2 * DIFF_HEADS):
        lo = 2 * dw + r * DIFF_HEAD_DIM
        x1 = vt[lo:lo + half, :]
        x2 = vt[lo + half:lo + 2 * half, :]
        out = lo - 2 * dw
        dqt_ref[out:out + half, :] = ((x1 * ct - x2 * st) * q_scale).astype(BF16)
        dqt_ref[out + half:out + 2 * half, :] = ((x2 * ct + x1 * st) * q_scale).astype(BF16)

    heads = 2 * DIFF_HEADS
    c = cos_ref[...]
    s = sin_ref[...]
    c4 = jnp.concatenate([c, c] * heads, axis=1)
    s4 = jnp.concatenate([-s, s] * heads, axis=1)
    lane4 = lax.broadcasted_iota(jnp.int32, c4.shape, 1)
    first4 = (lane4 & (DIFF_HEAD_DIM - 1)) < half
    k = seg(1536, 2048)
    partner = jnp.where(first4, pltpu.roll(k, dw - half, 1), pltpu.roll(k, half, 1))
    dk = (k * c4 + partner * s4).astype(BF16)
    for h in range(DIFF_HEADS):
        dk_ref[h] = dk[:, h * DIFF_V_DIM:(h + 1) * DIFF_V_DIM]


def _rope_table_kernel(pos_ref, invf_ref, cos_ref, sin_ref):
    ang = invf_ref[...] * pos_ref[...].astype(F32)
    cos_ref[...] = jnp.cos(ang)
    sin_ref[...] = jnp.sin(ang)


def _rope_table(positions):
    t = positions.shape[0]
    half = DIFF_HEAD_DIM // 2
    inv_freq = (1.0 / (ROPE_THETA ** (jnp.arange(half, dtype=F32) / half))).reshape(half, 1)
    tl = min(t, 4096)
    return pl.pallas_call(
        _rope_table_kernel,
        grid=(t // tl,),
        in_specs=[pl.BlockSpec((1, tl), lambda i: (0, i)), _full(inv_freq.shape)],
        out_specs=[pl.BlockSpec((half, tl), lambda i: (0, i))] * 2,
        out_shape=[jax.ShapeDtypeStruct((half, t), F32)] * 2,
        compiler_params=_params("parallel"),
        name="rope_table",
    )(positions.reshape(1, t), inv_freq)


def _proj0(x, cost, sint, g, w, wvt, gb):
    t, d = x.shape
    tm = ROW_TILE
    row = lambda n: pl.BlockSpec((tm, n), lambda i: (i, 0))
    col = lambda n: pl.BlockSpec((n, tm), lambda i: (0, i))
    heads = pl.BlockSpec((DIFF_HEADS, tm, DIFF_V_DIM), lambda i: (0, i, 0))
    cos, sin = cost.T, sint.T
    half = cost.shape[0]
    out_shapes = [
        jax.ShapeDtypeStruct((t, 1024), BF16), jax.ShapeDtypeStruct((t, 512), BF16),
        jax.ShapeDtypeStruct((512, t), BF16), jax.ShapeDtypeStruct((DIFF_HEADS, t, DIFF_V_DIM), BF16),
        jax.ShapeDtypeStruct((512, t), BF16), jax.ShapeDtypeStruct((512, t), BF16),
        jax.ShapeDtypeStruct((LANES, t), F32),
    ]
    return pl.pallas_call(
        _proj0_kernel,
        grid=(t // tm,),
        in_specs=[row(d), row(half), row(half), col(half), col(half), _full(g.shape), _full(w.shape),
                  _full(wvt.shape), _full(gb.shape)],
        out_specs=[row(1024), row(512), col(512), heads, col(512), col(512), col(LANES)],
        out_shape=out_shapes,
        compiler_params=_params("parallel"),
        name="proj0",
    )(x, cos, sin, cost, sint, g, w, wvt, gb)


def _mlstm_kernel(mqk_ref, mvt_ref, mo_ref, gates_ref, shift_ref, cw_ref, norm_ref, hm_ref,
                  tail_ref, cn_ref, m_ref):
    L = MLSTM_CHUNK
    hd = MLSTM_HEAD_DIM
    width = MLSTM_HEADS * hd

    @pl.when(pl.program_id(1) == 0)
    def _():
        tail_ref[...] = jnp.zeros_like(tail_ref)
        cn_ref[...] = jnp.zeros_like(cn_ref)
        m_ref[...] = jnp.zeros_like(m_ref)

    conv = _shift_conv(mqk_ref[...], shift_ref, cw_ref[...], tail_ref[...], MLSTM_CONV)
    tail_ref[...] = mqk_ref[L - HALO:L, :].astype(F32)
    qk = conv * _sigmoid(conv)
    qb = (qk[:, :width] * (hd ** -0.5)).astype(BF16)
    k = qk[:, width:]
    kb = k.astype(BF16)

    g8 = gates_ref[0:SUBLANES, :]
    logf = (jnp.minimum(g8, 0.0) - jnp.log(1.0 + jnp.exp(-jnp.abs(g8)))) * LOG2E
    krow = lax.broadcasted_iota(jnp.int32, (L, L), 0)
    qcol = lax.broadcasted_iota(jnp.int32, (L, L), 1)
    causal_t = krow <= qcol
    triu = jnp.where(causal_t, 1.0, 0.0).astype(BF16)
    f_hi, f_mid, f_lo = _split3(logf)
    b8 = _dot(f_hi, triu) + (_dot(f_mid, triu) + _dot(f_lo, triu))
    cj8 = g8 * LOG2E - pltpu.roll(b8, MLSTM_HEADS, 0)
    cj_all = jnp.concatenate([cj8, jnp.zeros((LANES - SUBLANES, L), F32)], axis=0).T
    ones = jnp.ones((SUBLANES, L), BF16)

    for h in range(MLSTM_HEADS):
        sl = slice(h * hd, (h + 1) * hd)
        b_row = b8[MLSTM_HEADS + h:MLSTM_HEADS + h + 1, :]
        m_prev = m_ref[h:h + 1, 0:1]
        cn_prev = cn_ref[h]
        vt1 = jnp.concatenate([mvt_ref[sl, :], ones], axis=0)

        cmat = jnp.where(causal_t, cj_all[:, h:h + 1], -jnp.inf)
        mm = jnp.maximum(m_prev, jnp.max(cmat, axis=0, keepdims=True))
        w_t = jnp.exp2(cmat - mm)
        s_t = _dot_nt(kb[:, sl], qb[:, sl]) * w_t
        inter_w = jnp.exp2(m_prev - mm)
        tot = _dot(vt1, s_t.astype(BF16)) + inter_w * _dot_nt(cn_prev.astype(BF16), qb[:, sl])
        den = tot[hd:hd + 1, :]
        h_t = tot[0:hd, :] / jnp.maximum(jnp.abs(den), jnp.exp2(-(b_row + mm)))

        kw = k[:, sl] * w_t[:, L - 1:L]
        cn_ref[h] = inter_w[:, L - 1:L] * cn_prev + _dot(vt1, kw.astype(BF16))
        m_ref[h:h + 1, :] = jnp.broadcast_to(b_row[:, L - 1:L] + mm[:, L - 1:L], (1, LANES))

        scale = lax.rsqrt(jnp.mean(h_t * h_t, axis=0, keepdims=True) + EPS)
        hn = (h_t * scale).T * norm_ref[...]
        hm_ref[:, sl] = (hn * _sigmoid(mo_ref[:, sl].astype(F32))).astype(BF16)


def _mlstm(mqk, mvt, mo, gates, cw, norm, batch, seq):
    t = mqk.shape[0]
    L = MLSTM_CHUNK
    nc = seq // L
    width = MLSTM_HEADS * MLSTM_HEAD_DIM
    row = lambda n: pl.BlockSpec((L, n), lambda b, c: (b * nc + c, 0))
    col = lambda n: pl.BlockSpec((n, L), lambda b, c: (0, b * nc + c))
    shift = _shift_matrix(L, MLSTM_CONV)
    return pl.pallas_call(
        _mlstm_kernel,
        grid=(batch, nc),
        in_specs=[row(2 * width), col(width), row(width), col(LANES), _full(shift.shape), _full(cw.shape),
                  _full(norm.shape)],
        out_specs=row(width),
        out_shape=jax.ShapeDtypeStruct((t, width), BF16),
        scratch_shapes=[
            pltpu.VMEM((HALO, 2 * width), F32),
            pltpu.VMEM((MLSTM_HEADS, MLSTM_HEAD_DIM + SUBLANES, MLSTM_HEAD_DIM), F32),
            pltpu.VMEM((SUBLANES, LANES), F32),
        ],
        compiler_params=_params("parallel", "arbitrary"),
        name="mlstm",
    )(mqk, mvt, mo, gates, shift, cw, norm)


def _diff_attn_kernel(qt_ref, k_ref, vt_ref, lam_ref, norm_ref, o_ref,
                      acc0_ref, acc1_ref, m_ref, sa0_ref, sa1_ref, sb0_ref, sb1_ref, *, lam_init):
    tq = ATTN_BLOCK
    dv = DIFF_V_DIM
    nq = k_ref.shape[1] // tq
    frow = lax.broadcasted_iota(jnp.int32, (dv, tq), 0)
    krow = lax.broadcasted_iota(jnp.int32, (tq, tq), 0)
    qcol = lax.broadcasted_iota(jnp.int32, (tq, tq), 1)
    causal_t = krow <= qcol
    ones = jnp.ones((SUBLANES, tq), BF16)
    accs = (acc0_ref, acc1_ref)
    buf_a = (sa0_ref, sa1_ref)
    buf_b = (sb0_ref, sb1_ref)
    lv = lam_ref[...]
    lam = (jnp.exp(jnp.sum(lv[0:1, :] * lv[1:2, :], axis=-1, keepdims=True))
           - jnp.exp(jnp.sum(lv[2:3, :] * lv[3:4, :], axis=-1, keepdims=True)) + lam_init)

    def q_block(qi, carry):
        qrow = pl.multiple_of(qi * tq, tq)
        qt = qt_ref[:, pl.ds(qrow, tq)]
        zero = jnp.zeros_like(qt)
        qs = (jnp.where(frow < DIFF_HEAD_DIM, qt, zero), jnp.where(frow >= DIFF_HEAD_DIM, qt, zero))
        for acc in accs:
            acc[...] = jnp.zeros_like(acc)
        m_ref[...] = jnp.full(m_ref.shape, -jnp.inf, F32)

        def scores(j, dst):
            kb = k_ref[0, pl.ds(pl.multiple_of(j * tq, tq), tq), :]
            for c in range(2):
                dst[c][...] = _dot(kb, qs[c])

        def absorb(j, src, masked):
            start = pl.multiple_of(j * tq, tq)
            vt1 = jnp.concatenate([vt_ref[:, pl.ds(start, tq)], ones], axis=0)
            s = [src[c][...] for c in range(2)]
            if masked:
                s = [jnp.where(causal_t, sc, -jnp.inf) for sc in s]
            m_old = [m_ref[c:c + 1, :] for c in range(2)]
            m_new = [jnp.maximum(m_old[c], jnp.max(s[c], axis=0, keepdims=True)) for c in range(2)]
            p = [jnp.exp2(s[c] - m_new[c]).astype(BF16) for c in range(2)]
            alpha = [jnp.exp2(m_old[c] - m_new[c]) for c in range(2)]
            pv = [_dot(vt1, p[c]) for c in range(2)]
            for c in range(2):
                accs[c][...] = alpha[c] * accs[c][...] + pv[c]
                m_ref[c:c + 1, :] = m_new[c]

        scores(0, buf_a)

        def pair(jj, inner):
            j = 2 * jj
            scores(j + 1, buf_b)
            absorb(j, buf_a, False)
            scores(j + 2, buf_a)
            absorb(j + 1, buf_b, False)
            return inner

        lax.fori_loop(0, qi // 2, pair, 0)

        @pl.when(qi % 2 == 0)
        def _():
            absorb(qi, buf_a, True)

        @pl.when(qi % 2 == 1)
        def _():
            scores(qi, buf_b)
            absorb(qi - 1, buf_a, False)
            absorb(qi, buf_b, True)

        a1 = acc0_ref[...]
        a2 = acc1_ref[...]
        o_t = a1[0:dv, :] / a1[dv:dv + 1, :] - lam * (a2[0:dv, :] / a2[dv:dv + 1, :])
        o_ref[0, pl.ds(qrow, tq), :] = (_rms(o_t.T, norm_ref[...]) * (1.0 - lam_init)).astype(BF16)
        return carry

    lax.fori_loop(0, nq, q_block, 0)


def _diff_attn(dqt, dk, dvt, lam_vecs, norm, batch, seq, lam_init):
    t = dk.shape[1]
    tq = ATTN_BLOCK
    rows = pl.BlockSpec((1, seq, DIFF_V_DIM), lambda b, h: (h, b, 0))
    cols = pl.BlockSpec((DIFF_V_DIM, seq), lambda b, h: (h, b))
    return pl.pallas_call(
        functools.partial(_diff_attn_kernel, lam_init=lam_init),
        grid=(batch, DIFF_HEADS),
        in_specs=[cols, rows, cols, _full(lam_vecs.shape), _full(norm.shape)],
        out_specs=rows,
        out_shape=jax.ShapeDtypeStruct((DIFF_HEADS, t, DIFF_V_DIM), BF16),
        scratch_shapes=([pltpu.VMEM((DIFF_V_DIM + SUBLANES, tq), F32)] * 2 + [pltpu.VMEM((SUBLANES, tq), F32)]
                        + [pltpu.VMEM((tq, tq), F32)] * 4),
        compiler_params=_params("parallel", "parallel"),
        name="diff_attn",
    )(dqt, dk, dvt, lam_vecs, norm)


ROUTE_ROWS = 2 * SUBLANES
GROUP_LANE = EXPERTS_PER_GROUP
DEST_LANE = EXPERTS_PER_GROUP
EXPERT_LANE0 = 8


def _route(xn, wrt_ref, br_ref):
    xh = xn.astype(BF16)
    xl = (xn - xh.astype(F32)).astype(BF16)
    both = _dot_nt(wrt_ref[...], xh)
    logits = both[0:LANES, :] + (both[LANES:2 * LANES, :] + _dot_nt(wrt_ref[0:LANES, :], xl)) + br_ref[...]
    e = EXPERTS_PER_GROUP
    tm = logits.shape[1]
    row = lax.broadcasted_iota(jnp.int32, (e, tm), 0)
    ninf = -jnp.inf
    big = jnp.int32(LANES)

    is_g = row < N_GROUPS
    gl = jnp.where(is_g, logits[0:e, :], ninf)
    gmax = jnp.max(gl, axis=0, keepdims=True)
    gsum = jnp.sum(jnp.where(is_g, jnp.exp(gl - gmax), 0.0), axis=0, keepdims=True)
    g_val = 1.0 / gsum
    g_idx = jnp.min(jnp.where(gl == gmax, row, big), axis=0, keepdims=True)

    el = logits[EXPERT_LANE0:EXPERT_LANE0 + e, :]
    for g in range(1, N_GROUPS):
        lo = EXPERT_LANE0 + e * g
        el = jnp.where(g_idx == g, logits[lo:lo + e, :], el)
    v1 = jnp.max(el, axis=0, keepdims=True)
    i1 = jnp.min(jnp.where(el == v1, row, big), axis=0, keepdims=True)
    el2 = jnp.where(row == i1, ninf, el)
    v2 = jnp.max(el2, axis=0, keepdims=True)
    i2 = jnp.min(jnp.where(el2 == v2, row, big), axis=0, keepdims=True)
    e21 = jnp.exp(v2 - v1)
    w1 = 1.0 / (1.0 + e21)
    w2 = e21 * w1
    gates = g_val * (jnp.where(row == i1, w1, 0.0) + jnp.where(row == i2, w2, 0.0))
    tail = jnp.where(row == 0, g_idx.astype(F32), 0.0)
    return xh, jnp.concatenate([gates, tail], axis=0)


def _mix0_tail_kernel(x_ref, hm_ref, hd_ref, wo_ref, nf_ref, wrt_ref, br_ref,
                      xmid_ref, xn_ref, route_ref):
    half = hm_ref.shape[1]
    hd = jnp.concatenate([hd_ref[h] for h in range(DIFF_HEADS)], axis=1)
    mix = _dot(hm_ref[...], wo_ref[0:half, :]) + _dot(hd, wo_ref[half:2 * half, :])
    xmid = x_ref[...] + mix
    xmid_ref[...] = xmid
    xn_ref[...], route_ref[...] = _route(_rms(xmid, nf_ref[...]), wrt_ref, br_ref)


def _mix0_tail(x, hm, hd, wo, nf, wrt, br):
    t, d = x.shape
    tm = ROW_TILE
    row = lambda n: pl.BlockSpec((tm, n), lambda i: (i, 0))
    return pl.pallas_call(
        _mix0_tail_kernel,
        grid=(t // tm,),
        in_specs=[row(d), row(hm.shape[1]), pl.BlockSpec((DIFF_HEADS, tm, DIFF_V_DIM), lambda i: (0, i, 0)),
                  _full(wo.shape), _full(nf.shape), _full(wrt.shape), _full(br.shape)],
        out_specs=[row(d), row(d), pl.BlockSpec((ROUTE_ROWS, tm), lambda i: (0, i))],
        out_shape=[jax.ShapeDtypeStruct((t, d), F32), jax.ShapeDtypeStruct((t, d), BF16),
                   jax.ShapeDtypeStruct((ROUTE_ROWS, t), F32)],
        compiler_params=_params("parallel"),
        name="mix0_tail",
    )(x, hm, hd, wo, nf, wrt, br)


def _proj1_kernel(x_ref, y_ref, g_ref, w_ref, x1_ref, b_ref, u_ref):
    d = x_ref.shape[1]
    x1 = x_ref[...] + y_ref[...].astype(F32)
    x1_ref[...] = x1
    xb = _rms(x1, g_ref[...]).astype(BF16)
    b_ref[...] = _dot(xb, w_ref[:, 0:d]).astype(BF16)
    u_ref[...] = (_dot(xb, w_ref[:, d:2 * d]) * _dot(xb, w_ref[:, 2 * d:3 * d])).astype(BF16)


def _proj1(x, y, g, w):
    t, d = x.shape
    tm = ROW_TILE
    row = lambda n: pl.BlockSpec((tm, n), lambda i: (i, 0))
    return pl.pallas_call(
        _proj1_kernel,
        grid=(t // tm,),
        in_specs=[row(d), row(d), _full(g.shape), _full(w.shape)],
        out_specs=[row(d), row(d), row(d)],
        out_shape=[jax.ShapeDtypeStruct((t, d), F32), jax.ShapeDtypeStruct((t, d), BF16),
                   jax.ShapeDtypeStruct((t, d), BF16)],
        compiler_params=_params("parallel"),
        name="proj1",
    )(x, y, g, w)


def _mix1_tail_kernel(x_ref, b_ref, u_ref, halo_ref, cw_ref, wo_ref, nf_ref, wrt_ref, br_ref,
                      xmid_ref, xn_ref, route_ref, *, tiles_per_seq):
    seq_start = pl.program_id(0) % tiles_per_seq == 0
    halo = halo_ref[...].astype(F32)
    tail = jnp.where(seq_start, jnp.zeros_like(halo), halo)
    conv = _roll_conv(u_ref[...], cw_ref[...], tail, CONV_WIDTH)
    mixed = (b_ref[...].astype(F32) * conv).astype(BF16)
    xmid = x_ref[...] + _dot(mixed, wo_ref[...])
    xmid_ref[...] = xmid
    xn_ref[...], route_ref[...] = _route(_rms(xmid, nf_ref[...]), wrt_ref, br_ref)


def _mix1_tail(x, b, u, cw, wo, nf, wrt, br, seq):
    t, d = x.shape
    tm = CONV_ROW_TILE
    row = lambda n: pl.BlockSpec((tm, n), lambda i: (i, 0))
    halo = pl.BlockSpec((HALO, d), lambda i: (jnp.maximum(i * (tm // HALO) - 1, 0), 0))
    return pl.pallas_call(
        functools.partial(_mix1_tail_kernel, tiles_per_seq=seq // tm),
        grid=(t // tm,),
        in_specs=[row(d), row(d), row(d), halo, _full(cw.shape), _full(wo.shape),
                  _full(nf.shape), _full(wrt.shape), _full(br.shape)],
        out_specs=[row(d), row(d), pl.BlockSpec((ROUTE_ROWS, tm), lambda i: (0, i))],
        out_shape=[jax.ShapeDtypeStruct((t, d), F32), jax.ShapeDtypeStruct((t, d), BF16),
                   jax.ShapeDtypeStruct((ROUTE_ROWS, t), F32)],
        compiler_params=_params("parallel"),
        name="mix1_tail",
    )(x, b, u, u, cw, wo, nf, wrt, br)


def _moe_kernel(off_ref, nsub_ref, x_ref, r_ref, wg_ref, wu_ref, wd_ref, y_ref,
                xs_ref, gs_ref, ys_ref, dest_ref):
    i = pl.program_id(0)
    g = pl.program_id(1)
    tb = MOE_BLOCK
    rows = MOE_ROWS
    ch = MOE_CHUNK
    d = x_ref.shape[1]
    tail_row = rows - ch
    last_group = i * N_GROUPS + N_GROUPS - 1
    used = off_ref[last_group] + nsub_ref[last_group] * MOE_SUB

    @pl.when(g == 0)
    def _():
        r = r_ref[...]
        grp = r[GROUP_LANE:GROUP_LANE + 1, :]
        row8 = lax.broadcasted_iota(jnp.int32, (SUBLANES, tb), 0)
        onehot = jnp.where(row8.astype(F32) == grp, 1.0, 0.0)
        row1 = lax.broadcasted_iota(jnp.int32, (SUBLANES, 1), 0)
        base = jnp.zeros((SUBLANES, 1), F32)
        for gg in range(N_GROUPS):
            base = jnp.where(row1 == gg, off_ref[i * N_GROUPS + gg].astype(F32), base)
        rr = lax.broadcasted_iota(jnp.int32, (ch, ch), 0)
        cc = lax.broadcasted_iota(jnp.int32, (ch, ch), 1)
        upper = jnp.where(rr <= cc, 1.0, 0.0).astype(BF16)
        parts = []
        for c in range(tb // ch):
            sel = onehot[:, c * ch:(c + 1) * ch]
            rank = _dot(sel.astype(BF16), upper) + base
            parts.append(jnp.sum(sel * rank, axis=0, keepdims=True) - 1.0)
            base = base + jnp.sum(sel, axis=1, keepdims=True)
        dest_row = jnp.concatenate(parts, axis=1)
        side = jnp.concatenate([r[0:SUBLANES, :], jnp.broadcast_to(dest_row, (SUBLANES, tb)),
                                jnp.zeros((LANES - 2 * SUBLANES, tb), F32)], axis=0).T
        dest_ref[...] = side
        lane = lax.broadcasted_iota(jnp.int32, (tb, LANES), 1)
        gates = jnp.where(lane < EXPERTS_PER_GROUP, side, 0.0)
        g_hi = gates.astype(BF16)
        g_lo = (gates - g_hi.astype(F32)).astype(BF16)
        xcat = jnp.concatenate([x_ref[...], g_hi, g_lo], axis=1)

        def move(c):
            rr = (lax.broadcasted_iota(jnp.int32, (ch, tb), 0) + c * ch).astype(F32)
            p = jnp.where(rr == dest_row, 1.0, 0.0).astype(BF16)
            moved = _dot(p, xcat)
            xs_ref[c * ch:(c + 1) * ch, :] = moved[:, 0:d].astype(BF16)
            gs_ref[c * ch:(c + 1) * ch, :] = moved[:, d:d + LANES] + moved[:, d + LANES:d + 2 * LANES]

        for c in range(rows // ch - 1):
            move(c)

        @pl.when(used > tail_row)
        def _():
            move(rows // ch - 1)

        ys_ref[...] = jnp.zeros_like(ys_ref)

    off = off_ref[i * N_GROUPS + g]
    n_sub = nsub_ref[i * N_GROUPS + g]

    def expert_rows(r0, size):
        xt = xs_ref[pl.ds(r0, size), :]
        gt = gs_ref[pl.ds(r0, size), :]
        acc = jnp.zeros((size, d), F32)
        for e in range(EXPERTS_PER_GROUP):
            hg = _dot(xt, wg_ref[0, 0, e])
            up = _dot(xt, wu_ref[0, 0, e])
            act = hg * _sigmoid(hg) * up * gt[:, e:e + 1]
            acc = acc + _dot(act.astype(BF16), wd_ref[0, 0, e])
        ys_ref[pl.ds(r0, size), :] = acc.astype(BF16)

    def pair(j, carry):
        expert_rows(pl.multiple_of(off + j * (2 * MOE_SUB), MOE_SUB), 2 * MOE_SUB)
        return carry

    lax.fori_loop(0, n_sub // 2, pair, 0)

    @pl.when(n_sub % 2 == 1)
    def _():
        expert_rows(pl.multiple_of(off + (n_sub - 1) * MOE_SUB, MOE_SUB), MOE_SUB)

    @pl.when(g == N_GROUPS - 1)
    def _():
        def pick(c, lo, hi):
            cc = (lax.broadcasted_iota(jnp.int32, (ch, hi - lo), 1) + lo).astype(F32)
            dest = dest_ref[c * ch:(c + 1) * ch, DEST_LANE:DEST_LANE + 1]
            return _dot(jnp.where(cc == dest, 1.0, 0.0).astype(BF16), ys_ref[lo:hi, :])

        for c in range(tb // ch):
            y_ref[c * ch:(c + 1) * ch, :] = pick(c, 0, tail_row).astype(y_ref.dtype)

        @pl.when(used > tail_row)
        def _():
            for c in range(tb // ch):
                sl = slice(c * ch, (c + 1) * ch)
                y_ref[sl, :] = (y_ref[sl, :].astype(F32) + pick(c, tail_row, rows)).astype(y_ref.dtype)


def _hier_moe(xn, route, wg, wu, wd, layer):
    t, d = xn.shape
    tb = MOE_BLOCK
    nb = t // tb
    e, ff = wg.shape[2], wg.shape[4]
    group = route[GROUP_LANE, :].astype(jnp.int32).reshape(nb, tb)
    counts = jnp.sum((group[:, :, None] == jnp.arange(N_GROUPS)[None, None, :]).astype(jnp.int32), axis=1)
    nsub = (counts + MOE_SUB - 1) // MOE_SUB
    padded = nsub * MOE_SUB
    off = jnp.cumsum(padded, axis=1) - padded
    grid_spec = pltpu.PrefetchScalarGridSpec(
        num_scalar_prefetch=2,
        grid=(nb, N_GROUPS),
        in_specs=[
            pl.BlockSpec((tb, d), lambda i, g, *_: (i, 0)),
            pl.BlockSpec((ROUTE_ROWS, tb), lambda i, g, *_: (0, i)),
            pl.BlockSpec((1, 1, e, d, ff), lambda i, g, *_: (layer, g, 0, 0, 0)),
            pl.BlockSpec((1, 1, e, d, ff), lambda i, g, *_: (layer, g, 0, 0, 0)),
            pl.BlockSpec((1, 1, e, ff, d), lambda i, g, *_: (layer, g, 0, 0, 0)),
        ],
        out_specs=pl.BlockSpec((tb, d), lambda i, g, *_: (i, 0)),
        scratch_shapes=[
            pltpu.VMEM((MOE_ROWS, d), BF16),
            pltpu.VMEM((MOE_ROWS, LANES), F32),
            pltpu.VMEM((MOE_ROWS, d), BF16),
            pltpu.VMEM((tb, LANES), F32),
        ],
    )
    return pl.pallas_call(
        _moe_kernel,
        grid_spec=grid_spec,
        out_shape=jax.ShapeDtypeStruct((t, d), BF16),
        compiler_params=_params("parallel", "arbitrary"),
        name="moe",
    )(off.reshape(-1).astype(jnp.int32), nsub.reshape(-1).astype(jnp.int32), xn, route, wg, wu, wd)


def _final_kernel(x_ref, y_ref, g_ref, o_ref):
    o_ref[...] = _rms(x_ref[...] + y_ref[...].astype(F32), g_ref[...])


def _final(x, y, g):
    t, d = x.shape
    tm = ROW_TILE
    row = pl.BlockSpec((tm, d), lambda i: (i, 0))
    return pl.pallas_call(
        _final_kernel,
        grid=(t // tm,),
        in_specs=[row, row, _full(g.shape)],
        out_specs=row,
        out_shape=jax.ShapeDtypeStruct((t, d), F32),
        compiler_params=_params("parallel"),
        name="final_norm",
    )(x, y, g)


def _hi_lo(w):
    hi = w.astype(BF16)
    return hi, (w - hi.astype(F32)).astype(BF16)


def _router_params(w_rg, b_rg, w_re, b_re):
    d = w_rg.shape[0]
    n_e = N_GROUPS * EXPERTS_PER_GROUP
    pad = jnp.zeros((d, EXPERT_LANE0 - N_GROUPS), F32)
    tail = jnp.zeros((d, LANES - EXPERT_LANE0 - n_e), F32)
    w = jnp.concatenate([w_rg, pad, w_re, tail], axis=1)
    b = jnp.concatenate([b_rg, pad[0], b_re, tail[0]]).reshape(LANES, 1)
    hi, lo = _hi_lo(w)
    return jnp.concatenate([hi, lo], axis=1).T, b


def kernel(x, positions, norm_mix, norm_ffn, ab_w_in, ab_gate_bias, ab_conv_w, ab_mlstm_norm, ab_lambda,
           ab_diff_norm, ab_w_out, c_w_in, c_conv_w, c_w_out, moe_w_rg, moe_b_rg, moe_w_re, moe_b_re,
           moe_w_gate, moe_w_up, moe_w_down, final_norm):
    batch, seq, d = x.shape
    t = batch * seq
    xf = x.reshape(t, d)
    row2 = lambda v: v.reshape(1, -1).astype(F32)

    w_in = ab_w_in[0]
    mw = 2 * MLSTM_HEADS * MLSTM_HEAD_DIM
    hw = MLSTM_HEADS * MLSTM_HEAD_DIM
    ng = 2 * MLSTM_HEADS
    dw = DIFF_HEADS * DIFF_V_DIM
    o_g = mw + 2 * hw
    o_q = o_g + ng
    o_k, o_dv = o_q + dw, o_q + 2 * dw
    w0 = jnp.concatenate([w_in[:, :mw], w_in[:, mw + hw:o_g], w_in[:, o_k:o_dv]], axis=1).astype(BF16)
    w_gates = jnp.concatenate([w_in[:, o_g:o_q], jnp.zeros((d, LANES - ng), F32)], axis=1)
    wgh, wgl = _hi_lo(w_gates)
    wvt = jnp.concatenate([w_in[:, mw:mw + hw].astype(BF16), w_in[:, o_dv:o_dv + dw].astype(BF16),
                           w_in[:, o_q:o_k].astype(BF16), wgh, wgl], axis=1).T
    gb = jnp.concatenate([ab_gate_bias[0].astype(F32), jnp.zeros((LANES - ng,), F32)]).reshape(LANES, 1)

    cost, sint = _rope_table(positions.reshape(t).astype(jnp.int32))
    mqk, mo, dqt, dk, mvt, dvt, gates = _proj0(xf, cost, sint, row2(norm_mix[0]), w0, wvt, gb)
    hm = _mlstm(mqk, mvt, mo, gates, ab_conv_w[0].astype(F32), row2(ab_mlstm_norm[0]), batch, seq)
    lam_init = 0.8 - 0.6 * math.exp(-0.3 * 0)
    hd = _diff_attn(dqt, dk, dvt, ab_lambda[0].astype(F32), row2(ab_diff_norm[0]), batch, seq, lam_init)

    wrt, br = _router_params(moe_w_rg[0], moe_b_rg[0], moe_w_re[0], moe_b_re[0])
    xmid, xn, route = _mix0_tail(xf, hm, hd, ab_w_out[0].astype(BF16), row2(norm_ffn[0]), wrt, br)
    wg, wu, wd = moe_w_gate.astype(BF16), moe_w_up.astype(BF16), moe_w_down.astype(BF16)
    y = _hier_moe(xn, route, wg, wu, wd, 0)

    x1, bgate, u = _proj1(xmid, y, row2(norm_mix[1]), c_w_in[0].astype(BF16))
    wrt, br = _router_params(moe_w_rg[1], moe_b_rg[1], moe_w_re[1], moe_b_re[1])
    xmid, xn, route = _mix1_tail(x1, bgate, u, c_conv_w[0].astype(F32), c_w_out[0].astype(BF16),
                                 row2(norm_ffn[1]), wrt, br, seq)
    y = _hier_moe(xn, route, wg, wu, wd, 1)

    return _final(xmid, y, row2(final_norm)).reshape(batch, seq, d)
```

```python
import functools
import math

import jax
import jax.numpy as jnp
from jax import lax
from jax.experimental import pallas as pl
from jax.experimental.pallas import tpu as pltpu

F32 = jnp.float32
BF16 = jnp.bfloat16

EPS = 1e-6
LOG2E = math.log2(math.e)
ROPE_THETA = 10000.0
LANES = 128
SUBLANES = 8
MLSTM_HEADS = 4
MLSTM_HEAD_DIM = 128
MLSTM_CONV = 4
DIFF_HEADS = 4
DIFF_HEAD_DIM = 64
DIFF_V_DIM = 128
CONV_WIDTH = 3
N_GROUPS = 4
EXPERTS_PER_GROUP = 8
EXPERT_FF = 256
HALO = 8

ROW_TILE = 1024
CONV_ROW_TILE = 512
MLSTM_CHUNK = 256
ATTN_BLOCK = 512
MOE_BLOCK = 1024
MOE_SUB = 128
MOE_ROWS = MOE_BLOCK + N_GROUPS * MOE_SUB
MOE_CHUNK = 256
VMEM_LIMIT = 56 * 1024 * 1024


def _dot(a, b):
    return jnp.dot(a, b, preferred_element_type=F32)


def _dot_nt(a, b):
    return lax.dot_general(a, b, (((1,), (1,)), ((), ())), preferred_element_type=F32)


def _dot_tn(a, b):
    return lax.dot_general(a, b, (((0,), (0,)), ((), ())), preferred_element_type=F32)


def _split3(x):
    hi = x.astype(BF16)
    r1 = x - hi.astype(F32)
    mid = r1.astype(BF16)
    lo = (r1 - mid.astype(F32)).astype(BF16)
    return hi, mid, lo


def _dot_f32ish(x, w_hi, w_lo):
    xh = x.astype(BF16)
    xl = (x - xh.astype(F32)).astype(BF16)
    return _dot(xh, w_hi) + (_dot(xl, w_hi) + _dot(xh, w_lo))


def _rms(x, g):
    return x * lax.rsqrt(jnp.mean(x * x, axis=-1, keepdims=True) + EPS) * g


def _sigmoid(x):
    return 0.5 * jnp.tanh(0.5 * x) + 0.5


def _shift_matrix(rows, taps):
    r = jnp.arange(rows)
    return jnp.concatenate([r[:, None] - s == r[None, :] for s in range(1, taps)], axis=0).astype(BF16)


def _shift_conv(xb, shift_ref, cw, tail, taps):
    rows, c = xb.shape
    x32 = xb.astype(F32)
    shifted = _dot(shift_ref[...], xb)
    last = taps - 1
    conv = cw[last:last + 1, :] * x32
    for s in range(1, taps):
        conv = conv + cw[last - s:last - s + 1, :] * shifted[(s - 1) * rows:s * rows, :]
    head = jnp.concatenate([tail, x32[0:HALO, :]], axis=0)
    rowi = lax.broadcasted_iota(jnp.int32, (HALO, c), 0)
    fix = jnp.zeros((HALO, c), F32)
    for s in range(1, taps):
        fix = fix + cw[last - s:last - s + 1, :] * jnp.where(rowi < s, head[HALO - s:2 * HALO - s, :], 0.0)
    return jnp.concatenate([conv[0:HALO, :] + fix, conv[HALO:, :]], axis=0)


def _roll_conv(xb, cw, tail, taps):
    rows, c = xb.shape
    x32 = xb.astype(F32)
    last = taps - 1
    conv = cw[last:last + 1, :] * x32
    head = jnp.concatenate([tail, x32[0:HALO, :]], axis=0)
    first = cw[last:last + 1, :] * x32[0:HALO, :]
    for s in range(1, taps):
        w = cw[last - s:last - s + 1, :]
        conv = conv + w * pltpu.roll(x32, s, 0)
        first = first + w * head[HALO - s:2 * HALO - s, :]
    return jnp.concatenate([first, conv[HALO:, :]], axis=0)


def _params(*sem):
    return pltpu.CompilerParams(dimension_semantics=sem, vmem_limit_bytes=VMEM_LIMIT)


def _full(shape):
    return pl.BlockSpec(shape, lambda *_: (0,) * len(shape))


def _proj0_kernel(x_ref, cos_ref, sin_ref, cost_ref, sint_ref, g_ref, w_ref, wvt_ref, gb_ref,
                  mqk_ref, mo_ref, dqt_ref, dk_ref, mvt_ref, dvt_ref, gates_ref):
    xn = _rms(x_ref[...], g_ref[...])
    xb = xn.astype(BF16)
    xl = (xn - xb.astype(F32)).astype(BF16)
    half = DIFF_HEAD_DIM // 2
    dw = DIFF_HEADS * DIFF_V_DIM

    def seg(lo, hi):
        return _dot(xb, w_ref[:, lo:hi])

    mqk_ref[...] = seg(0, 1024).astype(BF16)
    mo_ref[...] = seg(1024, 1536).astype(BF16)
    vt = _dot_nt(wvt_ref[...], xb)
    mvt_ref[...] = vt[0:dw, :].astype(BF16)
    dvt_ref[...] = vt[dw:2 * dw, :].astype(BF16)
    g0 = 3 * dw
    gates_ref[...] = (vt[g0:g0 + LANES, :] + (vt[g0 + LANES:g0 + 2 * LANES, :]
                      + _dot_nt(wvt_ref[g0:g0 + LANES, :], xl))) + gb_ref[...]

    ct = cost_ref[...]
    st = sint_ref[...]
    q_scale = LOG2E * DIFF_HEAD_DIM ** -0.5
    for r in range(2 * DIFF_HEADS):
        lo = 2 * dw + r * DIFF_HEAD_DIM
        x1 = vt[lo:lo + half, :]
        x2 = vt[lo + half:lo + 2 * half, :]
        out = lo - 2 * dw
        dqt_ref[out:out + half, :] = ((x1 * ct - x2 * st) * q_scale).astype(BF16)
        dqt_ref[out + half:out + 2 * half, :] = ((x2 * ct + x1 * st) * q_scale).astype(BF16)

    heads = 2 * DIFF_HEADS
    c = cos_ref[...]
    s = sin_ref[...]
    c4 = jnp.concatenate([c, c] * heads, axis=1)
    s4 = jnp.concatenate([-s, s] * heads, axis=1)
    lane4 = lax.broadcasted_iota(jnp.int32, c4.shape, 1)
    first4 = (lane4 & (DIFF_HEAD_DIM - 1)) < half
    k = seg(1536, 2048)
    partner = jnp.where(first4, pltpu.roll(k, dw - half, 1), pltpu.roll(k, half, 1))
    dk = (k * c4 + partner * s4).astype(BF16)
    for h in range(DIFF_HEADS):
        dk_ref[h] = dk[:, h * DIFF_V_DIM:(h + 1) * DIFF_V_DIM]


def _rope_table_kernel(pos_ref, invf_ref, cos_ref, sin_ref):
    ang = invf_ref[...] * pos_ref[...].astype(F32)
    cos_ref[...] = jnp.cos(ang)
    sin_ref[...] = jnp.sin(ang)


def _rope_table(positions):
    t = positions.shape[0]
    half = DIFF_HEAD_DIM // 2
    inv_freq = (1.0 / (ROPE_THETA ** (jnp.arange(half, dtype=F32) / half))).reshape(half, 1)
    tl = min(t, 4096)
    return pl.pallas_call(
        _rope_table_kernel,
        grid=(t // tl,),
        in_specs=[pl.BlockSpec((1, tl), lambda i: (0, i)), _full(inv_freq.shape)],
        out_specs=[pl.BlockSpec((half, tl), lambda i: (0, i))] * 2,
        out_shape=[jax.ShapeDtypeStruct((half, t), F32)] * 2,
        compiler_params=_params("parallel"),
        name="rope_table",
    )(positions.reshape(1, t), inv_freq)


def _proj0(x, cost, sint, g, w, wvt, gb):
    t, d = x.shape
    tm = ROW_TILE
    row = lambda n: pl.BlockSpec((tm, n), lambda i: (i, 0))
    col = lambda n: pl.BlockSpec((n, tm), lambda i: (0, i))
    heads = pl.BlockSpec((DIFF_HEADS, tm, DIFF_V_DIM), lambda i: (0, i, 0))
    cos, sin = cost.T, sint.T
    half = cost.shape[0]
    out_shapes = [
        jax.ShapeDtypeStruct((t, 1024), BF16), jax.ShapeDtypeStruct((t, 512), BF16),
        jax.ShapeDtypeStruct((512, t), BF16), jax.ShapeDtypeStruct((DIFF_HEADS, t, DIFF_V_DIM), BF16),
        jax.ShapeDtypeStruct((512, t), BF16), jax.ShapeDtypeStruct((512, t), BF16),
        jax.ShapeDtypeStruct((LANES, t), F32),
    ]
    return pl.pallas_call(
        _proj0_kernel,
        grid=(t // tm,),
        in_specs=[row(d), row(half), row(half), col(half), col(half), _full(g.shape), _full(w.shape),
                  _full(wvt.shape), _full(gb.shape)],
        out_specs=[row(1024), row(512), col(512), heads, col(512), col(512), col(LANES)],
        out_shape=out_shapes,
        compiler_params=_params("parallel"),
        name="proj0",
    )(x, cos, sin, cost, sint, g, w, wvt, gb)


def _mlstm_kernel(mqk_ref, mvt_ref, mo_ref, gates_ref, shift_ref, cw_ref, norm_ref, hm_ref,
                  tail_ref, cn_ref, m_ref):
    L = MLSTM_CHUNK
    hd = MLSTM_HEAD_DIM
    width = MLSTM_HEADS * hd

    @pl.when(pl.program_id(1) == 0)
    def _():
        tail_ref[...] = jnp.zeros_like(tail_ref)
        cn_ref[...] = jnp.zeros_like(cn_ref)
        m_ref[...] = jnp.zeros_like(m_ref)

    conv = _shift_conv(mqk_ref[...], shift_ref, cw_ref[...], tail_ref[...], MLSTM_CONV)
    tail_ref[...] = mqk_ref[L - HALO:L, :].astype(F32)
    qk = conv * _sigmoid(conv)
    qb = (qk[:, :width] * (hd ** -0.5)).astype(BF16)
    k = qk[:, width:]
    kb = k.astype(BF16)

    g8 = gates_ref[0:SUBLANES, :]
    logf = (jnp.minimum(g8, 0.0) - jnp.log(1.0 + jnp.exp(-jnp.abs(g8)))) * LOG2E
    krow = lax.broadcasted_iota(jnp.int32, (L, L), 0)
    qcol = lax.broadcasted_iota(jnp.int32, (L, L), 1)
    causal_t = krow <= qcol
    triu = jnp.where(causal_t, 1.0, 0.0).astype(BF16)
    f_hi, f_mid, f_lo = _split3(logf)
    b8 = _dot(f_hi, triu) + (_dot(f_mid, triu) + _dot(f_lo, triu))
    cj8 = g8 * LOG2E - pltpu.roll(b8, MLSTM_HEADS, 0)
    cj_all = jnp.concatenate([cj8, jnp.zeros((LANES - SUBLANES, L), F32)], axis=0).T
    ones = jnp.ones((SUBLANES, L), BF16)

    for h in range(MLSTM_HEADS):
        sl = slice(h * hd, (h + 1) * hd)
        b_row = b8[MLSTM_HEADS + h:MLSTM_HEADS + h + 1, :]
        m_prev = m_ref[h:h + 1, 0:1]
        cn_prev = cn_ref[h]
        vt1 = jnp.concatenate([mvt_ref[sl, :], ones], axis=0)

        cmat = jnp.where(causal_t, cj_all[:, h:h + 1], -jnp.inf)
        mm = jnp.maximum(m_prev, jnp.max(cmat, axis=0, keepdims=True))
        w_t = jnp.exp2(cmat - mm)
        s_t = _dot_nt(kb[:, sl], qb[:, sl]) * w_t
        inter_w = jnp.exp2(m_prev - mm)
        tot = _dot(vt1, s_t.astype(BF16)) + inter_w * _dot_nt(cn_prev.astype(BF16), qb[:, sl])
        den = tot[hd:hd + 1, :]
        h_t = tot[0:hd, :] / jnp.maximum(jnp.abs(den), jnp.exp2(-(b_row + mm)))

        kw = k[:, sl] * w_t[:, L - 1:L]
        cn_ref[h] = inter_w[:, L - 1:L] * cn_prev + _dot(vt1, kw.astype(BF16))
        m_ref[h:h + 1, :] = jnp.broadcast_to(b_row[:, L - 1:L] + mm[:, L - 1:L], (1, LANES))

        scale = lax.rsqrt(jnp.mean(h_t * h_t, axis=0, keepdims=True) + EPS)
        hn = (h_t * scale).T * norm_ref[...]
        hm_ref[:, sl] = (hn * _sigmoid(mo_ref[:, sl].astype(F32))).astype(BF16)


def _mlstm(mqk, mvt, mo, gates, cw, norm, batch, seq):
    t = mqk.shape[0]
    L = MLSTM_CHUNK
    nc = seq // L
    width = MLSTM_HEADS * MLSTM_HEAD_DIM
    row = lambda n: pl.BlockSpec((L, n), lambda b, c: (b * nc + c, 0))
    col = lambda n: pl.BlockSpec((n, L), lambda b, c: (0, b * nc + c))
    shift = _shift_matrix(L, MLSTM_CONV)
    return pl.pallas_call(
        _mlstm_kernel,
        grid=(batch, nc),
        in_specs=[row(2 * width), col(width), row(width), col(LANES), _full(shift.shape), _full(cw.shape),
                  _full(norm.shape)],
        out_specs=row(width),
        out_shape=jax.ShapeDtypeStruct((t, width), BF16),
        scratch_shapes=[
            pltpu.VMEM((HALO, 2 * width), F32),
            pltpu.VMEM((MLSTM_HEADS, MLSTM_HEAD_DIM + SUBLANES, MLSTM_HEAD_DIM), F32),
            pltpu.VMEM((SUBLANES, LANES), F32),
        ],
        compiler_params=_params("parallel", "arbitrary"),
        name="mlstm",
    )(mqk, mvt, mo, gates, shift, cw, norm)


def _diff_attn_kernel(qt_ref, k_ref, vt_ref, lam_ref, norm_ref, o_ref,
                      acc0_ref, acc1_ref, m_ref, sa0_ref, sa1_ref, sb0_ref, sb1_ref, *, lam_init):
    tq = ATTN_BLOCK
    dv = DIFF_V_DIM
    nq = k_ref.shape[1] // tq
    frow = lax.broadcasted_iota(jnp.int32, (dv, tq), 0)
    krow = lax.broadcasted_iota(jnp.int32, (tq, tq), 0)
    qcol = lax.broadcasted_iota(jnp.int32, (tq, tq), 1)
    causal_t = krow <= qcol
    ones = jnp.ones((SUBLANES, tq), BF16)
    accs = (acc0_ref, acc1_ref)
    buf_a = (sa0_ref, sa1_ref)
    buf_b = (sb0_ref, sb1_ref)
    lv = lam_ref[...]
    lam = (jnp.exp(jnp.sum(lv[0:1, :] * lv[1:2, :], axis=-1, keepdims=True))
           - jnp.exp(jnp.sum(lv[2:3, :] * lv[3:4, :], axis=-1, keepdims=True)) + lam_init)

    def queries(qi):
        qt = qt_ref[:, pl.ds(pl.multiple_of(qi * tq, tq), tq)]
        zero = jnp.zeros_like(qt)
        return (jnp.where(frow < DIFF_HEAD_DIM, qt, zero), jnp.where(frow >= DIFF_HEAD_DIM, qt, zero))

    def scores_of(qs, j, dst):
        kb = k_ref[0, pl.ds(pl.multiple_of(j * tq, tq), tq), :]
        for c in range(2):
            dst[c][...] = _dot(kb, qs[c])

    def reset():
        for acc in accs:
            acc[...] = jnp.zeros_like(acc)
        m_ref[...] = jnp.full(m_ref.shape, -jnp.inf, F32)

    reset()
    scores_of(queries(0), 0, buf_a)

    def q_block(qi, carry):
        qrow = pl.multiple_of(qi * tq, tq)
        qs = queries(qi)

        def scores(j, dst):
            scores_of(qs, j, dst)

        def absorb(j, src, masked):
            start = pl.multiple_of(j * tq, tq)
            vt1 = jnp.concatenate([vt_ref[:, pl.ds(start, tq)], ones], axis=0)
            s = [src[c][...] for c in range(2)]
            if masked:
                s = [jnp.where(causal_t, sc, -jnp.inf) for sc in s]
            m_old = [m_ref[c:c + 1, :] for c in range(2)]
            m_new = [jnp.maximum(m_old[c], jnp.max(s[c], axis=0, keepdims=True)) for c in range(2)]
            p = [jnp.exp2(s[c] - m_new[c]).astype(BF16) for c in range(2)]
            alpha = [jnp.exp2(m_old[c] - m_new[c]) for c in range(2)]
            pv = [_dot(vt1, p[c]) for c in range(2)]
            for c in range(2):
                accs[c][...] = alpha[c] * accs[c][...] + pv[c]
                m_ref[c:c + 1, :] = m_new[c]

        def pair(jj, inner):
            j = 2 * jj
            scores(j + 1, buf_b)
            absorb(j, buf_a, False)
            scores(j + 2, buf_a)
            absorb(j + 1, buf_b, False)
            return inner

        lax.fori_loop(0, qi // 2, pair, 0)

        @pl.when(qi % 2 == 0)
        def _():
            absorb(qi, buf_a, True)

        @pl.when(qi % 2 == 1)
        def _():
            scores(qi, buf_b)
            absorb(qi - 1, buf_a, False)
            absorb(qi, buf_b, True)

        scores_of(queries(jnp.minimum(qi + 1, nq - 1)), 0, buf_a)

        a1 = acc0_ref[...]
        a2 = acc1_ref[...]
        o_t = a1[0:dv, :] / a1[dv:dv + 1, :] - lam * (a2[0:dv, :] / a2[dv:dv + 1, :])
        o_ref[0, pl.ds(qrow, tq), :] = (_rms(o_t.T, norm_ref[...]) * (1.0 - lam_init)).astype(BF16)
        reset()
        return carry

    lax.fori_loop(0, nq, q_block, 0)


def _diff_attn(dqt, dk, dvt, lam_vecs, norm, batch, seq, lam_init):
    t = dk.shape[1]
    tq = ATTN_BLOCK
    rows = pl.BlockSpec((1, seq, DIFF_V_DIM), lambda b, h: (h, b, 0))
    cols = pl.BlockSpec((DIFF_V_DIM, seq), lambda b, h: (h, b))
    return pl.pallas_call(
        functools.partial(_diff_attn_kernel, lam_init=lam_init),
        grid=(batch, DIFF_HEADS),
        in_specs=[cols, rows, cols, _full(lam_vecs.shape), _full(norm.shape)],
        out_specs=rows,
        out_shape=jax.ShapeDtypeStruct((DIFF_HEADS, t, DIFF_V_DIM), BF16),
        scratch_shapes=([pltpu.VMEM((DIFF_V_DIM + SUBLANES, tq), F32)] * 2 + [pltpu.VMEM((SUBLANES, tq), F32)]
                        + [pltpu.VMEM((tq, tq), F32)] * 4),
        compiler_params=_params("parallel", "parallel"),
        name="diff_attn",
    )(dqt, dk, dvt, lam_vecs, norm)


ROUTE_ROWS = 2 * SUBLANES
GROUP_LANE = EXPERTS_PER_GROUP
DEST_LANE = EXPERTS_PER_GROUP
EXPERT_LANE0 = 8


def _route(xn, wrt_ref, br_ref):
    xh = xn.astype(BF16)
    xl = (xn - xh.astype(F32)).astype(BF16)
    both = _dot_nt(wrt_ref[...], xh)
    logits = both[0:LANES, :] + (both[LANES:2 * LANES, :] + _dot_nt(wrt_ref[0:LANES, :], xl)) + br_ref[...]
    e = EXPERTS_PER_GROUP
    tm = logits.shape[1]
    row = lax.broadcasted_iota(jnp.int32, (e, tm), 0)
    ninf = -jnp.inf
    big = jnp.int32(LANES)

    is_g = row < N_GROUPS
    gl = jnp.where(is_g, logits[0:e, :], ninf)
    gmax = jnp.max(gl, axis=0, keepdims=True)
    gsum = jnp.sum(jnp.where(is_g, jnp.exp(gl - gmax), 0.0), axis=0, keepdims=True)
    g_val = 1.0 / gsum
    g_idx = jnp.min(jnp.where(gl == gmax, row, big), axis=0, keepdims=True)

    el = logits[EXPERT_LANE0:EXPERT_LANE0 + e, :]
    for g in range(1, N_GROUPS):
        lo = EXPERT_LANE0 + e * g
        el = jnp.where(g_idx == g, logits[lo:lo + e, :], el)
    v1 = jnp.max(el, axis=0, keepdims=True)
    i1 = jnp.min(jnp.where(el == v1, row, big), axis=0, keepdims=True)
    el2 = jnp.where(row == i1, ninf, el)
    v2 = jnp.max(el2, axis=0, keepdims=True)
    i2 = jnp.min(jnp.where(el2 == v2, row, big), axis=0, keepdims=True)
    e21 = jnp.exp(v2 - v1)
    w1 = 1.0 / (1.0 + e21)
    w2 = e21 * w1
    gates = g_val * (jnp.where(row == i1, w1, 0.0) + jnp.where(row == i2, w2, 0.0))
    tail = jnp.where(row == 0, g_idx.astype(F32), 0.0)
    return xh, jnp.concatenate([gates, tail], axis=0)


def _mix0_tail_kernel(x_ref, hm_ref, hd_ref, wo_ref, nf_ref, wrt_ref, br_ref,
                      xmid_ref, xn_ref, route_ref):
    half = hm_ref.shape[1]
    hd = jnp.concatenate([hd_ref[h] for h in range(DIFF_HEADS)], axis=1)
    mix = _dot(hm_ref[...], wo_ref[0:half, :]) + _dot(hd, wo_ref[half:2 * half, :])
    xmid = x_ref[...] + mix
    xmid_ref[...] = xmid
    xn_ref[...], route_ref[...] = _route(_rms(xmid, nf_ref[...]), wrt_ref, br_ref)


def _mix0_tail(x, hm, hd, wo, nf, wrt, br):
    t, d = x.shape
    tm = ROW_TILE
    row = lambda n: pl.BlockSpec((tm, n), lambda i: (i, 0))
    return pl.pallas_call(
        _mix0_tail_kernel,
        grid=(t // tm,),
        in_specs=[row(d), row(hm.shape[1]), pl.BlockSpec((DIFF_HEADS, tm, DIFF_V_DIM), lambda i: (0, i, 0)),
                  _full(wo.shape), _full(nf.shape), _full(wrt.shape), _full(br.shape)],
        out_specs=[row(d), row(d), pl.BlockSpec((ROUTE_ROWS, tm), lambda i: (0, i))],
        out_shape=[jax.ShapeDtypeStruct((t, d), F32), jax.ShapeDtypeStruct((t, d), BF16),
                   jax.ShapeDtypeStruct((ROUTE_ROWS, t), F32)],
        compiler_params=_params("parallel"),
        name="mix0_tail",
    )(x, hm, hd, wo, nf, wrt, br)


def _proj1_kernel(x_ref, y_ref, g_ref, w_ref, x1_ref, b_ref, u_ref):
    d = x_ref.shape[1]
    x1 = x_ref[...] + y_ref[...].astype(F32)
    x1_ref[...] = x1
    xb = _rms(x1, g_ref[...]).astype(BF16)
    b_ref[...] = _dot(xb, w_ref[:, 0:d]).astype(BF16)
    u_ref[...] = (_dot(xb, w_ref[:, d:2 * d]) * _dot(xb, w_ref[:, 2 * d:3 * d])).astype(BF16)


def _proj1(x, y, g, w):
    t, d = x.shape
    tm = ROW_TILE
    row = lambda n: pl.BlockSpec((tm, n), lambda i: (i, 0))
    return pl.pallas_call(
        _proj1_kernel,
        grid=(t // tm,),
        in_specs=[row(d), row(d), _full(g.shape), _full(w.shape)],
        out_specs=[row(d), row(d), row(d)],
        out_shape=[jax.ShapeDtypeStruct((t, d), F32), jax.ShapeDtypeStruct((t, d), BF16),
                   jax.ShapeDtypeStruct((t, d), BF16)],
        compiler_params=_params("parallel"),
        name="proj1",
    )(x, y, g, w)


def _mix1_tail_kernel(x_ref, b_ref, u_ref, halo_ref, cw_ref, wo_ref, nf_ref, wrt_ref, br_ref,
                      xmid_ref, xn_ref, route_ref, *, tiles_per_seq):
    seq_start = pl.program_id(0) % tiles_per_seq == 0
    halo = halo_ref[...].astype(F32)
    tail = jnp.where(seq_start, jnp.zeros_like(halo), halo)
    conv = _roll_conv(u_ref[...], cw_ref[...], tail, CONV_WIDTH)
    mixed = (b_ref[...].astype(F32) * conv).astype(BF16)
    xmid = x_ref[...] + _dot(mixed, wo_ref[...])
    xmid_ref[...] = xmid
    xn_ref[...], route_ref[...] = _route(_rms(xmid, nf_ref[...]), wrt_ref, br_ref)


def _mix1_tail(x, b, u, cw, wo, nf, wrt, br, seq):
    t, d = x.shape
    tm = CONV_ROW_TILE
    row = lambda n: pl.BlockSpec((tm, n), lambda i: (i, 0))
    halo = pl.BlockSpec((HALO, d), lambda i: (jnp.maximum(i * (tm // HALO) - 1, 0), 0))
    return pl.pallas_call(
        functools.partial(_mix1_tail_kernel, tiles_per_seq=seq // tm),
        grid=(t // tm,),
        in_specs=[row(d), row(d), row(d), halo, _full(cw.shape), _full(wo.shape),
                  _full(nf.shape), _full(wrt.shape), _full(br.shape)],
        out_specs=[row(d), row(d), pl.BlockSpec((ROUTE_ROWS, tm), lambda i: (0, i))],
        out_shape=[jax.ShapeDtypeStruct((t, d), F32), jax.ShapeDtypeStruct((t, d), BF16),
                   jax.ShapeDtypeStruct((ROUTE_ROWS, t), F32)],
        compiler_params=_params("parallel"),
        name="mix1_tail",
    )(x, b, u, u, cw, wo, nf, wrt, br)


def _moe_kernel(off_ref, nsub_ref, x_ref, r_ref, wg_ref, wu_ref, wd_ref, y_ref,
                xs_ref, gs_ref, ys_ref, dest_ref):
    i = pl.program_id(0)
    g = pl.program_id(1)
    tb = MOE_BLOCK
    rows = MOE_ROWS
    ch = MOE_CHUNK
    d = x_ref.shape[1]
    tail_row = rows - ch
    last_group = i * N_GROUPS + N_GROUPS - 1
    used = off_ref[last_group] + nsub_ref[last_group] * MOE_SUB

    @pl.when(g == 0)
    def _():
        r = r_ref[...]
        grp = r[GROUP_LANE:GROUP_LANE + 1, :]
        row8 = lax.broadcasted_iota(jnp.int32, (SUBLANES, tb), 0)
        onehot = jnp.where(row8.astype(F32) == grp, 1.0, 0.0)
        row1 = lax.broadcasted_iota(jnp.int32, (SUBLANES, 1), 0)
        base = jnp.zeros((SUBLANES, 1), F32)
        for gg in range(N_GROUPS):
            base = jnp.where(row1 == gg, off_ref[i * N_GROUPS + gg].astype(F32), base)
        rr = lax.broadcasted_iota(jnp.int32, (ch, ch), 0)
        cc = lax.broadcasted_iota(jnp.int32, (ch, ch), 1)
        upper = jnp.where(rr <= cc, 1.0, 0.0).astype(BF16)
        parts = []
        for c in range(tb // ch):
            sel = onehot[:, c * ch:(c + 1) * ch]
            rank = _dot(sel.astype(BF16), upper) + base
            parts.append(jnp.sum(sel * rank, axis=0, keepdims=True) - 1.0)
            base = base + jnp.sum(sel, axis=1, keepdims=True)
        dest_row = jnp.concatenate(parts, axis=1)
        side = jnp.concatenate([r[0:SUBLANES, :], jnp.broadcast_to(dest_row, (SUBLANES, tb)),
                                jnp.zeros((LANES - 2 * SUBLANES, tb), F32)], axis=0).T
        dest_ref[...] = side
        lane = lax.broadcasted_iota(jnp.int32, (tb, LANES), 1)
        gates = jnp.where(lane < EXPERTS_PER_GROUP, side, 0.0)
        g_hi = gates.astype(BF16)
        g_lo = (gates - g_hi.astype(F32)).astype(BF16)
        xcat = jnp.concatenate([x_ref[...], g_hi, g_lo], axis=1)

        def move(c):
            rr = (lax.broadcasted_iota(jnp.int32, (ch, tb), 0) + c * ch).astype(F32)
            p = jnp.where(rr == dest_row, 1.0, 0.0).astype(BF16)
            moved = _dot(p, xcat)
            xs_ref[c * ch:(c + 1) * ch, :] = moved[:, 0:d].astype(BF16)
            gs_ref[c * ch:(c + 1) * ch, :] = moved[:, d:d + LANES] + moved[:, d + LANES:d + 2 * LANES]

        for c in range(rows // ch - 1):
            move(c)

        @pl.when(used > tail_row)
        def _():
            move(rows // ch - 1)

        ys_ref[...] = jnp.zeros_like(ys_ref)

    off = off_ref[i * N_GROUPS + g]
    n_sub = nsub_ref[i * N_GROUPS + g]

    def expert_rows(r0, size):
        xt = xs_ref[pl.ds(r0, size), :]
        gt = gs_ref[pl.ds(r0, size), :]
        acc = jnp.zeros((size, d), F32)
        for e in range(EXPERTS_PER_GROUP):
            hg = _dot(xt, wg_ref[0, 0, e])
            up = _dot(xt, wu_ref[0, 0, e])
            act = hg * _sigmoid(hg) * up * gt[:, e:e + 1]
            acc = acc + _dot(act.astype(BF16), wd_ref[0, 0, e])
        ys_ref[pl.ds(r0, size), :] = acc.astype(BF16)

    def pair(j, carry):
        expert_rows(pl.multiple_of(off + j * (2 * MOE_SUB), MOE_SUB), 2 * MOE_SUB)
        return carry

    lax.fori_loop(0, n_sub // 2, pair, 0)

    @pl.when(n_sub % 2 == 1)
    def _():
        expert_rows(pl.multiple_of(off + (n_sub - 1) * MOE_SUB, MOE_SUB), MOE_SUB)

    @pl.when(g == N_GROUPS - 1)
    def _():
        def pick(c, lo, hi):
            cc = (lax.broadcasted_iota(jnp.int32, (ch, hi - lo), 1) + lo).astype(F32)
            dest = dest_ref[c * ch:(c + 1) * ch, DEST_LANE:DEST_LANE + 1]
            return _dot(jnp.where(cc == dest, 1.0, 0.0).astype(BF16), ys_ref[lo:hi, :])

        for c in range(tb // ch):
            y_ref[c * ch:(c + 1) * ch, :] = pick(c, 0, tail_row).astype(y_ref.dtype)

        @pl.when(used > tail_row)
        def _():
            for c in range(tb // ch):
                sl = slice(c * ch, (c + 1) * ch)
                y_ref[sl, :] = (y_ref[sl, :].astype(F32) + pick(c, tail_row, rows)).astype(y_ref.dtype)


def _hier_moe(xn, route, wg, wu, wd, layer):
    t, d = xn.shape
    tb = MOE_BLOCK
    nb = t // tb
    e, ff = wg.shape[2], wg.shape[4]
    group = route[GROUP_LANE, :].astype(jnp.int32).reshape(nb, tb)
    counts = jnp.sum((group[:, :, None] == jnp.arange(N_GROUPS)[None, None, :]).astype(jnp.int32), axis=1)
    nsub = (counts + MOE_SUB - 1) // MOE_SUB
    padded = nsub * MOE_SUB
    off = jnp.cumsum(padded, axis=1) - padded
    grid_spec = pltpu.PrefetchScalarGridSpec(
        num_scalar_prefetch=2,
        grid=(nb, N_GROUPS),
        in_specs=[
            pl.BlockSpec((tb, d), lambda i, g, *_: (i, 0)),
            pl.BlockSpec((ROUTE_ROWS, tb), lambda i, g, *_: (0, i)),
            pl.BlockSpec((1, 1, e, d, ff), lambda i, g, *_: (layer, g, 0, 0, 0)),
            pl.BlockSpec((1, 1, e, d, ff), lambda i, g, *_: (layer, g, 0, 0, 0)),
            pl.BlockSpec((1, 1, e, ff, d), lambda i, g, *_: (layer, g, 0, 0, 0)),
        ],
        out_specs=pl.BlockSpec((tb, d), lambda i, g, *_: (i, 0)),
        scratch_shapes=[
            pltpu.VMEM((MOE_ROWS, d), BF16),
            pltpu.VMEM((MOE_ROWS, LANES), F32),
            pltpu.VMEM((MOE_ROWS, d), BF16),
            pltpu.VMEM((tb, LANES), F32),
        ],
    )
    return pl.pallas_call(
        _moe_kernel,
        grid_spec=grid_spec,
        out_shape=jax.ShapeDtypeStruct((t, d), BF16),
        compiler_params=_params("parallel", "arbitrary"),
        name="moe",
    )(off.reshape(-1).astype(jnp.int32), nsub.reshape(-1).astype(jnp.int32), xn, route, wg, wu, wd)


def _final_kernel(x_ref, y_ref, g_ref, o_ref):
    o_ref[...] = _rms(x_ref[...] + y_ref[...].astype(F32), g_ref[...])


def _final(x, y, g):
    t, d = x.shape
    tm = ROW_TILE
    row = pl.BlockSpec((tm, d), lambda i: (i, 0))
    return pl.pallas_call(
        _final_kernel,
        grid=(t // tm,),
        in_specs=[row, row, _full(g.shape)],
        out_specs=row,
        out_shape=jax.ShapeDtypeStruct((t, d), F32),
        compiler_params=_params("parallel"),
        name="final_norm",
    )(x, y, g)


def _hi_lo(w):
    hi = w.astype(BF16)
    return hi, (w - hi.astype(F32)).astype(BF16)


def _router_params(w_rg, b_rg, w_re, b_re):
    d = w_rg.shape[0]
    n_e = N_GROUPS * EXPERTS_PER_GROUP
    pad = jnp.zeros((d, EXPERT_LANE0 - N_GROUPS), F32)
    tail = jnp.zeros((d, LANES - EXPERT_LANE0 - n_e), F32)
    w = jnp.concatenate([w_rg, pad, w_re, tail], axis=1)
    b = jnp.concatenate([b_rg, pad[0], b_re, tail[0]]).reshape(LANES, 1)
    hi, lo = _hi_lo(w)
    return jnp.concatenate([hi, lo], axis=1).T, b


def kernel(x, positions, norm_mix, norm_ffn, ab_w_in, ab_gate_bias, ab_conv_w, ab_mlstm_norm, ab_lambda,
           ab_diff_norm, ab_w_out, c_w_in, c_conv_w, c_w_out, moe_w_rg, moe_b_rg, moe_w_re, moe_b_re,
           moe_w_gate, moe_w_up, moe_w_down, final_norm):
    batch, seq, d = x.shape
    t = batch * seq
    xf = x.reshape(t, d)
    row2 = lambda v: v.reshape(1, -1).astype(F32)

    w_in = ab_w_in[0]
    mw = 2 * MLSTM_HEADS * MLSTM_HEAD_DIM
    hw = MLSTM_HEADS * MLSTM_HEAD_DIM
    ng = 2 * MLSTM_HEADS
    dw = DIFF_HEADS * DIFF_V_DIM
    o_g = mw + 2 * hw
    o_q = o_g + ng
    o_k, o_dv = o_q + dw, o_q + 2 * dw
    w0 = jnp.concatenate([w_in[:, :mw], w_in[:, mw + hw:o_g], w_in[:, o_k:o_dv]], axis=1).astype(BF16)
    w_gates = jnp.concatenate([w_in[:, o_g:o_q], jnp.zeros((d, LANES - ng), F32)], axis=1)
    wgh, wgl = _hi_lo(w_gates)
    wvt = jnp.concatenate([w_in[:, mw:mw + hw].astype(BF16), w_in[:, o_dv:o_dv + dw].astype(BF16),
                           w_in[:, o_q:o_k].astype(BF16), wgh, wgl], axis=1).T
    gb = jnp.concatenate([ab_gate_bias[0].astype(F32), jnp.zeros((LANES - ng,), F32)]).reshape(LANES, 1)

    cost, sint = _rope_table(positions.reshape(t).astype(jnp.int32))
    mqk, mo, dqt, dk, mvt, dvt, gates = _proj0(xf, cost, sint, row2(norm_mix[0]), w0, wvt, gb)
    hm = _mlstm(mqk, mvt, mo, gates, ab_conv_w[0].astype(F32), row2(ab_mlstm_norm[0]), batch, seq)
    lam_init = 0.8 - 0.6 * math.exp(-0.3 * 0)
    hd = _diff_attn(dqt, dk, dvt, ab_lambda[0].astype(F32), row2(ab_diff_norm[0]), batch, seq, lam_init)

    wrt, br = _router_params(moe_w_rg[0], moe_b_rg[0], moe_w_re[0], moe_b_re[0])
    xmid, xn, route = _mix0_tail(xf, hm, hd, ab_w_out[0].astype(BF16), row2(norm_ffn[0]), wrt, br)
    wg, wu, wd = moe_w_gate.astype(BF16), moe_w_up.astype(BF16), moe_w_down.astype(BF16)
    y = _hier_moe(xn, route, wg, wu, wd, 0)

    x1, bgate, u = _proj1(xmid, y, row2(norm_mix[1]), c_w_in[0].astype(BF16))
    wrt, br = _router_params(moe_w_rg[1], moe_b_rg[1], moe_w_re[1], moe_b_re[1])
    xmid, xn, route = _mix1_tail(x1, bgate, u, c_conv_w[0].astype(F32), c_w_out[0].astype(BF16),
                                 row2(norm_ffn[1]), wrt, br, seq)
    y = _hier_moe(xn, route, wg, wu, wd, 1)

    return _final(xmid, y, row2(final_norm)).reshape(batch, seq, d)
```

```python
import functools
import math

import jax
import jax.numpy as jnp
from jax import lax
from jax.experimental import pallas as pl
from jax.experimental.pallas import tpu as pltpu

F32 = jnp.float32
BF16 = jnp.bfloat16

EPS = 1e-6
LOG2E = math.log2(math.e)
ROPE_THETA = 10000.0
LANES = 128
SUBLANES = 8
MLSTM_HEADS = 4
MLSTM_HEAD_DIM = 128
MLSTM_CONV = 4
DIFF_HEADS = 4
DIFF_HEAD_DIM = 64
DIFF_V_DIM = 128
CONV_WIDTH = 3
N_GROUPS = 4
EXPERTS_PER_GROUP = 8
HALO = 8

ROW_TILE = 1024
CONV_ROW_TILE = 512
MLSTM_CHUNK = 256
ATTN_BLOCK = 512
MOE_BLOCK = 1024
MOE_SUB = 128
MOE_ROWS = MOE_BLOCK + N_GROUPS * MOE_SUB
MOE_CHUNK = 256
VMEM_LIMIT = 56 * 1024 * 1024


def _dot(a, b):
    return jnp.dot(a, b, preferred_element_type=F32)


def _dot_nt(a, b):
    return lax.dot_general(a, b, (((1,), (1,)), ((), ())), preferred_element_type=F32)


def _split3(x):
    hi = x.astype(BF16)
    r1 = x - hi.astype(F32)
    mid = r1.astype(BF16)
    lo = (r1 - mid.astype(F32)).astype(BF16)
    return hi, mid, lo


def _rms(x, g):
    return x * lax.rsqrt(jnp.mean(x * x, axis=-1, keepdims=True) + EPS) * g


def _sigmoid(x):
    return 0.5 * jnp.tanh(0.5 * x) + 0.5


def _shift_matrix(rows, taps):
    r = jnp.arange(rows)
    return jnp.concatenate([r[:, None] - s == r[None, :] for s in range(1, taps)], axis=0).astype(BF16)


def _shift_conv(xb, shift_ref, cw, tail, taps):
    rows, c = xb.shape
    x32 = xb.astype(F32)
    shifted = _dot(shift_ref[...], xb)
    last = taps - 1
    conv = cw[last:last + 1, :] * x32
    for s in range(1, taps):
        conv = conv + cw[last - s:last - s + 1, :] * shifted[(s - 1) * rows:s * rows, :]
    head = jnp.concatenate([tail, x32[0:HALO, :]], axis=0)
    rowi = lax.broadcasted_iota(jnp.int32, (HALO, c), 0)
    fix = jnp.zeros((HALO, c), F32)
    for s in range(1, taps):
        fix = fix + cw[last - s:last - s + 1, :] * jnp.where(rowi < s, head[HALO - s:2 * HALO - s, :], 0.0)
    return jnp.concatenate([conv[0:HALO, :] + fix, conv[HALO:, :]], axis=0)


def _roll_conv(xb, cw, tail, taps):
    rows, c = xb.shape
    x32 = xb.astype(F32)
    last = taps - 1
    conv = cw[last:last + 1, :] * x32
    head = jnp.concatenate([tail, x32[0:HALO, :]], axis=0)
    first = cw[last:last + 1, :] * x32[0:HALO, :]
    for s in range(1, taps):
        w = cw[last - s:last - s + 1, :]
        conv = conv + w * pltpu.roll(x32, s, 0)
        first = first + w * head[HALO - s:2 * HALO - s, :]
    return jnp.concatenate([first, conv[HALO:, :]], axis=0)


def _params(*sem):
    return pltpu.CompilerParams(dimension_semantics=sem, vmem_limit_bytes=VMEM_LIMIT)


def _full(shape):
    return pl.BlockSpec(shape, lambda *_: (0,) * len(shape))


def _proj0_kernel(x_ref, cos_ref, sin_ref, cost_ref, sint_ref, g_ref, w_ref, wvt_ref, gb_ref,
                  mqk_ref, mo_ref, dqt_ref, dk_ref, mvt_ref, dvt_ref, gates_ref):
    xn = _rms(x_ref[...], g_ref[...])
    xb = xn.astype(BF16)
    xl = (xn - xb.astype(F32)).astype(BF16)
    half = DIFF_HEAD_DIM // 2
    dw = DIFF_HEADS * DIFF_V_DIM

    def seg(lo, hi):
        return _dot(xb, w_ref[:, lo:hi])

    mqk_ref[...] = seg(0, 1024).astype(BF16)
    mo_ref[...] = seg(1024, 1536).astype(BF16)
    vt = _dot_nt(wvt_ref[...], xb)
    mvt_ref[...] = vt[0:dw, :].astype(BF16)
    dvt_ref[...] = vt[dw:2 * dw, :].astype(BF16)
    g0 = 3 * dw
    gates_ref[...] = (vt[g0:g0 + LANES, :] + (vt[g0 + LANES:g0 + 2 * LANES, :]
                      + _dot_nt(wvt_ref[g0:g0 + LANES, :], xl))) + gb_ref[...]

    ct = cost_ref[...]
    st = sint_ref[...]
    q_scale = LOG2E * DIFF_HEAD_DIM ** -0.5
    for r in range(2 * DIFF_HEADS):
        lo = 2 * dw + r * DIFF_HEAD_DIM
        x1 = vt[lo:lo + half, :]
        x2 = vt[lo + half:lo + 2 * half, :]
        out = lo - 2 * dw
        dqt_ref[out:out + half, :] = ((x1 * ct - x2 * st) * q_scale).astype(BF16)
        dqt_ref[out + half:out + 2 * half, :] = ((x2 * ct + x1 * st) * q_scale).astype(BF16)

    heads = 2 * DIFF_HEADS
    c = cos_ref[...]
    s = sin_ref[...]
    c4 = jnp.concatenate([c, c] * heads, axis=1)
    s4 = jnp.concatenate([-s, s] * heads, axis=1)
    lane4 = lax.broadcasted_iota(jnp.int32, c4.shape, 1)
    first4 = (lane4 & (DIFF_HEAD_DIM - 1)) < half
    k = seg(1536, 2048)
    partner = jnp.where(first4, pltpu.roll(k, dw - half, 1), pltpu.roll(k, half, 1))
    dk = (k * c4 + partner * s4).astype(BF16)
    for h in range(DIFF_HEADS):
        dk_ref[h] = dk[:, h * DIFF_V_DIM:(h + 1) * DIFF_V_DIM]


def _rope_table_kernel(pos_ref, invf_ref, cos_ref, sin_ref):
    ang = invf_ref[...] * pos_ref[...].astype(F32)
    cos_ref[...] = jnp.cos(ang)
    sin_ref[...] = jnp.sin(ang)


def _rope_table(positions):
    t = positions.shape[0]
    half = DIFF_HEAD_DIM // 2
    inv_freq = (1.0 / (ROPE_THETA ** (jnp.arange(half, dtype=F32) / half))).reshape(half, 1)
    tl = min(t, 4096)
    return pl.pallas_call(
        _rope_table_kernel,
        grid=(t // tl,),
        in_specs=[pl.BlockSpec((1, tl), lambda i: (0, i)), _full(inv_freq.shape)],
        out_specs=[pl.BlockSpec((half, tl), lambda i: (0, i))] * 2,
        out_shape=[jax.ShapeDtypeStruct((half, t), F32)] * 2,
        compiler_params=_params("parallel"),
        name="rope_table",
    )(positions.reshape(1, t), inv_freq)


def _proj0(x, cost, sint, g, w, wvt, gb):
    t, d = x.shape
    tm = ROW_TILE
    row = lambda n: pl.BlockSpec((tm, n), lambda i: (i, 0))
    col = lambda n: pl.BlockSpec((n, tm), lambda i: (0, i))
    heads = pl.BlockSpec((DIFF_HEADS, tm, DIFF_V_DIM), lambda i: (0, i, 0))
    cos, sin = cost.T, sint.T
    half = cost.shape[0]
    out_shapes = [
        jax.ShapeDtypeStruct((t, 1024), BF16), jax.ShapeDtypeStruct((t, 512), BF16),
        jax.ShapeDtypeStruct((512, t), BF16), jax.ShapeDtypeStruct((DIFF_HEADS, t, DIFF_V_DIM), BF16),
        jax.ShapeDtypeStruct((512, t), BF16), jax.ShapeDtypeStruct((512, t), BF16),
        jax.ShapeDtypeStruct((LANES, t), F32),
    ]
    return pl.pallas_call(
        _proj0_kernel,
        grid=(t // tm,),
        in_specs=[row(d), row(half), row(half), col(half), col(half), _full(g.shape), _full(w.shape),
                  _full(wvt.shape), _full(gb.shape)],
        out_specs=[row(1024), row(512), col(512), heads, col(512), col(512), col(LANES)],
        out_shape=out_shapes,
        compiler_params=_params("parallel"),
        name="proj0",
    )(x, cos, sin, cost, sint, g, w, wvt, gb)


def _mlstm_kernel(mqk_ref, mvt_ref, mo_ref, gates_ref, shift_ref, cw_ref, norm_ref, hm_ref,
                  tail_ref, cn_ref, m_ref):
    L = MLSTM_CHUNK
    hd = MLSTM_HEAD_DIM
    width = MLSTM_HEADS * hd

    @pl.when(pl.program_id(1) == 0)
    def _():
        tail_ref[...] = jnp.zeros_like(tail_ref)
        cn_ref[...] = jnp.zeros_like(cn_ref)
        m_ref[...] = jnp.zeros_like(m_ref)

    conv = _shift_conv(mqk_ref[...], shift_ref, cw_ref[...], tail_ref[...], MLSTM_CONV)
    tail_ref[...] = mqk_ref[L - HALO:L, :].astype(F32)
    qk = conv * _sigmoid(conv)
    qb = (qk[:, :width] * (hd ** -0.5)).astype(BF16)
    k = qk[:, width:]
    kb = k.astype(BF16)

    g8 = gates_ref[0:SUBLANES, :]
    logf = (jnp.minimum(g8, 0.0) - jnp.log(1.0 + jnp.exp(-jnp.abs(g8)))) * LOG2E
    krow = lax.broadcasted_iota(jnp.int32, (L, L), 0)
    qcol = lax.broadcasted_iota(jnp.int32, (L, L), 1)
    causal_t = krow <= qcol
    triu = jnp.where(causal_t, 1.0, 0.0).astype(BF16)
    f_hi, f_mid, f_lo = _split3(logf)
    b8 = _dot(f_hi, triu) + (_dot(f_mid, triu) + _dot(f_lo, triu))
    cj8 = g8 * LOG2E - pltpu.roll(b8, MLSTM_HEADS, 0)
    cj_all = jnp.concatenate([cj8, jnp.zeros((LANES - SUBLANES, L), F32)], axis=0).T
    ones = jnp.ones((SUBLANES, L), BF16)

    for h in range(MLSTM_HEADS):
        sl = slice(h * hd, (h + 1) * hd)
        b_row = b8[MLSTM_HEADS + h:MLSTM_HEADS + h + 1, :]
        m_prev = m_ref[h:h + 1, 0:1]
        cn_prev = cn_ref[h]
        vt1 = jnp.concatenate([mvt_ref[sl, :], ones], axis=0)

        cmat = jnp.where(causal_t, cj_all[:, h:h + 1], -jnp.inf)
        mm = jnp.maximum(m_prev, jnp.max(cmat, axis=0, keepdims=True))
        w_t = jnp.exp2(cmat - mm)
        s_t = _dot_nt(kb[:, sl], qb[:, sl]) * w_t
        inter_w = jnp.exp2(m_prev - mm)
        tot = _dot(vt1, s_t.astype(BF16)) + inter_w * _dot_nt(cn_prev.astype(BF16), qb[:, sl])
        den = tot[hd:hd + 1, :]
        h_t = tot[0:hd, :] / jnp.maximum(jnp.abs(den), jnp.exp2(-(b_row + mm)))

        kw = k[:, sl] * w_t[:, L - 1:L]
        cn_ref[h] = inter_w[:, L - 1:L] * cn_prev + _dot(vt1, kw.astype(BF16))
        m_ref[h:h + 1, :] = jnp.broadcast_to(b_row[:, L - 1:L] + mm[:, L - 1:L], (1, LANES))

        scale = lax.rsqrt(jnp.mean(h_t * h_t, axis=0, keepdims=True) + EPS)
        hn = (h_t * scale).T * norm_ref[...]
        hm_ref[:, sl] = (hn * _sigmoid(mo_ref[:, sl].astype(F32))).astype(BF16)


def _mlstm(mqk, mvt, mo, gates, cw, norm, batch, seq):
    t = mqk.shape[0]
    L = MLSTM_CHUNK
    nc = seq // L
    width = MLSTM_HEADS * MLSTM_HEAD_DIM
    row = lambda n: pl.BlockSpec((L, n), lambda b, c: (b * nc + c, 0))
    col = lambda n: pl.BlockSpec((n, L), lambda b, c: (0, b * nc + c))
    shift = _shift_matrix(L, MLSTM_CONV)
    return pl.pallas_call(
        _mlstm_kernel,
        grid=(batch, nc),
        in_specs=[row(2 * width), col(width), row(width), col(LANES), _full(shift.shape), _full(cw.shape),
                  _full(norm.shape)],
        out_specs=row(width),
        out_shape=jax.ShapeDtypeStruct((t, width), BF16),
        scratch_shapes=[
            pltpu.VMEM((HALO, 2 * width), F32),
            pltpu.VMEM((MLSTM_HEADS, MLSTM_HEAD_DIM + SUBLANES, MLSTM_HEAD_DIM), F32),
            pltpu.VMEM((SUBLANES, LANES), F32),
        ],
        compiler_params=_params("parallel", "arbitrary"),
        name="mlstm",
    )(mqk, mvt, mo, gates, shift, cw, norm)


def _diff_attn_kernel(qt_ref, k_ref, vt_ref, lam_ref, norm_ref, o_ref,
                      acc0_ref, acc1_ref, m_ref, sa0_ref, sa1_ref, sb0_ref, sb1_ref, *, lam_init):
    tq = ATTN_BLOCK
    dv = DIFF_V_DIM
    nq = k_ref.shape[1] // tq
    frow = lax.broadcasted_iota(jnp.int32, (dv, tq), 0)
    krow = lax.broadcasted_iota(jnp.int32, (tq, tq), 0)
    qcol = lax.broadcasted_iota(jnp.int32, (tq, tq), 1)
    causal_t = krow <= qcol
    ones = jnp.ones((SUBLANES, tq), BF16)
    accs = (acc0_ref, acc1_ref)
    buf_a = (sa0_ref, sa1_ref)
    buf_b = (sb0_ref, sb1_ref)
    lv = lam_ref[...]
    lam = (jnp.exp(jnp.sum(lv[0:1, :] * lv[1:2, :], axis=-1, keepdims=True))
           - jnp.exp(jnp.sum(lv[2:3, :] * lv[3:4, :], axis=-1, keepdims=True)) + lam_init)

    def queries(qi):
        qt = qt_ref[:, pl.ds(pl.multiple_of(qi * tq, tq), tq)]
        zero = jnp.zeros_like(qt)
        return (jnp.where(frow < DIFF_HEAD_DIM, qt, zero), jnp.where(frow >= DIFF_HEAD_DIM, qt, zero))

    def scores_of(qs, j, dst):
        kb = k_ref[0, pl.ds(pl.multiple_of(j * tq, tq), tq), :]
        for c in range(2):
            dst[c][...] = _dot(kb, qs[c])

    def reset():
        for acc in accs:
            acc[...] = jnp.zeros_like(acc)
        m_ref[...] = jnp.full(m_ref.shape, -jnp.inf, F32)

    reset()
    scores_of(queries(0), 0, buf_a)

    def q_block(qi, carry):
        qrow = pl.multiple_of(qi * tq, tq)
        qs = queries(qi)

        def scores(j, dst):
            scores_of(qs, j, dst)

        def absorb(j, src, masked):
            start = pl.multiple_of(j * tq, tq)
            vt1 = jnp.concatenate([vt_ref[:, pl.ds(start, tq)], ones], axis=0)
            s = [src[c][...] for c in range(2)]
            if masked:
                s = [jnp.where(causal_t, sc, -jnp.inf) for sc in s]
            m_old = [m_ref[c:c + 1, :] for c in range(2)]
            m_new = [jnp.maximum(m_old[c], jnp.max(s[c], axis=0, keepdims=True)) for c in range(2)]
            p = [jnp.exp2(s[c] - m_new[c]).astype(BF16) for c in range(2)]
            alpha = [jnp.exp2(m_old[c] - m_new[c]) for c in range(2)]
            pv = [_dot(vt1, p[c]) for c in range(2)]
            for c in range(2):
                accs[c][...] = alpha[c] * accs[c][...] + pv[c]
                m_ref[c:c + 1, :] = m_new[c]

        def pair(jj, inner):
            j = 2 * jj
            scores(j + 1, buf_b)
            absorb(j, buf_a, False)
            scores(j + 2, buf_a)
            absorb(j + 1, buf_b, False)
            return inner

        lax.fori_loop(0, qi // 2, pair, 0)

        @pl.when(qi % 2 == 0)
        def _():
            absorb(qi, buf_a, True)

        @pl.when(qi % 2 == 1)
        def _():
            scores(qi, buf_b)
            absorb(qi - 1, buf_a, False)
            absorb(qi, buf_b, True)

        scores_of(queries(jnp.minimum(qi + 1, nq - 1)), 0, buf_a)

        a1 = acc0_ref[...]
        a2 = acc1_ref[...]
        o_t = a1[0:dv, :] / a1[dv:dv + 1, :] - lam * (a2[0:dv, :] / a2[dv:dv + 1, :])
        o_ref[0, pl.ds(qrow, tq), :] = (_rms(o_t.T, norm_ref[...]) * (1.0 - lam_init)).astype(BF16)
        reset()
        return carry

    lax.fori_loop(0, nq, q_block, 0)


def _diff_attn(dqt, dk, dvt, lam_vecs, norm, batch, seq, lam_init):
    t = dk.shape[1]
    tq = ATTN_BLOCK
    rows = pl.BlockSpec((1, seq, DIFF_V_DIM), lambda b, h: (h, b, 0))
    cols = pl.BlockSpec((DIFF_V_DIM, seq), lambda b, h: (h, b))
    return pl.pallas_call(
        functools.partial(_diff_attn_kernel, lam_init=lam_init),
        grid=(batch, DIFF_HEADS),
        in_specs=[cols, rows, cols, _full(lam_vecs.shape), _full(norm.shape)],
        out_specs=rows,
        out_shape=jax.ShapeDtypeStruct((DIFF_HEADS, t, DIFF_V_DIM), BF16),
        scratch_shapes=([pltpu.VMEM((DIFF_V_DIM + SUBLANES, tq), F32)] * 2 + [pltpu.VMEM((SUBLANES, tq), F32)]
                        + [pltpu.VMEM((tq, tq), F32)] * 4),
        compiler_params=_params("parallel", "parallel"),
        name="diff_attn",
    )(dqt, dk, dvt, lam_vecs, norm)


ROUTE_ROWS = 2 * SUBLANES
GROUP_LANE = EXPERTS_PER_GROUP
DEST_LANE = EXPERTS_PER_GROUP
EXPERT_LANE0 = 8


def _route(xn, wrt_ref, br_ref):
    xh = xn.astype(BF16)
    xl = (xn - xh.astype(F32)).astype(BF16)
    both = _dot_nt(wrt_ref[...], xh)
    logits = both[0:LANES, :] + (both[LANES:2 * LANES, :] + _dot_nt(wrt_ref[0:LANES, :], xl)) + br_ref[...]
    e = EXPERTS_PER_GROUP
    tm = logits.shape[1]
    row = lax.broadcasted_iota(jnp.int32, (e, tm), 0)
    ninf = -jnp.inf
    big = jnp.int32(LANES)

    is_g = row < N_GROUPS
    gl = jnp.where(is_g, logits[0:e, :], ninf)
    gmax = jnp.max(gl, axis=0, keepdims=True)
    gsum = jnp.sum(jnp.where(is_g, jnp.exp(gl - gmax), 0.0), axis=0, keepdims=True)
    g_val = 1.0 / gsum
    g_idx = jnp.min(jnp.where(gl == gmax, row, big), axis=0, keepdims=True)

    el = logits[EXPERT_LANE0:EXPERT_LANE0 + e, :]
    for g in range(1, N_GROUPS):
        lo = EXPERT_LANE0 + e * g
        el = jnp.where(g_idx == g, logits[lo:lo + e, :], el)
    v1 = jnp.max(el, axis=0, keepdims=True)
    i1 = jnp.min(jnp.where(el == v1, row, big), axis=0, keepdims=True)
    el2 = jnp.where(row == i1, ninf, el)
    v2 = jnp.max(el2, axis=0, keepdims=True)
    i2 = jnp.min(jnp.where(el2 == v2, row, big), axis=0, keepdims=True)
    e21 = jnp.exp(v2 - v1)
    w1 = 1.0 / (1.0 + e21)
    w2 = e21 * w1
    gates = g_val * (jnp.where(row == i1, w1, 0.0) + jnp.where(row == i2, w2, 0.0))
    tail = jnp.where(row == 0, g_idx.astype(F32), 0.0)
    return xh, jnp.concatenate([gates, tail], axis=0)


def _mix0_tail_kernel(x_ref, hm_ref, hd_ref, wo_ref, nf_ref, wrt_ref, br_ref,
                      xmid_ref, xn_ref, route_ref):
    merged = jnp.concatenate([hm_ref[...]] + [hd_ref[h] for h in range(DIFF_HEADS)], axis=1)
    xmid = x_ref[...] + _dot(merged, wo_ref[...])
    xmid_ref[...] = xmid
    xn_ref[...], route_ref[...] = _route(_rms(xmid, nf_ref[...]), wrt_ref, br_ref)


def _mix0_tail(x, hm, hd, wo, nf, wrt, br):
    t, d = x.shape
    tm = ROW_TILE
    row = lambda n: pl.BlockSpec((tm, n), lambda i: (i, 0))
    return pl.pallas_call(
        _mix0_tail_kernel,
        grid=(t // tm,),
        in_specs=[row(d), row(hm.shape[1]), pl.BlockSpec((DIFF_HEADS, tm, DIFF_V_DIM), lambda i: (0, i, 0)),
                  _full(wo.shape), _full(nf.shape), _full(wrt.shape), _full(br.shape)],
        out_specs=[row(d), row(d), pl.BlockSpec((ROUTE_ROWS, tm), lambda i: (0, i))],
        out_shape=[jax.ShapeDtypeStruct((t, d), F32), jax.ShapeDtypeStruct((t, d), BF16),
                   jax.ShapeDtypeStruct((ROUTE_ROWS, t), F32)],
        compiler_params=_params("parallel"),
        name="mix0_tail",
    )(x, hm, hd, wo, nf, wrt, br)


def _proj1_kernel(x_ref, y_ref, g_ref, w_ref, x1_ref, b_ref, u_ref):
    d = x_ref.shape[1]
    x1 = x_ref[...] + y_ref[...].astype(F32)
    x1_ref[...] = x1
    xb = _rms(x1, g_ref[...]).astype(BF16)
    b_ref[...] = _dot(xb, w_ref[:, 0:d]).astype(BF16)
    u_ref[...] = (_dot(xb, w_ref[:, d:2 * d]) * _dot(xb, w_ref[:, 2 * d:3 * d])).astype(BF16)


def _proj1(x, y, g, w):
    t, d = x.shape
    tm = ROW_TILE
    row = lambda n: pl.BlockSpec((tm, n), lambda i: (i, 0))
    return pl.pallas_call(
        _proj1_kernel,
        grid=(t // tm,),
        in_specs=[row(d), row(d), _full(g.shape), _full(w.shape)],
        out_specs=[row(d), row(d), row(d)],
        out_shape=[jax.ShapeDtypeStruct((t, d), F32), jax.ShapeDtypeStruct((t, d), BF16),
                   jax.ShapeDtypeStruct((t, d), BF16)],
        compiler_params=_params("parallel"),
        name="proj1",
    )(x, y, g, w)


def _mix1_tail_kernel(x_ref, b_ref, u_ref, halo_ref, cw_ref, wo_ref, nf_ref, wrt_ref, br_ref,
                      xmid_ref, xn_ref, route_ref, *, tiles_per_seq):
    seq_start = pl.program_id(0) % tiles_per_seq == 0
    halo = halo_ref[...].astype(F32)
    tail = jnp.where(seq_start, jnp.zeros_like(halo), halo)
    conv = _roll_conv(u_ref[...], cw_ref[...], tail, CONV_WIDTH)
    mixed = (b_ref[...].astype(F32) * conv).astype(BF16)
    xmid = x_ref[...] + _dot(mixed, wo_ref[...])
    xmid_ref[...] = xmid
    xn_ref[...], route_ref[...] = _route(_rms(xmid, nf_ref[...]), wrt_ref, br_ref)


def _mix1_tail(x, b, u, cw, wo, nf, wrt, br, seq):
    t, d = x.shape
    tm = CONV_ROW_TILE
    row = lambda n: pl.BlockSpec((tm, n), lambda i: (i, 0))
    halo = pl.BlockSpec((HALO, d), lambda i: (jnp.maximum(i * (tm // HALO) - 1, 0), 0))
    return pl.pallas_call(
        functools.partial(_mix1_tail_kernel, tiles_per_seq=seq // tm),
        grid=(t // tm,),
        in_specs=[row(d), row(d), row(d), halo, _full(cw.shape), _full(wo.shape),
                  _full(nf.shape), _full(wrt.shape), _full(br.shape)],
        out_specs=[row(d), row(d), pl.BlockSpec((ROUTE_ROWS, tm), lambda i: (0, i))],
        out_shape=[jax.ShapeDtypeStruct((t, d), F32), jax.ShapeDtypeStruct((t, d), BF16),
                   jax.ShapeDtypeStruct((ROUTE_ROWS, t), F32)],
        compiler_params=_params("parallel"),
        name="mix1_tail",
    )(x, b, u, u, cw, wo, nf, wrt, br)


def _moe_kernel(off_ref, nsub_ref, x_ref, r_ref, wg_ref, wu_ref, wd_ref, y_ref,
                xs_ref, gs_ref, ys_ref, dest_ref):
    i = pl.program_id(0)
    g = pl.program_id(1)
    tb = MOE_BLOCK
    rows = MOE_ROWS
    ch = MOE_CHUNK
    d = x_ref.shape[1]
    tail_row = rows - ch
    last_group = i * N_GROUPS + N_GROUPS - 1
    used = off_ref[last_group] + nsub_ref[last_group] * MOE_SUB

    @pl.when(g == 0)
    def _():
        r = r_ref[...]
        grp = r[GROUP_LANE:GROUP_LANE + 1, :]
        row8 = lax.broadcasted_iota(jnp.int32, (SUBLANES, tb), 0)
        onehot = jnp.where(row8.astype(F32) == grp, 1.0, 0.0)
        row1 = lax.broadcasted_iota(jnp.int32, (SUBLANES, 1), 0)
        base = jnp.zeros((SUBLANES, 1), F32)
        for gg in range(N_GROUPS):
            base = jnp.where(row1 == gg, off_ref[i * N_GROUPS + gg].astype(F32), base)
        rr = lax.broadcasted_iota(jnp.int32, (ch, ch), 0)
        cc = lax.broadcasted_iota(jnp.int32, (ch, ch), 1)
        upper = jnp.where(rr <= cc, 1.0, 0.0).astype(BF16)
        parts = []
        for c in range(tb // ch):
            sel = onehot[:, c * ch:(c + 1) * ch]
            rank = _dot(sel.astype(BF16), upper) + base
            parts.append(jnp.sum(sel * rank, axis=0, keepdims=True) - 1.0)
            base = base + jnp.sum(sel, axis=1, keepdims=True)
        dest_row = jnp.concatenate(parts, axis=1)
        side = jnp.concatenate([r[0:SUBLANES, :], jnp.broadcast_to(dest_row, (SUBLANES, tb)),
                                jnp.zeros((LANES - 2 * SUBLANES, tb), F32)], axis=0).T
        dest_ref[...] = side
        lane = lax.broadcasted_iota(jnp.int32, (tb, LANES), 1)
        gates = jnp.where(lane < EXPERTS_PER_GROUP, side, 0.0)
        g_hi = gates.astype(BF16)
        g_lo = (gates - g_hi.astype(F32)).astype(BF16)
        xcat = jnp.concatenate([x_ref[...], g_hi, g_lo], axis=1)

        def move(c):
            rr = (lax.broadcasted_iota(jnp.int32, (ch, tb), 0) + c * ch).astype(F32)
            p = jnp.where(rr == dest_row, 1.0, 0.0).astype(BF16)
            moved = _dot(p, xcat)
            xs_ref[c * ch:(c + 1) * ch, :] = moved[:, 0:d].astype(BF16)
            gs_ref[c * ch:(c + 1) * ch, :] = moved[:, d:d + LANES] + moved[:, d + LANES:d + 2 * LANES]

        for c in range(rows // ch - 1):
            move(c)

        @pl.when(used > tail_row)
        def _():
            move(rows // ch - 1)

        ys_ref[...] = jnp.zeros_like(ys_ref)

    off = off_ref[i * N_GROUPS + g]
    n_sub = nsub_ref[i * N_GROUPS + g]

    def expert_rows(r0, size):
        xt = xs_ref[pl.ds(r0, size), :]
        gt = gs_ref[pl.ds(r0, size), :]
        acts = []
        for e in range(EXPERTS_PER_GROUP):
            hg = _dot(xt, wg_ref[0, 0, e])
            up = _dot(xt, wu_ref[0, 0, e])
            acts.append((hg * _sigmoid(hg) * up * gt[:, e:e + 1]).astype(BF16))
        act_all = jnp.concatenate(acts, axis=1)
        wd_all = wd_ref[0, 0].reshape(EXPERTS_PER_GROUP * wd_ref.shape[3], d)
        ys_ref[pl.ds(r0, size), :] = _dot(act_all, wd_all).astype(BF16)

    def pair(j, carry):
        expert_rows(pl.multiple_of(off + j * (2 * MOE_SUB), MOE_SUB), 2 * MOE_SUB)
        return carry

    lax.fori_loop(0, n_sub // 2, pair, 0)

    @pl.when(n_sub % 2 == 1)
    def _():
        expert_rows(pl.multiple_of(off + (n_sub - 1) * MOE_SUB, MOE_SUB), MOE_SUB)

    @pl.when(g == N_GROUPS - 1)
    def _():
        def pick(c, lo, hi):
            cc = (lax.broadcasted_iota(jnp.int32, (ch, hi - lo), 1) + lo).astype(F32)
            dest = dest_ref[c * ch:(c + 1) * ch, DEST_LANE:DEST_LANE + 1]
            return _dot(jnp.where(cc == dest, 1.0, 0.0).astype(BF16), ys_ref[lo:hi, :])

        for c in range(tb // ch):
            y_ref[c * ch:(c + 1) * ch, :] = pick(c, 0, tail_row).astype(y_ref.dtype)

        @pl.when(used > tail_row)
        def _():
            for c in range(tb // ch):
                sl = slice(c * ch, (c + 1) * ch)
                y_ref[sl, :] = (y_ref[sl, :].astype(F32) + pick(c, tail_row, rows)).astype(y_ref.dtype)


def _hier_moe(xn, route, wg, wu, wd, layer):
    t, d = xn.shape
    tb = MOE_BLOCK
    nb = t // tb
    e, ff = wg.shape[2], wg.shape[4]
    group = route[GROUP_LANE, :].astype(jnp.int32).reshape(nb, tb)
    counts = jnp.sum((group[:, :, None] == jnp.arange(N_GROUPS)[None, None, :]).astype(jnp.int32), axis=1)
    nsub = (counts + MOE_SUB - 1) // MOE_SUB
    padded = nsub * MOE_SUB
    off = jnp.cumsum(padded, axis=1) - padded
    grid_spec = pltpu.PrefetchScalarGridSpec(
        num_scalar_prefetch=2,
        grid=(nb, N_GROUPS),
        in_specs=[
            pl.BlockSpec((tb, d), lambda i, g, *_: (i, 0)),
            pl.BlockSpec((ROUTE_ROWS, tb), lambda i, g, *_: (0, i)),
            pl.BlockSpec((1, 1, e, d, ff), lambda i, g, *_: (layer, g, 0, 0, 0)),
            pl.BlockSpec((1, 1, e, d, ff), lambda i, g, *_: (layer, g, 0, 0, 0)),
            pl.BlockSpec((1, 1, e, ff, d), lambda i, g, *_: (layer, g, 0, 0, 0)),
        ],
        out_specs=pl.BlockSpec((tb, d), lambda i, g, *_: (i, 0)),
        scratch_shapes=[
            pltpu.VMEM((MOE_ROWS, d), BF16),
            pltpu.VMEM((MOE_ROWS, LANES), F32),
            pltpu.VMEM((MOE_ROWS, d), BF16),
            pltpu.VMEM((tb, LANES), F32),
        ],
    )
    return pl.pallas_call(
        _moe_kernel,
        grid_spec=grid_spec,
        out_shape=jax.ShapeDtypeStruct((t, d), BF16),
        compiler_params=_params("parallel", "arbitrary"),
        name="moe",
    )(off.reshape(-1).astype(jnp.int32), nsub.reshape(-1).astype(jnp.int32), xn, route, wg, wu, wd)


def _final_kernel(x_ref, y_ref, g_ref, o_ref):
    o_ref[...] = _rms(x_ref[...] + y_ref[...].astype(F32), g_ref[...])


def _final(x, y, g):
    t, d = x.shape
    tm = ROW_TILE
    row = pl.BlockSpec((tm, d), lambda i: (i, 0))
    return pl.pallas_call(
        _final_kernel,
        grid=(t // tm,),
        in_specs=[row, row, _full(g.shape)],
        out_specs=row,
        out_shape=jax.ShapeDtypeStruct((t, d), F32),
        compiler_params=_params("parallel"),
        name="final_norm",
    )(x, y, g)


def _hi_lo(w):
    hi = w.astype(BF16)
    return hi, (w - hi.astype(F32)).astype(BF16)


def _router_params(w_rg, b_rg, w_re, b_re):
    d = w_rg.shape[0]
    n_e = N_GROUPS * EXPERTS_PER_GROUP
    pad = jnp.zeros((d, EXPERT_LANE0 - N_GROUPS), F32)
    tail = jnp.zeros((d, LANES - EXPERT_LANE0 - n_e), F32)
    w = jnp.concatenate([w_rg, pad, w_re, tail], axis=1)
    b = jnp.concatenate([b_rg, pad[0], b_re, tail[0]]).reshape(LANES, 1)
    hi, lo = _hi_lo(w)
    return jnp.concatenate([hi, lo], axis=1).T, b


def kernel(x, positions, norm_mix, norm_ffn, ab_w_in, ab_gate_bias, ab_conv_w, ab_mlstm_norm, ab_lambda,
           ab_diff_norm, ab_w_out, c_w_in, c_conv_w, c_w_out, moe_w_rg, moe_b_rg, moe_w_re, moe_b_re,
           moe_w_gate, moe_w_up, moe_w_down, final_norm):
    batch, seq, d = x.shape
    t = batch * seq
    xf = x.reshape(t, d)
    row2 = lambda v: v.reshape(1, -1).astype(F32)

    w_in = ab_w_in[0]
    mw = 2 * MLSTM_HEADS * MLSTM_HEAD_DIM
    hw = MLSTM_HEADS * MLSTM_HEAD_DIM
    ng = 2 * MLSTM_HEADS
    dw = DIFF_HEADS * DIFF_V_DIM
    o_g = mw + 2 * hw
    o_q = o_g + ng
    o_k, o_dv = o_q + dw, o_q + 2 * dw
    w0 = jnp.concatenate([w_in[:, :mw], w_in[:, mw + hw:o_g], w_in[:, o_k:o_dv]], axis=1).astype(BF16)
    w_gates = jnp.concatenate([w_in[:, o_g:o_q], jnp.zeros((d, LANES - ng), F32)], axis=1)
    wgh, wgl = _hi_lo(w_gates)
    wvt = jnp.concatenate([w_in[:, mw:mw + hw].astype(BF16), w_in[:, o_dv:o_dv + dw].astype(BF16),
                           w_in[:, o_q:o_k].astype(BF16), wgh, wgl], axis=1).T
    gb = jnp.concatenate([ab_gate_bias[0].astype(F32), jnp.zeros((LANES - ng,), F32)]).reshape(LANES, 1)

    cost, sint = _rope_table(positions.reshape(t).astype(jnp.int32))
    mqk, mo, dqt, dk, mvt, dvt, gates = _proj0(xf, cost, sint, row2(norm_mix[0]), w0, wvt, gb)
    hm = _mlstm(mqk, mvt, mo, gates, ab_conv_w[0].astype(F32), row2(ab_mlstm_norm[0]), batch, seq)
    lam_init = 0.8 - 0.6 * math.exp(-0.3 * 0)
    hd = _diff_attn(dqt, dk, dvt, ab_lambda[0].astype(F32), row2(ab_diff_norm[0]), batch, seq, lam_init)

    wrt, br = _router_params(moe_w_rg[0], moe_b_rg[0], moe_w_re[0], moe_b_re[0])
    xmid, xn, route = _mix0_tail(xf, hm, hd, ab_w_out[0].astype(BF16), row2(norm_ffn[0]), wrt, br)
    wg, wu, wd = moe_w_gate.astype(BF16), moe_w_up.astype(BF16), moe_w_down.astype(BF16)
    y = _hier_moe(xn, route, wg, wu, wd, 0)

    x1, bgate, u = _proj1(xmid, y, row2(norm_mix[1]), c_w_in[0].astype(BF16))
    wrt, br = _router_params(moe_w_rg[1], moe_b_rg[1], moe_w_re[1], moe_b_re[1])
    xmid, xn, route = _mix1_tail(x1, bgate, u, c_conv_w[0].astype(F32), c_w_out[0].astype(BF16),
                                 row2(norm_ffn[1]), wrt, br, seq)
    y = _hier_moe(xn, route, wg, wu, wd, 1)

    return _final(xmid, y, row2(final_norm)).reshape(batch, seq, d)
```

```python
import functools
import math

import jax
import jax.numpy as jnp
from jax import lax
from jax.experimental import pallas as pl
from jax.experimental.pallas import tpu as pltpu

F32 = jnp.float32
BF16 = jnp.bfloat16

EPS = 1e-6
LOG2E = math.log2(math.e)
ROPE_THETA = 10000.0
LANES = 128
SUBLANES = 8
MLSTM_HEADS = 4
MLSTM_HEAD_DIM = 128
MLSTM_CONV = 4
DIFF_HEADS = 4
DIFF_HEAD_DIM = 64
DIFF_V_DIM = 128
CONV_WIDTH = 3
N_GROUPS = 4
EXPERTS_PER_GROUP = 8
HALO = 8

ROW_TILE = 1024
CONV_ROW_TILE = 512
MLSTM_CHUNK = 256
MLSTM_STEP_ROWS = 1024
ATTN_BLOCK = 512
MOE_BLOCK = 1024
MOE_SUB = 128
MOE_ROWS = MOE_BLOCK + N_GROUPS * MOE_SUB
MOE_CHUNK = 256
VMEM_LIMIT = 56 * 1024 * 1024


def _dot(a, b):
    return jnp.dot(a, b, preferred_element_type=F32)


def _dot_nt(a, b):
    return lax.dot_general(a, b, (((1,), (1,)), ((), ())), preferred_element_type=F32)


def _split3(x):
    hi = x.astype(BF16)
    r1 = x - hi.astype(F32)
    mid = r1.astype(BF16)
    lo = (r1 - mid.astype(F32)).astype(BF16)
    return hi, mid, lo


def _rms(x, g):
    return x * lax.rsqrt(jnp.mean(x * x, axis=-1, keepdims=True) + EPS) * g


def _sigmoid(x):
    return 0.5 * jnp.tanh(0.5 * x) + 0.5


def _shift_matrix(rows, taps):
    r = jnp.arange(rows)
    return jnp.concatenate([r[:, None] - s == r[None, :] for s in range(1, taps)], axis=0).astype(BF16)


def _shift_conv(xb, shift_ref, cw, tail, taps):
    rows, c = xb.shape
    x32 = xb.astype(F32)
    shifted = _dot(shift_ref[...], xb)
    last = taps - 1
    conv = cw[last:last + 1, :] * x32
    for s in range(1, taps):
        conv = conv + cw[last - s:last - s + 1, :] * shifted[(s - 1) * rows:s * rows, :]
    head = jnp.concatenate([tail, x32[0:HALO, :]], axis=0)
    rowi = lax.broadcasted_iota(jnp.int32, (HALO, c), 0)
    fix = jnp.zeros((HALO, c), F32)
    for s in range(1, taps):
        fix = fix + cw[last - s:last - s + 1, :] * jnp.where(rowi < s, head[HALO - s:2 * HALO - s, :], 0.0)
    return jnp.concatenate([conv[0:HALO, :] + fix, conv[HALO:, :]], axis=0)


def _roll_conv(xb, cw, tail, taps):
    rows, c = xb.shape
    x32 = xb.astype(F32)
    last = taps - 1
    conv = cw[last:last + 1, :] * x32
    head = jnp.concatenate([tail, x32[0:HALO, :]], axis=0)
    first = cw[last:last + 1, :] * x32[0:HALO, :]
    for s in range(1, taps):
        w = cw[last - s:last - s + 1, :]
        conv = conv + w * pltpu.roll(x32, s, 0)
        first = first + w * head[HALO - s:2 * HALO - s, :]
    return jnp.concatenate([first, conv[HALO:, :]], axis=0)


def _params(*sem):
    return pltpu.CompilerParams(dimension_semantics=sem, vmem_limit_bytes=VMEM_LIMIT)


def _full(shape):
    return pl.BlockSpec(shape, lambda *_: (0,) * len(shape))


def _proj0_kernel(x_ref, cos_ref, sin_ref, cost_ref, sint_ref, g_ref, w_ref, wvt_ref, gb_ref,
                  mqk_ref, mo_ref, dqt_ref, dk_ref, mvt_ref, dvt_ref, gates_ref):
    xn = _rms(x_ref[...], g_ref[...])
    xb = xn.astype(BF16)
    xl = (xn - xb.astype(F32)).astype(BF16)
    half = DIFF_HEAD_DIM // 2
    dw = DIFF_HEADS * DIFF_V_DIM

    def seg(lo, hi):
        return _dot(xb, w_ref[:, lo:hi])

    mqk_ref[...] = seg(0, 1024).astype(BF16)
    mo_ref[...] = seg(1024, 1536).astype(BF16)
    vt = _dot_nt(wvt_ref[...], xb)
    mvt_ref[...] = vt[0:dw, :].astype(BF16)
    dvt_ref[...] = vt[dw:2 * dw, :].astype(BF16)
    g0 = 3 * dw
    gates_ref[...] = (vt[g0:g0 + LANES, :] + (vt[g0 + LANES:g0 + 2 * LANES, :]
                      + _dot_nt(wvt_ref[g0:g0 + LANES, :], xl))) + gb_ref[...]

    ct = cost_ref[...]
    st = sint_ref[...]
    q_scale = LOG2E * DIFF_HEAD_DIM ** -0.5
    for r in range(2 * DIFF_HEADS):
        lo = 2 * dw + r * DIFF_HEAD_DIM
        x1 = vt[lo:lo + half, :]
        x2 = vt[lo + half:lo + 2 * half, :]
        out = lo - 2 * dw
        dqt_ref[out:out + half, :] = ((x1 * ct - x2 * st) * q_scale).astype(BF16)
        dqt_ref[out + half:out + 2 * half, :] = ((x2 * ct + x1 * st) * q_scale).astype(BF16)

    heads = 2 * DIFF_HEADS
    c = cos_ref[...]
    s = sin_ref[...]
    c4 = jnp.concatenate([c, c] * heads, axis=1)
    s4 = jnp.concatenate([-s, s] * heads, axis=1)
    lane4 = lax.broadcasted_iota(jnp.int32, c4.shape, 1)
    first4 = (lane4 & (DIFF_HEAD_DIM - 1)) < half
    k = seg(1536, 2048)
    partner = jnp.where(first4, pltpu.roll(k, dw - half, 1), pltpu.roll(k, half, 1))
    dk = (k * c4 + partner * s4).astype(BF16)
    for h in range(DIFF_HEADS):
        dk_ref[h] = dk[:, h * DIFF_V_DIM:(h + 1) * DIFF_V_DIM]


def _rope_table_kernel(pos_ref, invf_ref, cos_ref, sin_ref):
    ang = invf_ref[...] * pos_ref[...].astype(F32)
    cos_ref[...] = jnp.cos(ang)
    sin_ref[...] = jnp.sin(ang)


def _rope_table(positions):
    t = positions.shape[0]
    half = DIFF_HEAD_DIM // 2
    inv_freq = (1.0 / (ROPE_THETA ** (jnp.arange(half, dtype=F32) / half))).reshape(half, 1)
    tl = min(t, 4096)
    return pl.pallas_call(
        _rope_table_kernel,
        grid=(t // tl,),
        in_specs=[pl.BlockSpec((1, tl), lambda i: (0, i)), _full(inv_freq.shape)],
        out_specs=[pl.BlockSpec((half, tl), lambda i: (0, i))] * 2,
        out_shape=[jax.ShapeDtypeStruct((half, t), F32)] * 2,
        compiler_params=_params("parallel"),
        name="rope_table",
    )(positions.reshape(1, t), inv_freq)


def _proj0(x, cost, sint, g, w, wvt, gb):
    t, d = x.shape
    tm = ROW_TILE
    row = lambda n: pl.BlockSpec((tm, n), lambda i: (i, 0))
    col = lambda n: pl.BlockSpec((n, tm), lambda i: (0, i))
    heads = pl.BlockSpec((DIFF_HEADS, tm, DIFF_V_DIM), lambda i: (0, i, 0))
    cos, sin = cost.T, sint.T
    half = cost.shape[0]
    out_shapes = [
        jax.ShapeDtypeStruct((t, 1024), BF16), jax.ShapeDtypeStruct((t, 512), BF16),
        jax.ShapeDtypeStruct((512, t), BF16), jax.ShapeDtypeStruct((DIFF_HEADS, t, DIFF_V_DIM), BF16),
        jax.ShapeDtypeStruct((512, t), BF16), jax.ShapeDtypeStruct((512, t), BF16),
        jax.ShapeDtypeStruct((LANES, t), F32),
    ]
    return pl.pallas_call(
        _proj0_kernel,
        grid=(t // tm,),
        in_specs=[row(d), row(half), row(half), col(half), col(half), _full(g.shape), _full(w.shape),
                  _full(wvt.shape), _full(gb.shape)],
        out_specs=[row(1024), row(512), col(512), heads, col(512), col(512), col(LANES)],
        out_shape=out_shapes,
        compiler_params=_params("parallel"),
        name="proj0",
    )(x, cos, sin, cost, sint, g, w, wvt, gb)


def _mlstm_kernel(mqk_ref, mvt_ref, mo_ref, gates_ref, shift_ref, cw_ref, norm_ref, hm_ref,
                  tail_ref, cn_ref, m_ref):
    L = MLSTM_CHUNK
    hd = MLSTM_HEAD_DIM
    width = MLSTM_HEADS * hd

    @pl.when(pl.program_id(1) == 0)
    def _():
        tail_ref[...] = jnp.zeros_like(tail_ref)
        cn_ref[...] = jnp.zeros_like(cn_ref)
        m_ref[...] = jnp.zeros_like(m_ref)

    krow = lax.broadcasted_iota(jnp.int32, (L, L), 0)
    qcol = lax.broadcasted_iota(jnp.int32, (L, L), 1)
    causal_t = krow <= qcol
    triu = jnp.where(causal_t, 1.0, 0.0).astype(BF16)
    ones = jnp.ones((SUBLANES, L), BF16)

    def chunk(ci, carry):
        rows = pl.ds(pl.multiple_of(ci * L, L), L)
        xb = mqk_ref[rows, :]
        conv = _shift_conv(xb, shift_ref, cw_ref[...], tail_ref[...], MLSTM_CONV)
        tail_ref[...] = xb[L - HALO:L, :].astype(F32)
        qk = conv * _sigmoid(conv)
        qb = (qk[:, :width] * (hd ** -0.5)).astype(BF16)
        k = qk[:, width:]
        kb = k.astype(BF16)

        g8 = gates_ref[0:SUBLANES, rows]
        logf = (jnp.minimum(g8, 0.0) - jnp.log(1.0 + jnp.exp(-jnp.abs(g8)))) * LOG2E
        f_hi, f_mid, f_lo = _split3(logf)
        b8 = _dot(f_hi, triu) + (_dot(f_mid, triu) + _dot(f_lo, triu))
        cj8 = g8 * LOG2E - pltpu.roll(b8, MLSTM_HEADS, 0)
        cj_all = jnp.concatenate([cj8, jnp.zeros((LANES - SUBLANES, L), F32)], axis=0).T

        for h in range(MLSTM_HEADS):
            sl = slice(h * hd, (h + 1) * hd)
            b_row = b8[MLSTM_HEADS + h:MLSTM_HEADS + h + 1, :]
            m_prev = m_ref[h:h + 1, 0:1]
            cn_prev = cn_ref[h]
            vt1 = jnp.concatenate([mvt_ref[sl, rows], ones], axis=0)

            cmat = jnp.where(causal_t, cj_all[:, h:h + 1], -jnp.inf)
            mm = jnp.maximum(m_prev, jnp.max(cmat, axis=0, keepdims=True))
            w_t = jnp.exp2(cmat - mm)
            s_t = _dot_nt(kb[:, sl], qb[:, sl]) * w_t
            inter_w = jnp.exp2(m_prev - mm)
            tot = _dot(vt1, s_t.astype(BF16)) + inter_w * _dot_nt(cn_prev.astype(BF16), qb[:, sl])
            den = tot[hd:hd + 1, :]
            h_t = tot[0:hd, :] / jnp.maximum(jnp.abs(den), jnp.exp2(-(b_row + mm)))

            kw = k[:, sl] * w_t[:, L - 1:L]
            cn_ref[h] = inter_w[:, L - 1:L] * cn_prev + _dot(vt1, kw.astype(BF16))
            m_ref[h:h + 1, :] = jnp.broadcast_to(b_row[:, L - 1:L] + mm[:, L - 1:L], (1, LANES))

            scale = lax.rsqrt(jnp.mean(h_t * h_t, axis=0, keepdims=True) + EPS)
            hn = (h_t * scale).T * norm_ref[...]
            hm_ref[rows, sl] = (hn * _sigmoid(mo_ref[rows, sl].astype(F32))).astype(BF16)
        return carry

    lax.fori_loop(0, mqk_ref.shape[0] // L, chunk, 0)


def _mlstm(mqk, mvt, mo, gates, cw, norm, batch, seq):
    t = mqk.shape[0]
    L = MLSTM_CHUNK
    rows = min(seq, MLSTM_STEP_ROWS)
    nc = seq // rows
    width = MLSTM_HEADS * MLSTM_HEAD_DIM
    row = lambda n: pl.BlockSpec((rows, n), lambda b, c: (b * nc + c, 0))
    col = lambda n: pl.BlockSpec((n, rows), lambda b, c: (0, b * nc + c))
    shift = _shift_matrix(L, MLSTM_CONV)
    return pl.pallas_call(
        _mlstm_kernel,
        grid=(batch, nc),
        in_specs=[row(2 * width), col(width), row(width), col(LANES), _full(shift.shape), _full(cw.shape),
                  _full(norm.shape)],
        out_specs=row(width),
        out_shape=jax.ShapeDtypeStruct((t, width), BF16),
        scratch_shapes=[
            pltpu.VMEM((HALO, 2 * width), F32),
            pltpu.VMEM((MLSTM_HEADS, MLSTM_HEAD_DIM + SUBLANES, MLSTM_HEAD_DIM), F32),
            pltpu.VMEM((SUBLANES, LANES), F32),
        ],
        compiler_params=_params("parallel", "arbitrary"),
        name="mlstm",
    )(mqk, mvt, mo, gates, shift, cw, norm)


def _diff_attn_kernel(qt_ref, k_ref, vt_ref, lam_ref, norm_ref, o_ref,
                      acc0_ref, acc1_ref, m_ref, sa0_ref, sa1_ref, sb0_ref, sb1_ref, *, lam_init):
    tq = ATTN_BLOCK
    dv = DIFF_V_DIM
    nq = k_ref.shape[1] // tq
    frow = lax.broadcasted_iota(jnp.int32, (dv, tq), 0)
    krow = lax.broadcasted_iota(jnp.int32, (tq, tq), 0)
    qcol = lax.broadcasted_iota(jnp.int32, (tq, tq), 1)
    causal_t = krow <= qcol
    ones = jnp.ones((SUBLANES, tq), BF16)
    accs = (acc0_ref, acc1_ref)
    buf_a = (sa0_ref, sa1_ref)
    buf_b = (sb0_ref, sb1_ref)
    lv = lam_ref[...]
    lam = (jnp.exp(jnp.sum(lv[0:1, :] * lv[1:2, :], axis=-1, keepdims=True))
           - jnp.exp(jnp.sum(lv[2:3, :] * lv[3:4, :], axis=-1, keepdims=True)) + lam_init)

    def queries(qi):
        qt = qt_ref[:, pl.ds(pl.multiple_of(qi * tq, tq), tq)]
        zero = jnp.zeros_like(qt)
        return (jnp.where(frow < DIFF_HEAD_DIM, qt, zero), jnp.where(frow >= DIFF_HEAD_DIM, qt, zero))

    def scores_of(qs, j, dst):
        kb = k_ref[0, pl.ds(pl.multiple_of(j * tq, tq), tq), :]
        for c in range(2):
            dst[c][...] = _dot(kb, qs[c])

    def reset():
        for acc in accs:
            acc[...] = jnp.zeros_like(acc)
        m_ref[...] = jnp.full(m_ref.shape, -jnp.inf, F32)

    reset()
    scores_of(queries(0), 0, buf_a)

    def q_block(qi, carry):
        qrow = pl.multiple_of(qi * tq, tq)
        qs = queries(qi)

        def scores(j, dst):
            scores_of(qs, j, dst)

        def absorb(j, src, masked):
            start = pl.multiple_of(j * tq, tq)
            vt1 = jnp.concatenate([vt_ref[:, pl.ds(start, tq)], ones], axis=0)
            s = [src[c][...] for c in range(2)]
            if masked:
                s = [jnp.where(causal_t, sc, -jnp.inf) for sc in s]
            m_old = [m_ref[c:c + 1, :] for c in range(2)]
            m_new = [jnp.maximum(m_old[c], jnp.max(s[c], axis=0, keepdims=True)) for c in range(2)]
            p = [jnp.exp2(s[c] - m_new[c]).astype(BF16) for c in range(2)]
            alpha = [jnp.exp2(m_old[c] - m_new[c]) for c in range(2)]
            pv = [_dot(vt1, p[c]) for c in range(2)]
            for c in range(2):
                accs[c][...] = alpha[c] * accs[c][...] + pv[c]
                m_ref[c:c + 1, :] = m_new[c]

        def pair(jj, inner):
            j = 2 * jj
            scores(j + 1, buf_b)
            absorb(j, buf_a, False)
            scores(j + 2, buf_a)
            absorb(j + 1, buf_b, False)
            return inner

        lax.fori_loop(0, qi // 2, pair, 0)

        @pl.when(qi % 2 == 0)
        def _():
            absorb(qi, buf_a, True)

        @pl.when(qi % 2 == 1)
        def _():
            scores(qi, buf_b)
            absorb(qi - 1, buf_a, False)
            absorb(qi, buf_b, True)

        scores_of(queries(jnp.minimum(qi + 1, nq - 1)), 0, buf_a)

        a1 = acc0_ref[...]
        a2 = acc1_ref[...]
        o_t = a1[0:dv, :] / a1[dv:dv + 1, :] - lam * (a2[0:dv, :] / a2[dv:dv + 1, :])
        o_ref[0, pl.ds(qrow, tq), :] = (_rms(o_t.T, norm_ref[...]) * (1.0 - lam_init)).astype(BF16)
        reset()
        return carry

    lax.fori_loop(0, nq, q_block, 0)


def _diff_attn(dqt, dk, dvt, lam_vecs, norm, batch, seq, lam_init):
    t = dk.shape[1]
    tq = ATTN_BLOCK
    rows = pl.BlockSpec((1, seq, DIFF_V_DIM), lambda b, h: (h, b, 0))
    cols = pl.BlockSpec((DIFF_V_DIM, seq), lambda b, h: (h, b))
    return pl.pallas_call(
        functools.partial(_diff_attn_kernel, lam_init=lam_init),
        grid=(batch, DIFF_HEADS),
        in_specs=[cols, rows, cols, _full(lam_vecs.shape), _full(norm.shape)],
        out_specs=rows,
        out_shape=jax.ShapeDtypeStruct((DIFF_HEADS, t, DIFF_V_DIM), BF16),
        scratch_shapes=([pltpu.VMEM((DIFF_V_DIM + SUBLANES, tq), F32)] * 2 + [pltpu.VMEM((SUBLANES, tq), F32)]
                        + [pltpu.VMEM((tq, tq), F32)] * 4),
        compiler_params=_params("parallel", "parallel"),
        name="diff_attn",
    )(dqt, dk, dvt, lam_vecs, norm)


ROUTE_ROWS = 2 * SUBLANES
GROUP_LANE = EXPERTS_PER_GROUP
DEST_LANE = EXPERTS_PER_GROUP
EXPERT_LANE0 = 8


def _route(xn, wrt_ref, br_ref):
    xh = xn.astype(BF16)
    xl = (xn - xh.astype(F32)).astype(BF16)
    both = _dot_nt(wrt_ref[...], xh)
    logits = both[0:LANES, :] + (both[LANES:2 * LANES, :] + _dot_nt(wrt_ref[0:LANES, :], xl)) + br_ref[...]
    e = EXPERTS_PER_GROUP
    tm = logits.shape[1]
    row = lax.broadcasted_iota(jnp.int32, (e, tm), 0)
    ninf = -jnp.inf
    big = jnp.int32(LANES)

    is_g = row < N_GROUPS
    gl = jnp.where(is_g, logits[0:e, :], ninf)
    gmax = jnp.max(gl, axis=0, keepdims=True)
    gsum = jnp.sum(jnp.where(is_g, jnp.exp(gl - gmax), 0.0), axis=0, keepdims=True)
    g_val = 1.0 / gsum
    g_idx = jnp.min(jnp.where(gl == gmax, row, big), axis=0, keepdims=True)

    el = logits[EXPERT_LANE0:EXPERT_LANE0 + e, :]
    for g in range(1, N_GROUPS):
        lo = EXPERT_LANE0 + e * g
        el = jnp.where(g_idx == g, logits[lo:lo + e, :], el)
    v1 = jnp.max(el, axis=0, keepdims=True)
    i1 = jnp.min(jnp.where(el == v1, row, big), axis=0, keepdims=True)
    el2 = jnp.where(row == i1, ninf, el)
    v2 = jnp.max(el2, axis=0, keepdims=True)
    i2 = jnp.min(jnp.where(el2 == v2, row, big), axis=0, keepdims=True)
    e21 = jnp.exp(v2 - v1)
    w1 = 1.0 / (1.0 + e21)
    w2 = e21 * w1
    gates = g_val * (jnp.where(row == i1, w1, 0.0) + jnp.where(row == i2, w2, 0.0))
    tail = jnp.where(row == 0, g_idx.astype(F32), 0.0)
    return xh, jnp.concatenate([gates, tail], axis=0)


def _mix0_tail_kernel(x_ref, hm_ref, hd_ref, wo_ref, nf_ref, wrt_ref, br_ref,
                      xmid_ref, xn_ref, route_ref):
    merged = jnp.concatenate([hm_ref[...]] + [hd_ref[h] for h in range(DIFF_HEADS)], axis=1)
    xmid = x_ref[...] + _dot(merged, wo_ref[...])
    xmid_ref[...] = xmid
    xn_ref[...], route_ref[...] = _route(_rms(xmid, nf_ref[...]), wrt_ref, br_ref)


def _mix0_tail(x, hm, hd, wo, nf, wrt, br):
    t, d = x.shape
    tm = ROW_TILE
    row = lambda n: pl.BlockSpec((tm, n), lambda i: (i, 0))
    return pl.pallas_call(
        _mix0_tail_kernel,
        grid=(t // tm,),
        in_specs=[row(d), row(hm.shape[1]), pl.BlockSpec((DIFF_HEADS, tm, DIFF_V_DIM), lambda i: (0, i, 0)),
                  _full(wo.shape), _full(nf.shape), _full(wrt.shape), _full(br.shape)],
        out_specs=[row(d), row(d), pl.BlockSpec((ROUTE_ROWS, tm), lambda i: (0, i))],
        out_shape=[jax.ShapeDtypeStruct((t, d), F32), jax.ShapeDtypeStruct((t, d), BF16),
                   jax.ShapeDtypeStruct((ROUTE_ROWS, t), F32)],
        compiler_params=_params("parallel"),
        name="mix0_tail",
    )(x, hm, hd, wo, nf, wrt, br)


def _proj1_kernel(x_ref, y_ref, g_ref, w_ref, x1_ref, b_ref, u_ref):
    d = x_ref.shape[1]
    x1 = x_ref[...] + y_ref[...].astype(F32)
    x1_ref[...] = x1
    xb = _rms(x1, g_ref[...]).astype(BF16)
    b_ref[...] = _dot(xb, w_ref[:, 0:d]).astype(BF16)
    u_ref[...] = (_dot(xb, w_ref[:, d:2 * d]) * _dot(xb, w_ref[:, 2 * d:3 * d])).astype(BF16)


def _proj1(x, y, g, w):
    t, d = x.shape
    tm = ROW_TILE
    row = lambda n: pl.BlockSpec((tm, n), lambda i: (i, 0))
    return pl.pallas_call(
        _proj1_kernel,
        grid=(t // tm,),
        in_specs=[row(d), row(d), _full(g.shape), _full(w.shape)],
        out_specs=[row(d), row(d), row(d)],
        out_shape=[jax.ShapeDtypeStruct((t, d), F32), jax.ShapeDtypeStruct((t, d), BF16),
                   jax.ShapeDtypeStruct((t, d), BF16)],
        compiler_params=_params("parallel"),
        name="proj1",
    )(x, y, g, w)


def _mix1_tail_kernel(x_ref, b_ref, u_ref, halo_ref, cw_ref, wo_ref, nf_ref, wrt_ref, br_ref,
                      xmid_ref, xn_ref, route_ref, *, tiles_per_seq):
    seq_start = pl.program_id(0) % tiles_per_seq == 0
    halo = halo_ref[...].astype(F32)
    tail = jnp.where(seq_start, jnp.zeros_like(halo), halo)
    conv = _roll_conv(u_ref[...], cw_ref[...], tail, CONV_WIDTH)
    mixed = (b_ref[...].astype(F32) * conv).astype(BF16)
    xmid = x_ref[...] + _dot(mixed, wo_ref[...])
    xmid_ref[...] = xmid
    xn_ref[...], route_ref[...] = _route(_rms(xmid, nf_ref[...]), wrt_ref, br_ref)


def _mix1_tail(x, b, u, cw, wo, nf, wrt, br, seq):
    t, d = x.shape
    tm = CONV_ROW_TILE
    row = lambda n: pl.BlockSpec((tm, n), lambda i: (i, 0))
    halo = pl.BlockSpec((HALO, d), lambda i: (jnp.maximum(i * (tm // HALO) - 1, 0), 0))
    return pl.pallas_call(
        functools.partial(_mix1_tail_kernel, tiles_per_seq=seq // tm),
        grid=(t // tm,),
        in_specs=[row(d), row(d), row(d), halo, _full(cw.shape), _full(wo.shape),
                  _full(nf.shape), _full(wrt.shape), _full(br.shape)],
        out_specs=[row(d), row(d), pl.BlockSpec((ROUTE_ROWS, tm), lambda i: (0, i))],
        out_shape=[jax.ShapeDtypeStruct((t, d), F32), jax.ShapeDtypeStruct((t, d), BF16),
                   jax.ShapeDtypeStruct((ROUTE_ROWS, t), F32)],
        compiler_params=_params("parallel"),
        name="mix1_tail",
    )(x, b, u, u, cw, wo, nf, wrt, br)


def _moe_kernel(off_ref, nsub_ref, x_ref, r_ref, wg_ref, wu_ref, wd_ref, y_ref,
                xs_ref, gs_ref, ys_ref, dest_ref):
    i = pl.program_id(0)
    g = pl.program_id(1)
    tb = MOE_BLOCK
    rows = MOE_ROWS
    ch = MOE_CHUNK
    d = x_ref.shape[1]
    tail_row = rows - ch
    last_group = i * N_GROUPS + N_GROUPS - 1
    used = off_ref[last_group] + nsub_ref[last_group] * MOE_SUB

    @pl.when(g == 0)
    def _():
        r = r_ref[...]
        grp = r[GROUP_LANE:GROUP_LANE + 1, :]
        row8 = lax.broadcasted_iota(jnp.int32, (SUBLANES, tb), 0)
        onehot = jnp.where(row8.astype(F32) == grp, 1.0, 0.0)
        row1 = lax.broadcasted_iota(jnp.int32, (SUBLANES, 1), 0)
        base = jnp.zeros((SUBLANES, 1), F32)
        for gg in range(N_GROUPS):
            base = jnp.where(row1 == gg, off_ref[i * N_GROUPS + gg].astype(F32), base)
        rr = lax.broadcasted_iota(jnp.int32, (ch, ch), 0)
        cc = lax.broadcasted_iota(jnp.int32, (ch, ch), 1)
        upper = jnp.where(rr <= cc, 1.0, 0.0).astype(BF16)
        parts = []
        for c in range(tb // ch):
            sel = onehot[:, c * ch:(c + 1) * ch]
            rank = _dot(sel.astype(BF16), upper) + base
            parts.append(jnp.sum(sel * rank, axis=0, keepdims=True) - 1.0)
            base = base + jnp.sum(sel, axis=1, keepdims=True)
        dest_row = jnp.concatenate(parts, axis=1)
        side = jnp.concatenate([r[0:SUBLANES, :], jnp.broadcast_to(dest_row, (SUBLANES, tb)),
                                jnp.zeros((LANES - 2 * SUBLANES, tb), F32)], axis=0).T
        dest_ref[...] = side
        lane = lax.broadcasted_iota(jnp.int32, (tb, LANES), 1)
        gates = jnp.where(lane < EXPERTS_PER_GROUP, side, 0.0)
        g_hi = gates.astype(BF16)
        g_lo = (gates - g_hi.astype(F32)).astype(BF16)
        xcat = jnp.concatenate([x_ref[...], g_hi, g_lo], axis=1)

        def move(c):
            rr = (lax.broadcasted_iota(jnp.int32, (ch, tb), 0) + c * ch).astype(F32)
            p = jnp.where(rr == dest_row, 1.0, 0.0).astype(BF16)
            moved = _dot(p, xcat)
            xs_ref[c * ch:(c + 1) * ch, :] = moved[:, 0:d].astype(BF16)
            gs_ref[c * ch:(c + 1) * ch, :] = moved[:, d:d + LANES] + moved[:, d + LANES:d + 2 * LANES]

        for c in range(rows // ch - 1):
            move(c)

        @pl.when(used > tail_row)
        def _():
            move(rows // ch - 1)

        ys_ref[...] = jnp.zeros_like(ys_ref)

    off = off_ref[i * N_GROUPS + g]
    n_sub = nsub_ref[i * N_GROUPS + g]

    def expert_rows(r0, size):
        xt = xs_ref[pl.ds(r0, size), :]
        gt = gs_ref[pl.ds(r0, size), :]
        acts = []
        for e in range(EXPERTS_PER_GROUP):
            hg = _dot(xt, wg_ref[0, 0, e])
            up = _dot(xt, wu_ref[0, 0, e])
            acts.append((hg * _sigmoid(hg) * up * gt[:, e:e + 1]).astype(BF16))
        act_all = jnp.concatenate(acts, axis=1)
        wd_all = wd_ref[0, 0].reshape(EXPERTS_PER_GROUP * wd_ref.shape[3], d)
        ys_ref[pl.ds(r0, size), :] = _dot(act_all, wd_all).astype(BF16)

    def pair(j, carry):
        expert_rows(pl.multiple_of(off + j * (2 * MOE_SUB), MOE_SUB), 2 * MOE_SUB)
        return carry

    lax.fori_loop(0, n_sub // 2, pair, 0)

    @pl.when(n_sub % 2 == 1)
    def _():
        expert_rows(pl.multiple_of(off + (n_sub - 1) * MOE_SUB, MOE_SUB), MOE_SUB)

    @pl.when(g == N_GROUPS - 1)
    def _():
        def pick(c, lo, hi):
            cc = (lax.broadcasted_iota(jnp.int32, (ch, hi - lo), 1) + lo).astype(F32)
            dest = dest_ref[c * ch:(c + 1) * ch, DEST_LANE:DEST_LANE + 1]
            return _dot(jnp.where(cc == dest, 1.0, 0.0).astype(BF16), ys_ref[lo:hi, :])

        for c in range(tb // ch):
            y_ref[c * ch:(c + 1) * ch, :] = pick(c, 0, tail_row).astype(y_ref.dtype)

        @pl.when(used > tail_row)
        def _():
            for c in range(tb // ch):
                sl = slice(c * ch, (c + 1) * ch)
                y_ref[sl, :] = (y_ref[sl, :].astype(F32) + pick(c, tail_row, rows)).astype(y_ref.dtype)


def _hier_moe(xn, route, wg, wu, wd, layer):
    t, d = xn.shape
    tb = MOE_BLOCK
    nb = t // tb
    e, ff = wg.shape[2], wg.shape[4]
    group = route[GROUP_LANE, :].astype(jnp.int32).reshape(nb, tb)
    counts = jnp.sum((group[:, :, None] == jnp.arange(N_GROUPS)[None, None, :]).astype(jnp.int32), axis=1)
    nsub = (counts + MOE_SUB - 1) // MOE_SUB
    padded = nsub * MOE_SUB
    off = jnp.cumsum(padded, axis=1) - padded
    grid_spec = pltpu.PrefetchScalarGridSpec(
        num_scalar_prefetch=2,
        grid=(nb, N_GROUPS),
        in_specs=[
            pl.BlockSpec((tb, d), lambda i, g, *_: (i, 0)),
            pl.BlockSpec((ROUTE_ROWS, tb), lambda i, g, *_: (0, i)),
            pl.BlockSpec((1, 1, e, d, ff), lambda i, g, *_: (layer, g, 0, 0, 0)),
            pl.BlockSpec((1, 1, e, d, ff), lambda i, g, *_: (layer, g, 0, 0, 0)),
            pl.BlockSpec((1, 1, e, ff, d), lambda i, g, *_: (layer, g, 0, 0, 0)),
        ],
        out_specs=pl.BlockSpec((tb, d), lambda i, g, *_: (i, 0)),
        scratch_shapes=[
            pltpu.VMEM((MOE_ROWS, d), BF16),
            pltpu.VMEM((MOE_ROWS, LANES), F32),
            pltpu.VMEM((MOE_ROWS, d), BF16),
            pltpu.VMEM((tb, LANES), F32),
        ],
    )
    return pl.pallas_call(
        _moe_kernel,
        grid_spec=grid_spec,
        out_shape=jax.ShapeDtypeStruct((t, d), BF16),
        compiler_params=_params("parallel", "arbitrary"),
        name="moe",
    )(off.reshape(-1).astype(jnp.int32), nsub.reshape(-1).astype(jnp.int32), xn, route, wg, wu, wd)


def _final_kernel(x_ref, y_ref, g_ref, o_ref):
    o_ref[...] = _rms(x_ref[...] + y_ref[...].astype(F32), g_ref[...])


def _final(x, y, g):
    t, d = x.shape
    tm = ROW_TILE
    row = pl.BlockSpec((tm, d), lambda i: (i, 0))
    return pl.pallas_call(
        _final_kernel,
        grid=(t // tm,),
        in_specs=[row, row, _full(g.shape)],
        out_specs=row,
        out_shape=jax.ShapeDtypeStruct((t, d), F32),
        compiler_params=_params("parallel"),
        name="final_norm",
    )(x, y, g)


def _hi_lo(w):
    hi = w.astype(BF16)
    return hi, (w - hi.astype(F32)).astype(BF16)


def _router_params(w_rg, b_rg, w_re, b_re):
    d = w_rg.shape[0]
    n_e = N_GROUPS * EXPERTS_PER_GROUP
    pad = jnp.zeros((d, EXPERT_LANE0 - N_GROUPS), F32)
    tail = jnp.zeros((d, LANES - EXPERT_LANE0 - n_e), F32)
    w = jnp.concatenate([w_rg, pad, w_re, tail], axis=1)
    b = jnp.concatenate([b_rg, pad[0], b_re, tail[0]]).reshape(LANES, 1)
    hi, lo = _hi_lo(w)
    return jnp.concatenate([hi, lo], axis=1).T, b


def kernel(x, positions, norm_mix, norm_ffn, ab_w_in, ab_gate_bias, ab_conv_w, ab_mlstm_norm, ab_lambda,
           ab_diff_norm, ab_w_out, c_w_in, c_conv_w, c_w_out, moe_w_rg, moe_b_rg, moe_w_re, moe_b_re,
           moe_w_gate, moe_w_up, moe_w_down, final_norm):
    batch, seq, d = x.shape
    t = batch * seq
    xf = x.reshape(t, d)
    row2 = lambda v: v.reshape(1, -1).astype(F32)

    w_in = ab_w_in[0]
    mw = 2 * MLSTM_HEADS * MLSTM_HEAD_DIM
    hw = MLSTM_HEADS * MLSTM_HEAD_DIM
    ng = 2 * MLSTM_HEADS
    dw = DIFF_HEADS * DIFF_V_DIM
    o_g = mw + 2 * hw
    o_q = o_g + ng
    o_k, o_dv = o_q + dw, o_q + 2 * dw
    w0 = jnp.concatenate([w_in[:, :mw], w_in[:, mw + hw:o_g], w_in[:, o_k:o_dv]], axis=1).astype(BF16)
    w_gates = jnp.concatenate([w_in[:, o_g:o_q], jnp.zeros((d, LANES - ng), F32)], axis=1)
    wgh, wgl = _hi_lo(w_gates)
    wvt = jnp.concatenate([w_in[:, mw:mw + hw].astype(BF16), w_in[:, o_dv:o_dv + dw].astype(BF16),
                           w_in[:, o_q:o_k].astype(BF16), wgh, wgl], axis=1).T
    gb = jnp.concatenate([ab_gate_bias[0].astype(F32), jnp.zeros((LANES - ng,), F32)]).reshape(LANES, 1)

    cost, sint = _rope_table(positions.reshape(t).astype(jnp.int32))
    mqk, mo, dqt, dk, mvt, dvt, gates = _proj0(xf, cost, sint, row2(norm_mix[0]), w0, wvt, gb)
    hm = _mlstm(mqk, mvt, mo, gates, ab_conv_w[0].astype(F32), row2(ab_mlstm_norm[0]), batch, seq)
    lam_init = 0.8 - 0.6 * math.exp(-0.3 * 0)
    hd = _diff_attn(dqt, dk, dvt, ab_lambda[0].astype(F32), row2(ab_diff_norm[0]), batch, seq, lam_init)

    wrt, br = _router_params(moe_w_rg[0], moe_b_rg[0], moe_w_re[0], moe_b_re[0])
    xmid, xn, route = _mix0_tail(xf, hm, hd, ab_w_out[0].astype(BF16), row2(norm_ffn[0]), wrt, br)
    wg, wu, wd = moe_w_gate.astype(BF16), moe_w_up.astype(BF16), moe_w_down.astype(BF16)
    y = _hier_moe(xn, route, wg, wu, wd, 0)

    x1, bgate, u = _proj1(xmid, y, row2(norm_mix[1]), c_w_in[0].astype(BF16))
    wrt, br = _router_params(moe_w_rg[1], moe_b_rg[1], moe_w_re[1], moe_b_re[1])
    xmid, xn, route = _mix1_tail(x1, bgate, u, c_conv_w[0].astype(F32), c_w_out[0].astype(BF16),
                                 row2(norm_ffn[1]), wrt, br, seq)
    y = _hier_moe(xn, route, wg, wu, wd, 1)

    return _final(xmid, y, row2(final_norm)).reshape(batch, seq, d)
```

```python
import functools
import math

import jax
import jax.numpy as jnp
from jax import lax
from jax.experimental import pallas as pl
from jax.experimental.pallas import tpu as pltpu

F32 = jnp.float32
BF16 = jnp.bfloat16

EPS = 1e-6
LOG2E = math.log2(math.e)
ROPE_THETA = 10000.0
LANES = 128
SUBLANES = 8
MLSTM_HEADS = 4
MLSTM_HEAD_DIM = 128
MLSTM_CONV = 4
DIFF_HEADS = 4
DIFF_HEAD_DIM = 64
DIFF_V_DIM = 128
CONV_WIDTH = 3
N_GROUPS = 4
EXPERTS_PER_GROUP = 8
HALO = 8

ROW_TILE = 1024
CONV_ROW_TILE = 1024
MLSTM_CHUNK = 256
MLSTM_STEP_ROWS = 1024
ATTN_BLOCK = 512
MOE_BLOCK = 1024
MOE_SUB = 128
MOE_ROWS = MOE_BLOCK + N_GROUPS * MOE_SUB
MOE_CHUNK = 256
VMEM_LIMIT = 56 * 1024 * 1024


def _dot(a, b):
    return jnp.dot(a, b, preferred_element_type=F32)


def _dot_nt(a, b):
    return lax.dot_general(a, b, (((1,), (1,)), ((), ())), preferred_element_type=F32)


def _split3(x):
    hi = x.astype(BF16)
    r1 = x - hi.astype(F32)
    mid = r1.astype(BF16)
    lo = (r1 - mid.astype(F32)).astype(BF16)
    return hi, mid, lo


def _rms(x, g):
    return x * lax.rsqrt(jnp.mean(x * x, axis=-1, keepdims=True) + EPS) * g


def _silu(x):
    h = 0.5 * x
    return h + h * jnp.tanh(h)


def _sigmoid(x):
    return 0.5 * jnp.tanh(0.5 * x) + 0.5


def _shift_matrix(rows, taps):
    r = jnp.arange(rows)
    return jnp.concatenate([r[:, None] - s == r[None, :] for s in range(1, taps)], axis=0).astype(BF16)


def _shift_conv(xb, shift_ref, cw, tail, taps):
    rows, c = xb.shape
    x32 = xb.astype(F32)
    shifted = _dot(shift_ref[...], xb)
    last = taps - 1
    conv = cw[last:last + 1, :] * x32
    for s in range(1, taps):
        conv = conv + cw[last - s:last - s + 1, :] * shifted[(s - 1) * rows:s * rows, :]
    head = jnp.concatenate([tail, x32[0:HALO, :]], axis=0)
    rowi = lax.broadcasted_iota(jnp.int32, (HALO, c), 0)
    fix = jnp.zeros((HALO, c), F32)
    for s in range(1, taps):
        fix = fix + cw[last - s:last - s + 1, :] * jnp.where(rowi < s, head[HALO - s:2 * HALO - s, :], 0.0)
    return jnp.concatenate([conv[0:HALO, :] + fix, conv[HALO:, :]], axis=0)


def _roll_conv(xb, cw, tail, taps):
    rows, c = xb.shape
    x32 = xb.astype(F32)
    last = taps - 1
    conv = cw[last:last + 1, :] * x32
    head = jnp.concatenate([tail, x32[0:HALO, :]], axis=0)
    first = cw[last:last + 1, :] * x32[0:HALO, :]
    for s in range(1, taps):
        w = cw[last - s:last - s + 1, :]
        conv = conv + w * pltpu.roll(x32, s, 0)
        first = first + w * head[HALO - s:2 * HALO - s, :]
    return jnp.concatenate([first, conv[HALO:, :]], axis=0)


def _params(*sem):
    return pltpu.CompilerParams(dimension_semantics=sem, vmem_limit_bytes=VMEM_LIMIT)


def _full(shape):
    return pl.BlockSpec(shape, lambda *_: (0,) * len(shape))


def _proj0_kernel(x_ref, cos_ref, sin_ref, cost_ref, sint_ref, g_ref, w_ref, wvt_ref, gb_ref,
                  mqk_ref, mo_ref, dqt_ref, dk_ref, mvt_ref, dvt_ref, gates_ref):
    xn = _rms(x_ref[...], g_ref[...])
    xb = xn.astype(BF16)
    xl = (xn - xb.astype(F32)).astype(BF16)
    half = DIFF_HEAD_DIM // 2
    dw = DIFF_HEADS * DIFF_V_DIM

    def seg(lo, hi):
        return _dot(xb, w_ref[:, lo:hi])

    mqk_ref[...] = seg(0, 1024).astype(BF16)
    mo_ref[...] = seg(1024, 1536).astype(BF16)
    vt = _dot_nt(wvt_ref[...], xb)
    mvt_ref[...] = vt[0:dw, :].astype(BF16)
    dvt_ref[...] = vt[dw:2 * dw, :].astype(BF16)
    g0 = 3 * dw
    gates_ref[...] = (vt[g0:g0 + LANES, :] + (vt[g0 + LANES:g0 + 2 * LANES, :]
                      + _dot_nt(wvt_ref[g0:g0 + LANES, :], xl))) + gb_ref[...]

    ct = cost_ref[...]
    st = sint_ref[...]
    q_scale = LOG2E * DIFF_HEAD_DIM ** -0.5
    for r in range(2 * DIFF_HEADS):
        lo = 2 * dw + r * DIFF_HEAD_DIM
        x1 = vt[lo:lo + half, :]
        x2 = vt[lo + half:lo + 2 * half, :]
        out = lo - 2 * dw
        dqt_ref[out:out + half, :] = ((x1 * ct - x2 * st) * q_scale).astype(BF16)
        dqt_ref[out + half:out + 2 * half, :] = ((x2 * ct + x1 * st) * q_scale).astype(BF16)

    heads = 2 * DIFF_HEADS
    c = cos_ref[...]
    s = sin_ref[...]
    c4 = jnp.concatenate([c, c] * heads, axis=1)
    s4 = jnp.concatenate([-s, s] * heads, axis=1)
    lane4 = lax.broadcasted_iota(jnp.int32, c4.shape, 1)
    first4 = (lane4 & (DIFF_HEAD_DIM - 1)) < half
    k = seg(1536, 2048)
    partner = jnp.where(first4, pltpu.roll(k, dw - half, 1), pltpu.roll(k, half, 1))
    dk = (k * c4 + partner * s4).astype(BF16)
    for h in range(DIFF_HEADS):
        dk_ref[h] = dk[:, h * DIFF_V_DIM:(h + 1) * DIFF_V_DIM]


def _rope_table_kernel(pos_ref, invf_ref, cos_ref, sin_ref):
    ang = invf_ref[...] * pos_ref[...].astype(F32)
    cos_ref[...] = jnp.cos(ang)
    sin_ref[...] = jnp.sin(ang)


def _rope_table(positions):
    t = positions.shape[0]
    half = DIFF_HEAD_DIM // 2
    inv_freq = (1.0 / (ROPE_THETA ** (jnp.arange(half, dtype=F32) / half))).reshape(half, 1)
    tl = min(t, 4096)
    return pl.pallas_call(
        _rope_table_kernel,
        grid=(t // tl,),
        in_specs=[pl.BlockSpec((1, tl), lambda i: (0, i)), _full(inv_freq.shape)],
        out_specs=[pl.BlockSpec((half, tl), lambda i: (0, i))] * 2,
        out_shape=[jax.ShapeDtypeStruct((half, t), F32)] * 2,
        compiler_params=_params("parallel"),
        name="rope_table",
    )(positions.reshape(1, t), inv_freq)


def _proj0(x, cost, sint, g, w, wvt, gb):
    t, d = x.shape
    tm = ROW_TILE
    row = lambda n: pl.BlockSpec((tm, n), lambda i: (i, 0))
    col = lambda n: pl.BlockSpec((n, tm), lambda i: (0, i))
    heads = pl.BlockSpec((DIFF_HEADS, tm, DIFF_V_DIM), lambda i: (0, i, 0))
    cos, sin = cost.T, sint.T
    half = cost.shape[0]
    out_shapes = [
        jax.ShapeDtypeStruct((t, 1024), BF16), jax.ShapeDtypeStruct((t, 512), BF16),
        jax.ShapeDtypeStruct((512, t), BF16), jax.ShapeDtypeStruct((DIFF_HEADS, t, DIFF_V_DIM), BF16),
        jax.ShapeDtypeStruct((512, t), BF16), jax.ShapeDtypeStruct((512, t), BF16),
        jax.ShapeDtypeStruct((LANES, t), F32),
    ]
    return pl.pallas_call(
        _proj0_kernel,
        grid=(t // tm,),
        in_specs=[row(d), row(half), row(half), col(half), col(half), _full(g.shape), _full(w.shape),
                  _full(wvt.shape), _full(gb.shape)],
        out_specs=[row(1024), row(512), col(512), heads, col(512), col(512), col(LANES)],
        out_shape=out_shapes,
        compiler_params=_params("parallel"),
        name="proj0",
    )(x, cos, sin, cost, sint, g, w, wvt, gb)


def _mlstm_kernel(mqk_ref, mvt_ref, mo_ref, gates_ref, shift_ref, cw_ref, norm_ref, hm_ref,
                  tail_ref, cn_ref, m_ref):
    L = MLSTM_CHUNK
    hd = MLSTM_HEAD_DIM
    width = MLSTM_HEADS * hd

    @pl.when(pl.program_id(1) == 0)
    def _():
        tail_ref[...] = jnp.zeros_like(tail_ref)
        cn_ref[...] = jnp.zeros_like(cn_ref)
        m_ref[...] = jnp.zeros_like(m_ref)

    krow = lax.broadcasted_iota(jnp.int32, (L, L), 0)
    qcol = lax.broadcasted_iota(jnp.int32, (L, L), 1)
    causal_t = krow <= qcol
    triu = jnp.where(causal_t, 1.0, 0.0).astype(BF16)
    ones = jnp.ones((SUBLANES, L), BF16)

    def chunk(ci, carry):
        rows = pl.ds(pl.multiple_of(ci * L, L), L)
        xb = mqk_ref[rows, :]
        conv = _shift_conv(xb, shift_ref, cw_ref[...], tail_ref[...], MLSTM_CONV)
        tail_ref[...] = xb[L - HALO:L, :].astype(F32)
        qk = _silu(conv)
        qb = (qk[:, :width] * (hd ** -0.5)).astype(BF16)
        k = qk[:, width:]
        kb = k.astype(BF16)

        g8 = gates_ref[0:SUBLANES, rows]
        logf = (jnp.minimum(g8, 0.0) - jnp.log(1.0 + jnp.exp(-jnp.abs(g8)))) * LOG2E
        f_hi, f_mid, f_lo = _split3(logf)
        b8 = _dot(f_hi, triu) + (_dot(f_mid, triu) + _dot(f_lo, triu))
        cj8 = g8 * LOG2E - pltpu.roll(b8, MLSTM_HEADS, 0)
        cj_all = jnp.concatenate([cj8, jnp.zeros((LANES - SUBLANES, L), F32)], axis=0).T

        for h in range(MLSTM_HEADS):
            sl = slice(h * hd, (h + 1) * hd)
            b_row = b8[MLSTM_HEADS + h:MLSTM_HEADS + h + 1, :]
            m_prev = m_ref[h:h + 1, 0:1]
            cn_prev = cn_ref[h]
            vt1 = jnp.concatenate([mvt_ref[sl, rows], ones], axis=0)

            cmat = jnp.where(causal_t, cj_all[:, h:h + 1], -jnp.inf)
            mm = jnp.maximum(m_prev, jnp.max(cmat, axis=0, keepdims=True))
            w_t = jnp.exp2(cmat - mm)
            s_t = _dot_nt(kb[:, sl], qb[:, sl]) * w_t
            inter_w = jnp.exp2(m_prev - mm)
            tot = _dot(vt1, s_t.astype(BF16)) + inter_w * _dot_nt(cn_prev.astype(BF16), qb[:, sl])
            den = tot[hd:hd + 1, :]
            h_t = tot[0:hd, :] / jnp.maximum(jnp.abs(den), jnp.exp2(-(b_row + mm)))

            kw = k[:, sl] * w_t[:, L - 1:L]
            cn_ref[h] = inter_w[:, L - 1:L] * cn_prev + _dot(vt1, kw.astype(BF16))
            m_ref[h:h + 1, :] = jnp.broadcast_to(b_row[:, L - 1:L] + mm[:, L - 1:L], (1, LANES))

            scale = lax.rsqrt(jnp.mean(h_t * h_t, axis=0, keepdims=True) + EPS)
            hn = (h_t * scale).T * norm_ref[...]
            hm_ref[rows, sl] = (hn * _sigmoid(mo_ref[rows, sl].astype(F32))).astype(BF16)
        return carry

    lax.fori_loop(0, mqk_ref.shape[0] // L, chunk, 0)


def _mlstm(mqk, mvt, mo, gates, cw, norm, batch, seq):
    t = mqk.shape[0]
    L = MLSTM_CHUNK
    rows = min(seq, MLSTM_STEP_ROWS)
    nc = seq // rows
    width = MLSTM_HEADS * MLSTM_HEAD_DIM
    row = lambda n: pl.BlockSpec((rows, n), lambda b, c: (b * nc + c, 0))
    col = lambda n: pl.BlockSpec((n, rows), lambda b, c: (0, b * nc + c))
    shift = _shift_matrix(L, MLSTM_CONV)
    return pl.pallas_call(
        _mlstm_kernel,
        grid=(batch, nc),
        in_specs=[row(2 * width), col(width), row(width), col(LANES), _full(shift.shape), _full(cw.shape),
                  _full(norm.shape)],
        out_specs=row(width),
        out_shape=jax.ShapeDtypeStruct((t, width), BF16),
        scratch_shapes=[
            pltpu.VMEM((HALO, 2 * width), F32),
            pltpu.VMEM((MLSTM_HEADS, MLSTM_HEAD_DIM + SUBLANES, MLSTM_HEAD_DIM), F32),
            pltpu.VMEM((SUBLANES, LANES), F32),
        ],
        compiler_params=_params("parallel", "arbitrary"),
        name="mlstm",
    )(mqk, mvt, mo, gates, shift, cw, norm)


def _diff_attn_kernel(qt_ref, k_ref, vt_ref, lam_ref, norm_ref, o_ref,
                      acc0_ref, acc1_ref, m_ref, sa0_ref, sa1_ref, sb0_ref, sb1_ref, *, lam_init):
    tq = ATTN_BLOCK
    dv = DIFF_V_DIM
    nq = k_ref.shape[1] // tq
    frow = lax.broadcasted_iota(jnp.int32, (dv, tq), 0)
    krow = lax.broadcasted_iota(jnp.int32, (tq, tq), 0)
    qcol = lax.broadcasted_iota(jnp.int32, (tq, tq), 1)
    causal_t = krow <= qcol
    ones = jnp.ones((SUBLANES, tq), BF16)
    accs = (acc0_ref, acc1_ref)
    buf_a = (sa0_ref, sa1_ref)
    buf_b = (sb0_ref, sb1_ref)
    lv = lam_ref[...]
    lam = (jnp.exp(jnp.sum(lv[0:1, :] * lv[1:2, :], axis=-1, keepdims=True))
           - jnp.exp(jnp.sum(lv[2:3, :] * lv[3:4, :], axis=-1, keepdims=True)) + lam_init)

    def queries(qi):
        qt = qt_ref[:, pl.ds(pl.multiple_of(qi * tq, tq), tq)]
        zero = jnp.zeros_like(qt)
        return (jnp.where(frow < DIFF_HEAD_DIM, qt, zero), jnp.where(frow >= DIFF_HEAD_DIM, qt, zero))

    def scores_of(qs, j, dst):
        kb = k_ref[0, pl.ds(pl.multiple_of(j * tq, tq), tq), :]
        for c in range(2):
            dst[c][...] = _dot(kb, qs[c])

    def reset():
        for acc in accs:
            acc[...] = jnp.zeros_like(acc)
        m_ref[...] = jnp.full(m_ref.shape, -jnp.inf, F32)

    reset()
    scores_of(queries(0), 0, buf_a)

    def q_block(qi, carry):
        qrow = pl.multiple_of(qi * tq, tq)
        qs = queries(qi)

        def scores(j, dst):
            scores_of(qs, j, dst)

        def absorb(j, src, masked):
            start = pl.multiple_of(j * tq, tq)
            vt1 = jnp.concatenate([vt_ref[:, pl.ds(start, tq)], ones], axis=0)
            s = [src[c][...] for c in range(2)]
            if masked:
                s = [jnp.where(causal_t, sc, -jnp.inf) for sc in s]
            m_old = [m_ref[c:c + 1, :] for c in range(2)]
            m_new = [jnp.maximum(m_old[c], jnp.max(s[c], axis=0, keepdims=True)) for c in range(2)]
            p = [jnp.exp2(s[c] - m_new[c]).astype(BF16) for c in range(2)]
            alpha = [jnp.exp2(m_old[c] - m_new[c]) for c in range(2)]
            pv = [_dot(vt1, p[c]) for c in range(2)]
            for c in range(2):
                accs[c][...] = alpha[c] * accs[c][...] + pv[c]
                m_ref[c:c + 1, :] = m_new[c]

        def pair(jj, inner):
            j = 2 * jj
            scores(j + 1, buf_b)
            absorb(j, buf_a, False)
            scores(j + 2, buf_a)
            absorb(j + 1, buf_b, False)
            return inner

        lax.fori_loop(0, qi // 2, pair, 0)

        @pl.when(qi % 2 == 0)
        def _():
            absorb(qi, buf_a, True)

        @pl.when(qi % 2 == 1)
        def _():
            scores(qi, buf_b)
            absorb(qi - 1, buf_a, False)
            absorb(qi, buf_b, True)

        scores_of(queries(jnp.minimum(qi + 1, nq - 1)), 0, buf_a)

        a1 = acc0_ref[...]
        a2 = acc1_ref[...]
        o_t = a1[0:dv, :] / a1[dv:dv + 1, :] - lam * (a2[0:dv, :] / a2[dv:dv + 1, :])
        o_ref[0, pl.ds(qrow, tq), :] = (_rms(o_t.T, norm_ref[...]) * (1.0 - lam_init)).astype(BF16)
        reset()
        return carry

    lax.fori_loop(0, nq, q_block, 0)


def _diff_attn(dqt, dk, dvt, lam_vecs, norm, batch, seq, lam_init):
    t = dk.shape[1]
    tq = ATTN_BLOCK
    rows = pl.BlockSpec((1, seq, DIFF_V_DIM), lambda b, h: (h, b, 0))
    cols = pl.BlockSpec((DIFF_V_DIM, seq), lambda b, h: (h, b))
    return pl.pallas_call(
        functools.partial(_diff_attn_kernel, lam_init=lam_init),
        grid=(batch, DIFF_HEADS),
        in_specs=[cols, rows, cols, _full(lam_vecs.shape), _full(norm.shape)],
        out_specs=rows,
        out_shape=jax.ShapeDtypeStruct((DIFF_HEADS, t, DIFF_V_DIM), BF16),
        scratch_shapes=([pltpu.VMEM((DIFF_V_DIM + SUBLANES, tq), F32)] * 2 + [pltpu.VMEM((SUBLANES, tq), F32)]
                        + [pltpu.VMEM((tq, tq), F32)] * 4),
        compiler_params=_params("parallel", "parallel"),
        name="diff_attn",
    )(dqt, dk, dvt, lam_vecs, norm)


ROUTE_ROWS = 2 * SUBLANES
GROUP_LANE = EXPERTS_PER_GROUP
DEST_LANE = EXPERTS_PER_GROUP
EXPERT_LANE0 = 8


def _route(xn, wrt_ref, br_ref):
    xh = xn.astype(BF16)
    xl = (xn - xh.astype(F32)).astype(BF16)
    both = _dot_nt(wrt_ref[...], xh)
    logits = both[0:LANES, :] + (both[LANES:2 * LANES, :] + _dot_nt(wrt_ref[0:LANES, :], xl)) + br_ref[...]
    e = EXPERTS_PER_GROUP
    tm = logits.shape[1]
    row = lax.broadcasted_iota(jnp.int32, (e, tm), 0)
    ninf = -jnp.inf
    big = jnp.int32(LANES)

    is_g = row < N_GROUPS
    gl = jnp.where(is_g, logits[0:e, :], ninf)
    gmax = jnp.max(gl, axis=0, keepdims=True)
    gsum = jnp.sum(jnp.where(is_g, jnp.exp(gl - gmax), 0.0), axis=0, keepdims=True)
    g_val = 1.0 / gsum
    g_idx = jnp.min(jnp.where(gl == gmax, row, big), axis=0, keepdims=True)

    el = logits[EXPERT_LANE0:EXPERT_LANE0 + e, :]
    for g in range(1, N_GROUPS):
        lo = EXPERT_LANE0 + e * g
        el = jnp.where(g_idx == g, logits[lo:lo + e, :], el)
    v1 = jnp.max(el, axis=0, keepdims=True)
    i1 = jnp.min(jnp.where(el == v1, row, big), axis=0, keepdims=True)
    el2 = jnp.where(row == i1, ninf, el)
    v2 = jnp.max(el2, axis=0, keepdims=True)
    i2 = jnp.min(jnp.where(el2 == v2, row, big), axis=0, keepdims=True)
    e21 = jnp.exp(v2 - v1)
    w1 = 1.0 / (1.0 + e21)
    w2 = e21 * w1
    gates = g_val * (jnp.where(row == i1, w1, 0.0) + jnp.where(row == i2, w2, 0.0))
    tail = jnp.where(row == 0, g_idx.astype(F32), 0.0)
    return xh, jnp.concatenate([gates, tail], axis=0)


def _mix0_tail_kernel(x_ref, hm_ref, hd_ref, wo_ref, nf_ref, wrt_ref, br_ref,
                      xmid_ref, xn_ref, route_ref):
    merged = jnp.concatenate([hm_ref[...]] + [hd_ref[h] for h in range(DIFF_HEADS)], axis=1)
    xmid = x_ref[...] + _dot(merged, wo_ref[...])
    xmid_ref[...] = xmid
    xn_ref[...], route_ref[...] = _route(_rms(xmid, nf_ref[...]), wrt_ref, br_ref)


def _mix0_tail(x, hm, hd, wo, nf, wrt, br):
    t, d = x.shape
    tm = ROW_TILE
    row = lambda n: pl.BlockSpec((tm, n), lambda i: (i, 0))
    return pl.pallas_call(
        _mix0_tail_kernel,
        grid=(t // tm,),
        in_specs=[row(d), row(hm.shape[1]), pl.BlockSpec((DIFF_HEADS, tm, DIFF_V_DIM), lambda i: (0, i, 0)),
                  _full(wo.shape), _full(nf.shape), _full(wrt.shape), _full(br.shape)],
        out_specs=[row(d), row(d), pl.BlockSpec((ROUTE_ROWS, tm), lambda i: (0, i))],
        out_shape=[jax.ShapeDtypeStruct((t, d), F32), jax.ShapeDtypeStruct((t, d), BF16),
                   jax.ShapeDtypeStruct((ROUTE_ROWS, t), F32)],
        compiler_params=_params("parallel"),
        name="mix0_tail",
    )(x, hm, hd, wo, nf, wrt, br)


def _proj1_kernel(x_ref, y_ref, g_ref, w_ref, x1_ref, b_ref, u_ref):
    d = x_ref.shape[1]
    x1 = x_ref[...] + y_ref[...].astype(F32)
    x1_ref[...] = x1
    xb = _rms(x1, g_ref[...]).astype(BF16)
    b_ref[...] = _dot(xb, w_ref[:, 0:d]).astype(BF16)
    u_ref[...] = (_dot(xb, w_ref[:, d:2 * d]) * _dot(xb, w_ref[:, 2 * d:3 * d])).astype(BF16)


def _proj1(x, y, g, w):
    t, d = x.shape
    tm = ROW_TILE
    row = lambda n: pl.BlockSpec((tm, n), lambda i: (i, 0))
    return pl.pallas_call(
        _proj1_kernel,
        grid=(t // tm,),
        in_specs=[row(d), row(d), _full(g.shape), _full(w.shape)],
        out_specs=[row(d), row(d), row(d)],
        out_shape=[jax.ShapeDtypeStruct((t, d), F32), jax.ShapeDtypeStruct((t, d), BF16),
                   jax.ShapeDtypeStruct((t, d), BF16)],
        compiler_params=_params("parallel"),
        name="proj1",
    )(x, y, g, w)


def _mix1_tail_kernel(x_ref, b_ref, u_ref, halo_ref, cw_ref, wo_ref, nf_ref, wrt_ref, br_ref,
                      xmid_ref, xn_ref, route_ref, *, tiles_per_seq):
    seq_start = pl.program_id(0) % tiles_per_seq == 0
    halo = halo_ref[...].astype(F32)
    tail = jnp.where(seq_start, jnp.zeros_like(halo), halo)
    conv = _roll_conv(u_ref[...], cw_ref[...], tail, CONV_WIDTH)
    mixed = (b_ref[...].astype(F32) * conv).astype(BF16)
    xmid = x_ref[...] + _dot(mixed, wo_ref[...])
    xmid_ref[...] = xmid
    xn_ref[...], route_ref[...] = _route(_rms(xmid, nf_ref[...]), wrt_ref, br_ref)


def _mix1_tail(x, b, u, cw, wo, nf, wrt, br, seq):
    t, d = x.shape
    tm = CONV_ROW_TILE
    row = lambda n: pl.BlockSpec((tm, n), lambda i: (i, 0))
    halo = pl.BlockSpec((HALO, d), lambda i: (jnp.maximum(i * (tm // HALO) - 1, 0), 0))
    return pl.pallas_call(
        functools.partial(_mix1_tail_kernel, tiles_per_seq=seq // tm),
        grid=(t // tm,),
        in_specs=[row(d), row(d), row(d), halo, _full(cw.shape), _full(wo.shape),
                  _full(nf.shape), _full(wrt.shape), _full(br.shape)],
        out_specs=[row(d), row(d), pl.BlockSpec((ROUTE_ROWS, tm), lambda i: (0, i))],
        out_shape=[jax.ShapeDtypeStruct((t, d), F32), jax.ShapeDtypeStruct((t, d), BF16),
                   jax.ShapeDtypeStruct((ROUTE_ROWS, t), F32)],
        compiler_params=_params("parallel"),
        name="mix1_tail",
    )(x, b, u, u, cw, wo, nf, wrt, br)


def _moe_kernel(off_ref, nsub_ref, x_ref, r_ref, wg_ref, wu_ref, wd_ref, y_ref,
                xs_ref, gs_ref, ys_ref, dest_ref):
    i = pl.program_id(0)
    g = pl.program_id(1)
    tb = MOE_BLOCK
    rows = MOE_ROWS
    ch = MOE_CHUNK
    d = x_ref.shape[1]
    tail_row = rows - ch
    last_group = i * N_GROUPS + N_GROUPS - 1
    used = off_ref[last_group] + nsub_ref[last_group] * MOE_SUB

    @pl.when(g == 0)
    def _():
        r = r_ref[...]
        grp = r[GROUP_LANE:GROUP_LANE + 1, :]
        row8 = lax.broadcasted_iota(jnp.int32, (SUBLANES, tb), 0)
        onehot = jnp.where(row8.astype(F32) == grp, 1.0, 0.0)
        row1 = lax.broadcasted_iota(jnp.int32, (SUBLANES, 1), 0)
        base = jnp.zeros((SUBLANES, 1), F32)
        for gg in range(N_GROUPS):
            base = jnp.where(row1 == gg, off_ref[i * N_GROUPS + gg].astype(F32), base)
        rr = lax.broadcasted_iota(jnp.int32, (ch, ch), 0)
        cc = lax.broadcasted_iota(jnp.int32, (ch, ch), 1)
        upper = jnp.where(rr <= cc, 1.0, 0.0).astype(BF16)
        parts = []
        for c in range(tb // ch):
            sel = onehot[:, c * ch:(c + 1) * ch]
            rank = _dot(sel.astype(BF16), upper) + base
            parts.append(jnp.sum(sel * rank, axis=0, keepdims=True) - 1.0)
            base = base + jnp.sum(sel, axis=1, keepdims=True)
        dest_row = jnp.concatenate(parts, axis=1)
        side = jnp.concatenate([r[0:SUBLANES, :], jnp.broadcast_to(dest_row, (SUBLANES, tb)),
                                jnp.zeros((LANES - 2 * SUBLANES, tb), F32)], axis=0).T
        dest_ref[...] = side
        lane = lax.broadcasted_iota(jnp.int32, (tb, LANES), 1)
        gates = jnp.where(lane < EXPERTS_PER_GROUP, side, 0.0)
        g_hi = gates.astype(BF16)
        g_lo = (gates - g_hi.astype(F32)).astype(BF16)
        xcat = jnp.concatenate([x_ref[...], g_hi, g_lo], axis=1)

        def move(c):
            rr = (lax.broadcasted_iota(jnp.int32, (ch, tb), 0) + c * ch).astype(F32)
            p = jnp.where(rr == dest_row, 1.0, 0.0).astype(BF16)
            moved = _dot(p, xcat)
            xs_ref[c * ch:(c + 1) * ch, :] = moved[:, 0:d].astype(BF16)
            gs_ref[c * ch:(c + 1) * ch, :] = moved[:, d:d + LANES] + moved[:, d + LANES:d + 2 * LANES]

        for c in range(rows // ch - 1):
            move(c)

        @pl.when(used > tail_row)
        def _():
            move(rows // ch - 1)

        ys_ref[...] = jnp.zeros_like(ys_ref)

    off = off_ref[i * N_GROUPS + g]
    n_sub = nsub_ref[i * N_GROUPS + g]

    def expert_rows(r0, size):
        xt = xs_ref[pl.ds(r0, size), :]
        gt = gs_ref[pl.ds(r0, size), :]
        acts = []
        for e in range(EXPERTS_PER_GROUP):
            hg = _dot(xt, wg_ref[0, 0, e])
            up = _dot(xt, wu_ref[0, 0, e])
            acts.append((_silu(hg) * up * gt[:, e:e + 1]).astype(BF16))
        act_all = jnp.concatenate(acts, axis=1)
        wd_all = wd_ref[0, 0].reshape(EXPERTS_PER_GROUP * wd_ref.shape[3], d)
        ys_ref[pl.ds(r0, size), :] = _dot(act_all, wd_all).astype(BF16)

    def pair(j, carry):
        expert_rows(pl.multiple_of(off + j * (2 * MOE_SUB), MOE_SUB), 2 * MOE_SUB)
        return carry

    lax.fori_loop(0, n_sub // 2, pair, 0)

    @pl.when(n_sub % 2 == 1)
    def _():
        expert_rows(pl.multiple_of(off + (n_sub - 1) * MOE_SUB, MOE_SUB), MOE_SUB)

    @pl.when(g == N_GROUPS - 1)
    def _():
        def pick(c, lo, hi):
            cc = (lax.broadcasted_iota(jnp.int32, (ch, hi - lo), 1) + lo).astype(F32)
            dest = dest_ref[c * ch:(c + 1) * ch, DEST_LANE:DEST_LANE + 1]
            return _dot(jnp.where(cc == dest, 1.0, 0.0).astype(BF16), ys_ref[lo:hi, :])

        for c in range(tb // ch):
            y_ref[c * ch:(c + 1) * ch, :] = pick(c, 0, tail_row).astype(y_ref.dtype)

        @pl.when(used > tail_row)
        def _():
            for c in range(tb // ch):
                sl = slice(c * ch, (c + 1) * ch)
                y_ref[sl, :] = (y_ref[sl, :].astype(F32) + pick(c, tail_row, rows)).astype(y_ref.dtype)


def _hier_moe(xn, route, wg, wu, wd, layer):
    t, d = xn.shape
    tb = MOE_BLOCK
    nb = t // tb
    e, ff = wg.shape[2], wg.shape[4]
    group = route[GROUP_LANE, :].astype(jnp.int32).reshape(nb, tb)
    counts = jnp.sum((group[:, :, None] == jnp.arange(N_GROUPS)[None, None, :]).astype(jnp.int32), axis=1)
    nsub = (counts + MOE_SUB - 1) // MOE_SUB
    padded = nsub * MOE_SUB
    off = jnp.cumsum(padded, axis=1) - padded
    grid_spec = pltpu.PrefetchScalarGridSpec(
        num_scalar_prefetch=2,
        grid=(nb, N_GROUPS),
        in_specs=[
            pl.BlockSpec((tb, d), lambda i, g, *_: (i, 0)),
            pl.BlockSpec((ROUTE_ROWS, tb), lambda i, g, *_: (0, i)),
            pl.BlockSpec((1, 1, e, d, ff), lambda i, g, *_: (layer, g, 0, 0, 0)),
            pl.BlockSpec((1, 1, e, d, ff), lambda i, g, *_: (layer, g, 0, 0, 0)),
            pl.BlockSpec((1, 1, e, ff, d), lambda i, g, *_: (layer, g, 0, 0, 0)),
        ],
        out_specs=pl.BlockSpec((tb, d), lambda i, g, *_: (i, 0)),
        scratch_shapes=[
            pltpu.VMEM((MOE_ROWS, d), BF16),
            pltpu.VMEM((MOE_ROWS, LANES), F32),
            pltpu.VMEM((MOE_ROWS, d), BF16),
            pltpu.VMEM((tb, LANES), F32),
        ],
    )
    return pl.pallas_call(
        _moe_kernel,
        grid_spec=grid_spec,
        out_shape=jax.ShapeDtypeStruct((t, d), BF16),
        compiler_params=_params("parallel", "arbitrary"),
        name="moe",
    )(off.reshape(-1).astype(jnp.int32), nsub.reshape(-1).astype(jnp.int32), xn, route, wg, wu, wd)


def _final_kernel(x_ref, y_ref, g_ref, o_ref):
    o_ref[...] = _rms(x_ref[...] + y_ref[...].astype(F32), g_ref[...])


def _final(x, y, g):
    t, d = x.shape
    tm = ROW_TILE
    row = pl.BlockSpec((tm, d), lambda i: (i, 0))
    return pl.pallas_call(
        _final_kernel,
        grid=(t // tm,),
        in_specs=[row, row, _full(g.shape)],
        out_specs=row,
        out_shape=jax.ShapeDtypeStruct((t, d), F32),
        compiler_params=_params("parallel"),
        name="final_norm",
    )(x, y, g)


def _hi_lo(w):
    hi = w.astype(BF16)
    return hi, (w - hi.astype(F32)).astype(BF16)


def _router_params(w_rg, b_rg, w_re, b_re):
    d = w_rg.shape[0]
    n_e = N_GROUPS * EXPERTS_PER_GROUP
    pad = jnp.zeros((d, EXPERT_LANE0 - N_GROUPS), F32)
    tail = jnp.zeros((d, LANES - EXPERT_LANE0 - n_e), F32)
    w = jnp.concatenate([w_rg, pad, w_re, tail], axis=1)
    b = jnp.concatenate([b_rg, pad[0], b_re, tail[0]]).reshape(LANES, 1)
    hi, lo = _hi_lo(w)
    return jnp.concatenate([hi, lo], axis=1).T, b


def kernel(x, positions, norm_mix, norm_ffn, ab_w_in, ab_gate_bias, ab_conv_w, ab_mlstm_norm, ab_lambda,
           ab_diff_norm, ab_w_out, c_w_in, c_conv_w, c_w_out, moe_w_rg, moe_b_rg, moe_w_re, moe_b_re,
           moe_w_gate, moe_w_up, moe_w_down, final_norm):
    batch, seq, d = x.shape
    t = batch * seq
    xf = x.reshape(t, d)
    row2 = lambda v: v.reshape(1, -1).astype(F32)

    w_in = ab_w_in[0]
    mw = 2 * MLSTM_HEADS * MLSTM_HEAD_DIM
    hw = MLSTM_HEADS * MLSTM_HEAD_DIM
    ng = 2 * MLSTM_HEADS
    dw = DIFF_HEADS * DIFF_V_DIM
    o_g = mw + 2 * hw
    o_q = o_g + ng
    o_k, o_dv = o_q + dw, o_q + 2 * dw
    w0 = jnp.concatenate([w_in[:, :mw], w_in[:, mw + hw:o_g], w_in[:, o_k:o_dv]], axis=1).astype(BF16)
    w_gates = jnp.concatenate([w_in[:, o_g:o_q], jnp.zeros((d, LANES - ng), F32)], axis=1)
    wgh, wgl = _hi_lo(w_gates)
    wvt = jnp.concatenate([w_in[:, mw:mw + hw].astype(BF16), w_in[:, o_dv:o_dv + dw].astype(BF16),
                           w_in[:, o_q:o_k].astype(BF16), wgh, wgl], axis=1).T
    gb = jnp.concatenate([ab_gate_bias[0].astype(F32), jnp.zeros((LANES - ng,), F32)]).reshape(LANES, 1)

    cost, sint = _rope_table(positions.reshape(t).astype(jnp.int32))
    mqk, mo, dqt, dk, mvt, dvt, gates = _proj0(xf, cost, sint, row2(norm_mix[0]), w0, wvt, gb)
    hm = _mlstm(mqk, mvt, mo, gates, ab_conv_w[0].astype(F32), row2(ab_mlstm_norm[0]), batch, seq)
    lam_init = 0.8 - 0.6 * math.exp(-0.3 * 0)
    hd = _diff_attn(dqt, dk, dvt, ab_lambda[0].astype(F32), row2(ab_diff_norm[0]), batch, seq, lam_init)

    wrt, br = _router_params(moe_w_rg[0], moe_b_rg[0], moe_w_re[0], moe_b_re[0])
    xmid, xn, route = _mix0_tail(xf, hm, hd, ab_w_out[0].astype(BF16), row2(norm_ffn[0]), wrt, br)
    wg, wu, wd = moe_w_gate.astype(BF16), moe_w_up.astype(BF16), moe_w_down.astype(BF16)
    y = _hier_moe(xn, route, wg, wu, wd, 0)

    x1, bgate, u = _proj1(xmid, y, row2(norm_mix[1]), c_w_in[0].astype(BF16))
    wrt, br = _router_params(moe_w_rg[1], moe_b_rg[1], moe_w_re[1], moe_b_re[1])
    xmid, xn, route = _mix1_tail(x1, bgate, u, c_conv_w[0].astype(F32), c_w_out[0].astype(BF16),
                                 row2(norm_ffn[1]), wrt, br, seq)
    y = _hier_moe(xn, route, wg, wu, wd, 1)

    return _final(xmid, y, row2(final_norm)).reshape(batch, seq, d)
```

```python
import functools
import math

import jax
import jax.numpy as jnp
from jax import lax
from jax.experimental import pallas as pl
from jax.experimental.pallas import tpu as pltpu

F32 = jnp.float32
BF16 = jnp.bfloat16

EPS = 1e-6
LOG2E = math.log2(math.e)
ROPE_THETA = 10000.0
LANES = 128
SUBLANES = 8
MLSTM_HEADS = 4
MLSTM_HEAD_DIM = 128
MLSTM_CONV = 4
DIFF_HEADS = 4
DIFF_HEAD_DIM = 64
DIFF_V_DIM = 128
CONV_WIDTH = 3
N_GROUPS = 4
EXPERTS_PER_GROUP = 8
HALO = 8

ROW_TILE = 1024
CONV_ROW_TILE = 1024
MLSTM_CHUNK = 256
MLSTM_STEP_ROWS = 1024
ATTN_BLOCK = 512
MOE_BLOCK = 1024
MOE_SUB = 128
MOE_ROWS = MOE_BLOCK + N_GROUPS * MOE_SUB
MOE_CHUNK = 256
VMEM_LIMIT = 56 * 1024 * 1024


def _dot(a, b):
    return jnp.dot(a, b, preferred_element_type=F32)


def _dot_nt(a, b):
    return lax.dot_general(a, b, (((1,), (1,)), ((), ())), preferred_element_type=F32)


def _split3(x):
    hi = x.astype(BF16)
    r1 = x - hi.astype(F32)
    mid = r1.astype(BF16)
    lo = (r1 - mid.astype(F32)).astype(BF16)
    return hi, mid, lo


def _rms(x, g):
    return x * lax.rsqrt(jnp.mean(x * x, axis=-1, keepdims=True) + EPS) * g


def _silu(x):
    h = 0.5 * x
    return h + h * jnp.tanh(h)


def _sigmoid(x):
    return 0.5 * jnp.tanh(0.5 * x) + 0.5


def _shift_matrix(rows, taps):
    r = jnp.arange(rows)
    return jnp.concatenate([r[:, None] - s == r[None, :] for s in range(1, taps)], axis=0).astype(BF16)


def _shift_conv(xb, shift_ref, cw, tail, taps):
    rows, c = xb.shape
    x32 = xb.astype(F32)
    shifted = _dot(shift_ref[...], xb)
    last = taps - 1
    conv = cw[last:last + 1, :] * x32
    for s in range(1, taps):
        conv = conv + cw[last - s:last - s + 1, :] * shifted[(s - 1) * rows:s * rows, :]
    head = jnp.concatenate([tail, x32[0:HALO, :]], axis=0)
    rowi = lax.broadcasted_iota(jnp.int32, (HALO, c), 0)
    fix = jnp.zeros((HALO, c), F32)
    for s in range(1, taps):
        fix = fix + cw[last - s:last - s + 1, :] * jnp.where(rowi < s, head[HALO - s:2 * HALO - s, :], 0.0)
    return jnp.concatenate([conv[0:HALO, :] + fix, conv[HALO:, :]], axis=0)


def _roll_conv(xb, cw, tail, taps):
    rows, c = xb.shape
    x32 = xb.astype(F32)
    last = taps - 1
    conv = cw[last:last + 1, :] * x32
    head = jnp.concatenate([tail, x32[0:HALO, :]], axis=0)
    first = cw[last:last + 1, :] * x32[0:HALO, :]
    for s in range(1, taps):
        w = cw[last - s:last - s + 1, :]
        conv = conv + w * pltpu.roll(x32, s, 0)
        first = first + w * head[HALO - s:2 * HALO - s, :]
    return jnp.concatenate([first, conv[HALO:, :]], axis=0)


def _params(*sem):
    return pltpu.CompilerParams(dimension_semantics=sem, vmem_limit_bytes=VMEM_LIMIT)


def _full(shape):
    return pl.BlockSpec(shape, lambda *_: (0,) * len(shape))


def _proj0_kernel(x_ref, cos_ref, sin_ref, cost_ref, sint_ref, g_ref, w_ref, wvt_ref, gb_ref,
                  mqk_ref, mo_ref, dqt_ref, dk_ref, mvt_ref, dvt_ref, gates_ref):
    xn = _rms(x_ref[...], g_ref[...])
    xb = xn.astype(BF16)
    xl = (xn - xb.astype(F32)).astype(BF16)
    half = DIFF_HEAD_DIM // 2
    dw = DIFF_HEADS * DIFF_V_DIM

    def seg(lo, hi):
        return _dot(xb, w_ref[:, lo:hi])

    mqk_ref[...] = seg(0, 1024).astype(BF16)
    mo_ref[...] = seg(1024, 1536).astype(BF16)
    vt = _dot_nt(wvt_ref[...], xb)
    mvt_ref[...] = vt[0:dw, :].astype(BF16)
    dvt_ref[...] = vt[dw:2 * dw, :].astype(BF16)
    g0 = 3 * dw
    gates_ref[...] = (vt[g0:g0 + LANES, :] + (vt[g0 + LANES:g0 + 2 * LANES, :]
                      + _dot_nt(wvt_ref[g0:g0 + LANES, :], xl))) + gb_ref[...]

    ct = cost_ref[...]
    st = sint_ref[...]
    q_scale = LOG2E * DIFF_HEAD_DIM ** -0.5
    for r in range(2 * DIFF_HEADS):
        lo = 2 * dw + r * DIFF_HEAD_DIM
        x1 = vt[lo:lo + half, :]
        x2 = vt[lo + half:lo + 2 * half, :]
        out = lo - 2 * dw
        dqt_ref[out:out + half, :] = ((x1 * ct - x2 * st) * q_scale).astype(BF16)
        dqt_ref[out + half:out + 2 * half, :] = ((x2 * ct + x1 * st) * q_scale).astype(BF16)

    heads = 2 * DIFF_HEADS
    c = cos_ref[...]
    s = sin_ref[...]
    c4 = jnp.concatenate([c, c] * heads, axis=1)
    s4 = jnp.concatenate([-s, s] * heads, axis=1)
    lane4 = lax.broadcasted_iota(jnp.int32, c4.shape, 1)
    first4 = (lane4 & (DIFF_HEAD_DIM - 1)) < half
    k = seg(1536, 2048)
    partner = jnp.where(first4, pltpu.roll(k, dw - half, 1), pltpu.roll(k, half, 1))
    dk = (k * c4 + partner * s4).astype(BF16)
    for h in range(DIFF_HEADS):
        dk_ref[h] = dk[:, h * DIFF_V_DIM:(h + 1) * DIFF_V_DIM]


def _rope_table_kernel(pos_ref, invf_ref, cos_ref, sin_ref):
    ang = invf_ref[...] * pos_ref[...].astype(F32)
    cos_ref[...] = jnp.cos(ang)
    sin_ref[...] = jnp.sin(ang)


def _rope_table(positions):
    t = positions.shape[0]
    half = DIFF_HEAD_DIM // 2
    inv_freq = (1.0 / (ROPE_THETA ** (jnp.arange(half, dtype=F32) / half))).reshape(half, 1)
    tl = min(t, 4096)
    return pl.pallas_call(
        _rope_table_kernel,
        grid=(t // tl,),
        in_specs=[pl.BlockSpec((1, tl), lambda i: (0, i)), _full(inv_freq.shape)],
        out_specs=[pl.BlockSpec((half, tl), lambda i: (0, i))] * 2,
        out_shape=[jax.ShapeDtypeStruct((half, t), F32)] * 2,
        compiler_params=_params("parallel"),
        name="rope_table",
    )(positions.reshape(1, t), inv_freq)


def _proj0(x, cost, sint, g, w, wvt, gb):
    t, d = x.shape
    tm = ROW_TILE
    row = lambda n: pl.BlockSpec((tm, n), lambda i: (i, 0))
    col = lambda n: pl.BlockSpec((n, tm), lambda i: (0, i))
    heads = pl.BlockSpec((DIFF_HEADS, tm, DIFF_V_DIM), lambda i: (0, i, 0))
    cos, sin = cost.T, sint.T
    half = cost.shape[0]
    out_shapes = [
        jax.ShapeDtypeStruct((t, 1024), BF16), jax.ShapeDtypeStruct((t, 512), BF16),
        jax.ShapeDtypeStruct((512, t), BF16), jax.ShapeDtypeStruct((DIFF_HEADS, t, DIFF_V_DIM), BF16),
        jax.ShapeDtypeStruct((512, t), BF16), jax.ShapeDtypeStruct((512, t), BF16),
        jax.ShapeDtypeStruct((LANES, t), F32),
    ]
    return pl.pallas_call(
        _proj0_kernel,
        grid=(t // tm,),
        in_specs=[row(d), row(half), row(half), col(half), col(half), _full(g.shape), _full(w.shape),
                  _full(wvt.shape), _full(gb.shape)],
        out_specs=[row(1024), row(512), col(512), heads, col(512), col(512), col(LANES)],
        out_shape=out_shapes,
        compiler_params=_params("parallel"),
        name="proj0",
    )(x, cos, sin, cost, sint, g, w, wvt, gb)


def _mlstm_kernel(mqk_ref, mvt_ref, mo_ref, gates_ref, shift_ref, cw_ref, norm_ref, hm_ref,
                  tail_ref, cn_ref, m_ref):
    L = MLSTM_CHUNK
    hd = MLSTM_HEAD_DIM
    width = MLSTM_HEADS * hd

    @pl.when(pl.program_id(1) == 0)
    def _():
        tail_ref[...] = jnp.zeros_like(tail_ref)
        cn_ref[...] = jnp.zeros_like(cn_ref)
        m_ref[...] = jnp.zeros_like(m_ref)

    krow = lax.broadcasted_iota(jnp.int32, (L, L), 0)
    qcol = lax.broadcasted_iota(jnp.int32, (L, L), 1)
    causal_t = krow <= qcol
    triu = jnp.where(causal_t, 1.0, 0.0).astype(BF16)
    ones = jnp.ones((SUBLANES, L), BF16)

    def chunk(ci, carry):
        rows = pl.ds(pl.multiple_of(ci * L, L), L)
        xb = mqk_ref[rows, :]
        conv = _shift_conv(xb, shift_ref, cw_ref[...], tail_ref[...], MLSTM_CONV)
        tail_ref[...] = xb[L - HALO:L, :].astype(F32)
        qk = _silu(conv)
        qb = (qk[:, :width] * (hd ** -0.5)).astype(BF16)
        k = qk[:, width:]
        kb = k.astype(BF16)

        g8 = gates_ref[0:SUBLANES, rows]
        logf = (jnp.minimum(g8, 0.0) - jnp.log(1.0 + jnp.exp(-jnp.abs(g8)))) * LOG2E
        f_hi, f_mid, f_lo = _split3(logf)
        b8 = _dot(f_hi, triu) + (_dot(f_mid, triu) + _dot(f_lo, triu))
        cj8 = g8 * LOG2E - pltpu.roll(b8, MLSTM_HEADS, 0)
        cj_all = jnp.concatenate([cj8, jnp.zeros((LANES - SUBLANES, L), F32)], axis=0).T

        for h in range(MLSTM_HEADS):
            sl = slice(h * hd, (h + 1) * hd)
            b_row = b8[MLSTM_HEADS + h:MLSTM_HEADS + h + 1, :]
            m_prev = m_ref[h:h + 1, 0:1]
            cn_prev = cn_ref[h]
            vt1 = jnp.concatenate([mvt_ref[sl, rows], ones], axis=0)

            cmat = jnp.where(causal_t, cj_all[:, h:h + 1], -jnp.inf)
            mm = jnp.maximum(m_prev, jnp.max(cmat, axis=0, keepdims=True))
            w_t = jnp.exp2(cmat - mm)
            s_t = _dot_nt(kb[:, sl], qb[:, sl]) * w_t
            inter_w = jnp.exp2(m_prev - mm)
            tot = _dot(vt1, s_t.astype(BF16)) + inter_w * _dot_nt(cn_prev.astype(BF16), qb[:, sl])
            den = tot[hd:hd + 1, :]
            h_t = tot[0:hd, :] / jnp.maximum(jnp.abs(den), jnp.exp2(-(b_row + mm)))

            kw = k[:, sl] * w_t[:, L - 1:L]
            cn_ref[h] = inter_w[:, L - 1:L] * cn_prev + _dot(vt1, kw.astype(BF16))
            m_ref[h:h + 1, :] = jnp.broadcast_to(b_row[:, L - 1:L] + mm[:, L - 1:L], (1, LANES))

            scale = lax.rsqrt(jnp.mean(h_t * h_t, axis=0, keepdims=True) + EPS)
            hn = (h_t * scale).T * norm_ref[...]
            hm_ref[rows, sl] = (hn * _sigmoid(mo_ref[rows, sl].astype(F32))).astype(BF16)
        return carry

    lax.fori_loop(0, mqk_ref.shape[0] // L, chunk, 0)


def _mlstm(mqk, mvt, mo, gates, cw, norm, batch, seq):
    t = mqk.shape[0]
    L = MLSTM_CHUNK
    rows = min(seq, MLSTM_STEP_ROWS)
    nc = seq // rows
    width = MLSTM_HEADS * MLSTM_HEAD_DIM
    row = lambda n: pl.BlockSpec((rows, n), lambda b, c: (b * nc + c, 0))
    col = lambda n: pl.BlockSpec((n, rows), lambda b, c: (0, b * nc + c))
    shift = _shift_matrix(L, MLSTM_CONV)
    return pl.pallas_call(
        _mlstm_kernel,
        grid=(batch, nc),
        in_specs=[row(2 * width), col(width), row(width), col(LANES), _full(shift.shape), _full(cw.shape),
                  _full(norm.shape)],
        out_specs=row(width),
        out_shape=jax.ShapeDtypeStruct((t, width), BF16),
        scratch_shapes=[
            pltpu.VMEM((HALO, 2 * width), F32),
            pltpu.VMEM((MLSTM_HEADS, MLSTM_HEAD_DIM + SUBLANES, MLSTM_HEAD_DIM), F32),
            pltpu.VMEM((SUBLANES, LANES), F32),
        ],
        compiler_params=_params("parallel", "arbitrary"),
        name="mlstm",
    )(mqk, mvt, mo, gates, shift, cw, norm)


def _diff_attn_kernel(qt_ref, k_ref, vt_ref, lam_ref, norm_ref, o_ref,
                      acc0_ref, acc1_ref, m_ref, sa0_ref, sa1_ref, sb0_ref, sb1_ref, *, lam_init):
    tq = ATTN_BLOCK
    dv = DIFF_V_DIM
    nq = k_ref.shape[1] // tq
    frow = lax.broadcasted_iota(jnp.int32, (dv, tq), 0)
    krow = lax.broadcasted_iota(jnp.int32, (tq, tq), 0)
    qcol = lax.broadcasted_iota(jnp.int32, (tq, tq), 1)
    causal_t = krow <= qcol
    ones = jnp.ones((SUBLANES, tq), BF16)
    accs = (acc0_ref, acc1_ref)
    buf_a = (sa0_ref, sa1_ref)
    buf_b = (sb0_ref, sb1_ref)
    lv = lam_ref[...]
    lam = (jnp.exp(jnp.sum(lv[0:1, :] * lv[1:2, :], axis=-1, keepdims=True))
           - jnp.exp(jnp.sum(lv[2:3, :] * lv[3:4, :], axis=-1, keepdims=True)) + lam_init)

    def queries(qi):
        qt = qt_ref[:, pl.ds(pl.multiple_of(qi * tq, tq), tq)]
        zero = jnp.zeros_like(qt)
        return (jnp.where(frow < DIFF_HEAD_DIM, qt, zero), jnp.where(frow >= DIFF_HEAD_DIM, qt, zero))

    def scores_of(qs, j, dst):
        kb = k_ref[0, pl.ds(pl.multiple_of(j * tq, tq), tq), :]
        for c in range(2):
            dst[c][...] = _dot(kb, qs[c])

    def reset():
        for acc in accs:
            acc[...] = jnp.zeros_like(acc)
        m_ref[...] = jnp.full(m_ref.shape, -jnp.inf, F32)

    reset()
    scores_of(queries(0), 0, buf_a)

    def q_block(qi, carry):
        qrow = pl.multiple_of(qi * tq, tq)
        qs = queries(qi)

        def scores(j, dst):
            scores_of(qs, j, dst)

        def absorb(j, src, masked):
            start = pl.multiple_of(j * tq, tq)
            vt1 = jnp.concatenate([vt_ref[:, pl.ds(start, tq)], ones], axis=0)
            s = [src[c][...] for c in range(2)]
            if masked:
                s = [jnp.where(causal_t, sc, -jnp.inf) for sc in s]
            m_old = [m_ref[c:c + 1, :] for c in range(2)]
            m_new = [jnp.maximum(m_old[c], jnp.max(s[c], axis=0, keepdims=True)) for c in range(2)]
            p = [jnp.exp2(s[c] - m_new[c]).astype(BF16) for c in range(2)]
            alpha = [jnp.exp2(m_old[c] - m_new[c]) for c in range(2)]
            pv = [_dot(vt1, p[c]) for c in range(2)]
            for c in range(2):
                accs[c][...] = alpha[c] * accs[c][...] + pv[c]
                m_ref[c:c + 1, :] = m_new[c]

        def pair(jj, inner):
            j = 2 * jj
            scores(j + 1, buf_b)
            absorb(j, buf_a, False)
            scores(j + 2, buf_a)
            absorb(j + 1, buf_b, False)
            return inner

        lax.fori_loop(0, qi // 2, pair, 0)

        @pl.when(qi % 2 == 0)
        def _():
            absorb(qi, buf_a, True)

        @pl.when(qi % 2 == 1)
        def _():
            scores(qi, buf_b)
            absorb(qi - 1, buf_a, False)
            absorb(qi, buf_b, True)

        scores_of(queries(jnp.minimum(qi + 1, nq - 1)), 0, buf_a)

        a1 = acc0_ref[...]
        a2 = acc1_ref[...]
        o_t = a1[0:dv, :] / a1[dv:dv + 1, :] - lam * (a2[0:dv, :] / a2[dv:dv + 1, :])
        o_ref[0, pl.ds(qrow, tq), :] = (_rms(o_t.T, norm_ref[...]) * (1.0 - lam_init)).astype(BF16)
        reset()
        return carry

    lax.fori_loop(0, nq, q_block, 0)


def _diff_attn(dqt, dk, dvt, lam_vecs, norm, batch, seq, lam_init):
    t = dk.shape[1]
    tq = ATTN_BLOCK
    rows = pl.BlockSpec((1, seq, DIFF_V_DIM), lambda b, h: (h, b, 0))
    cols = pl.BlockSpec((DIFF_V_DIM, seq), lambda b, h: (h, b))
    return pl.pallas_call(
        functools.partial(_diff_attn_kernel, lam_init=lam_init),
        grid=(batch, DIFF_HEADS),
        in_specs=[cols, rows, cols, _full(lam_vecs.shape), _full(norm.shape)],
        out_specs=rows,
        out_shape=jax.ShapeDtypeStruct((DIFF_HEADS, t, DIFF_V_DIM), BF16),
        scratch_shapes=([pltpu.VMEM((DIFF_V_DIM + SUBLANES, tq), F32)] * 2 + [pltpu.VMEM((SUBLANES, tq), F32)]
                        + [pltpu.VMEM((tq, tq), F32)] * 4),
        compiler_params=_params("parallel", "parallel"),
        name="diff_attn",
    )(dqt, dk, dvt, lam_vecs, norm)


ROUTE_ROWS = 2 * SUBLANES
GROUP_LANE = EXPERTS_PER_GROUP
DEST_LANE = EXPERTS_PER_GROUP
EXPERT_LANE0 = 8


def _route(xn, wrt_ref, br_ref):
    xh = xn.astype(BF16)
    xl = (xn - xh.astype(F32)).astype(BF16)
    both = _dot_nt(wrt_ref[...], xh)
    logits = both[0:LANES, :] + (both[LANES:2 * LANES, :] + _dot_nt(wrt_ref[0:LANES, :], xl)) + br_ref[...]
    e = EXPERTS_PER_GROUP
    tm = logits.shape[1]
    row = lax.broadcasted_iota(jnp.int32, (e, tm), 0)
    ninf = -jnp.inf
    big = jnp.int32(LANES)

    is_g = row < N_GROUPS
    gl = jnp.where(is_g, logits[0:e, :], ninf)
    gmax = jnp.max(gl, axis=0, keepdims=True)
    gsum = jnp.sum(jnp.where(is_g, jnp.exp(gl - gmax), 0.0), axis=0, keepdims=True)
    g_val = 1.0 / gsum
    g_idx = jnp.min(jnp.where(gl == gmax, row, big), axis=0, keepdims=True)

    el = logits[EXPERT_LANE0:EXPERT_LANE0 + e, :]
    for g in range(1, N_GROUPS):
        lo = EXPERT_LANE0 + e * g
        el = jnp.where(g_idx == g, logits[lo:lo + e, :], el)
    v1 = jnp.max(el, axis=0, keepdims=True)
    i1 = jnp.min(jnp.where(el == v1, row, big), axis=0, keepdims=True)
    el2 = jnp.where(row == i1, ninf, el)
    v2 = jnp.max(el2, axis=0, keepdims=True)
    i2 = jnp.min(jnp.where(el2 == v2, row, big), axis=0, keepdims=True)
    e21 = jnp.exp(v2 - v1)
    w1 = 1.0 / (1.0 + e21)
    w2 = e21 * w1
    gates = g_val * (jnp.where(row == i1, w1, 0.0) + jnp.where(row == i2, w2, 0.0))
    tail = jnp.where(row == 0, g_idx.astype(F32), 0.0)
    return xh, jnp.concatenate([gates, tail], axis=0)


def _mix0_tail_kernel(x_ref, hm_ref, hd_ref, wo_ref, nf_ref, wrt_ref, br_ref,
                      xmid_ref, xn_ref, route_ref):
    merged = jnp.concatenate([hm_ref[...]] + [hd_ref[h] for h in range(DIFF_HEADS)], axis=1)
    xmid = x_ref[...] + _dot(merged, wo_ref[...])
    xmid_ref[...] = xmid
    xn_ref[...], route_ref[...] = _route(_rms(xmid, nf_ref[...]), wrt_ref, br_ref)


def _mix0_tail(x, hm, hd, wo, nf, wrt, br):
    t, d = x.shape
    tm = ROW_TILE
    row = lambda n: pl.BlockSpec((tm, n), lambda i: (i, 0))
    return pl.pallas_call(
        _mix0_tail_kernel,
        grid=(t // tm,),
        in_specs=[row(d), row(hm.shape[1]), pl.BlockSpec((DIFF_HEADS, tm, DIFF_V_DIM), lambda i: (0, i, 0)),
                  _full(wo.shape), _full(nf.shape), _full(wrt.shape), _full(br.shape)],
        out_specs=[row(d), row(d), pl.BlockSpec((ROUTE_ROWS, tm), lambda i: (0, i))],
        out_shape=[jax.ShapeDtypeStruct((t, d), F32), jax.ShapeDtypeStruct((t, d), BF16),
                   jax.ShapeDtypeStruct((ROUTE_ROWS, t), F32)],
        compiler_params=_params("parallel"),
        name="mix0_tail",
    )(x, hm, hd, wo, nf, wrt, br)


def _proj1_kernel(x_ref, y_ref, g_ref, w_ref, x1_ref, b_ref, u_ref):
    d = x_ref.shape[1]
    x1 = x_ref[...] + y_ref[...].astype(F32)
    x1_ref[...] = x1
    xb = _rms(x1, g_ref[...]).astype(BF16)
    b_ref[...] = _dot(xb, w_ref[:, 0:d]).astype(BF16)
    u_ref[...] = (_dot(xb, w_ref[:, d:2 * d]) * _dot(xb, w_ref[:, 2 * d:3 * d])).astype(BF16)


def _proj1(x, y, g, w):
    t, d = x.shape
    tm = ROW_TILE
    row = lambda n: pl.BlockSpec((tm, n), lambda i: (i, 0))
    return pl.pallas_call(
        _proj1_kernel,
        grid=(t // tm,),
        in_specs=[row(d), row(d), _full(g.shape), _full(w.shape)],
        out_specs=[row(d), row(d), row(d)],
        out_shape=[jax.ShapeDtypeStruct((t, d), F32), jax.ShapeDtypeStruct((t, d), BF16),
                   jax.ShapeDtypeStruct((t, d), BF16)],
        compiler_params=_params("parallel"),
        name="proj1",
    )(x, y, g, w)


def _mix1_tail_kernel(x_ref, b_ref, u_ref, halo_ref, cw_ref, wo_ref, nf_ref, wrt_ref, br_ref,
                      xmid_ref, xn_ref, route_ref, *, tiles_per_seq):
    seq_start = pl.program_id(0) % tiles_per_seq == 0
    halo = halo_ref[...].astype(F32)
    tail = jnp.where(seq_start, jnp.zeros_like(halo), halo)
    conv = _roll_conv(u_ref[...], cw_ref[...], tail, CONV_WIDTH)
    mixed = (b_ref[...].astype(F32) * conv).astype(BF16)
    xmid = x_ref[...] + _dot(mixed, wo_ref[...])
    xmid_ref[...] = xmid
    xn_ref[...], route_ref[...] = _route(_rms(xmid, nf_ref[...]), wrt_ref, br_ref)


def _mix1_tail(x, b, u, cw, wo, nf, wrt, br, seq):
    t, d = x.shape
    tm = CONV_ROW_TILE
    row = lambda n: pl.BlockSpec((tm, n), lambda i: (i, 0))
    halo = pl.BlockSpec((HALO, d), lambda i: (jnp.maximum(i * (tm // HALO) - 1, 0), 0))
    return pl.pallas_call(
        functools.partial(_mix1_tail_kernel, tiles_per_seq=seq // tm),
        grid=(t // tm,),
        in_specs=[row(d), row(d), row(d), halo, _full(cw.shape), _full(wo.shape),
                  _full(nf.shape), _full(wrt.shape), _full(br.shape)],
        out_specs=[row(d), row(d), pl.BlockSpec((ROUTE_ROWS, tm), lambda i: (0, i))],
        out_shape=[jax.ShapeDtypeStruct((t, d), F32), jax.ShapeDtypeStruct((t, d), BF16),
                   jax.ShapeDtypeStruct((ROUTE_ROWS, t), F32)],
        compiler_params=_params("parallel"),
        name="mix1_tail",
    )(x, b, u, u, cw, wo, nf, wrt, br)


def _moe_kernel(off_ref, nsub_ref, x_ref, r_ref, wg_ref, wu_ref, wd_ref, y_ref,
                xs_ref, gs_ref, ys_ref, dest_ref):
    i = pl.program_id(0)
    g = pl.program_id(1)
    tb = MOE_BLOCK
    rows = MOE_ROWS
    ch = MOE_CHUNK
    d = x_ref.shape[1]
    tail_row = rows - ch
    last_group = i * N_GROUPS + N_GROUPS - 1
    used = off_ref[last_group] + nsub_ref[last_group] * MOE_SUB

    @pl.when(g == 0)
    def _():
        r = r_ref[...]
        grp = r[GROUP_LANE:GROUP_LANE + 1, :]
        row8 = lax.broadcasted_iota(jnp.int32, (SUBLANES, tb), 0)
        onehot = jnp.where(row8.astype(F32) == grp, 1.0, 0.0)
        row1 = lax.broadcasted_iota(jnp.int32, (SUBLANES, 1), 0)
        base = jnp.zeros((SUBLANES, 1), F32)
        for gg in range(N_GROUPS):
            base = jnp.where(row1 == gg, off_ref[i * N_GROUPS + gg].astype(F32), base)
        rr = lax.broadcasted_iota(jnp.int32, (ch, ch), 0)
        cc = lax.broadcasted_iota(jnp.int32, (ch, ch), 1)
        upper = jnp.where(rr <= cc, 1.0, 0.0).astype(BF16)
        parts = []
        for c in range(tb // ch):
            sel = onehot[:, c * ch:(c + 1) * ch]
            rank = _dot(sel.astype(BF16), upper) + base
            parts.append(jnp.sum(sel * rank, axis=0, keepdims=True) - 1.0)
            base = base + jnp.sum(sel, axis=1, keepdims=True)
        dest_row = jnp.concatenate(parts, axis=1)
        side = jnp.concatenate([r[0:SUBLANES, :], jnp.broadcast_to(dest_row, (SUBLANES, tb)),
                                jnp.zeros((LANES - 2 * SUBLANES, tb), F32)], axis=0).T
        dest_ref[...] = side
        lane = lax.broadcasted_iota(jnp.int32, (tb, LANES), 1)
        gates = jnp.where(lane < EXPERTS_PER_GROUP, side, 0.0)
        g_hi = gates.astype(BF16)
        g_lo = (gates - g_hi.astype(F32)).astype(BF16)
        xcat = jnp.concatenate([x_ref[...], g_hi, g_lo], axis=1)

        def move(c):
            rr = (lax.broadcasted_iota(jnp.int32, (ch, tb), 0) + c * ch).astype(F32)
            p = jnp.where(rr == dest_row, 1.0, 0.0).astype(BF16)
            moved = _dot(p, xcat)
            xs_ref[c * ch:(c + 1) * ch, :] = moved[:, 0:d].astype(BF16)
            gs_ref[c * ch:(c + 1) * ch, :] = moved[:, d:d + LANES] + moved[:, d + LANES:d + 2 * LANES]

        for c in range(rows // ch - 1):
            move(c)

        @pl.when(used > tail_row)
        def _():
            move(rows // ch - 1)

        ys_ref[...] = jnp.zeros_like(ys_ref)

    off = off_ref[i * N_GROUPS + g]
    n_sub = nsub_ref[i * N_GROUPS + g]

    def expert_rows(r0, size):
        xt = xs_ref[pl.ds(r0, size), :]
        gt = gs_ref[pl.ds(r0, size), :]
        acts = []
        for e in range(EXPERTS_PER_GROUP):
            hg = _dot(xt, wg_ref[0, 0, e])
            up = _dot(xt, wu_ref[0, 0, e])
            acts.append((_silu(hg) * up * gt[:, e:e + 1]).astype(BF16))
        act_all = jnp.concatenate(acts, axis=1)
        wd_all = wd_ref[0, 0].reshape(EXPERTS_PER_GROUP * wd_ref.shape[3], d)
        ys_ref[pl.ds(r0, size), :] = _dot(act_all, wd_all).astype(BF16)

    def pair(j, carry):
        expert_rows(pl.multiple_of(off + j * (2 * MOE_SUB), MOE_SUB), 2 * MOE_SUB)
        return carry

    lax.fori_loop(0, n_sub // 2, pair, 0)

    @pl.when(n_sub % 2 == 1)
    def _():
        expert_rows(pl.multiple_of(off + (n_sub - 1) * MOE_SUB, MOE_SUB), MOE_SUB)

    @pl.when(g == N_GROUPS - 1)
    def _():
        def pick(c, lo, hi):
            cc = (lax.broadcasted_iota(jnp.int32, (ch, hi - lo), 1) + lo).astype(F32)
            dest = dest_ref[c * ch:(c + 1) * ch, DEST_LANE:DEST_LANE + 1]
            return _dot(jnp.where(cc == dest, 1.0, 0.0).astype(BF16), ys_ref[lo:hi, :])

        for c in range(tb // ch):
            y_ref[c * ch:(c + 1) * ch, :] = pick(c, 0, tail_row).astype(y_ref.dtype)

        @pl.when(used > tail_row)
        def _():
            for c in range(tb // ch):
                sl = slice(c * ch, (c + 1) * ch)
                y_ref[sl, :] = (y_ref[sl, :].astype(F32) + pick(c, tail_row, rows)).astype(y_ref.dtype)


def _hier_moe(xn, route, wg, wu, wd, layer):
    t, d = xn.shape
    tb = MOE_BLOCK
    nb = t // tb
    e, ff = wg.shape[2], wg.shape[4]
    group = route[GROUP_LANE, :].astype(jnp.int32).reshape(nb, tb)
    counts = jnp.sum((group[:, :, None] == jnp.arange(N_GROUPS)[None, None, :]).astype(jnp.int32), axis=1)
    nsub = (counts + MOE_SUB - 1) // MOE_SUB
    padded = nsub * MOE_SUB
    off = jnp.cumsum(padded, axis=1) - padded
    grid_spec = pltpu.PrefetchScalarGridSpec(
        num_scalar_prefetch=2,
        grid=(nb, N_GROUPS),
        in_specs=[
            pl.BlockSpec((tb, d), lambda i, g, *_: (i, 0)),
            pl.BlockSpec((ROUTE_ROWS, tb), lambda i, g, *_: (0, i)),
            pl.BlockSpec((1, 1, e, d, ff), lambda i, g, *_: (layer, g, 0, 0, 0)),
            pl.BlockSpec((1, 1, e, d, ff), lambda i, g, *_: (layer, g, 0, 0, 0)),
            pl.BlockSpec((1, 1, e, ff, d), lambda i, g, *_: (layer, g, 0, 0, 0)),
        ],
        out_specs=pl.BlockSpec((tb, d), lambda i, g, *_: (i, 0)),
        scratch_shapes=[
            pltpu.VMEM((MOE_ROWS, d), BF16),
            pltpu.VMEM((MOE_ROWS, LANES), F32),
            pltpu.VMEM((MOE_ROWS, d), BF16),
            pltpu.VMEM((tb, LANES), F32),
        ],
    )
    return pl.pallas_call(
        _moe_kernel,
        grid_spec=grid_spec,
        out_shape=jax.ShapeDtypeStruct((t, d), BF16),
        compiler_params=_params("parallel", "arbitrary"),
        name="moe",
    )(off.reshape(-1).astype(jnp.int32), nsub.reshape(-1).astype(jnp.int32), xn, route, wg, wu, wd)


def _final_kernel(x_ref, y_ref, g_ref, o_ref):
    o_ref[...] = _rms(x_ref[...] + y_ref[...].astype(F32), g_ref[...])


def _final(x, y, g):
    t, d = x.shape
    tm = ROW_TILE
    row = pl.BlockSpec((tm, d), lambda i: (i, 0))
    return pl.pallas_call(
        _final_kernel,
        grid=(t // tm,),
        in_specs=[row, row, _full(g.shape)],
        out_specs=row,
        out_shape=jax.ShapeDtypeStruct((t, d), F32),
        compiler_params=_params("parallel"),
        name="final_norm",
    )(x, y, g)


def _hi_lo(w):
    hi = w.astype(BF16)
    return hi, (w - hi.astype(F32)).astype(BF16)


def _router_params(w_rg, b_rg, w_re, b_re):
    d = w_rg.shape[0]
    n_e = N_GROUPS * EXPERTS_PER_GROUP
    pad = jnp.zeros((d, EXPERT_LANE0 - N_GROUPS), F32)
    tail = jnp.zeros((d, LANES - EXPERT_LANE0 - n_e), F32)
    w = jnp.concatenate([w_rg, pad, w_re, tail], axis=1)
    b = jnp.concatenate([b_rg, pad[0], b_re, tail[0]]).reshape(LANES, 1)
    hi, lo = _hi_lo(w)
    return jnp.concatenate([hi, lo], axis=1).T, b


def kernel(x, positions, norm_mix, norm_ffn, ab_w_in, ab_gate_bias, ab_conv_w, ab_mlstm_norm, ab_lambda,
           ab_diff_norm, ab_w_out, c_w_in, c_conv_w, c_w_out, moe_w_rg, moe_b_rg, moe_w_re, moe_b_re,
           moe_w_gate, moe_w_up, moe_w_down, final_norm):
    batch, seq, d = x.shape
    t = batch * seq
    assert seq % max(ATTN_BLOCK, CONV_ROW_TILE, min(seq, MLSTM_STEP_ROWS)) == 0 and seq % MLSTM_CHUNK == 0, seq
    assert t % max(ROW_TILE, MOE_BLOCK) == 0, (batch, seq)
    assert d == MLSTM_HEADS * MLSTM_HEAD_DIM + DIFF_HEADS * DIFF_V_DIM, d
    xf = x.reshape(t, d)
    row2 = lambda v: v.reshape(1, -1).astype(F32)

    w_in = ab_w_in[0]
    mw = 2 * MLSTM_HEADS * MLSTM_HEAD_DIM
    hw = MLSTM_HEADS * MLSTM_HEAD_DIM
    ng = 2 * MLSTM_HEADS
    dw = DIFF_HEADS * DIFF_V_DIM
    o_g = mw + 2 * hw
    o_q = o_g + ng
    o_k, o_dv = o_q + dw, o_q + 2 * dw
    w0 = jnp.concatenate([w_in[:, :mw], w_in[:, mw + hw:o_g], w_in[:, o_k:o_dv]], axis=1).astype(BF16)
    w_gates = jnp.concatenate([w_in[:, o_g:o_q], jnp.zeros((d, LANES - ng), F32)], axis=1)
    wgh, wgl = _hi_lo(w_gates)
    wvt = jnp.concatenate([w_in[:, mw:mw + hw].astype(BF16), w_in[:, o_dv:o_dv + dw].astype(BF16),
                           w_in[:, o_q:o_k].astype(BF16), wgh, wgl], axis=1).T
    gb = jnp.concatenate([ab_gate_bias[0].astype(F32), jnp.zeros((LANES - ng,), F32)]).reshape(LANES, 1)

    cost, sint = _rope_table(positions.reshape(t).astype(jnp.int32))
    mqk, mo, dqt, dk, mvt, dvt, gates = _proj0(xf, cost, sint, row2(norm_mix[0]), w0, wvt, gb)
    hm = _mlstm(mqk, mvt, mo, gates, ab_conv_w[0].astype(F32), row2(ab_mlstm_norm[0]), batch, seq)
    lam_init = 0.8 - 0.6 * math.exp(-0.3 * 0)
    hd = _diff_attn(dqt, dk, dvt, ab_lambda[0].astype(F32), row2(ab_diff_norm[0]), batch, seq, lam_init)

    wrt, br = _router_params(moe_w_rg[0], moe_b_rg[0], moe_w_re[0], moe_b_re[0])
    xmid, xn, route = _mix0_tail(xf, hm, hd, ab_w_out[0].astype(BF16), row2(norm_ffn[0]), wrt, br)
    wg, wu, wd = moe_w_gate.astype(BF16), moe_w_up.astype(BF16), moe_w_down.astype(BF16)
    y = _hier_moe(xn, route, wg, wu, wd, 0)

    x1, bgate, u = _proj1(xmid, y, row2(norm_mix[1]), c_w_in[0].astype(BF16))
    wrt, br = _router_params(moe_w_rg[1], moe_b_rg[1], moe_w_re[1], moe_b_re[1])
    xmid, xn, route = _mix1_tail(x1, bgate, u, c_conv_w[0].astype(F32), c_w_out[0].astype(BF16),
                                 row2(norm_ffn[1]), wrt, br, seq)
    y = _hier_moe(xn, route, wg, wu, wd, 1)

    return _final(xmid, y, row2(final_norm)).reshape(batch, seq, d)
```

```python
import functools
import math

import jax
import jax.numpy as jnp
from jax import lax
from jax.experimental import pallas as pl
from jax.experimental.pallas import tpu as pltpu

F32 = jnp.float32
BF16 = jnp.bfloat16

EPS = 1e-6
LOG2E = math.log2(math.e)
ROPE_THETA = 10000.0
LANES = 128
SUBLANES = 8
MLSTM_HEADS = 4
MLSTM_HEAD_DIM = 128
MLSTM_CONV = 4
DIFF_HEADS = 4
DIFF_HEAD_DIM = 64
DIFF_V_DIM = 128
CONV_WIDTH = 3
N_GROUPS = 4
EXPERTS_PER_GROUP = 8
HALO = 8

ROW_TILE = 1024
CONV_ROW_TILE = 1024
MLSTM_CHUNK = 256
MLSTM_STEP_ROWS = 1024
ATTN_BLOCK = 512
MOE_BLOCK = 1024
MOE_SUB = 128
MOE_ROWS = MOE_BLOCK + N_GROUPS * MOE_SUB
MOE_CHUNK = 256
VMEM_LIMIT = 56 * 1024 * 1024


def _dot(a, b):
    return jnp.dot(a, b, preferred_element_type=F32)


def _dot_nt(a, b):
    return lax.dot_general(a, b, (((1,), (1,)), ((), ())), preferred_element_type=F32)


def _split3(x):
    hi = x.astype(BF16)
    r1 = x - hi.astype(F32)
    mid = r1.astype(BF16)
    lo = (r1 - mid.astype(F32)).astype(BF16)
    return hi, mid, lo


def _rms(x, g):
    return x * lax.rsqrt(jnp.mean(x * x, axis=-1, keepdims=True) + EPS) * g


def _silu(x):
    h = 0.5 * x
    return h + h * jnp.tanh(h)


def _sigmoid(x):
    return 0.5 * jnp.tanh(0.5 * x) + 0.5


def _shift_matrix(rows, taps):
    r = jnp.arange(rows)
    return jnp.concatenate([r[:, None] - s == r[None, :] for s in range(1, taps)], axis=0).astype(BF16)


def _shift_conv(xb, shift_ref, cw, tail, taps):
    rows, c = xb.shape
    x32 = xb.astype(F32)
    shifted = _dot(shift_ref[...], xb)
    last = taps - 1
    conv = cw[last:last + 1, :] * x32
    for s in range(1, taps):
        conv = conv + cw[last - s:last - s + 1, :] * shifted[(s - 1) * rows:s * rows, :]
    head = jnp.concatenate([tail, x32[0:HALO, :]], axis=0)
    rowi = lax.broadcasted_iota(jnp.int32, (HALO, c), 0)
    fix = jnp.zeros((HALO, c), F32)
    for s in range(1, taps):
        fix = fix + cw[last - s:last - s + 1, :] * jnp.where(rowi < s, head[HALO - s:2 * HALO - s, :], 0.0)
    return jnp.concatenate([conv[0:HALO, :] + fix, conv[HALO:, :]], axis=0)


def _roll_conv(xb, cw, tail, taps):
    rows, c = xb.shape
    x32 = xb.astype(F32)
    last = taps - 1
    conv = cw[last:last + 1, :] * x32
    head = jnp.concatenate([tail, x32[0:HALO, :]], axis=0)
    first = cw[last:last + 1, :] * x32[0:HALO, :]
    for s in range(1, taps):
        w = cw[last - s:last - s + 1, :]
        conv = conv + w * pltpu.roll(x32, s, 0)
        first = first + w * head[HALO - s:2 * HALO - s, :]
    return jnp.concatenate([first, conv[HALO:, :]], axis=0)


def _params(*sem):
    return pltpu.CompilerParams(dimension_semantics=sem, vmem_limit_bytes=VMEM_LIMIT)


def _full(shape):
    return pl.BlockSpec(shape, lambda *_: (0,) * len(shape))


def _proj0_kernel(x_ref, cos_ref, sin_ref, cost_ref, sint_ref, g_ref, w_ref, wvt_ref, gb_ref,
                  mqk_ref, mo_ref, dqt_ref, dk_ref, mvt_ref, dvt_ref, gates_ref):
    xn = _rms(x_ref[...], g_ref[...])
    xb = xn.astype(BF16)
    xl = (xn - xb.astype(F32)).astype(BF16)
    half = DIFF_HEAD_DIM // 2
    dw = DIFF_HEADS * DIFF_V_DIM

    main = _dot(xb, w_ref[...])

    def seg(lo, hi):
        return main[:, lo:hi]

    mqk_ref[...] = seg(0, 1024).astype(BF16)
    mo_ref[...] = seg(1024, 1536).astype(BF16)
    vt = _dot_nt(wvt_ref[...], xb)
    mvt_ref[...] = vt[0:dw, :].astype(BF16)
    dvt_ref[...] = vt[dw:2 * dw, :].astype(BF16)
    g0 = 3 * dw
    gates_ref[...] = (vt[g0:g0 + LANES, :] + (vt[g0 + LANES:g0 + 2 * LANES, :]
                      + _dot_nt(wvt_ref[g0:g0 + LANES, :], xl))) + gb_ref[...]

    ct = cost_ref[...]
    st = sint_ref[...]
    q_scale = LOG2E * DIFF_HEAD_DIM ** -0.5
    for r in range(2 * DIFF_HEADS):
        lo = 2 * dw + r * DIFF_HEAD_DIM
        x1 = vt[lo:lo + half, :]
        x2 = vt[lo + half:lo + 2 * half, :]
        out = lo - 2 * dw
        dqt_ref[out:out + half, :] = ((x1 * ct - x2 * st) * q_scale).astype(BF16)
        dqt_ref[out + half:out + 2 * half, :] = ((x2 * ct + x1 * st) * q_scale).astype(BF16)

    heads = 2 * DIFF_HEADS
    c = cos_ref[...]
    s = sin_ref[...]
    c4 = jnp.concatenate([c, c] * heads, axis=1)
    s4 = jnp.concatenate([-s, s] * heads, axis=1)
    lane4 = lax.broadcasted_iota(jnp.int32, c4.shape, 1)
    first4 = (lane4 & (DIFF_HEAD_DIM - 1)) < half
    k = seg(1536, 2048)
    partner = jnp.where(first4, pltpu.roll(k, dw - half, 1), pltpu.roll(k, half, 1))
    dk = (k * c4 + partner * s4).astype(BF16)
    for h in range(DIFF_HEADS):
        dk_ref[h] = dk[:, h * DIFF_V_DIM:(h + 1) * DIFF_V_DIM]


def _rope_table_kernel(pos_ref, invf_ref, cos_ref, sin_ref):
    ang = invf_ref[...] * pos_ref[...].astype(F32)
    cos_ref[...] = jnp.cos(ang)
    sin_ref[...] = jnp.sin(ang)


def _rope_table(positions):
    t = positions.shape[0]
    half = DIFF_HEAD_DIM // 2
    inv_freq = (1.0 / (ROPE_THETA ** (jnp.arange(half, dtype=F32) / half))).reshape(half, 1)
    tl = min(t, 4096)
    return pl.pallas_call(
        _rope_table_kernel,
        grid=(t // tl,),
        in_specs=[pl.BlockSpec((1, tl), lambda i: (0, i)), _full(inv_freq.shape)],
        out_specs=[pl.BlockSpec((half, tl), lambda i: (0, i))] * 2,
        out_shape=[jax.ShapeDtypeStruct((half, t), F32)] * 2,
        compiler_params=_params("parallel"),
        name="rope_table",
    )(positions.reshape(1, t), inv_freq)


def _proj0(x, cost, sint, g, w, wvt, gb):
    t, d = x.shape
    tm = ROW_TILE
    row = lambda n: pl.BlockSpec((tm, n), lambda i: (i, 0))
    col = lambda n: pl.BlockSpec((n, tm), lambda i: (0, i))
    heads = pl.BlockSpec((DIFF_HEADS, tm, DIFF_V_DIM), lambda i: (0, i, 0))
    cos, sin = cost.T, sint.T
    half = cost.shape[0]
    out_shapes = [
        jax.ShapeDtypeStruct((t, 1024), BF16), jax.ShapeDtypeStruct((t, 512), BF16),
        jax.ShapeDtypeStruct((512, t), BF16), jax.ShapeDtypeStruct((DIFF_HEADS, t, DIFF_V_DIM), BF16),
        jax.ShapeDtypeStruct((512, t), BF16), jax.ShapeDtypeStruct((512, t), BF16),
        jax.ShapeDtypeStruct((LANES, t), F32),
    ]
    return pl.pallas_call(
        _proj0_kernel,
        grid=(t // tm,),
        in_specs=[row(d), row(half), row(half), col(half), col(half), _full(g.shape), _full(w.shape),
                  _full(wvt.shape), _full(gb.shape)],
        out_specs=[row(1024), row(512), col(512), heads, col(512), col(512), col(LANES)],
        out_shape=out_shapes,
        compiler_params=_params("parallel"),
        name="proj0",
    )(x, cos, sin, cost, sint, g, w, wvt, gb)


def _mlstm_kernel(mqk_ref, mvt_ref, mo_ref, gates_ref, shift_ref, cw_ref, norm_ref, hm_ref,
                  tail_ref, cn_ref, m_ref):
    L = MLSTM_CHUNK
    hd = MLSTM_HEAD_DIM
    width = MLSTM_HEADS * hd

    @pl.when(pl.program_id(1) == 0)
    def _():
        tail_ref[...] = jnp.zeros_like(tail_ref)
        cn_ref[...] = jnp.zeros_like(cn_ref)
        m_ref[...] = jnp.zeros_like(m_ref)

    krow = lax.broadcasted_iota(jnp.int32, (L, L), 0)
    qcol = lax.broadcasted_iota(jnp.int32, (L, L), 1)
    causal_t = krow <= qcol
    triu = jnp.where(causal_t, 1.0, 0.0).astype(BF16)
    ones = jnp.ones((SUBLANES, L), BF16)

    def chunk(ci, carry):
        rows = pl.ds(pl.multiple_of(ci * L, L), L)
        xb = mqk_ref[rows, :]
        conv = _shift_conv(xb, shift_ref, cw_ref[...], tail_ref[...], MLSTM_CONV)
        tail_ref[...] = xb[L - HALO:L, :].astype(F32)
        qk = _silu(conv)
        qb = (qk[:, :width] * (hd ** -0.5)).astype(BF16)
        k = qk[:, width:]
        kb = k.astype(BF16)

        g8 = gates_ref[0:SUBLANES, rows]
        logf = (jnp.minimum(g8, 0.0) - jnp.log(1.0 + jnp.exp(-jnp.abs(g8)))) * LOG2E
        f_hi, f_mid, f_lo = _split3(logf)
        b8 = _dot(f_hi, triu) + (_dot(f_mid, triu) + _dot(f_lo, triu))
        cj8 = g8 * LOG2E - pltpu.roll(b8, MLSTM_HEADS, 0)
        cj_all = jnp.concatenate([cj8, jnp.zeros((LANES - SUBLANES, L), F32)], axis=0).T

        for h in range(MLSTM_HEADS):
            sl = slice(h * hd, (h + 1) * hd)
            b_row = b8[MLSTM_HEADS + h:MLSTM_HEADS + h + 1, :]
            m_prev = m_ref[h:h + 1, 0:1]
            cn_prev = cn_ref[h]
            vt1 = jnp.concatenate([mvt_ref[sl, rows], ones], axis=0)

            cmat = jnp.where(causal_t, cj_all[:, h:h + 1], -jnp.inf)
            mm = jnp.maximum(m_prev, jnp.max(cmat, axis=0, keepdims=True))
            w_t = jnp.exp2(cmat - mm)
            s_t = _dot_nt(kb[:, sl], qb[:, sl]) * w_t
            inter_w = jnp.exp2(m_prev - mm)
            tot = _dot(vt1, s_t.astype(BF16)) + inter_w * _dot_nt(cn_prev.astype(BF16), qb[:, sl])
            den = tot[hd:hd + 1, :]
            h_t = tot[0:hd, :] / jnp.maximum(jnp.abs(den), jnp.exp2(-(b_row + mm)))

            kw = k[:, sl] * w_t[:, L - 1:L]
            cn_ref[h] = inter_w[:, L - 1:L] * cn_prev + _dot(vt1, kw.astype(BF16))
            m_ref[h:h + 1, :] = jnp.broadcast_to(b_row[:, L - 1:L] + mm[:, L - 1:L], (1, LANES))

            scale = lax.rsqrt(jnp.mean(h_t * h_t, axis=0, keepdims=True) + EPS)
            hn = (h_t * scale).T * norm_ref[...]
            hm_ref[rows, sl] = (hn * _sigmoid(mo_ref[rows, sl].astype(F32))).astype(BF16)
        return carry

    lax.fori_loop(0, mqk_ref.shape[0] // L, chunk, 0)


def _mlstm(mqk, mvt, mo, gates, cw, norm, batch, seq):
    t = mqk.shape[0]
    L = MLSTM_CHUNK
    rows = min(seq, MLSTM_STEP_ROWS)
    nc = seq // rows
    width = MLSTM_HEADS * MLSTM_HEAD_DIM
    row = lambda n: pl.BlockSpec((rows, n), lambda b, c: (b * nc + c, 0))
    col = lambda n: pl.BlockSpec((n, rows), lambda b, c: (0, b * nc + c))
    shift = _shift_matrix(L, MLSTM_CONV)
    return pl.pallas_call(
        _mlstm_kernel,
        grid=(batch, nc),
        in_specs=[row(2 * width), col(width), row(width), col(LANES), _full(shift.shape), _full(cw.shape),
                  _full(norm.shape)],
        out_specs=row(width),
        out_shape=jax.ShapeDtypeStruct((t, width), BF16),
        scratch_shapes=[
            pltpu.VMEM((HALO, 2 * width), F32),
            pltpu.VMEM((MLSTM_HEADS, MLSTM_HEAD_DIM + SUBLANES, MLSTM_HEAD_DIM), F32),
            pltpu.VMEM((SUBLANES, LANES), F32),
        ],
        compiler_params=_params("parallel", "arbitrary"),
        name="mlstm",
    )(mqk, mvt, mo, gates, shift, cw, norm)


def _diff_attn_kernel(qt_ref, k_ref, vt_ref, lam_ref, norm_ref, o_ref,
                      acc0_ref, acc1_ref, m_ref, sa0_ref, sa1_ref, sb0_ref, sb1_ref, *, lam_init):
    tq = ATTN_BLOCK
    dv = DIFF_V_DIM
    nq = k_ref.shape[1] // tq
    frow = lax.broadcasted_iota(jnp.int32, (dv, tq), 0)
    krow = lax.broadcasted_iota(jnp.int32, (tq, tq), 0)
    qcol = lax.broadcasted_iota(jnp.int32, (tq, tq), 1)
    causal_t = krow <= qcol
    ones = jnp.ones((SUBLANES, tq), BF16)
    accs = (acc0_ref, acc1_ref)
    buf_a = (sa0_ref, sa1_ref)
    buf_b = (sb0_ref, sb1_ref)
    lv = lam_ref[...]
    lam = (jnp.exp(jnp.sum(lv[0:1, :] * lv[1:2, :], axis=-1, keepdims=True))
           - jnp.exp(jnp.sum(lv[2:3, :] * lv[3:4, :], axis=-1, keepdims=True)) + lam_init)

    def queries(qi):
        qt = qt_ref[:, pl.ds(pl.multiple_of(qi * tq, tq), tq)]
        zero = jnp.zeros_like(qt)
        return (jnp.where(frow < DIFF_HEAD_DIM, qt, zero), jnp.where(frow >= DIFF_HEAD_DIM, qt, zero))

    def scores_of(qs, j, dst):
        kb = k_ref[0, pl.ds(pl.multiple_of(j * tq, tq), tq), :]
        for c in range(2):
            dst[c][...] = _dot(kb, qs[c])

    def reset():
        for acc in accs:
            acc[...] = jnp.zeros_like(acc)
        m_ref[...] = jnp.full(m_ref.shape, -jnp.inf, F32)

    reset()
    scores_of(queries(0), 0, buf_a)

    def q_block(qi, carry):
        qrow = pl.multiple_of(qi * tq, tq)
        qs = queries(qi)

        def scores(j, dst):
            scores_of(qs, j, dst)

        def absorb(j, src, masked):
            start = pl.multiple_of(j * tq, tq)
            vt1 = jnp.concatenate([vt_ref[:, pl.ds(start, tq)], ones], axis=0)
            s = [src[c][...] for c in range(2)]
            if masked:
                s = [jnp.where(causal_t, sc, -jnp.inf) for sc in s]
            m_old = [m_ref[c:c + 1, :] for c in range(2)]
            m_new = [jnp.maximum(m_old[c], jnp.max(s[c], axis=0, keepdims=True)) for c in range(2)]
            p = [jnp.exp2(s[c] - m_new[c]).astype(BF16) for c in range(2)]
            alpha = [jnp.exp2(m_old[c] - m_new[c]) for c in range(2)]
            pv = [_dot(vt1, p[c]) for c in range(2)]
            for c in range(2):
                accs[c][...] = alpha[c] * accs[c][...] + pv[c]
                m_ref[c:c + 1, :] = m_new[c]

        def pair(jj, inner):
            j = 2 * jj
            scores(j + 1, buf_b)
            absorb(j, buf_a, False)
            scores(j + 2, buf_a)
            absorb(j + 1, buf_b, False)
            return inner

        lax.fori_loop(0, qi // 2, pair, 0)

        @pl.when(qi % 2 == 0)
        def _():
            absorb(qi, buf_a, True)

        @pl.when(qi % 2 == 1)
        def _():
            scores(qi, buf_b)
            absorb(qi - 1, buf_a, False)
            absorb(qi, buf_b, True)

        scores_of(queries(jnp.minimum(qi + 1, nq - 1)), 0, buf_a)

        a1 = acc0_ref[...]
        a2 = acc1_ref[...]
        o_t = a1[0:dv, :] / a1[dv:dv + 1, :] - lam * (a2[0:dv, :] / a2[dv:dv + 1, :])
        o_ref[0, pl.ds(qrow, tq), :] = (_rms(o_t.T, norm_ref[...]) * (1.0 - lam_init)).astype(BF16)
        reset()
        return carry

    lax.fori_loop(0, nq, q_block, 0)


def _diff_attn(dqt, dk, dvt, lam_vecs, norm, batch, seq, lam_init):
    t = dk.shape[1]
    tq = ATTN_BLOCK
    rows = pl.BlockSpec((1, seq, DIFF_V_DIM), lambda b, h: (h, b, 0))
    cols = pl.BlockSpec((DIFF_V_DIM, seq), lambda b, h: (h, b))
    return pl.pallas_call(
        functools.partial(_diff_attn_kernel, lam_init=lam_init),
        grid=(batch, DIFF_HEADS),
        in_specs=[cols, rows, cols, _full(lam_vecs.shape), _full(norm.shape)],
        out_specs=rows,
        out_shape=jax.ShapeDtypeStruct((DIFF_HEADS, t, DIFF_V_DIM), BF16),
        scratch_shapes=([pltpu.VMEM((DIFF_V_DIM + SUBLANES, tq), F32)] * 2 + [pltpu.VMEM((SUBLANES, tq), F32)]
                        + [pltpu.VMEM((tq, tq), F32)] * 4),
        compiler_params=_params("parallel", "parallel"),
        name="diff_attn",
    )(dqt, dk, dvt, lam_vecs, norm)


ROUTE_ROWS = 2 * SUBLANES
GROUP_LANE = EXPERTS_PER_GROUP
DEST_LANE = EXPERTS_PER_GROUP
EXPERT_LANE0 = 8


def _route(xn, wrt_ref, br_ref):
    xh = xn.astype(BF16)
    xl = (xn - xh.astype(F32)).astype(BF16)
    both = _dot_nt(wrt_ref[...], xh)
    logits = both[0:LANES, :] + (both[LANES:2 * LANES, :] + _dot_nt(wrt_ref[0:LANES, :], xl)) + br_ref[...]
    e = EXPERTS_PER_GROUP
    tm = logits.shape[1]
    row = lax.broadcasted_iota(jnp.int32, (e, tm), 0)
    ninf = -jnp.inf
    big = jnp.int32(LANES)

    is_g = row < N_GROUPS
    gl = jnp.where(is_g, logits[0:e, :], ninf)
    gmax = jnp.max(gl, axis=0, keepdims=True)
    gsum = jnp.sum(jnp.where(is_g, jnp.exp(gl - gmax), 0.0), axis=0, keepdims=True)
    g_val = 1.0 / gsum
    g_idx = jnp.min(jnp.where(gl == gmax, row, big), axis=0, keepdims=True)

    el = logits[EXPERT_LANE0:EXPERT_LANE0 + e, :]
    for g in range(1, N_GROUPS):
        lo = EXPERT_LANE0 + e * g
        el = jnp.where(g_idx == g, logits[lo:lo + e, :], el)
    v1 = jnp.max(el, axis=0, keepdims=True)
    i1 = jnp.min(jnp.where(el == v1, row, big), axis=0, keepdims=True)
    el2 = jnp.where(row == i1, ninf, el)
    v2 = jnp.max(el2, axis=0, keepdims=True)
    i2 = jnp.min(jnp.where(el2 == v2, row, big), axis=0, keepdims=True)
    e21 = jnp.exp(v2 - v1)
    w1 = 1.0 / (1.0 + e21)
    w2 = e21 * w1
    gates = g_val * (jnp.where(row == i1, w1, 0.0) + jnp.where(row == i2, w2, 0.0))
    tail = jnp.where(row == 0, g_idx.astype(F32), 0.0)
    return xh, jnp.concatenate([gates, tail], axis=0)


def _mix0_tail_kernel(x_ref, hm_ref, hd_ref, wo_ref, nf_ref, wrt_ref, br_ref,
                      xmid_ref, xn_ref, route_ref):
    merged = jnp.concatenate([hm_ref[...]] + [hd_ref[h] for h in range(DIFF_HEADS)], axis=1)
    xmid = x_ref[...] + _dot(merged, wo_ref[...])
    xmid_ref[...] = xmid
    xn_ref[...], route_ref[...] = _route(_rms(xmid, nf_ref[...]), wrt_ref, br_ref)


def _mix0_tail(x, hm, hd, wo, nf, wrt, br):
    t, d = x.shape
    tm = ROW_TILE
    row = lambda n: pl.BlockSpec((tm, n), lambda i: (i, 0))
    return pl.pallas_call(
        _mix0_tail_kernel,
        grid=(t // tm,),
        in_specs=[row(d), row(hm.shape[1]), pl.BlockSpec((DIFF_HEADS, tm, DIFF_V_DIM), lambda i: (0, i, 0)),
                  _full(wo.shape), _full(nf.shape), _full(wrt.shape), _full(br.shape)],
        out_specs=[row(d), row(d), pl.BlockSpec((ROUTE_ROWS, tm), lambda i: (0, i))],
        out_shape=[jax.ShapeDtypeStruct((t, d), F32), jax.ShapeDtypeStruct((t, d), BF16),
                   jax.ShapeDtypeStruct((ROUTE_ROWS, t), F32)],
        compiler_params=_params("parallel"),
        name="mix0_tail",
    )(x, hm, hd, wo, nf, wrt, br)


def _proj1_kernel(x_ref, y_ref, g_ref, w_ref, x1_ref, b_ref, u_ref):
    d = x_ref.shape[1]
    x1 = x_ref[...] + y_ref[...].astype(F32)
    x1_ref[...] = x1
    xb = _rms(x1, g_ref[...]).astype(BF16)
    proj = _dot(xb, w_ref[...])
    b_ref[...] = proj[:, 0:d].astype(BF16)
    u_ref[...] = (proj[:, d:2 * d] * proj[:, 2 * d:3 * d]).astype(BF16)


def _proj1(x, y, g, w):
    t, d = x.shape
    tm = ROW_TILE
    row = lambda n: pl.BlockSpec((tm, n), lambda i: (i, 0))
    return pl.pallas_call(
        _proj1_kernel,
        grid=(t // tm,),
        in_specs=[row(d), row(d), _full(g.shape), _full(w.shape)],
        out_specs=[row(d), row(d), row(d)],
        out_shape=[jax.ShapeDtypeStruct((t, d), F32), jax.ShapeDtypeStruct((t, d), BF16),
                   jax.ShapeDtypeStruct((t, d), BF16)],
        compiler_params=_params("parallel"),
        name="proj1",
    )(x, y, g, w)


def _mix1_tail_kernel(x_ref, b_ref, u_ref, halo_ref, cw_ref, wo_ref, nf_ref, wrt_ref, br_ref,
                      xmid_ref, xn_ref, route_ref, *, tiles_per_seq):
    seq_start = pl.program_id(0) % tiles_per_seq == 0
    halo = halo_ref[...].astype(F32)
    tail = jnp.where(seq_start, jnp.zeros_like(halo), halo)
    conv = _roll_conv(u_ref[...], cw_ref[...], tail, CONV_WIDTH)
    mixed = (b_ref[...].astype(F32) * conv).astype(BF16)
    xmid = x_ref[...] + _dot(mixed, wo_ref[...])
    xmid_ref[...] = xmid
    xn_ref[...], route_ref[...] = _route(_rms(xmid, nf_ref[...]), wrt_ref, br_ref)


def _mix1_tail(x, b, u, cw, wo, nf, wrt, br, seq):
    t, d = x.shape
    tm = CONV_ROW_TILE
    row = lambda n: pl.BlockSpec((tm, n), lambda i: (i, 0))
    halo = pl.BlockSpec((HALO, d), lambda i: (jnp.maximum(i * (tm // HALO) - 1, 0), 0))
    return pl.pallas_call(
        functools.partial(_mix1_tail_kernel, tiles_per_seq=seq // tm),
        grid=(t // tm,),
        in_specs=[row(d), row(d), row(d), halo, _full(cw.shape), _full(wo.shape),
                  _full(nf.shape), _full(wrt.shape), _full(br.shape)],
        out_specs=[row(d), row(d), pl.BlockSpec((ROUTE_ROWS, tm), lambda i: (0, i))],
        out_shape=[jax.ShapeDtypeStruct((t, d), F32), jax.ShapeDtypeStruct((t, d), BF16),
                   jax.ShapeDtypeStruct((ROUTE_ROWS, t), F32)],
        compiler_params=_params("parallel"),
        name="mix1_tail",
    )(x, b, u, u, cw, wo, nf, wrt, br)


def _moe_kernel(off_ref, nsub_ref, x_ref, r_ref, wg_ref, wu_ref, wd_ref, y_ref,
                xs_ref, gs_ref, ys_ref, dest_ref):
    i = pl.program_id(0)
    g = pl.program_id(1)
    tb = MOE_BLOCK
    rows = MOE_ROWS
    ch = MOE_CHUNK
    d = x_ref.shape[1]
    tail_row = rows - ch
    last_group = i * N_GROUPS + N_GROUPS - 1
    used = off_ref[last_group] + nsub_ref[last_group] * MOE_SUB

    @pl.when(g == 0)
    def _():
        r = r_ref[...]
        grp = r[GROUP_LANE:GROUP_LANE + 1, :]
        row8 = lax.broadcasted_iota(jnp.int32, (SUBLANES, tb), 0)
        onehot = jnp.where(row8.astype(F32) == grp, 1.0, 0.0)
        row1 = lax.broadcasted_iota(jnp.int32, (SUBLANES, 1), 0)
        base = jnp.zeros((SUBLANES, 1), F32)
        for gg in range(N_GROUPS):
            base = jnp.where(row1 == gg, off_ref[i * N_GROUPS + gg].astype(F32), base)
        rr = lax.broadcasted_iota(jnp.int32, (ch, ch), 0)
        cc = lax.broadcasted_iota(jnp.int32, (ch, ch), 1)
        upper = jnp.where(rr <= cc, 1.0, 0.0).astype(BF16)
        parts = []
        for c in range(tb // ch):
            sel = onehot[:, c * ch:(c + 1) * ch]
            rank = _dot(sel.astype(BF16), upper) + base
            parts.append(jnp.sum(sel * rank, axis=0, keepdims=True) - 1.0)
            base = base + jnp.sum(sel, axis=1, keepdims=True)
        dest_row = jnp.concatenate(parts, axis=1)
        side = jnp.concatenate([r[0:SUBLANES, :], jnp.broadcast_to(dest_row, (SUBLANES, tb)),
                                jnp.zeros((LANES - 2 * SUBLANES, tb), F32)], axis=0).T
        dest_ref[...] = side
        lane = lax.broadcasted_iota(jnp.int32, (tb, LANES), 1)
        gates = jnp.where(lane < EXPERTS_PER_GROUP, side, 0.0)
        g_hi = gates.astype(BF16)
        g_lo = (gates - g_hi.astype(F32)).astype(BF16)
        xcat = jnp.concatenate([x_ref[...], g_hi, g_lo], axis=1)

        def move(c):
            rr = (lax.broadcasted_iota(jnp.int32, (ch, tb), 0) + c * ch).astype(F32)
            p = jnp.where(rr == dest_row, 1.0, 0.0).astype(BF16)
            moved = _dot(p, xcat)
            xs_ref[c * ch:(c + 1) * ch, :] = moved[:, 0:d].astype(BF16)
            gs_ref[c * ch:(c + 1) * ch, :] = moved[:, d:d + LANES] + moved[:, d + LANES:d + 2 * LANES]

        for c in range(rows // ch - 1):
            move(c)

        @pl.when(used > tail_row)
        def _():
            move(rows // ch - 1)

        ys_ref[...] = jnp.zeros_like(ys_ref)

    off = off_ref[i * N_GROUPS + g]
    n_sub = nsub_ref[i * N_GROUPS + g]

    def expert_rows(r0, size):
        xt = xs_ref[pl.ds(r0, size), :]
        gt = gs_ref[pl.ds(r0, size), :]
        acts = []
        for e in range(EXPERTS_PER_GROUP):
            hg = _dot(xt, wg_ref[0, 0, e])
            up = _dot(xt, wu_ref[0, 0, e])
            acts.append((_silu(hg) * up * gt[:, e:e + 1]).astype(BF16))
        act_all = jnp.concatenate(acts, axis=1)
        wd_all = wd_ref[0, 0].reshape(EXPERTS_PER_GROUP * wd_ref.shape[3], d)
        ys_ref[pl.ds(r0, size), :] = _dot(act_all, wd_all).astype(BF16)

    def pair(j, carry):
        expert_rows(pl.multiple_of(off + j * (2 * MOE_SUB), MOE_SUB), 2 * MOE_SUB)
        return carry

    lax.fori_loop(0, n_sub // 2, pair, 0)

    @pl.when(n_sub % 2 == 1)
    def _():
        expert_rows(pl.multiple_of(off + (n_sub - 1) * MOE_SUB, MOE_SUB), MOE_SUB)

    @pl.when(g == N_GROUPS - 1)
    def _():
        def pick(c, lo, hi):
            cc = (lax.broadcasted_iota(jnp.int32, (ch, hi - lo), 1) + lo).astype(F32)
            dest = dest_ref[c * ch:(c + 1) * ch, DEST_LANE:DEST_LANE + 1]
            return _dot(jnp.where(cc == dest, 1.0, 0.0).astype(BF16), ys_ref[lo:hi, :])

        for c in range(tb // ch):
            y_ref[c * ch:(c + 1) * ch, :] = pick(c, 0, tail_row).astype(y_ref.dtype)

        @pl.when(used > tail_row)
        def _():
            for c in range(tb // ch):
                sl = slice(c * ch, (c + 1) * ch)
                y_ref[sl, :] = (y_ref[sl, :].astype(F32) + pick(c, tail_row, rows)).astype(y_ref.dtype)


def _hier_moe(xn, route, wg, wu, wd, layer):
    t, d = xn.shape
    tb = MOE_BLOCK
    nb = t // tb
    e, ff = wg.shape[2], wg.shape[4]
    group = route[GROUP_LANE, :].astype(jnp.int32).reshape(nb, tb)
    counts = jnp.sum((group[:, :, None] == jnp.arange(N_GROUPS)[None, None, :]).astype(jnp.int32), axis=1)
    nsub = (counts + MOE_SUB - 1) // MOE_SUB
    padded = nsub * MOE_SUB
    off = jnp.cumsum(padded, axis=1) - padded
    grid_spec = pltpu.PrefetchScalarGridSpec(
        num_scalar_prefetch=2,
        grid=(nb, N_GROUPS),
        in_specs=[
            pl.BlockSpec((tb, d), lambda i, g, *_: (i, 0)),
            pl.BlockSpec((ROUTE_ROWS, tb), lambda i, g, *_: (0, i)),
            pl.BlockSpec((1, 1, e, d, ff), lambda i, g, *_: (layer, g, 0, 0, 0)),
            pl.BlockSpec((1, 1, e, d, ff), lambda i, g, *_: (layer, g, 0, 0, 0)),
            pl.BlockSpec((1, 1, e, ff, d), lambda i, g, *_: (layer, g, 0, 0, 0)),
        ],
        out_specs=pl.BlockSpec((tb, d), lambda i, g, *_: (i, 0)),
        scratch_shapes=[
            pltpu.VMEM((MOE_ROWS, d), BF16),
            pltpu.VMEM((MOE_ROWS, LANES), F32),
            pltpu.VMEM((MOE_ROWS, d), BF16),
            pltpu.VMEM((tb, LANES), F32),
        ],
    )
    return pl.pallas_call(
        _moe_kernel,
        grid_spec=grid_spec,
        out_shape=jax.ShapeDtypeStruct((t, d), BF16),
        compiler_params=_params("parallel", "arbitrary"),
        name="moe",
    )(off.reshape(-1).astype(jnp.int32), nsub.reshape(-1).astype(jnp.int32), xn, route, wg, wu, wd)


def _final_kernel(x_ref, y_ref, g_ref, o_ref):
    o_ref[...] = _rms(x_ref[...] + y_ref[...].astype(F32), g_ref[...])


def _final(x, y, g):
    t, d = x.shape
    tm = ROW_TILE
    row = pl.BlockSpec((tm, d), lambda i: (i, 0))
    return pl.pallas_call(
        _final_kernel,
        grid=(t // tm,),
        in_specs=[row, row, _full(g.shape)],
        out_specs=row,
        out_shape=jax.ShapeDtypeStruct((t, d), F32),
        compiler_params=_params("parallel"),
        name="final_norm",
    )(x, y, g)


def _hi_lo(w):
    hi = w.astype(BF16)
    return hi, (w - hi.astype(F32)).astype(BF16)


def _router_params(w_rg, b_rg, w_re, b_re):
    d = w_rg.shape[0]
    n_e = N_GROUPS * EXPERTS_PER_GROUP
    pad = jnp.zeros((d, EXPERT_LANE0 - N_GROUPS), F32)
    tail = jnp.zeros((d, LANES - EXPERT_LANE0 - n_e), F32)
    w = jnp.concatenate([w_rg, pad, w_re, tail], axis=1)
    b = jnp.concatenate([b_rg, pad[0], b_re, tail[0]]).reshape(LANES, 1)
    hi, lo = _hi_lo(w)
    return jnp.concatenate([hi, lo], axis=1).T, b


def kernel(x, positions, norm_mix, norm_ffn, ab_w_in, ab_gate_bias, ab_conv_w, ab_mlstm_norm, ab_lambda,
           ab_diff_norm, ab_w_out, c_w_in, c_conv_w, c_w_out, moe_w_rg, moe_b_rg, moe_w_re, moe_b_re,
           moe_w_gate, moe_w_up, moe_w_down, final_norm):
    batch, seq, d = x.shape
    t = batch * seq
    assert seq % max(ATTN_BLOCK, CONV_ROW_TILE, min(seq, MLSTM_STEP_ROWS)) == 0 and seq % MLSTM_CHUNK == 0, seq
    assert t % max(ROW_TILE, MOE_BLOCK) == 0, (batch, seq)
    assert d == MLSTM_HEADS * MLSTM_HEAD_DIM + DIFF_HEADS * DIFF_V_DIM, d
    xf = x.reshape(t, d)
    row2 = lambda v: v.reshape(1, -1).astype(F32)

    w_in = ab_w_in[0]
    mw = 2 * MLSTM_HEADS * MLSTM_HEAD_DIM
    hw = MLSTM_HEADS * MLSTM_HEAD_DIM
    ng = 2 * MLSTM_HEADS
    dw = DIFF_HEADS * DIFF_V_DIM
    o_g = mw + 2 * hw
    o_q = o_g + ng
    o_k, o_dv = o_q + dw, o_q + 2 * dw
    w0 = jnp.concatenate([w_in[:, :mw], w_in[:, mw + hw:o_g], w_in[:, o_k:o_dv]], axis=1).astype(BF16)
    w_gates = jnp.concatenate([w_in[:, o_g:o_q], jnp.zeros((d, LANES - ng), F32)], axis=1)
    wgh, wgl = _hi_lo(w_gates)
    wvt = jnp.concatenate([w_in[:, mw:mw + hw].astype(BF16), w_in[:, o_dv:o_dv + dw].astype(BF16),
                           w_in[:, o_q:o_k].astype(BF16), wgh, wgl], axis=1).T
    gb = jnp.concatenate([ab_gate_bias[0].astype(F32), jnp.zeros((LANES - ng,), F32)]).reshape(LANES, 1)

    cost, sint = _rope_table(positions.reshape(t).astype(jnp.int32))
    mqk, mo, dqt, dk, mvt, dvt, gates = _proj0(xf, cost, sint, row2(norm_mix[0]), w0, wvt, gb)
    hm = _mlstm(mqk, mvt, mo, gates, ab_conv_w[0].astype(F32), row2(ab_mlstm_norm[0]), batch, seq)
    lam_init = 0.8 - 0.6 * math.exp(-0.3 * 0)
    hd = _diff_attn(dqt, dk, dvt, ab_lambda[0].astype(F32), row2(ab_diff_norm[0]), batch, seq, lam_init)

    wrt, br = _router_params(moe_w_rg[0], moe_b_rg[0], moe_w_re[0], moe_b_re[0])
    xmid, xn, route = _mix0_tail(xf, hm, hd, ab_w_out[0].astype(BF16), row2(norm_ffn[0]), wrt, br)
    wg, wu, wd = moe_w_gate.astype(BF16), moe_w_up.astype(BF16), moe_w_down.astype(BF16)
    y = _hier_moe(xn, route, wg, wu, wd, 0)

    x1, bgate, u = _proj1(xmid, y, row2(norm_mix[1]), c_w_in[0].astype(BF16))
    wrt, br = _router_params(moe_w_rg[1], moe_b_rg[1], moe_w_re[1], moe_b_re[1])
    xmid, xn, route = _mix1_tail(x1, bgate, u, c_conv_w[0].astype(F32), c_w_out[0].astype(BF16),
                                 row2(norm_ffn[1]), wrt, br, seq)
    y = _hier_moe(xn, route, wg, wu, wd, 1)

    return _final(xmid, y, row2(final_norm)).reshape(batch, seq, d)
```

```python
import functools
import math

import jax
import jax.numpy as jnp
from jax import lax
from jax.experimental import pallas as pl
from jax.experimental.pallas import tpu as pltpu

F32 = jnp.float32
BF16 = jnp.bfloat16

EPS = 1e-6
LOG2E = math.log2(math.e)
ROPE_THETA = 10000.0
LANES = 128
SUBLANES = 8
MLSTM_HEADS = 4
MLSTM_HEAD_DIM = 128
MLSTM_CONV = 4
DIFF_HEADS = 4
DIFF_HEAD_DIM = 64
DIFF_V_DIM = 128
CONV_WIDTH = 3
N_GROUPS = 4
EXPERTS_PER_GROUP = 8
HALO = 8

ROW_TILE = 1024
CONV_ROW_TILE = 1024
MLSTM_CHUNK = 256
MLSTM_STEP_ROWS = 1024
ATTN_BLOCK = 512
ATTN_KEY_BLOCK = ATTN_BLOCK // 2
MOE_BLOCK = 1024
MOE_SUB = 128
MOE_ROWS = MOE_BLOCK + N_GROUPS * MOE_SUB
MOE_CHUNK = 256
VMEM_LIMIT = 56 * 1024 * 1024


def _dot(a, b):
    return jnp.dot(a, b, preferred_element_type=F32)


def _dot_nt(a, b):
    return lax.dot_general(a, b, (((1,), (1,)), ((), ())), preferred_element_type=F32)


def _split3(x):
    hi = x.astype(BF16)
    r1 = x - hi.astype(F32)
    mid = r1.astype(BF16)
    lo = (r1 - mid.astype(F32)).astype(BF16)
    return hi, mid, lo


def _rms(x, g):
    return x * lax.rsqrt(jnp.mean(x * x, axis=-1, keepdims=True) + EPS) * g


def _silu(x):
    h = 0.5 * x
    return h + h * jnp.tanh(h)


def _sigmoid(x):
    return 0.5 * jnp.tanh(0.5 * x) + 0.5


def _shift_matrix(rows, taps):
    r = jnp.arange(rows)
    return jnp.concatenate([r[:, None] - s == r[None, :] for s in range(1, taps)], axis=0).astype(BF16)


def _shift_conv(xb, shift_ref, cw, tail, taps):
    rows, c = xb.shape
    x32 = xb.astype(F32)
    shifted = _dot(shift_ref[...], xb)
    last = taps - 1
    conv = cw[last:last + 1, :] * x32
    for s in range(1, taps):
        conv = conv + cw[last - s:last - s + 1, :] * shifted[(s - 1) * rows:s * rows, :]
    head = jnp.concatenate([tail, x32[0:HALO, :]], axis=0)
    rowi = lax.broadcasted_iota(jnp.int32, (HALO, c), 0)
    fix = jnp.zeros((HALO, c), F32)
    for s in range(1, taps):
        fix = fix + cw[last - s:last - s + 1, :] * jnp.where(rowi < s, head[HALO - s:2 * HALO - s, :], 0.0)
    return jnp.concatenate([conv[0:HALO, :] + fix, conv[HALO:, :]], axis=0)


def _roll_conv(xb, cw, tail, taps):
    rows, c = xb.shape
    x32 = xb.astype(F32)
    last = taps - 1
    conv = cw[last:last + 1, :] * x32
    head = jnp.concatenate([tail, x32[0:HALO, :]], axis=0)
    first = cw[last:last + 1, :] * x32[0:HALO, :]
    for s in range(1, taps):
        w = cw[last - s:last - s + 1, :]
        conv = conv + w * pltpu.roll(x32, s, 0)
        first = first + w * head[HALO - s:2 * HALO - s, :]
    return jnp.concatenate([first, conv[HALO:, :]], axis=0)


def _params(*sem):
    return pltpu.CompilerParams(dimension_semantics=sem, vmem_limit_bytes=VMEM_LIMIT)


def _full(shape):
    return pl.BlockSpec(shape, lambda *_: (0,) * len(shape))


def _proj0_kernel(x_ref, cos_ref, sin_ref, cost_ref, sint_ref, g_ref, w_ref, wvt_ref, gb_ref,
                  mqk_ref, mo_ref, dqt_ref, dk_ref, mvt_ref, dvt_ref, gates_ref):
    xn = _rms(x_ref[...], g_ref[...])
    xb = xn.astype(BF16)
    xl = (xn - xb.astype(F32)).astype(BF16)
    half = DIFF_HEAD_DIM // 2
    dw = DIFF_HEADS * DIFF_V_DIM

    main = _dot(xb, w_ref[...])

    def seg(lo, hi):
        return main[:, lo:hi]

    mqk_ref[...] = seg(0, 1024).astype(BF16)
    mo_ref[...] = seg(1024, 1536).astype(BF16)
    vt = _dot_nt(wvt_ref[...], xb)
    mvt_ref[...] = vt[0:dw, :].astype(BF16)
    dvt_ref[...] = vt[dw:2 * dw, :].astype(BF16)
    g0 = 3 * dw
    gates_ref[...] = (vt[g0:g0 + LANES, :] + (vt[g0 + LANES:g0 + 2 * LANES, :]
                      + _dot_nt(wvt_ref[g0:g0 + LANES, :], xl))) + gb_ref[...]

    ct = cost_ref[...]
    st = sint_ref[...]
    q_scale = LOG2E * DIFF_HEAD_DIM ** -0.5
    for r in range(2 * DIFF_HEADS):
        lo = 2 * dw + r * DIFF_HEAD_DIM
        x1 = vt[lo:lo + half, :]
        x2 = vt[lo + half:lo + 2 * half, :]
        out = lo - 2 * dw
        dqt_ref[out:out + half, :] = ((x1 * ct - x2 * st) * q_scale).astype(BF16)
        dqt_ref[out + half:out + 2 * half, :] = ((x2 * ct + x1 * st) * q_scale).astype(BF16)

    heads = 2 * DIFF_HEADS
    c = cos_ref[...]
    s = sin_ref[...]
    c4 = jnp.concatenate([c, c] * heads, axis=1)
    s4 = jnp.concatenate([-s, s] * heads, axis=1)
    lane4 = lax.broadcasted_iota(jnp.int32, c4.shape, 1)
    first4 = (lane4 & (DIFF_HEAD_DIM - 1)) < half
    k = seg(1536, 2048)
    partner = jnp.where(first4, pltpu.roll(k, dw - half, 1), pltpu.roll(k, half, 1))
    dk = (k * c4 + partner * s4).astype(BF16)
    for h in range(DIFF_HEADS):
        dk_ref[h] = dk[:, h * DIFF_V_DIM:(h + 1) * DIFF_V_DIM]


def _rope_table_kernel(pos_ref, invf_ref, cos_ref, sin_ref):
    ang = invf_ref[...] * pos_ref[...].astype(F32)
    cos_ref[...] = jnp.cos(ang)
    sin_ref[...] = jnp.sin(ang)


def _rope_table(positions):
    t = positions.shape[0]
    half = DIFF_HEAD_DIM // 2
    inv_freq = (1.0 / (ROPE_THETA ** (jnp.arange(half, dtype=F32) / half))).reshape(half, 1)
    tl = min(t, 4096)
    return pl.pallas_call(
        _rope_table_kernel,
        grid=(t // tl,),
        in_specs=[pl.BlockSpec((1, tl), lambda i: (0, i)), _full(inv_freq.shape)],
        out_specs=[pl.BlockSpec((half, tl), lambda i: (0, i))] * 2,
        out_shape=[jax.ShapeDtypeStruct((half, t), F32)] * 2,
        compiler_params=_params("parallel"),
        name="rope_table",
    )(positions.reshape(1, t), inv_freq)


def _proj0(x, cost, sint, g, w, wvt, gb):
    t, d = x.shape
    tm = ROW_TILE
    row = lambda n: pl.BlockSpec((tm, n), lambda i: (i, 0))
    col = lambda n: pl.BlockSpec((n, tm), lambda i: (0, i))
    heads = pl.BlockSpec((DIFF_HEADS, tm, DIFF_V_DIM), lambda i: (0, i, 0))
    cos, sin = cost.T, sint.T
    half = cost.shape[0]
    out_shapes = [
        jax.ShapeDtypeStruct((t, 1024), BF16), jax.ShapeDtypeStruct((t, 512), BF16),
        jax.ShapeDtypeStruct((512, t), BF16), jax.ShapeDtypeStruct((DIFF_HEADS, t, DIFF_V_DIM), BF16),
        jax.ShapeDtypeStruct((512, t), BF16), jax.ShapeDtypeStruct((512, t), BF16),
        jax.ShapeDtypeStruct((LANES, t), F32),
    ]
    return pl.pallas_call(
        _proj0_kernel,
        grid=(t // tm,),
        in_specs=[row(d), row(half), row(half), col(half), col(half), _full(g.shape), _full(w.shape),
                  _full(wvt.shape), _full(gb.shape)],
        out_specs=[row(1024), row(512), col(512), heads, col(512), col(512), col(LANES)],
        out_shape=out_shapes,
        compiler_params=_params("parallel"),
        name="proj0",
    )(x, cos, sin, cost, sint, g, w, wvt, gb)


def _mlstm_kernel(mqk_ref, mvt_ref, mo_ref, gates_ref, shift_ref, cw_ref, norm_ref, hm_ref,
                  tail_ref, cn_ref, m_ref):
    L = MLSTM_CHUNK
    hd = MLSTM_HEAD_DIM
    width = MLSTM_HEADS * hd

    @pl.when(pl.program_id(1) == 0)
    def _():
        tail_ref[...] = jnp.zeros_like(tail_ref)
        cn_ref[...] = jnp.zeros_like(cn_ref)
        m_ref[...] = jnp.zeros_like(m_ref)

    krow = lax.broadcasted_iota(jnp.int32, (L, L), 0)
    qcol = lax.broadcasted_iota(jnp.int32, (L, L), 1)
    causal_t = krow <= qcol
    triu = jnp.where(causal_t, 1.0, 0.0).astype(BF16)
    ones = jnp.ones((SUBLANES, L), BF16)

    def chunk(ci, carry):
        rows = pl.ds(pl.multiple_of(ci * L, L), L)
        xb = mqk_ref[rows, :]
        conv = _shift_conv(xb, shift_ref, cw_ref[...], tail_ref[...], MLSTM_CONV)
        tail_ref[...] = xb[L - HALO:L, :].astype(F32)
        qk = _silu(conv)
        qb = (qk[:, :width] * (hd ** -0.5)).astype(BF16)
        k = qk[:, width:]
        kb = k.astype(BF16)

        g8 = gates_ref[0:SUBLANES, rows]
        logf = (jnp.minimum(g8, 0.0) - jnp.log(1.0 + jnp.exp(-jnp.abs(g8)))) * LOG2E
        f_hi, f_mid, f_lo = _split3(logf)
        b8 = _dot(f_hi, triu) + (_dot(f_mid, triu) + _dot(f_lo, triu))
        cj8 = g8 * LOG2E - pltpu.roll(b8, MLSTM_HEADS, 0)
        cj_all = jnp.concatenate([cj8, jnp.zeros((LANES - SUBLANES, L), F32)], axis=0).T

        for h in range(MLSTM_HEADS):
            sl = slice(h * hd, (h + 1) * hd)
            b_row = b8[MLSTM_HEADS + h:MLSTM_HEADS + h + 1, :]
            m_prev = m_ref[h:h + 1, 0:1]
            cn_prev = cn_ref[h]
            vt1 = jnp.concatenate([mvt_ref[sl, rows], ones], axis=0)

            cmat = jnp.where(causal_t, cj_all[:, h:h + 1], -jnp.inf)
            mm = jnp.maximum(m_prev, jnp.max(cmat, axis=0, keepdims=True))
            w_t = jnp.exp2(cmat - mm)
            s_t = _dot_nt(kb[:, sl], qb[:, sl]) * w_t
            inter_w = jnp.exp2(m_prev - mm)
            tot = _dot(vt1, s_t.astype(BF16)) + inter_w * _dot_nt(cn_prev.astype(BF16), qb[:, sl])
            den = tot[hd:hd + 1, :]
            h_t = tot[0:hd, :] / jnp.maximum(jnp.abs(den), jnp.exp2(-(b_row + mm)))

            kw = k[:, sl] * w_t[:, L - 1:L]
            cn_ref[h] = inter_w[:, L - 1:L] * cn_prev + _dot(vt1, kw.astype(BF16))
            m_ref[h:h + 1, :] = jnp.broadcast_to(b_row[:, L - 1:L] + mm[:, L - 1:L], (1, LANES))

            scale = lax.rsqrt(jnp.mean(h_t * h_t, axis=0, keepdims=True) + EPS)
            hn = (h_t * scale).T * norm_ref[...]
            hm_ref[rows, sl] = (hn * _sigmoid(mo_ref[rows, sl].astype(F32))).astype(BF16)
        return carry

    lax.fori_loop(0, mqk_ref.shape[0] // L, chunk, 0)


def _mlstm(mqk, mvt, mo, gates, cw, norm, batch, seq):
    t = mqk.shape[0]
    L = MLSTM_CHUNK
    rows = min(seq, MLSTM_STEP_ROWS)
    nc = seq // rows
    width = MLSTM_HEADS * MLSTM_HEAD_DIM
    row = lambda n: pl.BlockSpec((rows, n), lambda b, c: (b * nc + c, 0))
    col = lambda n: pl.BlockSpec((n, rows), lambda b, c: (0, b * nc + c))
    shift = _shift_matrix(L, MLSTM_CONV)
    return pl.pallas_call(
        _mlstm_kernel,
        grid=(batch, nc),
        in_specs=[row(2 * width), col(width), row(width), col(LANES), _full(shift.shape), _full(cw.shape),
                  _full(norm.shape)],
        out_specs=row(width),
        out_shape=jax.ShapeDtypeStruct((t, width), BF16),
        scratch_shapes=[
            pltpu.VMEM((HALO, 2 * width), F32),
            pltpu.VMEM((MLSTM_HEADS, MLSTM_HEAD_DIM + SUBLANES, MLSTM_HEAD_DIM), F32),
            pltpu.VMEM((SUBLANES, LANES), F32),
        ],
        compiler_params=_params("parallel", "arbitrary"),
        name="mlstm",
    )(mqk, mvt, mo, gates, shift, cw, norm)


def _diff_attn_kernel(qt_ref, k_ref, vt_ref, lam_ref, norm_ref, o_ref,
                      acc0_ref, acc1_ref, m_ref, sa0_ref, sa1_ref, sb0_ref, sb1_ref, *, lam_init):
    tq = ATTN_BLOCK
    tk = ATTN_KEY_BLOCK
    dv = DIFF_V_DIM
    nq = k_ref.shape[1] // tq
    frow = lax.broadcasted_iota(jnp.int32, (dv, tq), 0)
    krow = lax.broadcasted_iota(jnp.int32, (tk, tq), 0)
    qcol = lax.broadcasted_iota(jnp.int32, (tk, tq), 1)
    diag_masks = (krow <= qcol, krow + tk <= qcol)
    ones = jnp.ones((SUBLANES, tk), BF16)
    accs = (acc0_ref, acc1_ref)
    buf_a = (sa0_ref, sa1_ref)
    buf_b = (sb0_ref, sb1_ref)
    lv = lam_ref[...]
    lam = (jnp.exp(jnp.sum(lv[0:1, :] * lv[1:2, :], axis=-1, keepdims=True))
           - jnp.exp(jnp.sum(lv[2:3, :] * lv[3:4, :], axis=-1, keepdims=True)) + lam_init)

    def queries(qi):
        qt = qt_ref[:, pl.ds(pl.multiple_of(qi * tq, tq), tq)]
        zero = jnp.zeros_like(qt)
        return (jnp.where(frow < DIFF_HEAD_DIM, qt, zero), jnp.where(frow >= DIFF_HEAD_DIM, qt, zero))

    def scores_of(qs, j, dst):
        kb = k_ref[0, pl.ds(pl.multiple_of(j * tk, tk), tk), :]
        for c in range(2):
            dst[c][...] = _dot(kb, qs[c])

    def reset():
        for acc in accs:
            acc[...] = jnp.zeros_like(acc)
        m_ref[...] = jnp.full(m_ref.shape, -jnp.inf, F32)

    reset()
    scores_of(queries(0), 0, buf_a)

    def q_block(qi, carry):
        qrow = pl.multiple_of(qi * tq, tq)
        qs = queries(qi)

        def scores(j, dst):
            scores_of(qs, j, dst)

        def absorb(j, src, mask):
            start = pl.multiple_of(j * tk, tk)
            vt1 = jnp.concatenate([vt_ref[:, pl.ds(start, tk)], ones], axis=0)
            s = [src[c][...] for c in range(2)]
            if mask is not None:
                s = [jnp.where(mask, sc, -jnp.inf) for sc in s]
            m_old = [m_ref[c:c + 1, :] for c in range(2)]
            m_new = [jnp.maximum(m_old[c], jnp.max(s[c], axis=0, keepdims=True)) for c in range(2)]
            p = [jnp.exp2(s[c] - m_new[c]).astype(BF16) for c in range(2)]
            alpha = [jnp.exp2(m_old[c] - m_new[c]) for c in range(2)]
            pv = [_dot(vt1, p[c]) for c in range(2)]
            for c in range(2):
                accs[c][...] = alpha[c] * accs[c][...] + pv[c]
                m_ref[c:c + 1, :] = m_new[c]

        def pair(jj, inner):
            j = 2 * jj
            scores(j + 1, buf_b)
            absorb(j, buf_a, None)
            scores(j + 2, buf_a)
            absorb(j + 1, buf_b, None)
            return inner

        lax.fori_loop(0, qi, pair, 0)

        scores(2 * qi + 1, buf_b)
        absorb(2 * qi, buf_a, diag_masks[0])
        absorb(2 * qi + 1, buf_b, diag_masks[1])

        scores_of(queries(jnp.minimum(qi + 1, nq - 1)), 0, buf_a)

        a1 = acc0_ref[...]
        a2 = acc1_ref[...]
        o_t = a1[0:dv, :] / a1[dv:dv + 1, :] - lam * (a2[0:dv, :] / a2[dv:dv + 1, :])
        o_ref[0, pl.ds(qrow, tq), :] = (_rms(o_t.T, norm_ref[...]) * (1.0 - lam_init)).astype(BF16)
        reset()
        return carry

    lax.fori_loop(0, nq, q_block, 0)


def _diff_attn(dqt, dk, dvt, lam_vecs, norm, batch, seq, lam_init):
    t = dk.shape[1]
    tq = ATTN_BLOCK
    rows = pl.BlockSpec((1, seq, DIFF_V_DIM), lambda b, h: (h, b, 0))
    cols = pl.BlockSpec((DIFF_V_DIM, seq), lambda b, h: (h, b))
    return pl.pallas_call(
        functools.partial(_diff_attn_kernel, lam_init=lam_init),
        grid=(batch, DIFF_HEADS),
        in_specs=[cols, rows, cols, _full(lam_vecs.shape), _full(norm.shape)],
        out_specs=rows,
        out_shape=jax.ShapeDtypeStruct((DIFF_HEADS, t, DIFF_V_DIM), BF16),
        scratch_shapes=([pltpu.VMEM((DIFF_V_DIM + SUBLANES, tq), F32)] * 2 + [pltpu.VMEM((SUBLANES, tq), F32)]
                        + [pltpu.VMEM((ATTN_KEY_BLOCK, tq), F32)] * 4),
        compiler_params=_params("parallel", "parallel"),
        name="diff_attn",
    )(dqt, dk, dvt, lam_vecs, norm)


ROUTE_ROWS = 2 * SUBLANES
GROUP_LANE = EXPERTS_PER_GROUP
DEST_LANE = EXPERTS_PER_GROUP
EXPERT_LANE0 = 8


def _route(xn, wrt_ref, br_ref):
    xh = xn.astype(BF16)
    xl = (xn - xh.astype(F32)).astype(BF16)
    both = _dot_nt(wrt_ref[...], xh)
    logits = both[0:LANES, :] + (both[LANES:2 * LANES, :] + _dot_nt(wrt_ref[0:LANES, :], xl)) + br_ref[...]
    e = EXPERTS_PER_GROUP
    tm = logits.shape[1]
    row = lax.broadcasted_iota(jnp.int32, (e, tm), 0)
    ninf = -jnp.inf
    big = jnp.int32(LANES)

    is_g = row < N_GROUPS
    gl = jnp.where(is_g, logits[0:e, :], ninf)
    gmax = jnp.max(gl, axis=0, keepdims=True)
    gsum = jnp.sum(jnp.where(is_g, jnp.exp(gl - gmax), 0.0), axis=0, keepdims=True)
    g_val = 1.0 / gsum
    g_idx = jnp.min(jnp.where(gl == gmax, row, big), axis=0, keepdims=True)

    el = logits[EXPERT_LANE0:EXPERT_LANE0 + e, :]
    for g in range(1, N_GROUPS):
        lo = EXPERT_LANE0 + e * g
        el = jnp.where(g_idx == g, logits[lo:lo + e, :], el)
    v1 = jnp.max(el, axis=0, keepdims=True)
    i1 = jnp.min(jnp.where(el == v1, row, big), axis=0, keepdims=True)
    el2 = jnp.where(row == i1, ninf, el)
    v2 = jnp.max(el2, axis=0, keepdims=True)
    i2 = jnp.min(jnp.where(el2 == v2, row, big), axis=0, keepdims=True)
    e21 = jnp.exp(v2 - v1)
    w1 = 1.0 / (1.0 + e21)
    w2 = e21 * w1
    gates = g_val * (jnp.where(row == i1, w1, 0.0) + jnp.where(row == i2, w2, 0.0))
    tail = jnp.where(row == 0, g_idx.astype(F32), 0.0)
    return xh, jnp.concatenate([gates, tail], axis=0)


def _mix0_tail_kernel(x_ref, hm_ref, hd_ref, wo_ref, nf_ref, wrt_ref, br_ref,
                      xmid_ref, xn_ref, route_ref):
    merged = jnp.concatenate([hm_ref[...]] + [hd_ref[h] for h in range(DIFF_HEADS)], axis=1)
    xmid = x_ref[...] + _dot(merged, wo_ref[...])
    xmid_ref[...] = xmid
    xn_ref[...], route_ref[...] = _route(_rms(xmid, nf_ref[...]), wrt_ref, br_ref)


def _mix0_tail(x, hm, hd, wo, nf, wrt, br):
    t, d = x.shape
    tm = ROW_TILE
    row = lambda n: pl.BlockSpec((tm, n), lambda i: (i, 0))
    return pl.pallas_call(
        _mix0_tail_kernel,
        grid=(t // tm,),
        in_specs=[row(d), row(hm.shape[1]), pl.BlockSpec((DIFF_HEADS, tm, DIFF_V_DIM), lambda i: (0, i, 0)),
                  _full(wo.shape), _full(nf.shape), _full(wrt.shape), _full(br.shape)],
        out_specs=[row(d), row(d), pl.BlockSpec((ROUTE_ROWS, tm), lambda i: (0, i))],
        out_shape=[jax.ShapeDtypeStruct((t, d), F32), jax.ShapeDtypeStruct((t, d), BF16),
                   jax.ShapeDtypeStruct((ROUTE_ROWS, t), F32)],
        compiler_params=_params("parallel"),
        name="mix0_tail",
    )(x, hm, hd, wo, nf, wrt, br)


def _proj1_kernel(x_ref, y_ref, g_ref, w_ref, x1_ref, b_ref, u_ref):
    d = x_ref.shape[1]
    x1 = x_ref[...] + y_ref[...].astype(F32)
    x1_ref[...] = x1
    xb = _rms(x1, g_ref[...]).astype(BF16)
    proj = _dot(xb, w_ref[...])
    b_ref[...] = proj[:, 0:d].astype(BF16)
    u_ref[...] = (proj[:, d:2 * d] * proj[:, 2 * d:3 * d]).astype(BF16)


def _proj1(x, y, g, w):
    t, d = x.shape
    tm = ROW_TILE
    row = lambda n: pl.BlockSpec((tm, n), lambda i: (i, 0))
    return pl.pallas_call(
        _proj1_kernel,
        grid=(t // tm,),
        in_specs=[row(d), row(d), _full(g.shape), _full(w.shape)],
        out_specs=[row(d), row(d), row(d)],
        out_shape=[jax.ShapeDtypeStruct((t, d), F32), jax.ShapeDtypeStruct((t, d), BF16),
                   jax.ShapeDtypeStruct((t, d), BF16)],
        compiler_params=_params("parallel"),
        name="proj1",
    )(x, y, g, w)


def _mix1_tail_kernel(x_ref, b_ref, u_ref, halo_ref, cw_ref, wo_ref, nf_ref, wrt_ref, br_ref,
                      xmid_ref, xn_ref, route_ref, *, tiles_per_seq):
    seq_start = pl.program_id(0) % tiles_per_seq == 0
    halo = halo_ref[...].astype(F32)
    tail = jnp.where(seq_start, jnp.zeros_like(halo), halo)
    conv = _roll_conv(u_ref[...], cw_ref[...], tail, CONV_WIDTH)
    mixed = (b_ref[...].astype(F32) * conv).astype(BF16)
    xmid = x_ref[...] + _dot(mixed, wo_ref[...])
    xmid_ref[...] = xmid
    xn_ref[...], route_ref[...] = _route(_rms(xmid, nf_ref[...]), wrt_ref, br_ref)


def _mix1_tail(x, b, u, cw, wo, nf, wrt, br, seq):
    t, d = x.shape
    tm = CONV_ROW_TILE
    row = lambda n: pl.BlockSpec((tm, n), lambda i: (i, 0))
    halo = pl.BlockSpec((HALO, d), lambda i: (jnp.maximum(i * (tm // HALO) - 1, 0), 0))
    return pl.pallas_call(
        functools.partial(_mix1_tail_kernel, tiles_per_seq=seq // tm),
        grid=(t // tm,),
        in_specs=[row(d), row(d), row(d), halo, _full(cw.shape), _full(wo.shape),
                  _full(nf.shape), _full(wrt.shape), _full(br.shape)],
        out_specs=[row(d), row(d), pl.BlockSpec((ROUTE_ROWS, tm), lambda i: (0, i))],
        out_shape=[jax.ShapeDtypeStruct((t, d), F32), jax.ShapeDtypeStruct((t, d), BF16),
                   jax.ShapeDtypeStruct((ROUTE_ROWS, t), F32)],
        compiler_params=_params("parallel"),
        name="mix1_tail",
    )(x, b, u, u, cw, wo, nf, wrt, br)


def _moe_kernel(off_ref, nsub_ref, x_ref, r_ref, wg_ref, wu_ref, wd_ref, y_ref,
                xs_ref, gs_ref, ys_ref, dest_ref):
    i = pl.program_id(0)
    g = pl.program_id(1)
    tb = MOE_BLOCK
    rows = MOE_ROWS
    ch = MOE_CHUNK
    d = x_ref.shape[1]
    tail_row = rows - ch
    last_group = i * N_GROUPS + N_GROUPS - 1
    used = off_ref[last_group] + nsub_ref[last_group] * MOE_SUB

    @pl.when(g == 0)
    def _():
        r = r_ref[...]
        grp = r[GROUP_LANE:GROUP_LANE + 1, :]
        row8 = lax.broadcasted_iota(jnp.int32, (SUBLANES, tb), 0)
        onehot = jnp.where(row8.astype(F32) == grp, 1.0, 0.0)
        row1 = lax.broadcasted_iota(jnp.int32, (SUBLANES, 1), 0)
        base = jnp.zeros((SUBLANES, 1), F32)
        for gg in range(N_GROUPS):
            base = jnp.where(row1 == gg, off_ref[i * N_GROUPS + gg].astype(F32), base)
        rr = lax.broadcasted_iota(jnp.int32, (ch, ch), 0)
        cc = lax.broadcasted_iota(jnp.int32, (ch, ch), 1)
        upper = jnp.where(rr <= cc, 1.0, 0.0).astype(BF16)
        parts = []
        for c in range(tb // ch):
            sel = onehot[:, c * ch:(c + 1) * ch]
            rank = _dot(sel.astype(BF16), upper) + base
            parts.append(jnp.sum(sel * rank, axis=0, keepdims=True) - 1.0)
            base = base + jnp.sum(sel, axis=1, keepdims=True)
        dest_row = jnp.concatenate(parts, axis=1)
        side = jnp.concatenate([r[0:SUBLANES, :], jnp.broadcast_to(dest_row, (SUBLANES, tb)),
                                jnp.zeros((LANES - 2 * SUBLANES, tb), F32)], axis=0).T
        dest_ref[...] = side
        lane = lax.broadcasted_iota(jnp.int32, (tb, LANES), 1)
        gates = jnp.where(lane < EXPERTS_PER_GROUP, side, 0.0)
        g_hi = gates.astype(BF16)
        g_lo = (gates - g_hi.astype(F32)).astype(BF16)
        xcat = jnp.concatenate([x_ref[...], g_hi, g_lo], axis=1)

        def move(c):
            rr = (lax.broadcasted_iota(jnp.int32, (ch, tb), 0) + c * ch).astype(F32)
            p = jnp.where(rr == dest_row, 1.0, 0.0).astype(BF16)
            moved = _dot(p, xcat)
            xs_ref[c * ch:(c + 1) * ch, :] = moved[:, 0:d].astype(BF16)
            gs_ref[c * ch:(c + 1) * ch, :] = moved[:, d:d + LANES] + moved[:, d + LANES:d + 2 * LANES]

        for c in range(rows // ch - 1):
            move(c)

        @pl.when(used > tail_row)
        def _():
            move(rows // ch - 1)

        ys_ref[...] = jnp.zeros_like(ys_ref)

    off = off_ref[i * N_GROUPS + g]
    n_sub = nsub_ref[i * N_GROUPS + g]

    def expert_rows(r0, size):
        xt = xs_ref[pl.ds(r0, size), :]
        gt = gs_ref[pl.ds(r0, size), :]
        acts = []
        for e in range(EXPERTS_PER_GROUP):
            hg = _dot(xt, wg_ref[0, 0, e])
            up = _dot(xt, wu_ref[0, 0, e])
            acts.append((_silu(hg) * up * gt[:, e:e + 1]).astype(BF16))
        act_all = jnp.concatenate(acts, axis=1)
        wd_all = wd_ref[0, 0].reshape(EXPERTS_PER_GROUP * wd_ref.shape[3], d)
        ys_ref[pl.ds(r0, size), :] = _dot(act_all, wd_all).astype(BF16)

    def pair(j, carry):
        expert_rows(pl.multiple_of(off + j * (2 * MOE_SUB), MOE_SUB), 2 * MOE_SUB)
        return carry

    lax.fori_loop(0, n_sub // 2, pair, 0)

    @pl.when(n_sub % 2 == 1)
    def _():
        expert_rows(pl.multiple_of(off + (n_sub - 1) * MOE_SUB, MOE_SUB), MOE_SUB)

    @pl.when(g == N_GROUPS - 1)
    def _():
        def pick(c, lo, hi):
            cc = (lax.broadcasted_iota(jnp.int32, (ch, hi - lo), 1) + lo).astype(F32)
            dest = dest_ref[c * ch:(c + 1) * ch, DEST_LANE:DEST_LANE + 1]
            return _dot(jnp.where(cc == dest, 1.0, 0.0).astype(BF16), ys_ref[lo:hi, :])

        for c in range(tb // ch):
            y_ref[c * ch:(c + 1) * ch, :] = pick(c, 0, tail_row).astype(y_ref.dtype)

        @pl.when(used > tail_row)
        def _():
            for c in range(tb // ch):
                sl = slice(c * ch, (c + 1) * ch)
                y_ref[sl, :] = (y_ref[sl, :].astype(F32) + pick(c, tail_row, rows)).astype(y_ref.dtype)


def _hier_moe(xn, route, wg, wu, wd, layer):
    t, d = xn.shape
    tb = MOE_BLOCK
    nb = t // tb
    e, ff = wg.shape[2], wg.shape[4]
    group = route[GROUP_LANE, :].astype(jnp.int32).reshape(nb, tb)
    counts = jnp.sum((group[:, :, None] == jnp.arange(N_GROUPS)[None, None, :]).astype(jnp.int32), axis=1)
    nsub = (counts + MOE_SUB - 1) // MOE_SUB
    padded = nsub * MOE_SUB
    off = jnp.cumsum(padded, axis=1) - padded
    grid_spec = pltpu.PrefetchScalarGridSpec(
        num_scalar_prefetch=2,
        grid=(nb, N_GROUPS),
        in_specs=[
            pl.BlockSpec((tb, d), lambda i, g, *_: (i, 0)),
            pl.BlockSpec((ROUTE_ROWS, tb), lambda i, g, *_: (0, i)),
            pl.BlockSpec((1, 1, e, d, ff), lambda i, g, *_: (layer, g, 0, 0, 0)),
            pl.BlockSpec((1, 1, e, d, ff), lambda i, g, *_: (layer, g, 0, 0, 0)),
            pl.BlockSpec((1, 1, e, ff, d), lambda i, g, *_: (layer, g, 0, 0, 0)),
        ],
        out_specs=pl.BlockSpec((tb, d), lambda i, g, *_: (i, 0)),
        scratch_shapes=[
            pltpu.VMEM((MOE_ROWS, d), BF16),
            pltpu.VMEM((MOE_ROWS, LANES), F32),
            pltpu.VMEM((MOE_ROWS, d), BF16),
            pltpu.VMEM((tb, LANES), F32),
        ],
    )
    return pl.pallas_call(
        _moe_kernel,
        grid_spec=grid_spec,
        out_shape=jax.ShapeDtypeStruct((t, d), BF16),
        compiler_params=_params("parallel", "arbitrary"),
        name="moe",
    )(off.reshape(-1).astype(jnp.int32), nsub.reshape(-1).astype(jnp.int32), xn, route, wg, wu, wd)


def _final_kernel(x_ref, y_ref, g_ref, o_ref):
    o_ref[...] = _rms(x_ref[...] + y_ref[...].astype(F32), g_ref[...])


def _final(x, y, g):
    t, d = x.shape
    tm = ROW_TILE
    row = pl.BlockSpec((tm, d), lambda i: (i, 0))
    return pl.pallas_call(
        _final_kernel,
        grid=(t // tm,),
        in_specs=[row, row, _full(g.shape)],
        out_specs=row,
        out_shape=jax.ShapeDtypeStruct((t, d), F32),
        compiler_params=_params("parallel"),
        name="final_norm",
    )(x, y, g)


def _hi_lo(w):
    hi = w.astype(BF16)
    return hi, (w - hi.astype(F32)).astype(BF16)


def _router_params(w_rg, b_rg, w_re, b_re):
    d = w_rg.shape[0]
    n_e = N_GROUPS * EXPERTS_PER_GROUP
    pad = jnp.zeros((d, EXPERT_LANE0 - N_GROUPS), F32)
    tail = jnp.zeros((d, LANES - EXPERT_LANE0 - n_e), F32)
    w = jnp.concatenate([w_rg, pad, w_re, tail], axis=1)
    b = jnp.concatenate([b_rg, pad[0], b_re, tail[0]]).reshape(LANES, 1)
    hi, lo = _hi_lo(w)
    return jnp.concatenate([hi, lo], axis=1).T, b


def kernel(x, positions, norm_mix, norm_ffn, ab_w_in, ab_gate_bias, ab_conv_w, ab_mlstm_norm, ab_lambda,
           ab_diff_norm, ab_w_out, c_w_in, c_conv_w, c_w_out, moe_w_rg, moe_b_rg, moe_w_re, moe_b_re,
           moe_w_gate, moe_w_up, moe_w_down, final_norm):
    batch, seq, d = x.shape
    t = batch * seq
    assert seq % max(ATTN_BLOCK, CONV_ROW_TILE, min(seq, MLSTM_STEP_ROWS)) == 0 and seq % MLSTM_CHUNK == 0, seq
    assert t % max(ROW_TILE, MOE_BLOCK) == 0, (batch, seq)
    assert d == MLSTM_HEADS * MLSTM_HEAD_DIM + DIFF_HEADS * DIFF_V_DIM, d
    xf = x.reshape(t, d)
    row2 = lambda v: v.reshape(1, -1).astype(F32)

    w_in = ab_w_in[0]
    mw = 2 * MLSTM_HEADS * MLSTM_HEAD_DIM
    hw = MLSTM_HEADS * MLSTM_HEAD_DIM
    ng = 2 * MLSTM_HEADS
    dw = DIFF_HEADS * DIFF_V_DIM
    o_g = mw + 2 * hw
    o_q = o_g + ng
    o_k, o_dv = o_q + dw, o_q + 2 * dw
    w0 = jnp.concatenate([w_in[:, :mw], w_in[:, mw + hw:o_g], w_in[:, o_k:o_dv]], axis=1).astype(BF16)
    w_gates = jnp.concatenate([w_in[:, o_g:o_q], jnp.zeros((d, LANES - ng), F32)], axis=1)
    wgh, wgl = _hi_lo(w_gates)
    wvt = jnp.concatenate([w_in[:, mw:mw + hw].astype(BF16), w_in[:, o_dv:o_dv + dw].astype(BF16),
                           w_in[:, o_q:o_k].astype(BF16), wgh, wgl], axis=1).T
    gb = jnp.concatenate([ab_gate_bias[0].astype(F32), jnp.zeros((LANES - ng,), F32)]).reshape(LANES, 1)

    cost, sint = _rope_table(positions.reshape(t).astype(jnp.int32))
    mqk, mo, dqt, dk, mvt, dvt, gates = _proj0(xf, cost, sint, row2(norm_mix[0]), w0, wvt, gb)
    hm = _mlstm(mqk, mvt, mo, gates, ab_conv_w[0].astype(F32), row2(ab_mlstm_norm[0]), batch, seq)
    lam_init = 0.8 - 0.6 * math.exp(-0.3 * 0)
    hd = _diff_attn(dqt, dk, dvt, ab_lambda[0].astype(F32), row2(ab_diff_norm[0]), batch, seq, lam_init)

    wrt, br = _router_params(moe_w_rg[0], moe_b_rg[0], moe_w_re[0], moe_b_re[0])
    xmid, xn, route = _mix0_tail(xf, hm, hd, ab_w_out[0].astype(BF16), row2(norm_ffn[0]), wrt, br)
    wg, wu, wd = moe_w_gate.astype(BF16), moe_w_up.astype(BF16), moe_w_down.astype(BF16)
    y = _hier_moe(xn, route, wg, wu, wd, 0)

    x1, bgate, u = _proj1(xmid, y, row2(norm_mix[1]), c_w_in[0].astype(BF16))
    wrt, br = _router_params(moe_w_rg[1], moe_b_rg[1], moe_w_re[1], moe_b_re[1])
    xmid, xn, route = _mix1_tail(x1, bgate, u, c_conv_w[0].astype(F32), c_w_out[0].astype(BF16),
                                 row2(norm_ffn[1]), wrt, br, seq)
    y = _hier_moe(xn, route, wg, wu, wd, 1)

    return _final(xmid, y, row2(final_norm)).reshape(batch, seq, d)
```
